```python
import math
import jax, jax.numpy as jnp
from jax import lax
import numpy as np

D_MODEL = 1024
BATCH = 32
SEQ = 256
DEPTH = 1
DEC_BATCH = 2
DEC_SEQ = 1024
PAST_LEN = 512

GRID_W = 64
HEAD_DIM_A = 64
WIDTH_A = D_MODEL
N_HEADS_A = WIDTH_A // (2 * HEAD_DIM_A)
HEAD_DIM_B = 64
WIDTH_B = D_MODEL
N_HEADS_B = WIDTH_B // HEAD_DIM_B
LORA_R = 64
D_MIX = WIDTH_A + WIDTH_B
D_SHIFT = 3 * WIDTH_B + 4 * LORA_R
D_IN = 4 * WIDTH_A + D_SHIFT + WIDTH_B
Q_BLOCK = 128
ROPE_BASE = 10000.0
NORM_EPS = 1e-6
SUBLN_EPS = 1e-5
LNX_EPS = 64e-5

kernel_name = 'hymba_diffattn_rwkv7_prefix_dit'

F32 = jnp.float32


def rmsnorm(x, g, eps):
    xf = x.astype(F32)
    y = xf * lax.rsqrt(jnp.mean(xf * xf, axis=-1, keepdims=True) + eps)
    return (y * g.astype(F32)).astype(x.dtype)


def axial_rope(x, rows):
    T = rows * GRID_W
    row = jnp.repeat(jnp.arange(rows), GRID_W)
    col = jnp.arange(T) % GRID_W
    n_freq = HEAD_DIM_A // 4
    inv = ROPE_BASE ** (-jnp.arange(n_freq, dtype=F32) / n_freq)

    def rot(seg, pos):
        ang = pos.astype(F32)[:, None] * inv[None, :]
        cos = jnp.cos(ang)[None, :, None, None, :].astype(x.dtype)
        sin = jnp.sin(ang)[None, :, None, None, :].astype(x.dtype)
        x1, x2 = jnp.split(seg, 2, axis=-1)
        return jnp.concatenate([x1 * cos - x2 * sin, x2 * cos + x1 * sin], axis=-1)

    half = HEAD_DIM_A // 2
    return jnp.concatenate([rot(x[..., :half], row), rot(x[..., half:], col)], axis=-1)


def diff_attention(q, k, v, lam, lam_init, subln_g):
    B, Tq = q.shape[0], q.shape[1]
    nq = Tq // Q_BLOCK
    qb = jnp.moveaxis(q.reshape(B, nq, Q_BLOCK, N_HEADS_A, 2, HEAD_DIM_A), 1, 0)
    scale = HEAD_DIM_A ** -0.5

    def block(qblk):
        s = jnp.einsum('bqhmd,bkhmd->bhmqk', qblk, k).astype(F32) * scale
        p = jax.nn.softmax(s, axis=-1)
        a = p[:, :, 0] - lam * p[:, :, 1]
        return jnp.einsum('bhqk,bkhv->bqhv', a.astype(v.dtype), v)

    o = lax.map(block, qb)
    o = jnp.moveaxis(o, 0, 1).reshape(B, Tq, N_HEADS_A, 2 * HEAD_DIM_A)
    o = rmsnorm(o, subln_g, SUBLN_EPS) * (1.0 - lam_init)
    return o.reshape(B, Tq, WIDTH_A)


def token_shift(u, mu_prev, mu_next):
    zeros = jnp.zeros_like(u[:, :1])
    prev = jnp.concatenate([zeros, u[:, :-1]], axis=1)
    nxt = jnp.concatenate([u[:, 1:], zeros], axis=1)
    return u + mu_prev * (prev - u) + mu_next * (nxt - u)


def rwkv_scan(r, w, k, v, kk, b, s0, reverse):
    xs = tuple(jnp.moveaxis(t.astype(F32), 1, 0) for t in (r, w, k, v, kk, b))

    def step(S, inp):
        r_t, w_t, k_t, v_t, kk_t, b_t = inp
        sa = jnp.einsum('bhvk,bhk->bhv', S, -kk_t)
        S = S * w_t[:, :, None, :] + sa[..., None] * b_t[:, :, None, :] + v_t[..., None] * k_t[:, :, None, :]
        return S, jnp.einsum('bhvk,bhk->bhv', S, r_t)

    s_final, y = lax.scan(step, s0.astype(F32), xs, reverse=reverse)
    return jnp.moveaxis(y, 0, 1), s_final


def rwkv_branch(u_b, g_b, s0, lp):
    B, T, _ = u_b.shape
    dt = u_b.dtype
    us = token_shift(u_b, lp['shift_mu'][0], lp['shift_mu'][1])
    r, k, v, lora = jnp.split(us, [WIDTH_B, 2 * WIDTH_B, 3 * WIDTH_B], axis=-1)
    lora = lora.reshape(B, T, 4, LORA_R).astype(F32)

    def heads(t):
        return t.reshape(B, T, N_HEADS_B, HEAD_DIM_B)

    rf, kf, vf = r.astype(F32), k.astype(F32), v.astype(F32)
    kk = heads(kf * lp['k_k'])
    kk = kk * lax.rsqrt(jnp.sum(kk * kk, axis=-1, keepdims=True) + 1e-12)
    ys, finals = [], []
    for d in range(2):
        w_log = -jax.nn.softplus(-(lp['decay_w0'][d] + jnp.tanh(lora[:, :, d]) @ lp['decay_w2'][d])) - 0.5
        decay = jnp.exp(-jnp.exp(w_log))
        a = jax.nn.sigmoid(lp['iclr_a0'][d] + lora[:, :, 2 + d] @ lp['iclr_a2'][d])
        k_d = kf * (1.0 + (a - 1.0) * lp['k_a'])
        y_d, s_d = rwkv_scan(heads(rf), heads(decay), heads(k_d), heads(vf), kk, kk * heads(a),
                             s0[:, d], reverse=(d == 1))
        ys.append(y_d)
        finals.append(s_d)
    y = ys[0] + ys[1]
    mu = jnp.mean(y, axis=-1, keepdims=True)
    var = jnp.mean(jnp.square(y - mu), axis=-1, keepdims=True)
    yn = ((y - mu) * lax.rsqrt(var + LNX_EPS)).reshape(B, T, WIDTH_B) * lp['lnx_g'] + lp['lnx_b']
    bonus = jnp.sum(heads(rf) * heads(kf) * lp['r_k'], axis=-1, keepdims=True) * heads(vf)
    out = (yn + bonus.reshape(B, T, WIDTH_B)).astype(dt) * jax.nn.silu(g_b)
    return out, jnp.stack(finals, axis=1)


def mixer_layer(x, mod, rows, ctx_k, ctx_v, s0, lp, lam_init):
    B, T, _ = x.shape
    shift, scale, gate = jnp.split(mod, 3, axis=-1)
    h = rmsnorm(x, lp['norm_g'], NORM_EPS) * (1.0 + scale[:, None, :]) + shift[:, None, :]
    u = h @ lp['w_in']
    q, k, v, g_a, u_b, g_b = jnp.split(
        u, [WIDTH_A, 2 * WIDTH_A, 3 * WIDTH_A, 4 * WIDTH_A, 4 * WIDTH_A + D_SHIFT], axis=-1)
    q = q.reshape(B, T, N_HEADS_A, 2, HEAD_DIM_A)
    k = k.reshape(B, T, N_HEADS_A, 2, HEAD_DIM_A)
    v = v.reshape(B, T, N_HEADS_A, 2 * HEAD_DIM_A)
    if rows is None:
        k_all, v_all = k, v
    else:
        q = axial_rope(q, rows)
        k = axial_rope(k, rows)
        k_all = jnp.concatenate([ctx_k.astype(k.dtype), k], axis=1)
        v_all = jnp.concatenate([ctx_v.astype(v.dtype), v], axis=1)
    lam = (jnp.exp(jnp.sum(lp['lam_q1'].astype(F32) * lp['lam_k1'].astype(F32)))
           - jnp.exp(jnp.sum(lp['lam_q2'].astype(F32) * lp['lam_k2'].astype(F32))) + lam_init)
    o_a = diff_attention(q, k_all, v_all, lam, lam_init, lp['subln_g']) * jax.nn.silu(g_a)
    o_b, s_final = rwkv_branch(u_b, g_b, s0, lp)
    x = x + gate[:, None, :] * (jnp.concatenate([o_a, o_b], axis=-1) @ lp['w_out'])
    return x, k, v, s_final


def setup_inputs(seed: int = 0) -> dict:
    key = jax.random.key(seed)
    ks = jax.random.split(key, 32)

    def nrm(k, shape, s):
        return jax.random.normal(k, shape, F32) * s

    return {
        'x_prompt': nrm(ks[0], (BATCH, SEQ, D_MODEL), 1.0),
        'x_sample': nrm(ks[1], (DEC_BATCH, DEC_SEQ, D_MODEL), 1.0),
        'cache_k': nrm(ks[2], (DEC_BATCH, DEPTH, PAST_LEN, N_HEADS_A, 2, HEAD_DIM_A), 1.0),
        'cache_v': nrm(ks[3], (DEC_BATCH, DEPTH, PAST_LEN, N_HEADS_A, 2 * HEAD_DIM_A), 1.0),
        'state_rwkv': nrm(ks[4], (DEC_BATCH, DEPTH, 2, N_HEADS_B, HEAD_DIM_B, HEAD_DIM_B), 0.5),
        'c': nrm(ks[5], (DEC_BATCH, D_MODEL), 1.0),
        'c_ctx': nrm(ks[6], (D_MODEL,), 1.0),
        'norm_g': 1.0 + nrm(ks[7], (DEPTH, D_MODEL), 0.02),
        'w_ada': nrm(ks[8], (DEPTH, D_MODEL, 3 * D_MODEL), 0.5 * D_MODEL ** -0.5),
        'b_ada': nrm(ks[9], (DEPTH, 3 * D_MODEL), 0.02),
        'w_in': nrm(ks[10], (DEPTH, D_MODEL, D_IN), D_MODEL ** -0.5),
        'lam_q1': nrm(ks[11], (DEPTH, HEAD_DIM_A), 0.1),
        'lam_k1': nrm(ks[12], (DEPTH, HEAD_DIM_A), 0.1),
        'lam_q2': nrm(ks[13], (DEPTH, HEAD_DIM_A), 0.1),
        'lam_k2': nrm(ks[14], (DEPTH, HEAD_DIM_A), 0.1),
        'subln_g': 1.0 + nrm(ks[15], (DEPTH, 2 * HEAD_DIM_A), 0.02),
        'shift_mu': jax.random.uniform(ks[16], (DEPTH, 2, D_SHIFT), F32, 0.0, 0.5),
        'decay_w0': jax.random.uniform(ks[17], (DEPTH, 2, WIDTH_B), F32, -5.0, 1.0),
        'decay_w2': nrm(ks[18], (DEPTH, 2, LORA_R, WIDTH_B), 0.1 * LORA_R ** -0.5),
        'iclr_a0': nrm(ks[19], (DEPTH, 2, WIDTH_B), 0.1),
        'iclr_a2': nrm(ks[20], (DEPTH, 2, LORA_R, WIDTH_B), 0.1 * LORA_R ** -0.5),
        'k_k': 0.85 + nrm(ks[21], (DEPTH, WIDTH_B), 0.02),
        'k_a': 1.0 + nrm(ks[22], (DEPTH, WIDTH_B), 0.02),
        'r_k': nrm(ks[23], (DEPTH, N_HEADS_B, HEAD_DIM_B), 0.1),
        'lnx_g': 1.0 + nrm(ks[24], (DEPTH, WIDTH_B), 0.02),
        'lnx_b': nrm(ks[25], (DEPTH, WIDTH_B), 0.02),
        'w_out': nrm(ks[26], (DEPTH, D_MIX, D_MODEL), D_MIX ** -0.5),
        'final_g': 1.0 + nrm(ks[27], (D_MODEL,), 0.02),
    }


def reference(x_prompt, x_sample, cache_k, cache_v, state_rwkv, c, c_ctx, norm_g, w_ada, b_ada,
              w_in, lam_q1, lam_k1, lam_q2, lam_k2, subln_g, shift_mu, decay_w0, decay_w2,
              iclr_a0, iclr_a2, k_k, k_a, r_k, lnx_g, lnx_b, w_out, final_g):
    rows = x_sample.shape[1] // GRID_W
    xp, xs = x_prompt, x_sample
    s0_ctx = jnp.zeros((x_prompt.shape[0], 2, N_HEADS_B, HEAD_DIM_B, HEAD_DIM_B), F32)
    new_k, new_v, new_s = [], [], []
    for l in range(DEPTH):
        lp = {
            'norm_g': norm_g[l], 'w_in': w_in[l], 'lam_q1': lam_q1[l], 'lam_k1': lam_k1[l],
            'lam_q2': lam_q2[l], 'lam_k2': lam_k2[l], 'subln_g': subln_g[l], 'shift_mu': shift_mu[l],
            'decay_w0': decay_w0[l], 'decay_w2': decay_w2[l], 'iclr_a0': iclr_a0[l],
            'iclr_a2': iclr_a2[l], 'k_k': k_k[l], 'k_a': k_a[l], 'r_k': r_k[l],
            'lnx_g': lnx_g[l], 'lnx_b': lnx_b[l], 'w_out': w_out[l],
        }
        lam_init = 0.8 - 0.6 * math.exp(-0.3 * l)
        mod_ctx = (jax.nn.silu(c_ctx) @ w_ada[l] + b_ada[l])[None]
        mod_lat = jax.nn.silu(c) @ w_ada[l] + b_ada[l]
        xp, k_l, v_l, s_l = mixer_layer(xp, mod_ctx, None, None, None, s0_ctx, lp, lam_init)
        new_k.append(k_l)
        new_v.append(v_l)
        new_s.append(s_l.astype(x_prompt.dtype))
        xs, _, _, _ = mixer_layer(xs, mod_lat, rows, cache_k[:, l], cache_v[:, l],
                                  state_rwkv[:, l], lp, lam_init)
    y_prompt = rmsnorm(xp, final_g, NORM_EPS)
    y_sample = rmsnorm(xs, final_g, NORM_EPS)
    return (y_prompt, y_sample, jnp.stack(new_k, axis=1), jnp.stack(new_v, axis=1), jnp.stack(new_s, axis=1))
```

```python
import functools
import math

import jax
import jax.numpy as jnp
from jax import lax
from jax.experimental import pallas as pl
from jax.experimental.pallas import tpu as pltpu

F32 = jnp.float32
BF16 = jnp.bfloat16
HIGHEST = lax.Precision.HIGHEST

D_MODEL = 1024
GRID_W = 64
HEAD_DIM_A = 64
N_HEADS_A = 8
HEAD_DIM_B = 64
N_HEADS_B = 16
LORA_R = 64
D_SHIFT = 3 * D_MODEL + 4 * LORA_R
D_IN = 4 * D_MODEL + D_SHIFT + D_MODEL
ROPE_BASE = 10000.0
NORM_EPS = 1e-6
SUBLN_EPS = 1e-5
LNX_EPS = 64e-5
LAM_INIT = 0.8 - 0.6 * math.exp(-0.3 * 0)
EXP_NEG_HALF = math.exp(-0.5)

GW = 256
CHUNK = 64
BLOCK_T = 256
VMEM_LIMIT = 48 * 1024 * 1024


def _nn(a, b):
    return jnp.dot(a, b, preferred_element_type=F32)


def _nt(a, b):
    return lax.dot_general(a, b, (((1,), (1,)), ((), ())), preferred_element_type=F32)


def _tn(a, b):
    return lax.dot_general(a, b, (((0,), (0,)), ((), ())), preferred_element_type=F32)


def _split2(x):
    hi = x.astype(BF16)
    lo = (x - hi.astype(F32)).astype(BF16)
    return hi, lo


def _split3(x):
    hi = x.astype(BF16)
    r1 = x - hi.astype(F32)
    mid = r1.astype(BF16)
    lo = (r1 - mid.astype(F32)).astype(BF16)
    return hi, mid, lo


def _silu(x):
    return x * jax.nn.sigmoid(x)


def _mod_kernel(c_ref, w_ref, b_ref, o_ref):
    s = _silu(c_ref[...])
    o_ref[...] = jnp.dot(s, w_ref[...], precision=HIGHEST, preferred_element_type=F32) + b_ref[...]


def _mod(cvec, w_ada, b_ada):
    n = w_ada.shape[1]
    tn = 512
    return pl.pallas_call(
        _mod_kernel,
        grid=(n // tn,),
        in_specs=[
            pl.BlockSpec((8, D_MODEL), lambda j: (0, 0)),
            pl.BlockSpec((D_MODEL, tn), lambda j: (0, j)),
            pl.BlockSpec((1, tn), lambda j: (0, j)),
        ],
        out_specs=pl.BlockSpec((8, tn), lambda j: (0, j)),
        out_shape=jax.ShapeDtypeStruct((8, n), F32),
        name="mod",
    )(cvec, w_ada, b_ada.reshape(1, n))


def _inproj_kernel(x_ref, mod_ref, g_ref, w_ref, o_ref, h_ref):
    @pl.when(pl.program_id(1) == 0)
    def _():
        x = x_ref[...]
        y = x * lax.rsqrt(jnp.mean(x * x, axis=-1, keepdims=True) + NORM_EPS) * g_ref[...]
        mod = mod_ref[0]
        shift = mod[:, 0:D_MODEL]
        scale = mod[:, D_MODEL:2 * D_MODEL]
        h_ref[...] = (y * (1.0 + scale) + shift).astype(BF16)

    o_ref[...] = _nn(h_ref[...], w_ref[...])


def _inproj(x, mod3, norm_g, w_in_bf, mod_row):
    m = x.shape[0]
    tm, tn = 1024, 768
    return pl.pallas_call(
        _inproj_kernel,
        grid=(m // tm, D_IN // tn),
        in_specs=[
            pl.BlockSpec((tm, D_MODEL), lambda i, j: (i, 0)),
            pl.BlockSpec((1, 1, 3 * D_MODEL), lambda i, j: (mod_row(i * tm), 0, 0)),
            pl.BlockSpec((1, D_MODEL), lambda i, j: (0, 0)),
            pl.BlockSpec((D_MODEL, tn), lambda i, j: (0, j)),
        ],
        out_specs=pl.BlockSpec((tm, tn), lambda i, j: (i, j)),
        out_shape=jax.ShapeDtypeStruct((m, D_IN), F32),
        scratch_shapes=[pltpu.VMEM((tm, D_MODEL), BF16)],
        compiler_params=pltpu.CompilerParams(
            dimension_semantics=("arbitrary", "arbitrary"), vmem_limit_bytes=VMEM_LIMIT),
        name="inproj",
    )(x, mod3, norm_g, w_in_bf)


def _lam(q1_ref, k1_ref, q2_ref, k2_ref):
    s1 = jnp.sum(q1_ref[...] * k1_ref[...], axis=-1, keepdims=True)
    s2 = jnp.sum(q2_ref[...] * k2_ref[...], axis=-1, keepdims=True)
    return jnp.exp(s1) - jnp.exp(s2) + LAM_INIT


def _diff_head(q, kb, vb, g, lam, subg):
    lane = lax.broadcasted_iota(jnp.int32, q.shape, 1)
    qs = q * (HEAD_DIM_A ** -0.5)
    probs = []
    for m in range(2):
        qm = jnp.where((lane >> 6) == m, qs, 0.0).astype(BF16)
        s = _nt(qm, kb)
        s = s - jnp.max(s, axis=-1, keepdims=True)
        p = jnp.exp(s)
        probs.append(p * (1.0 / jnp.sum(p, axis=-1, keepdims=True)))
    a = probs[0] - lam * probs[1]
    o = _nn(a.astype(BF16), vb)
    o = o * lax.rsqrt(jnp.mean(o * o, axis=-1, keepdims=True) + SUBLN_EPS) * subg
    o = o * (1.0 - LAM_INIT)
    return o * _silu(g)


def _attn_ctx_kernel(q_ref, k_ref, v_ref, g_ref, q1_ref, k1_ref, q2_ref, k2_ref, sg_ref,
                     o_ref, nk_ref, nv_ref):
    lam = _lam(q1_ref, k1_ref, q2_ref, k2_ref)
    subg = sg_ref[...]
    nk_ref[...] = k_ref[...]
    nv_ref[...] = v_ref[...]
    for h in range(q_ref.shape[1] // 128):
        sl = slice(h * 128, (h + 1) * 128)
        o = _diff_head(q_ref[:, sl], k_ref[:, sl].astype(BF16), v_ref[:, sl].astype(BF16),
                       g_ref[:, sl], lam, subg)
        o_ref[:, sl] = o.astype(BF16)


def _attn_ctx(u, seq, lam_q1, lam_k1, lam_q2, lam_k2, subln_g):
    m = u.shape[0]
    w = D_MODEL
    nb = D_MODEL // w
    small = pl.BlockSpec((1, HEAD_DIM_A), lambda b: (0, 0))
    blk = lambda off: pl.BlockSpec((seq, w), lambda b: (b, off))
    return pl.pallas_call(
        _attn_ctx_kernel,
        grid=(m // seq,),
        in_specs=[blk(0), blk(nb), blk(2 * nb), blk(3 * nb), small, small, small, small,
                  pl.BlockSpec((1, 2 * HEAD_DIM_A), lambda b: (0, 0))],
        out_specs=[blk(0), blk(0), blk(0)],
        out_shape=[jax.ShapeDtypeStruct((m, D_MODEL), BF16),
                   jax.ShapeDtypeStruct((m, D_MODEL), F32),
                   jax.ShapeDtypeStruct((m, D_MODEL), F32)],
        compiler_params=pltpu.CompilerParams(vmem_limit_bytes=VMEM_LIMIT),
        name="attn_ctx",
    )(u, u, u, u, lam_q1, lam_k1, lam_q2, lam_k2, subln_g)


def _rope(x, cos, s1, s2):
    return x * cos + pltpu.roll(x, 112, 1) * s1 + pltpu.roll(x, 16, 1) * s2


def _attn_lat_kernel(q_ref, k_ref, v_ref, g_ref, ck_ref, cv_ref, cq_ref, s1q_ref, s2q_ref,
                     ckk_ref, s1k_ref, s2k_ref, q1_ref, k1_ref, q2_ref, k2_ref, sg_ref,
                     o_ref, kall_ref, vall_ref):
    past = ck_ref.shape[0]
    nh = q_ref.shape[1] // 128

    @pl.when(pl.program_id(2) == 0)
    def _():
        kall_ref[0:past, :] = ck_ref[...].astype(BF16)
        vall_ref[0:past, :] = cv_ref[...].astype(BF16)
        vall_ref[past:, :] = v_ref[...].astype(BF16)
        for h in range(nh):
            sl = slice(h * 128, (h + 1) * 128)
            kall_ref[past:, sl] = _rope(k_ref[:, sl], ckk_ref[...], s1k_ref[...], s2k_ref[...]).astype(BF16)

    lam = _lam(q1_ref, k1_ref, q2_ref, k2_ref)
    subg = sg_ref[...]
    for h in range(nh):
        sl = slice(h * 128, (h + 1) * 128)
        q = _rope(q_ref[:, sl], cq_ref[...], s1q_ref[...], s2q_ref[...])
        o = _diff_head(q, kall_ref[:, sl], vall_ref[:, sl], g_ref[:, sl], lam, subg)
        o_ref[:, sl] = o.astype(BF16)


def _rope_tables(t):
    pos = jnp.arange(t)
    row = (pos // GRID_W).astype(F32)
    col = (pos % GRID_W).astype(F32)
    n_freq = HEAD_DIM_A // 4
    inv = ROPE_BASE ** (-jnp.arange(n_freq, dtype=F32) / n_freq)
    lane = jnp.arange(128)
    d = lane % HEAD_DIM_A
    use_col = (d // 32) == 1
    second = ((d % 32) // 16) == 1
    f = d % 16
    ang = jnp.where(use_col[None, :], col[:, None], row[:, None]) * inv[f][None, :]
    cos = jnp.cos(ang)
    sin = jnp.sin(ang)
    s1 = jnp.where(second[None, :], 0.0, -sin)
    s2 = jnp.where(second[None, :], sin, 0.0)
    return cos, s1, s2


def _attn_lat(u, cache_k2, cache_v2, seq, past, lam_q1, lam_k1, lam_q2, lam_k2, subln_g):
    m = u.shape[0]
    nbat = m // seq
    w = GW
    ng = D_MODEL // w
    tq = 256
    nq = seq // tq
    cos, s1, s2 = _rope_tables(seq)
    small = pl.BlockSpec((1, HEAD_DIM_A), lambda b, g, i: (0, 0))
    qblk = lambda off: pl.BlockSpec((tq, w), lambda b, g, i: (b * nq + i, off + g))
    kblk = lambda off: pl.BlockSpec((seq, w), lambda b, g, i: (b, off + g))
    cblk = pl.BlockSpec((past, w), lambda b, g, i: (b, g))
    tq_blk = pl.BlockSpec((tq, 128), lambda b, g, i: (i, 0))
    tk_blk = pl.BlockSpec((seq, 128), lambda b, g, i: (0, 0))
    return pl.pallas_call(
        _attn_lat_kernel,
        grid=(nbat, ng, nq),
        in_specs=[qblk(0), kblk(ng), kblk(2 * ng), qblk(3 * ng), cblk, cblk,
                  tq_blk, tq_blk, tq_blk, tk_blk, tk_blk, tk_blk,
                  small, small, small, small,
                  pl.BlockSpec((1, 2 * HEAD_DIM_A), lambda b, g, i: (0, 0))],
        out_specs=pl.BlockSpec((tq, w), lambda b, g, i: (b * nq + i, g)),
        out_shape=jax.ShapeDtypeStruct((m, D_MODEL), BF16),
        scratch_shapes=[pltpu.VMEM((past + seq, w), BF16), pltpu.VMEM((past + seq, w), BF16)],
        compiler_params=pltpu.CompilerParams(
            dimension_semantics=("arbitrary", "arbitrary", "arbitrary"), vmem_limit_bytes=VMEM_LIMIT),
        name="attn_lat",
    )(u, u, u, u, cache_k2, cache_v2, cos, s1, s2, cos, s1, s2,
      lam_q1, lam_k1, lam_q2, lam_k2, subln_g)


def _rwkv_consts():
    ri = lax.broadcasted_iota(jnp.int32, (GW, GW), 0)
    ci = lax.broadcasted_iota(jnp.int32, (GW, GW), 1)
    bmask = (ri >> 6) == (ci >> 6)
    t = lax.broadcasted_iota(jnp.int32, (CHUNK, GW), 0)
    j = lax.broadcasted_iota(jnp.int32, (CHUNK, GW), 1) & (CHUNK - 1)
    ti = lax.broadcasted_iota(jnp.int32, (CHUNK, CHUNK), 0)
    tj = lax.broadcasted_iota(jnp.int32, (CHUNK, CHUNK), 1)
    return dict(
        bmask=bmask,
        eye=ri == ci,
        eye_sbs=jnp.where(t == j, 1.0, 0.0),
        strict=(j < t, j > t),
        incl=(j <= t, j >= t),
        off=[((t >> (n + 1)) == (j >> (n + 1))) & ((t >> n) != (j >> n)) for n in range(6)],
        tri=(jnp.where(tj <= ti, 1.0, 0.0).astype(BF16), jnp.where(tj >= ti, 1.0, 0.0).astype(BF16)),
    )


def _wbd(x, cst):
    return jnp.where(cst["bmask"], jnp.concatenate([x, x, x, x], axis=0), 0.0).astype(BF16)


def _group_sum(x, e):
    hi, lo = _split2(x)
    return _nn(hi, e) + _nn(lo, e)


def _chunk_step(lw, r, kd, v, kk, b, s, rev, cst):
    L = CHUNK
    d = 1 if rev else 0
    hi, mid, lo = _split3(lw)
    tri = cst["tri"][d]
    lg = _nn(tri, hi) + _nn(tri, mid) + _nn(tri, lo)
    g_l = lg[0:1, :] if rev else lg[L - 1:L, :]
    at = -(kk * jnp.exp(lg - lw))
    rt = r * jnp.exp(lg)
    e_inv = jnp.exp(-lg)
    bt = b * e_inv
    kt = kd * e_inv
    e_l = jnp.exp(g_l - lg)
    bh = b * e_l
    kh = kd * e_l

    ar = jnp.concatenate([at, rt], axis=0).astype(BF16)
    gb = _nt(ar, _wbd(bt, cst))
    gk = _nt(ar, _wbd(kt, cst))
    strict, incl = cst["strict"][d], cst["incl"][d]
    m_ab = jnp.where(strict, gb[:L], 0.0)
    m_rb = jnp.where(incl, gb[L:], 0.0)
    m_ak = jnp.where(strict, gk[:L], 0.0)
    m_rk = jnp.where(incl, gk[L:], 0.0)

    x = cst["eye_sbs"] + jnp.where(cst["off"][0], m_ab, 0.0)
    for off in cst["off"][1:]:
        p = _nn(jnp.where(off, m_ab, 0.0).astype(BF16), _wbd(x, cst))
        x = x + _nn(x.astype(BF16), _wbd(p, cst))

    dg = jnp.where(cst["eye"], jnp.exp(g_l), 0.0).astype(BF16)
    ars = _nn(jnp.concatenate([ar, dg], axis=0), s.astype(BF16))
    mv = _nn(jnp.concatenate([m_ak, m_rk], axis=0).astype(BF16), _wbd(v, cst))
    u = _nn(x.astype(BF16), _wbd(ars[:L] + mv[:L], cst))
    y = ars[L:2 * L] + _nn(m_rb.astype(BF16), _wbd(u, cst)) + mv[L:]
    upd = _tn(bh.astype(BF16), u.astype(BF16)) + _tn(kh.astype(BF16), v.astype(BF16))
    s_new = ars[2 * L:] + jnp.where(cst["bmask"], upd, 0.0)
    return y, s_new


def _rwkv_kernel(*refs, seq, has_s0, want_state):
    it = iter(refs)
    r_ref, k_ref, v_ref, lo_ref, gb_ref = (next(it) for _ in range(5))
    mur_ref, muk_ref, muv_ref, mul_ref = (next(it) for _ in range(4))
    w0_ref, w2_ref, a0_ref, a2_ref = (next(it) for _ in range(4))
    kk_ref, ka_ref, rk_ref, lng_ref, lnb_ref = (next(it) for _ in range(5))
    s0_ref = next(it) if has_s0 else None
    ob_ref = next(it)
    sf_ref = next(it) if want_state else None
    r_s, v_s, kn_s, lw_s, kd_s, b_s, y_s, st_s = (next(it) for _ in range(8))

    T = seq
    cst = _rwkv_consts()
    e = jnp.where(cst["bmask"], 1.0, 0.0).astype(BF16)
    row = lax.broadcasted_iota(jnp.int32, (T, GW), 0)

    def shifted(ref, mu_ref):
        x = ref[...]
        mu = mu_ref[...]
        prev = jnp.where(row == 0, 0.0, pltpu.roll(x, 1, 0))
        nxt = jnp.where(row == T - 1, 0.0, pltpu.roll(x, T - 1, 0))
        return x + mu[0:1, :] * (prev - x) + mu[1:2, :] * (nxt - x)

    r = shifted(r_ref, mur_ref)
    k = shifted(k_ref, muk_ref)
    v = shifted(v_ref, muv_ref)
    lo = shifted(lo_ref, mul_ref)
    r_s[...] = r
    v_s[...] = v
    kk = k * kk_ref[...]
    kk = kk * lax.rsqrt(_group_sum(kk * kk, e) + 1e-12)
    kn_s[...] = kk
    bonus_dot = _group_sum(r * k * rk_ref[...], e)

    tanh_lo = jnp.tanh(lo).astype(BF16)
    lo_b = lo.astype(BF16)

    def padded(w, before):
        parts = [jnp.zeros((before, GW), F32)] if before else []
        parts.append(w)
        after = GW - before - LORA_R
        if after:
            parts.append(jnp.zeros((after, GW), F32))
        return jnp.concatenate(parts, axis=0).astype(BF16)

    for d in range(2):
        w2p = padded(w2_ref[d], LORA_R * d)
        a2p = padded(a2_ref[d], LORA_R * (2 + d))
        z = w0_ref[d:d + 1, :] + _nn(tanh_lo, w2p)
        lw_s[d] = -EXP_NEG_HALF * jax.nn.sigmoid(z)
        a = jax.nn.sigmoid(a0_ref[d:d + 1, :] + _nn(lo_b, a2p))
        kd_s[d] = k * (1.0 + (a - 1.0) * ka_ref[...])
        b_s[d] = kk * a

    if has_s0:
        ident = jnp.where(cst["eye"], 1.0, 0.0).astype(BF16)
        for d in range(2):
            z0 = s0_ref[0, d].reshape(GW, HEAD_DIM_B)
            hi, lo2 = _split2(z0)
            zt = _tn(hi, ident) + _tn(lo2, ident)
            st_s[d] = jnp.where(cst["bmask"], jnp.concatenate([zt, zt, zt, zt], axis=0), 0.0)
    else:
        st_s[...] = jnp.zeros(st_s.shape, F32)

    nblk = T // BLOCK_T
    ncb = BLOCK_T // CHUNK

    def block(i, carry):
        for d in range(2):
            blk = i if d == 0 else nblk - 1 - i
            s = st_s[d]
            for cc in range(ncb):
                c = cc if d == 0 else ncb - 1 - cc
                start = blk * BLOCK_T + c * CHUNK
                if not isinstance(start, int):
                    start = pl.multiple_of(start, CHUNK)
                rows = pl.ds(start, CHUNK)
                y, s = _chunk_step(lw_s[d, rows, :], r_s[rows, :], kd_s[d, rows, :], v_s[rows, :],
                                   kn_s[rows, :], b_s[d, rows, :], s, d == 1, cst)
                y_s[d, rows, :] = y
            st_s[d] = s
        return carry

    if nblk == 1:
        block(0, 0)
    else:
        lax.fori_loop(0, nblk, block, 0)

    y = y_s[0] + y_s[1]
    mu = _group_sum(y, e) * (1.0 / HEAD_DIM_B)
    dlt = y - mu
    var = _group_sum(dlt * dlt, e) * (1.0 / HEAD_DIM_B)
    yn = dlt * lax.rsqrt(var + LNX_EPS) * lng_ref[...] + lnb_ref[...]
    out = (yn + bonus_dot * v_s[...]) * _silu(gb_ref[...])
    ob_ref[...] = out.astype(BF16)

    if want_state:
        ri = lax.broadcasted_iota(jnp.int32, (GW, HEAD_DIM_B), 0)
        ci = lax.broadcasted_iota(jnp.int32, (GW, HEAD_DIM_B), 1)
        fold = jnp.where((ri & (HEAD_DIM_B - 1)) == ci, 1.0, 0.0).astype(BF16)
        for d in range(2):
            hi, lo2 = _split2(st_s[d])
            sf = _tn(hi, fold) + _tn(lo2, fold)
            sf_ref[0, d] = sf.reshape(GW // HEAD_DIM_B, HEAD_DIM_B, HEAD_DIM_B)


def _rwkv(u, g_src, seq, s0, want_state, shift_mu, decay_w0, decay_w2, iclr_a0, iclr_a2,
          k_k, k_a, r_k, lnx_g, lnx_b):
    m = u.shape[0]
    nbat = m // seq
    ng = D_MODEL // GW
    hg = GW // HEAD_DIM_B
    c0 = 4 * D_MODEL // GW
    ublk = lambda off: pl.BlockSpec((seq, GW), lambda b, g: (b, off + g))
    lblk = pl.BlockSpec((seq, GW), lambda b, g: (b, c0 + 3 * ng))
    mublk = lambda off: pl.BlockSpec((2, GW), lambda b, g: (0, off + g))
    mulblk = pl.BlockSpec((2, GW), lambda b, g: (0, 3 * ng))
    vec2 = pl.BlockSpec((2, GW), lambda b, g: (0, g))
    mat2 = pl.BlockSpec((2, LORA_R, GW), lambda b, g: (0, 0, g))
    vec1 = pl.BlockSpec((1, GW), lambda b, g: (0, g))
    sblk = pl.BlockSpec((1, 2, hg, HEAD_DIM_B, HEAD_DIM_B), lambda b, g: (b, 0, g, 0, 0))
    in_specs = [ublk(c0), ublk(c0 + ng), ublk(c0 + 2 * ng), lblk, ublk(c0 + 3 * ng + 1),
                mublk(0), mublk(ng), mublk(2 * ng), mulblk,
                vec2, mat2, vec2, mat2, vec1, vec1, vec1, vec1, vec1]
    args = [u, u, u, u, g_src, shift_mu, shift_mu, shift_mu, shift_mu,
            decay_w0, decay_w2, iclr_a0, iclr_a2,
            k_k.reshape(1, D_MODEL), k_a.reshape(1, D_MODEL), r_k.reshape(1, D_MODEL),
            lnx_g.reshape(1, D_MODEL), lnx_b.reshape(1, D_MODEL)]
    if s0 is not None:
        in_specs.append(sblk)
        args.append(s0)
    out_specs = [pl.BlockSpec((seq, GW), lambda b, g: (b, g))]
    out_shape = [jax.ShapeDtypeStruct((m, D_MODEL), BF16)]
    if want_state:
        out_specs.append(sblk)
        out_shape.append(jax.ShapeDtypeStruct((nbat, 2, N_HEADS_B, HEAD_DIM_B, HEAD_DIM_B), F32))
    tw = pltpu.VMEM((seq, GW), F32)
    tw2 = pltpu.VMEM((2, seq, GW), F32)
    res = pl.pallas_call(
        functools.partial(_rwkv_kernel, seq=seq, has_s0=s0 is not None, want_state=want_state),
        grid=(nbat, ng),
        in_specs=in_specs,
        out_specs=out_specs,
        out_shape=out_shape,
        scratch_shapes=[tw, tw, tw, tw2, tw2, tw2, tw2, pltpu.VMEM((2, GW, GW), F32)],
        compiler_params=pltpu.CompilerParams(
            dimension_semantics=("arbitrary", "arbitrary"), vmem_limit_bytes=VMEM_LIMIT),
        name="rwkv_state" if want_state else "rwkv",
    )(*args)
    return res


def _outproj_kernel(oa_ref, ob_ref, x_ref, mod_ref, w_ref, fg_ref, y_ref):
    acc = _nn(oa_ref[...], w_ref[0:D_MODEL, :]) + _nn(ob_ref[...], w_ref[D_MODEL:, :])
    gate = mod_ref[0][:, 2 * D_MODEL:]
    xo = x_ref[...] + gate * acc
    y_ref[...] = xo * lax.rsqrt(jnp.mean(xo * xo, axis=-1, keepdims=True) + NORM_EPS) * fg_ref[...]


def _outproj(o_a, o_b, x, mod3, w_out_bf, final_g, mod_row):
    m = x.shape[0]
    tm = 512
    blk = pl.BlockSpec((tm, D_MODEL), lambda i: (i, 0))
    return pl.pallas_call(
        _outproj_kernel,
        grid=(m // tm,),
        in_specs=[blk, blk, blk,
                  pl.BlockSpec((1, 1, 3 * D_MODEL), lambda i: (mod_row(i * tm), 0, 0)),
                  pl.BlockSpec((2 * D_MODEL, D_MODEL), lambda i: (0, 0)),
                  pl.BlockSpec((1, D_MODEL), lambda i: (0, 0))],
        out_specs=blk,
        out_shape=jax.ShapeDtypeStruct((m, D_MODEL), F32),
        compiler_params=pltpu.CompilerParams(vmem_limit_bytes=VMEM_LIMIT),
        name="outproj",
    )(o_a, o_b, x, mod3, w_out_bf, final_g)


def kernel(x_prompt, x_sample, cache_k, cache_v, state_rwkv, c, c_ctx, norm_g, w_ada, b_ada, w_in, lam_q1, lam_k1, lam_q2, lam_k2, subln_g, shift_mu, decay_w0, decay_w2, iclr_a0, iclr_a2, k_k, k_a, r_k, lnx_g, lnx_b, w_out, final_g):
    batch, seq, d = x_prompt.shape
    dbatch, dseq, _ = x_sample.shape
    past = cache_k.shape[2]
    assert d == D_MODEL and w_in.shape == (1, D_MODEL, D_IN) and dseq % GRID_W == 0

    cvec = jnp.concatenate([c_ctx[None, :], c, jnp.zeros((8 - 1 - dbatch, d), F32)], axis=0)
    mod3 = _mod(cvec, w_ada[0], b_ada[0]).reshape(8, 1, 3 * d)
    w_in_bf = w_in[0].astype(BF16)
    w_out_bf = w_out[0].astype(BF16)
    ctx_row = lambda tok: 0
    lat_row = lambda tok: 1 + tok // dseq
    lp = (shift_mu[0], decay_w0[0], decay_w2[0], iclr_a0[0], iclr_a2[0], k_k[0], k_a[0], r_k[0],
          lnx_g[0], lnx_b[0])
    lam = (lam_q1, lam_k1, lam_q2, lam_k2, subln_g)
    fg = final_g.reshape(1, d)

    xp = x_prompt.reshape(batch * seq, d)
    u_c = _inproj(xp, mod3, norm_g, w_in_bf, ctx_row)
    oa_c, nk, nv = _attn_ctx(u_c, seq, *lam)
    ob_c, s_new = _rwkv(u_c, u_c, seq, None, True, *lp)
    y_prompt = _outproj(oa_c, ob_c, xp, mod3, w_out_bf, fg, ctx_row).reshape(batch, seq, d)

    xs = x_sample.reshape(dbatch * dseq, d)
    u_l = _inproj(xs, mod3, norm_g, w_in_bf, lat_row)
    ck = cache_k[:, 0].reshape(dbatch * past, d)
    cv = cache_v[:, 0].reshape(dbatch * past, d)
    oa_l = _attn_lat(u_l, ck, cv, dseq, past, *lam)
    (ob_l,) = _rwkv(u_l, u_l, dseq, state_rwkv[:, 0], False, *lp)
    y_sample = _outproj(oa_l, ob_l, xs, mod3, w_out_bf, fg, lat_row).reshape(dbatch, dseq, d)

    new_k = nk.reshape(batch, 1, seq, N_HEADS_A, 2, HEAD_DIM_A)
    new_v = nv.reshape(batch, 1, seq, N_HEADS_A, 2 * HEAD_DIM_A)
    new_s = s_new.reshape(batch, 1, 2, N_HEADS_B, HEAD_DIM_B, HEAD_DIM_B)
    return (y_prompt, y_sample, new_k, new_v, new_s)
```

```python
import functools
import math

import jax
import jax.numpy as jnp
from jax import lax
from jax.experimental import pallas as pl
from jax.experimental.pallas import tpu as pltpu

F32 = jnp.float32
BF16 = jnp.bfloat16
HIGHEST = lax.Precision.HIGHEST

D_MODEL = 1024
GRID_W = 64
HEAD_DIM_A = 64
N_HEADS_A = 8
HEAD_DIM_B = 64
N_HEADS_B = 16
LORA_R = 64
D_SHIFT = 3 * D_MODEL + 4 * LORA_R
D_IN = 4 * D_MODEL + D_SHIFT + D_MODEL
ROPE_BASE = 10000.0
NORM_EPS = 1e-6
SUBLN_EPS = 1e-5
LNX_EPS = 64e-5
LAM_INIT = 0.8 - 0.6 * math.exp(-0.3 * 0)
EXP_NEG_HALF = math.exp(-0.5)

GW = 256
CHUNK = 64
BLOCK_T = 256
VMEM_LIMIT = 48 * 1024 * 1024


def _nn(a, b):
    return jnp.dot(a, b, preferred_element_type=F32)


def _nt(a, b):
    return lax.dot_general(a, b, (((1,), (1,)), ((), ())), preferred_element_type=F32)


def _tn(a, b):
    return lax.dot_general(a, b, (((0,), (0,)), ((), ())), preferred_element_type=F32)


def _split2(x):
    hi = x.astype(BF16)
    lo = (x - hi.astype(F32)).astype(BF16)
    return hi, lo


def _split3(x):
    hi = x.astype(BF16)
    r1 = x - hi.astype(F32)
    mid = r1.astype(BF16)
    lo = (r1 - mid.astype(F32)).astype(BF16)
    return hi, mid, lo


def _silu(x):
    return x * jax.nn.sigmoid(x)


def _mod_kernel(c_ref, w_ref, b_ref, o_ref):
    s = _silu(c_ref[...])
    o_ref[...] = jnp.dot(s, w_ref[...], precision=HIGHEST, preferred_element_type=F32) + b_ref[...]


def _mod(cvec, w_ada, b_ada):
    n = w_ada.shape[1]
    tn = 512
    return pl.pallas_call(
        _mod_kernel,
        grid=(n // tn,),
        in_specs=[
            pl.BlockSpec((8, D_MODEL), lambda j: (0, 0)),
            pl.BlockSpec((D_MODEL, tn), lambda j: (0, j)),
            pl.BlockSpec((1, tn), lambda j: (0, j)),
        ],
        out_specs=pl.BlockSpec((8, tn), lambda j: (0, j)),
        out_shape=jax.ShapeDtypeStruct((8, n), F32),
        name="mod",
    )(cvec, w_ada, b_ada.reshape(1, n))


def _inproj_kernel(x_ref, mod_ref, g_ref, w_ref, o_ref, h_ref):
    @pl.when(pl.program_id(1) == 0)
    def _():
        x = x_ref[...]
        y = x * lax.rsqrt(jnp.mean(x * x, axis=-1, keepdims=True) + NORM_EPS) * g_ref[...]
        mod = mod_ref[0]
        shift = mod[:, 0:D_MODEL]
        scale = mod[:, D_MODEL:2 * D_MODEL]
        h_ref[...] = (y * (1.0 + scale) + shift).astype(BF16)

    o_ref[...] = _nn(h_ref[...], w_ref[...])


def _inproj(x, mod3, norm_g, w_in_bf, mod_row):
    m = x.shape[0]
    tm, tn = 1024, 768
    return pl.pallas_call(
        _inproj_kernel,
        grid=(m // tm, D_IN // tn),
        in_specs=[
            pl.BlockSpec((tm, D_MODEL), lambda i, j: (i, 0)),
            pl.BlockSpec((1, 1, 3 * D_MODEL), lambda i, j: (mod_row(i * tm), 0, 0)),
            pl.BlockSpec((1, D_MODEL), lambda i, j: (0, 0)),
            pl.BlockSpec((D_MODEL, tn), lambda i, j: (0, j)),
        ],
        out_specs=pl.BlockSpec((tm, tn), lambda i, j: (i, j)),
        out_shape=jax.ShapeDtypeStruct((m, D_IN), F32),
        scratch_shapes=[pltpu.VMEM((tm, D_MODEL), BF16)],
        compiler_params=pltpu.CompilerParams(
            dimension_semantics=("arbitrary", "arbitrary"), vmem_limit_bytes=VMEM_LIMIT),
        name="inproj",
    )(x, mod3, norm_g, w_in_bf)


def _lam(q1_ref, k1_ref, q2_ref, k2_ref):
    s1 = jnp.sum(q1_ref[...] * k1_ref[...], axis=-1, keepdims=True)
    s2 = jnp.sum(q2_ref[...] * k2_ref[...], axis=-1, keepdims=True)
    return jnp.exp(s1) - jnp.exp(s2) + LAM_INIT


def _diff_head(q, kb, vb, g, lam, subg):
    lane = lax.broadcasted_iota(jnp.int32, q.shape, 1)
    qs = q * (HEAD_DIM_A ** -0.5)
    probs = []
    for m in range(2):
        qm = jnp.where((lane >> 6) == m, qs, 0.0).astype(BF16)
        s = _nt(qm, kb)
        s = s - jnp.max(s, axis=-1, keepdims=True)
        p = jnp.exp(s)
        probs.append(p * (1.0 / jnp.sum(p, axis=-1, keepdims=True)))
    a = probs[0] - lam * probs[1]
    o = _nn(a.astype(BF16), vb)
    o = o * lax.rsqrt(jnp.mean(o * o, axis=-1, keepdims=True) + SUBLN_EPS) * subg
    o = o * (1.0 - LAM_INIT)
    return o * _silu(g)


def _attn_ctx_kernel(q_ref, k_ref, v_ref, g_ref, q1_ref, k1_ref, q2_ref, k2_ref, sg_ref,
                     o_ref, nk_ref, nv_ref):
    lam = _lam(q1_ref, k1_ref, q2_ref, k2_ref)
    subg = sg_ref[...]
    nk_ref[...] = k_ref[...]
    nv_ref[...] = v_ref[...]
    for h in range(q_ref.shape[1] // 128):
        sl = slice(h * 128, (h + 1) * 128)
        o = _diff_head(q_ref[:, sl], k_ref[:, sl].astype(BF16), v_ref[:, sl].astype(BF16),
                       g_ref[:, sl], lam, subg)
        o_ref[:, sl] = o.astype(BF16)


def _attn_ctx(u, seq, lam_q1, lam_k1, lam_q2, lam_k2, subln_g):
    m = u.shape[0]
    w = D_MODEL
    nb = D_MODEL // w
    small = pl.BlockSpec((1, HEAD_DIM_A), lambda b: (0, 0))
    blk = lambda off: pl.BlockSpec((seq, w), lambda b: (b, off))
    return pl.pallas_call(
        _attn_ctx_kernel,
        grid=(m // seq,),
        in_specs=[blk(0), blk(nb), blk(2 * nb), blk(3 * nb), small, small, small, small,
                  pl.BlockSpec((1, 2 * HEAD_DIM_A), lambda b: (0, 0))],
        out_specs=[blk(0), blk(0), blk(0)],
        out_shape=[jax.ShapeDtypeStruct((m, D_MODEL), BF16),
                   jax.ShapeDtypeStruct((m, D_MODEL), F32),
                   jax.ShapeDtypeStruct((m, D_MODEL), F32)],
        compiler_params=pltpu.CompilerParams(vmem_limit_bytes=VMEM_LIMIT),
        name="attn_ctx",
    )(u, u, u, u, lam_q1, lam_k1, lam_q2, lam_k2, subln_g)


def _rope(x, cos, s1, s2):
    return x * cos + pltpu.roll(x, 112, 1) * s1 + pltpu.roll(x, 16, 1) * s2


def _attn_lat_kernel(q_ref, k_ref, v_ref, g_ref, ck_ref, cv_ref, cq_ref, s1q_ref, s2q_ref,
                     ckk_ref, s1k_ref, s2k_ref, q1_ref, k1_ref, q2_ref, k2_ref, sg_ref,
                     o_ref, kall_ref, vall_ref):
    past = ck_ref.shape[0]
    nh = q_ref.shape[1] // 128

    @pl.when(pl.program_id(2) == 0)
    def _():
        kall_ref[0:past, :] = ck_ref[...].astype(BF16)
        vall_ref[0:past, :] = cv_ref[...].astype(BF16)
        vall_ref[past:, :] = v_ref[...].astype(BF16)
        for h in range(nh):
            sl = slice(h * 128, (h + 1) * 128)
            kall_ref[past:, sl] = _rope(k_ref[:, sl], ckk_ref[...], s1k_ref[...], s2k_ref[...]).astype(BF16)

    lam = _lam(q1_ref, k1_ref, q2_ref, k2_ref)
    subg = sg_ref[...]
    for h in range(nh):
        sl = slice(h * 128, (h + 1) * 128)
        q = _rope(q_ref[:, sl], cq_ref[...], s1q_ref[...], s2q_ref[...])
        o = _diff_head(q, kall_ref[:, sl], vall_ref[:, sl], g_ref[:, sl], lam, subg)
        o_ref[:, sl] = o.astype(BF16)


def _rope_tables(t):
    pos = jnp.arange(t)
    row = (pos // GRID_W).astype(F32)
    col = (pos % GRID_W).astype(F32)
    n_freq = HEAD_DIM_A // 4
    inv = ROPE_BASE ** (-jnp.arange(n_freq, dtype=F32) / n_freq)
    lane = jnp.arange(128)
    d = lane % HEAD_DIM_A
    use_col = (d // 32) == 1
    second = ((d % 32) // 16) == 1
    f = d % 16
    ang = jnp.where(use_col[None, :], col[:, None], row[:, None]) * inv[f][None, :]
    cos = jnp.cos(ang)
    sin = jnp.sin(ang)
    s1 = jnp.where(second[None, :], 0.0, -sin)
    s2 = jnp.where(second[None, :], sin, 0.0)
    return cos, s1, s2


def _attn_lat(u, cache_k2, cache_v2, seq, past, lam_q1, lam_k1, lam_q2, lam_k2, subln_g):
    m = u.shape[0]
    nbat = m // seq
    w = GW
    ng = D_MODEL // w
    tq = 256
    nq = seq // tq
    cos, s1, s2 = _rope_tables(seq)
    small = pl.BlockSpec((1, HEAD_DIM_A), lambda b, g, i: (0, 0))
    qblk = lambda off: pl.BlockSpec((tq, w), lambda b, g, i: (b * nq + i, off + g))
    kblk = lambda off: pl.BlockSpec((seq, w), lambda b, g, i: (b, off + g))
    cblk = pl.BlockSpec((past, w), lambda b, g, i: (b, g))
    tq_blk = pl.BlockSpec((tq, 128), lambda b, g, i: (i, 0))
    tk_blk = pl.BlockSpec((seq, 128), lambda b, g, i: (0, 0))
    return pl.pallas_call(
        _attn_lat_kernel,
        grid=(nbat, ng, nq),
        in_specs=[qblk(0), kblk(ng), kblk(2 * ng), qblk(3 * ng), cblk, cblk,
                  tq_blk, tq_blk, tq_blk, tk_blk, tk_blk, tk_blk,
                  small, small, small, small,
                  pl.BlockSpec((1, 2 * HEAD_DIM_A), lambda b, g, i: (0, 0))],
        out_specs=pl.BlockSpec((tq, w), lambda b, g, i: (b * nq + i, g)),
        out_shape=jax.ShapeDtypeStruct((m, D_MODEL), BF16),
        scratch_shapes=[pltpu.VMEM((past + seq, w), BF16), pltpu.VMEM((past + seq, w), BF16)],
        compiler_params=pltpu.CompilerParams(
            dimension_semantics=("arbitrary", "arbitrary", "arbitrary"), vmem_limit_bytes=VMEM_LIMIT),
        name="attn_lat",
    )(u, u, u, u, cache_k2, cache_v2, cos, s1, s2, cos, s1, s2,
      lam_q1, lam_k1, lam_q2, lam_k2, subln_g)


def _rwkv_consts():
    ri = lax.broadcasted_iota(jnp.int32, (GW, GW), 0)
    ci = lax.broadcasted_iota(jnp.int32, (GW, GW), 1)
    bmask = (ri >> 6) == (ci >> 6)
    t = lax.broadcasted_iota(jnp.int32, (CHUNK, GW), 0)
    j = lax.broadcasted_iota(jnp.int32, (CHUNK, GW), 1) & (CHUNK - 1)
    ti = lax.broadcasted_iota(jnp.int32, (CHUNK, CHUNK), 0)
    tj = lax.broadcasted_iota(jnp.int32, (CHUNK, CHUNK), 1)
    return dict(
        bmask=bmask,
        eye=ri == ci,
        eye_sbs=jnp.where(t == j, 1.0, 0.0),
        strict=(j < t, j > t),
        incl=(j <= t, j >= t),
        off=[((t >> (n + 1)) == (j >> (n + 1))) & ((t >> n) != (j >> n)) for n in range(6)],
        tri=(jnp.where(tj <= ti, 1.0, 0.0).astype(BF16), jnp.where(tj >= ti, 1.0, 0.0).astype(BF16)),
    )


def _wbd(x, cst):
    return jnp.where(cst["bmask"], jnp.concatenate([x, x, x, x], axis=0), 0.0).astype(BF16)


def _group_sum(x, e):
    hi, lo = _split2(x)
    return _nn(hi, e) + _nn(lo, e)


def _block_step(ops, states, cst, same_rows):
    L = CHUNK
    ncb = len(ops[0])
    idx = [(d, c) for d in range(2) for c in range(ncb)]

    lg = {}
    for d, c in idx:
        hi, mid, lo = _split3(ops[d][c][0])
        tri = cst["tri"][d]
        lg[d, c] = _nn(tri, hi) + _nn(tri, mid) + _nn(tri, lo)

    ar, wbt, wkt, bh, kh, dg = {}, {}, {}, {}, {}, {}
    for d, c in idx:
        lw, r, kd, v, kk, b = ops[d][c]
        g = lg[d, c]
        g_l = g[0:1, :] if d == 1 else g[L - 1:L, :]
        at = -(kk * jnp.exp(g - lw))
        rt = r * jnp.exp(g)
        e_inv = jnp.exp(-g)
        e_l = jnp.exp(g_l - g)
        ar[d, c] = jnp.concatenate([at, rt], axis=0).astype(BF16)
        wbt[d, c] = _wbd(b * e_inv, cst)
        wkt[d, c] = _wbd(kd * e_inv, cst)
        bh[d, c] = (b * e_l).astype(BF16)
        kh[d, c] = (kd * e_l).astype(BF16)
        dg[d, c] = jnp.where(cst["eye"], jnp.exp(g_l), 0.0).astype(BF16)

    gb = {i: _nt(ar[i], wbt[i]) for i in idx}
    gk = {i: _nt(ar[i], wkt[i]) for i in idx}
    m_ab = {(d, c): jnp.where(cst["strict"][d], gb[d, c][:L], 0.0) for d, c in idx}
    m_rb = {(d, c): jnp.where(cst["incl"][d], gb[d, c][L:], 0.0).astype(BF16) for d, c in idx}
    m_akrk = {(d, c): jnp.concatenate([jnp.where(cst["strict"][d], gk[d, c][:L], 0.0),
                                       jnp.where(cst["incl"][d], gk[d, c][L:], 0.0)], axis=0).astype(BF16)
              for d, c in idx}

    x = {i: cst["eye_sbs"] + jnp.where(cst["off"][0], m_ab[i], 0.0) for i in idx}
    for off in cst["off"][1:]:
        p = {i: _nn(jnp.where(off, m_ab[i], 0.0).astype(BF16), _wbd(x[i], cst)) for i in idx}
        x = {i: x[i] + _nn(x[i].astype(BF16), _wbd(p[i], cst)) for i in idx}
    xb = {i: x[i].astype(BF16) for i in idx}

    vsrc = [(0, c) if same_rows else (d, c) for d, c in idx]
    wv = {i: _wbd(ops[i[0]][i[1]][3], cst) for i in dict.fromkeys(vsrc)}
    vb = {i: ops[i[0]][i[1]][3].astype(BF16) for i in dict.fromkeys(vsrc)}
    mv = {i: _nn(m_akrk[i], wv[j]) for i, j in zip(idx, vsrc)}
    kv = {i: jnp.where(cst["bmask"], _tn(kh[i], vb[j]), 0.0) for i, j in zip(idx, vsrc)}

    s = list(states)
    ys = [[None] * ncb for _ in range(2)]
    for step in range(ncb):
        cur = [(0, step), (1, ncb - 1 - step)]
        ars = {i: _nn(jnp.concatenate([ar[i], dg[i]], axis=0), s[i[0]].astype(BF16)) for i in cur}
        u = {i: _nn(xb[i], _wbd(ars[i][:L] + mv[i][:L], cst)) for i in cur}
        for i in cur:
            d, c = i
            ys[d][c] = ars[i][L:2 * L] + _nn(m_rb[i], _wbd(u[i], cst)) + mv[i][L:]
            upd = _tn(bh[i], u[i].astype(BF16))
            s[d] = ars[i][2 * L:] + kv[i] + jnp.where(cst["bmask"], upd, 0.0)
    return ys, s


def _rwkv_kernel(*refs, seq, has_s0, want_state):
    it = iter(refs)
    r_ref, k_ref, v_ref, lo_ref, gb_ref = (next(it) for _ in range(5))
    mur_ref, muk_ref, muv_ref, mul_ref = (next(it) for _ in range(4))
    w0_ref, w2_ref, a0_ref, a2_ref = (next(it) for _ in range(4))
    kk_ref, ka_ref, rk_ref, lng_ref, lnb_ref = (next(it) for _ in range(5))
    s0_ref = next(it) if has_s0 else None
    ob_ref = next(it)
    sf_ref = next(it) if want_state else None
    r_s, v_s, kn_s, lw_s, kd_s, b_s, y_s, st_s = (next(it) for _ in range(8))

    T = seq
    cst = _rwkv_consts()
    e = jnp.where(cst["bmask"], 1.0, 0.0).astype(BF16)
    row = lax.broadcasted_iota(jnp.int32, (T, GW), 0)

    def shifted(ref, mu_ref):
        x = ref[...]
        mu = mu_ref[...]
        prev = jnp.where(row == 0, 0.0, pltpu.roll(x, 1, 0))
        nxt = jnp.where(row == T - 1, 0.0, pltpu.roll(x, T - 1, 0))
        return x + mu[0:1, :] * (prev - x) + mu[1:2, :] * (nxt - x)

    r = shifted(r_ref, mur_ref)
    k = shifted(k_ref, muk_ref)
    v = shifted(v_ref, muv_ref)
    lo = shifted(lo_ref, mul_ref)
    r_s[...] = r
    v_s[...] = v
    kk = k * kk_ref[...]
    kk = kk * lax.rsqrt(_group_sum(kk * kk, e) + 1e-12)
    kn_s[...] = kk
    bonus_dot = _group_sum(r * k * rk_ref[...], e)

    tanh_lo = jnp.tanh(lo).astype(BF16)
    lo_b = lo.astype(BF16)

    def padded(w, before):
        parts = [jnp.zeros((before, GW), F32)] if before else []
        parts.append(w)
        after = GW - before - LORA_R
        if after:
            parts.append(jnp.zeros((after, GW), F32))
        return jnp.concatenate(parts, axis=0).astype(BF16)

    for d in range(2):
        w2p = padded(w2_ref[d], LORA_R * d)
        a2p = padded(a2_ref[d], LORA_R * (2 + d))
        z = w0_ref[d:d + 1, :] + _nn(tanh_lo, w2p)
        lw_s[d] = -EXP_NEG_HALF * jax.nn.sigmoid(z)
        a = jax.nn.sigmoid(a0_ref[d:d + 1, :] + _nn(lo_b, a2p))
        kd_s[d] = k * (1.0 + (a - 1.0) * ka_ref[...])
        b_s[d] = kk * a

    if has_s0:
        ident = jnp.where(cst["eye"], 1.0, 0.0).astype(BF16)
        for d in range(2):
            z0 = s0_ref[0, d].reshape(GW, HEAD_DIM_B)
            hi, lo2 = _split2(z0)
            zt = _tn(hi, ident) + _tn(lo2, ident)
            st_s[d] = jnp.where(cst["bmask"], jnp.concatenate([zt, zt, zt, zt], axis=0), 0.0)
    else:
        st_s[...] = jnp.zeros(st_s.shape, F32)

    nblk = T // BLOCK_T
    ncb = BLOCK_T // CHUNK

    def block(i, carry):
        rows = []
        for d in range(2):
            blk = i if d == 0 else nblk - 1 - i
            rows.append([])
            for c in range(ncb):
                start = blk * BLOCK_T + c * CHUNK
                if not isinstance(start, int):
                    start = pl.multiple_of(start, CHUNK)
                rows[d].append(pl.ds(start, CHUNK))
        ops = [[(lw_s[d, rw, :], r_s[rw, :], kd_s[d, rw, :], v_s[rw, :], kn_s[rw, :], b_s[d, rw, :])
                for rw in rows[d]] for d in range(2)]
        ys, s_new = _block_step(ops, [st_s[0], st_s[1]], cst, nblk == 1)
        for d in range(2):
            for c in range(ncb):
                y_s[d, rows[d][c], :] = ys[d][c]
            st_s[d] = s_new[d]
        return carry

    if nblk == 1:
        block(0, 0)
    else:
        lax.fori_loop(0, nblk, block, 0)

    y = y_s[0] + y_s[1]
    mu = _group_sum(y, e) * (1.0 / HEAD_DIM_B)
    dlt = y - mu
    var = _group_sum(dlt * dlt, e) * (1.0 / HEAD_DIM_B)
    yn = dlt * lax.rsqrt(var + LNX_EPS) * lng_ref[...] + lnb_ref[...]
    out = (yn + bonus_dot * v_s[...]) * _silu(gb_ref[...])
    ob_ref[...] = out.astype(BF16)

    if want_state:
        ri = lax.broadcasted_iota(jnp.int32, (GW, HEAD_DIM_B), 0)
        ci = lax.broadcasted_iota(jnp.int32, (GW, HEAD_DIM_B), 1)
        fold = jnp.where((ri & (HEAD_DIM_B - 1)) == ci, 1.0, 0.0).astype(BF16)
        for d in range(2):
            hi, lo2 = _split2(st_s[d])
            sf = _tn(hi, fold) + _tn(lo2, fold)
            sf_ref[0, d] = sf.reshape(GW // HEAD_DIM_B, HEAD_DIM_B, HEAD_DIM_B)


def _rwkv(u, g_src, seq, s0, want_state, shift_mu, decay_w0, decay_w2, iclr_a0, iclr_a2,
          k_k, k_a, r_k, lnx_g, lnx_b):
    m = u.shape[0]
    nbat = m // seq
    ng = D_MODEL // GW
    hg = GW // HEAD_DIM_B
    c0 = 4 * D_MODEL // GW
    ublk = lambda off: pl.BlockSpec((seq, GW), lambda b, g: (b, off + g))
    lblk = pl.BlockSpec((seq, GW), lambda b, g: (b, c0 + 3 * ng))
    mublk = lambda off: pl.BlockSpec((2, GW), lambda b, g: (0, off + g))
    mulblk = pl.BlockSpec((2, GW), lambda b, g: (0, 3 * ng))
    vec2 = pl.BlockSpec((2, GW), lambda b, g: (0, g))
    mat2 = pl.BlockSpec((2, LORA_R, GW), lambda b, g: (0, 0, g))
    vec1 = pl.BlockSpec((1, GW), lambda b, g: (0, g))
    sblk = pl.BlockSpec((1, 2, hg, HEAD_DIM_B, HEAD_DIM_B), lambda b, g: (b, 0, g, 0, 0))
    in_specs = [ublk(c0), ublk(c0 + ng), ublk(c0 + 2 * ng), lblk, ublk(c0 + 3 * ng + 1),
                mublk(0), mublk(ng), mublk(2 * ng), mulblk,
                vec2, mat2, vec2, mat2, vec1, vec1, vec1, vec1, vec1]
    args = [u, u, u, u, g_src, shift_mu, shift_mu, shift_mu, shift_mu,
            decay_w0, decay_w2, iclr_a0, iclr_a2,
            k_k.reshape(1, D_MODEL), k_a.reshape(1, D_MODEL), r_k.reshape(1, D_MODEL),
            lnx_g.reshape(1, D_MODEL), lnx_b.reshape(1, D_MODEL)]
    if s0 is not None:
        in_specs.append(sblk)
        args.append(s0)
    out_specs = [pl.BlockSpec((seq, GW), lambda b, g: (b, g))]
    out_shape = [jax.ShapeDtypeStruct((m, D_MODEL), BF16)]
    if want_state:
        out_specs.append(sblk)
        out_shape.append(jax.ShapeDtypeStruct((nbat, 2, N_HEADS_B, HEAD_DIM_B, HEAD_DIM_B), F32))
    tw = pltpu.VMEM((seq, GW), F32)
    tw2 = pltpu.VMEM((2, seq, GW), F32)
    res = pl.pallas_call(
        functools.partial(_rwkv_kernel, seq=seq, has_s0=s0 is not None, want_state=want_state),
        grid=(nbat, ng),
        in_specs=in_specs,
        out_specs=out_specs,
        out_shape=out_shape,
        scratch_shapes=[tw, tw, tw, tw2, tw2, tw2, tw2, pltpu.VMEM((2, GW, GW), F32)],
        compiler_params=pltpu.CompilerParams(
            dimension_semantics=("arbitrary", "arbitrary"), vmem_limit_bytes=VMEM_LIMIT),
        name="rwkv_state" if want_state else "rwkv",
    )(*args)
    return res


def _outproj_kernel(oa_ref, ob_ref, x_ref, mod_ref, w_ref, fg_ref, y_ref):
    acc = _nn(oa_ref[...], w_ref[0:D_MODEL, :]) + _nn(ob_ref[...], w_ref[D_MODEL:, :])
    gate = mod_ref[0][:, 2 * D_MODEL:]
    xo = x_ref[...] + gate * acc
    y_ref[...] = xo * lax.rsqrt(jnp.mean(xo * xo, axis=-1, keepdims=True) + NORM_EPS) * fg_ref[...]


def _outproj(o_a, o_b, x, mod3, w_out_bf, final_g, mod_row):
    m = x.shape[0]
    tm = 512
    blk = pl.BlockSpec((tm, D_MODEL), lambda i: (i, 0))
    return pl.pallas_call(
        _outproj_kernel,
        grid=(m // tm,),
        in_specs=[blk, blk, blk,
                  pl.BlockSpec((1, 1, 3 * D_MODEL), lambda i: (mod_row(i * tm), 0, 0)),
                  pl.BlockSpec((2 * D_MODEL, D_MODEL), lambda i: (0, 0)),
                  pl.BlockSpec((1, D_MODEL), lambda i: (0, 0))],
        out_specs=blk,
        out_shape=jax.ShapeDtypeStruct((m, D_MODEL), F32),
        compiler_params=pltpu.CompilerParams(vmem_limit_bytes=VMEM_LIMIT),
        name="outproj",
    )(o_a, o_b, x, mod3, w_out_bf, final_g)


def kernel(x_prompt, x_sample, cache_k, cache_v, state_rwkv, c, c_ctx, norm_g, w_ada, b_ada, w_in, lam_q1, lam_k1, lam_q2, lam_k2, subln_g, shift_mu, decay_w0, decay_w2, iclr_a0, iclr_a2, k_k, k_a, r_k, lnx_g, lnx_b, w_out, final_g):
    batch, seq, d = x_prompt.shape
    dbatch, dseq, _ = x_sample.shape
    past = cache_k.shape[2]
    assert d == D_MODEL and w_in.shape == (1, D_MODEL, D_IN) and dseq % GRID_W == 0

    cvec = jnp.concatenate([c_ctx[None, :], c, jnp.zeros((8 - 1 - dbatch, d), F32)], axis=0)
    mod3 = _mod(cvec, w_ada[0], b_ada[0]).reshape(8, 1, 3 * d)
    w_in_bf = w_in[0].astype(BF16)
    w_out_bf = w_out[0].astype(BF16)
    ctx_row = lambda tok: 0
    lat_row = lambda tok: 1 + tok // dseq
    lp = (shift_mu[0], decay_w0[0], decay_w2[0], iclr_a0[0], iclr_a2[0], k_k[0], k_a[0], r_k[0],
          lnx_g[0], lnx_b[0])
    lam = (lam_q1, lam_k1, lam_q2, lam_k2, subln_g)
    fg = final_g.reshape(1, d)

    xp = x_prompt.reshape(batch * seq, d)
    u_c = _inproj(xp, mod3, norm_g, w_in_bf, ctx_row)
    oa_c, nk, nv = _attn_ctx(u_c, seq, *lam)
    ob_c, s_new = _rwkv(u_c, u_c, seq, None, True, *lp)
    y_prompt = _outproj(oa_c, ob_c, xp, mod3, w_out_bf, fg, ctx_row).reshape(batch, seq, d)

    xs = x_sample.reshape(dbatch * dseq, d)
    u_l = _inproj(xs, mod3, norm_g, w_in_bf, lat_row)
    ck = cache_k[:, 0].reshape(dbatch * past, d)
    cv = cache_v[:, 0].reshape(dbatch * past, d)
    oa_l = _attn_lat(u_l, ck, cv, dseq, past, *lam)
    (ob_l,) = _rwkv(u_l, u_l, dseq, state_rwkv[:, 0], False, *lp)
    y_sample = _outproj(oa_l, ob_l, xs, mod3, w_out_bf, fg, lat_row).reshape(dbatch, dseq, d)

    new_k = nk.reshape(batch, 1, seq, N_HEADS_A, 2, HEAD_DIM_A)
    new_v = nv.reshape(batch, 1, seq, N_HEADS_A, 2 * HEAD_DIM_A)
    new_s = s_new.reshape(batch, 1, 2, N_HEADS_B, HEAD_DIM_B, HEAD_DIM_B)
    return (y_prompt, y_sample, new_k, new_v, new_s)
```

```python
import functools
import math

import jax
import jax.numpy as jnp
from jax import lax
from jax.experimental import pallas as pl
from jax.experimental.pallas import tpu as pltpu

F32 = jnp.float32
BF16 = jnp.bfloat16
HIGHEST = lax.Precision.HIGHEST

D_MODEL = 1024
GRID_W = 64
HEAD_DIM_A = 64
N_HEADS_A = 8
HEAD_DIM_B = 64
N_HEADS_B = 16
LORA_R = 64
D_SHIFT = 3 * D_MODEL + 4 * LORA_R
D_IN = 4 * D_MODEL + D_SHIFT + D_MODEL
ROPE_BASE = 10000.0
NORM_EPS = 1e-6
SUBLN_EPS = 1e-5
LNX_EPS = 64e-5
LAM_INIT = 0.8 - 0.6 * math.exp(-0.3 * 0)
EXP_NEG_HALF = math.exp(-0.5)

GW = 256
LW = 4 * LORA_R
PW = 128
RWKV_GW_CTX = 1024
RWKV_GW_LAT = 256
CHUNK = 64
BLOCK_T = 256
VMEM_LIMIT = 48 * 1024 * 1024


def _nn(a, b):
    return jnp.dot(a, b, preferred_element_type=F32)


def _nt(a, b):
    return lax.dot_general(a, b, (((1,), (1,)), ((), ())), preferred_element_type=F32)


def _tn(a, b):
    return lax.dot_general(a, b, (((0,), (0,)), ((), ())), preferred_element_type=F32)


def _split2(x):
    hi = x.astype(BF16)
    lo = (x - hi.astype(F32)).astype(BF16)
    return hi, lo


def _split3(x):
    hi = x.astype(BF16)
    r1 = x - hi.astype(F32)
    mid = r1.astype(BF16)
    lo = (r1 - mid.astype(F32)).astype(BF16)
    return hi, mid, lo


def _silu(x):
    return x * jax.nn.sigmoid(x)


def _mod_kernel(c_ref, w_ref, b_ref, o_ref):
    s = _silu(c_ref[...])
    o_ref[...] = jnp.dot(s, w_ref[...], precision=HIGHEST, preferred_element_type=F32) + b_ref[...]


def _mod(cvec, w_ada, b_ada):
    n = w_ada.shape[1]
    tn = 512
    return pl.pallas_call(
        _mod_kernel,
        grid=(n // tn,),
        in_specs=[
            pl.BlockSpec((8, D_MODEL), lambda j: (0, 0)),
            pl.BlockSpec((D_MODEL, tn), lambda j: (0, j)),
            pl.BlockSpec((1, tn), lambda j: (0, j)),
        ],
        out_specs=pl.BlockSpec((8, tn), lambda j: (0, j)),
        out_shape=jax.ShapeDtypeStruct((8, n), F32),
        name="mod",
    )(cvec, w_ada, b_ada.reshape(1, n))


def _inproj_kernel(x_ref, mod_ref, g_ref, w_ref, o_ref, h_ref):
    @pl.when(pl.program_id(1) == 0)
    def _():
        x = x_ref[...]
        y = x * lax.rsqrt(jnp.mean(x * x, axis=-1, keepdims=True) + NORM_EPS) * g_ref[...]
        mod = mod_ref[0]
        shift = mod[:, 0:D_MODEL]
        scale = mod[:, D_MODEL:2 * D_MODEL]
        h_ref[...] = (y * (1.0 + scale) + shift).astype(BF16)

    o_ref[...] = _nn(h_ref[...], w_ref[...])


def _inproj(x, mod3, norm_g, w_in_bf, mod_row):
    m = x.shape[0]
    tm, tn = 1024, 768
    return pl.pallas_call(
        _inproj_kernel,
        grid=(m // tm, D_IN // tn),
        in_specs=[
            pl.BlockSpec((tm, D_MODEL), lambda i, j: (i, 0)),
            pl.BlockSpec((1, 1, 3 * D_MODEL), lambda i, j: (mod_row(i * tm), 0, 0)),
            pl.BlockSpec((1, D_MODEL), lambda i, j: (0, 0)),
            pl.BlockSpec((D_MODEL, tn), lambda i, j: (0, j)),
        ],
        out_specs=pl.BlockSpec((tm, tn), lambda i, j: (i, j)),
        out_shape=jax.ShapeDtypeStruct((m, D_IN), F32),
        scratch_shapes=[pltpu.VMEM((tm, D_MODEL), BF16)],
        compiler_params=pltpu.CompilerParams(
            dimension_semantics=("arbitrary", "arbitrary"), vmem_limit_bytes=VMEM_LIMIT),
        name="inproj",
    )(x, mod3, norm_g, w_in_bf)


def _lam(q1_ref, k1_ref, q2_ref, k2_ref):
    s1 = jnp.sum(q1_ref[...] * k1_ref[...], axis=-1, keepdims=True)
    s2 = jnp.sum(q2_ref[...] * k2_ref[...], axis=-1, keepdims=True)
    return jnp.exp(s1) - jnp.exp(s2) + LAM_INIT


def _diff_head(q, kb, vb, g, lam, subg):
    lane = lax.broadcasted_iota(jnp.int32, q.shape, 1)
    qs = q * (HEAD_DIM_A ** -0.5)
    probs = []
    for m in range(2):
        qm = jnp.where((lane >> 6) == m, qs, 0.0).astype(BF16)
        s = _nt(qm, kb)
        s = s - jnp.max(s, axis=-1, keepdims=True)
        p = jnp.exp(s)
        probs.append(p * (1.0 / jnp.sum(p, axis=-1, keepdims=True)))
    a = probs[0] - lam * probs[1]
    o = _nn(a.astype(BF16), vb)
    o = o * lax.rsqrt(jnp.mean(o * o, axis=-1, keepdims=True) + SUBLN_EPS) * subg
    o = o * (1.0 - LAM_INIT)
    return o * _silu(g)


def _attn_ctx_kernel(q_ref, k_ref, v_ref, g_ref, q1_ref, k1_ref, q2_ref, k2_ref, sg_ref,
                     o_ref, nk_ref, nv_ref):
    lam = _lam(q1_ref, k1_ref, q2_ref, k2_ref)
    subg = sg_ref[...]
    nk_ref[...] = k_ref[...]
    nv_ref[...] = v_ref[...]
    for h in range(q_ref.shape[1] // 128):
        sl = slice(h * 128, (h + 1) * 128)
        o = _diff_head(q_ref[:, sl], k_ref[:, sl].astype(BF16), v_ref[:, sl].astype(BF16),
                       g_ref[:, sl], lam, subg)
        o_ref[:, sl] = o.astype(BF16)


def _attn_ctx(u, seq, lam_q1, lam_k1, lam_q2, lam_k2, subln_g):
    m = u.shape[0]
    w = D_MODEL
    nb = D_MODEL // w
    small = pl.BlockSpec((1, HEAD_DIM_A), lambda b: (0, 0))
    blk = lambda off: pl.BlockSpec((seq, w), lambda b: (b, off))
    return pl.pallas_call(
        _attn_ctx_kernel,
        grid=(m // seq,),
        in_specs=[blk(0), blk(nb), blk(2 * nb), blk(3 * nb), small, small, small, small,
                  pl.BlockSpec((1, 2 * HEAD_DIM_A), lambda b: (0, 0))],
        out_specs=[blk(0), blk(0), blk(0)],
        out_shape=[jax.ShapeDtypeStruct((m, D_MODEL), BF16),
                   jax.ShapeDtypeStruct((m, D_MODEL), F32),
                   jax.ShapeDtypeStruct((m, D_MODEL), F32)],
        compiler_params=pltpu.CompilerParams(vmem_limit_bytes=VMEM_LIMIT),
        name="attn_ctx",
    )(u, u, u, u, lam_q1, lam_k1, lam_q2, lam_k2, subln_g)


def _rope(x, cos, s1, s2):
    return x * cos + pltpu.roll(x, 112, 1) * s1 + pltpu.roll(x, 16, 1) * s2


def _attn_lat_kernel(q_ref, k_ref, v_ref, g_ref, ck_ref, cv_ref, cq_ref, s1q_ref, s2q_ref,
                     ckk_ref, s1k_ref, s2k_ref, q1_ref, k1_ref, q2_ref, k2_ref, sg_ref,
                     o_ref, kall_ref, vall_ref):
    past = ck_ref.shape[0]
    nh = q_ref.shape[1] // 128

    @pl.when(pl.program_id(2) == 0)
    def _():
        kall_ref[0:past, :] = ck_ref[...].astype(BF16)
        vall_ref[0:past, :] = cv_ref[...].astype(BF16)
        vall_ref[past:, :] = v_ref[...].astype(BF16)
        for h in range(nh):
            sl = slice(h * 128, (h + 1) * 128)
            kall_ref[past:, sl] = _rope(k_ref[:, sl], ckk_ref[...], s1k_ref[...], s2k_ref[...]).astype(BF16)

    lam = _lam(q1_ref, k1_ref, q2_ref, k2_ref)
    subg = sg_ref[...]
    for h in range(nh):
        sl = slice(h * 128, (h + 1) * 128)
        q = _rope(q_ref[:, sl], cq_ref[...], s1q_ref[...], s2q_ref[...])
        o = _diff_head(q, kall_ref[:, sl], vall_ref[:, sl], g_ref[:, sl], lam, subg)
        o_ref[:, sl] = o.astype(BF16)


def _rope_tables(t):
    pos = jnp.arange(t)
    row = (pos // GRID_W).astype(F32)
    col = (pos % GRID_W).astype(F32)
    n_freq = HEAD_DIM_A // 4
    inv = ROPE_BASE ** (-jnp.arange(n_freq, dtype=F32) / n_freq)
    lane = jnp.arange(128)
    d = lane % HEAD_DIM_A
    use_col = (d // 32) == 1
    second = ((d % 32) // 16) == 1
    f = d % 16
    ang = jnp.where(use_col[None, :], col[:, None], row[:, None]) * inv[f][None, :]
    cos = jnp.cos(ang)
    sin = jnp.sin(ang)
    s1 = jnp.where(second[None, :], 0.0, -sin)
    s2 = jnp.where(second[None, :], sin, 0.0)
    return cos, s1, s2


def _attn_lat(u, cache_k2, cache_v2, seq, past, lam_q1, lam_k1, lam_q2, lam_k2, subln_g):
    m = u.shape[0]
    nbat = m // seq
    w = GW
    ng = D_MODEL // w
    tq = 256
    nq = seq // tq
    cos, s1, s2 = _rope_tables(seq)
    small = pl.BlockSpec((1, HEAD_DIM_A), lambda b, g, i: (0, 0))
    qblk = lambda off: pl.BlockSpec((tq, w), lambda b, g, i: (b * nq + i, off + g))
    kblk = lambda off: pl.BlockSpec((seq, w), lambda b, g, i: (b, off + g))
    cblk = pl.BlockSpec((past, w), lambda b, g, i: (b, g))
    tq_blk = pl.BlockSpec((tq, 128), lambda b, g, i: (i, 0))
    tk_blk = pl.BlockSpec((seq, 128), lambda b, g, i: (0, 0))
    return pl.pallas_call(
        _attn_lat_kernel,
        grid=(nbat, ng, nq),
        in_specs=[qblk(0), kblk(ng), kblk(2 * ng), qblk(3 * ng), cblk, cblk,
                  tq_blk, tq_blk, tq_blk, tk_blk, tk_blk, tk_blk,
                  small, small, small, small,
                  pl.BlockSpec((1, 2 * HEAD_DIM_A), lambda b, g, i: (0, 0))],
        out_specs=pl.BlockSpec((tq, w), lambda b, g, i: (b * nq + i, g)),
        out_shape=jax.ShapeDtypeStruct((m, D_MODEL), BF16),
        scratch_shapes=[pltpu.VMEM((past + seq, w), BF16), pltpu.VMEM((past + seq, w), BF16)],
        compiler_params=pltpu.CompilerParams(
            dimension_semantics=("arbitrary", "arbitrary", "arbitrary"), vmem_limit_bytes=VMEM_LIMIT),
        name="attn_lat",
    )(u, u, u, u, cache_k2, cache_v2, cos, s1, s2, cos, s1, s2,
      lam_q1, lam_k1, lam_q2, lam_k2, subln_g)


def _rwkv_consts():
    ri = lax.broadcasted_iota(jnp.int32, (PW, PW), 0)
    ci = lax.broadcasted_iota(jnp.int32, (PW, PW), 1)
    t = lax.broadcasted_iota(jnp.int32, (CHUNK, PW), 0)
    lane = lax.broadcasted_iota(jnp.int32, (CHUNK, PW), 1)
    j = lane & (CHUNK - 1)
    ti = lax.broadcasted_iota(jnp.int32, (CHUNK, CHUNK), 0)
    tj = lax.broadcasted_iota(jnp.int32, (CHUNK, CHUNK), 1)
    return dict(
        bmask=(ri >> 6) == (ci >> 6),
        eye=ri == ci,
        head0=lane < HEAD_DIM_B,
        eye_sbs=jnp.where(t == j, 1.0, 0.0),
        strict=(j < t, j > t),
        incl=(j <= t, j >= t),
        off=[((t >> (n + 1)) == (j >> (n + 1))) & ((t >> n) != (j >> n)) for n in range(6)],
        tri=(jnp.where(tj <= ti, 1.0, 0.0).astype(BF16), jnp.where(tj >= ti, 1.0, 0.0).astype(BF16)),
    )


def _wbd(x, cst):
    xb = x.astype(BF16)
    z = jnp.zeros_like(xb)
    return jnp.concatenate([jnp.where(cst["head0"], xb, z), jnp.where(cst["head0"], z, xb)], axis=0)


def _group_sum(x, e):
    hi, lo = _split2(x)
    ew = e.shape[0]
    parts = [_nn(hi[:, o:o + ew], e) + _nn(lo[:, o:o + ew], e) for o in range(0, x.shape[1], ew)]
    return parts[0] if len(parts) == 1 else jnp.concatenate(parts, axis=1)


def _block_step(ops, states, cst, same_rows):
    L = CHUNK
    ncb = len(ops[0])
    npair = ops[0][0][0].shape[1] // PW
    idx = [(d, c, h) for d in range(2) for c in range(ncb) for h in range(npair)]
    half = lambda a, h: a[:, h * PW:(h + 1) * PW]

    lg = {}
    for d in range(2):
        for c in range(ncb):
            hi, mid, lo = _split3(ops[d][c][0])
            tri = cst["tri"][d]
            lg[d, c] = _nn(tri, hi) + _nn(tri, mid) + _nn(tri, lo)

    ar, wbt, wkt, bh, kh, dg = {}, {}, {}, {}, {}, {}
    for d in range(2):
        for c in range(ncb):
            lw, r, kd, v, kk, b = ops[d][c]
            g = lg[d, c]
            g_l = g[0:1, :] if d == 1 else g[L - 1:L, :]
            at = -(kk * jnp.exp(g - lw))
            rt = r * jnp.exp(g)
            e_inv = jnp.exp(-g)
            e_l = jnp.exp(g_l - g)
            atrt = jnp.concatenate([at, rt], axis=0).astype(BF16)
            bt, kt = b * e_inv, kd * e_inv
            bhh, khh = (b * e_l).astype(BF16), (kd * e_l).astype(BF16)
            eg = jnp.exp(g_l)
            for h in range(npair):
                i = (d, c, h)
                ar[i] = half(atrt, h)
                wbt[i] = _wbd(half(bt, h), cst)
                wkt[i] = _wbd(half(kt, h), cst)
                bh[i] = half(bhh, h)
                kh[i] = half(khh, h)
                dg[i] = jnp.where(cst["eye"], half(eg, h), 0.0).astype(BF16)

    gb = {i: _nt(ar[i], wbt[i]) for i in idx}
    gk = {i: _nt(ar[i], wkt[i]) for i in idx}
    m_ab = {i: jnp.where(cst["strict"][i[0]], gb[i][:L], 0.0) for i in idx}
    m_rb = {i: jnp.where(cst["incl"][i[0]], gb[i][L:], 0.0).astype(BF16) for i in idx}
    m_akrk = {i: jnp.concatenate([jnp.where(cst["strict"][i[0]], gk[i][:L], 0.0),
                                  jnp.where(cst["incl"][i[0]], gk[i][L:], 0.0)], axis=0).astype(BF16)
              for i in idx}

    x = {i: cst["eye_sbs"] + jnp.where(cst["off"][0], m_ab[i], 0.0) for i in idx}
    for off in cst["off"][1:]:
        p = {i: _nn(jnp.where(off, m_ab[i], 0.0).astype(BF16), _wbd(x[i], cst)) for i in idx}
        x = {i: x[i] + _nn(x[i].astype(BF16), _wbd(p[i], cst)) for i in idx}
    xb = {i: x[i].astype(BF16) for i in idx}

    vsrc = [(0, c, h) if same_rows else (d, c, h) for d, c, h in idx]
    vhalf = {i: half(ops[i[0]][i[1]][3], i[2]) for i in dict.fromkeys(vsrc)}
    wv = {i: _wbd(vhalf[i], cst) for i in vhalf}
    vb = {i: vhalf[i].astype(BF16) for i in vhalf}
    mv = {i: _nn(m_akrk[i], wv[j]) for i, j in zip(idx, vsrc)}
    kv = {i: jnp.where(cst["bmask"], _tn(kh[i], vb[j]), 0.0) for i, j in zip(idx, vsrc)}

    s = [list(sd) for sd in states]
    ys = {}
    for step in range(ncb):
        cur = [(d, c, h) for d, c in ((0, step), (1, ncb - 1 - step)) for h in range(npair)]
        ars = {i: _nn(jnp.concatenate([ar[i], dg[i]], axis=0), s[i[0]][i[2]].astype(BF16)) for i in cur}
        u = {i: _nn(xb[i], _wbd(ars[i][:L] + mv[i][:L], cst)) for i in cur}
        for i in cur:
            d, c, h = i
            ys[i] = ars[i][L:2 * L] + _nn(m_rb[i], _wbd(u[i], cst)) + mv[i][L:]
            upd = _tn(bh[i], u[i].astype(BF16))
            s[d][h] = ars[i][2 * L:] + kv[i] + jnp.where(cst["bmask"], upd, 0.0)
    ys = [[jnp.concatenate([ys[d, c, h] for h in range(npair)], axis=1) for c in range(ncb)]
          for d in range(2)]
    return ys, s


def _rwkv_kernel(*refs, seq, gw, has_s0, want_state):
    it = iter(refs)
    r_ref, k_ref, v_ref, lo_ref = (next(it) for _ in range(4))
    gb_refs = [next(it) for _ in range(gw // LW)]
    mur_ref, muk_ref, muv_ref, mul_ref = (next(it) for _ in range(4))
    w0_ref, w2_ref, a0_ref, a2_ref = (next(it) for _ in range(4))
    kk_ref, ka_ref, rk_ref, lng_ref, lnb_ref = (next(it) for _ in range(5))
    s0_ref = next(it) if has_s0 else None
    ob_ref = next(it)
    sf_ref = next(it) if want_state else None
    r_s, v_s, kn_s, lw_s, kd_s, b_s, y_s, st_s = (next(it) for _ in range(8))

    T = seq
    cst = _rwkv_consts()
    npair = gw // PW
    hpp = PW // HEAD_DIM_B
    gi = lax.broadcasted_iota(jnp.int32, (LW, LW), 0)
    gj = lax.broadcasted_iota(jnp.int32, (LW, LW), 1)
    e = jnp.where((gi >> 6) == (gj >> 6), 1.0, 0.0).astype(BF16)
    rows_of = {w: lax.broadcasted_iota(jnp.int32, (T, w), 0) for w in {gw, LW}}

    def shifted(ref, mu_ref):
        x = ref[...]
        mu = mu_ref[...]
        row = rows_of[x.shape[1]]
        prev = jnp.where(row == 0, 0.0, pltpu.roll(x, 1, 0))
        nxt = jnp.where(row == T - 1, 0.0, pltpu.roll(x, T - 1, 0))
        return x + mu[0:1, :] * (prev - x) + mu[1:2, :] * (nxt - x)

    r = shifted(r_ref, mur_ref)
    k = shifted(k_ref, muk_ref)
    v = shifted(v_ref, muv_ref)
    lo = shifted(lo_ref, mul_ref)
    r_s[...] = r
    v_s[...] = v
    kk = k * kk_ref[...]
    kk = kk * lax.rsqrt(_group_sum(kk * kk, e) + 1e-12)
    kn_s[...] = kk
    bonus_dot = _group_sum(r * k * rk_ref[...], e)

    tanh_lo = jnp.tanh(lo).astype(BF16)
    lo_b = lo.astype(BF16)

    def padded(w, before):
        parts = [jnp.zeros((before, gw), F32)] if before else []
        parts.append(w)
        after = LW - before - LORA_R
        if after:
            parts.append(jnp.zeros((after, gw), F32))
        return jnp.concatenate(parts, axis=0).astype(BF16)

    for d in range(2):
        w2p = padded(w2_ref[d], LORA_R * d)
        a2p = padded(a2_ref[d], LORA_R * (2 + d))
        z = w0_ref[d:d + 1, :] + _nn(tanh_lo, w2p)
        lw_s[d] = -EXP_NEG_HALF * jax.nn.sigmoid(z)
        a = jax.nn.sigmoid(a0_ref[d:d + 1, :] + _nn(lo_b, a2p))
        kd_s[d] = k * (1.0 + (a - 1.0) * ka_ref[...])
        b_s[d] = kk * a

    if has_s0:
        ident = jnp.where(cst["eye"], 1.0, 0.0).astype(BF16)
        for d in range(2):
            for h in range(npair):
                z0 = s0_ref[0, d, h * hpp:(h + 1) * hpp].reshape(PW, HEAD_DIM_B)
                hi, lo2 = _split2(z0)
                zt = _tn(hi, ident) + _tn(lo2, ident)
                st_s[d, h] = jnp.where(cst["bmask"], jnp.concatenate([zt] * hpp, axis=0), 0.0)
    else:
        st_s[...] = jnp.zeros(st_s.shape, F32)

    nblk = T // BLOCK_T
    ncb = BLOCK_T // CHUNK

    def block(i, carry):
        rows = []
        for d in range(2):
            blk = i if d == 0 else nblk - 1 - i
            rows.append([])
            for c in range(ncb):
                start = blk * BLOCK_T + c * CHUNK
                if not isinstance(start, int):
                    start = pl.multiple_of(start, CHUNK)
                rows[d].append(pl.ds(start, CHUNK))
        ops = [[(lw_s[d, rw, :], r_s[rw, :], kd_s[d, rw, :], v_s[rw, :], kn_s[rw, :], b_s[d, rw, :])
                for rw in rows[d]] for d in range(2)]
        states = [[st_s[d, h] for h in range(npair)] for d in range(2)]
        ys, s_new = _block_step(ops, states, cst, nblk == 1)
        for d in range(2):
            for c in range(ncb):
                y_s[d, rows[d][c], :] = ys[d][c]
            for h in range(npair):
                st_s[d, h] = s_new[d][h]
        return carry

    if nblk == 1:
        block(0, 0)
    else:
        lax.fori_loop(0, nblk, block, 0)

    y = y_s[0] + y_s[1]
    mu = _group_sum(y, e) * (1.0 / HEAD_DIM_B)
    dlt = y - mu
    var = _group_sum(dlt * dlt, e) * (1.0 / HEAD_DIM_B)
    yn = dlt * lax.rsqrt(var + LNX_EPS) * lng_ref[...] + lnb_ref[...]
    gate = [g_ref[...] for g_ref in gb_refs]
    gate = gate[0] if len(gate) == 1 else jnp.concatenate(gate, axis=1)
    out = (yn + bonus_dot * v_s[...]) * _silu(gate)
    ob_ref[...] = out.astype(BF16)

    if want_state:
        ri = lax.broadcasted_iota(jnp.int32, (PW, HEAD_DIM_B), 0)
        ci = lax.broadcasted_iota(jnp.int32, (PW, HEAD_DIM_B), 1)
        fold = jnp.where((ri & (HEAD_DIM_B - 1)) == ci, 1.0, 0.0).astype(BF16)
        for d in range(2):
            for h in range(npair):
                hi, lo2 = _split2(st_s[d, h])
                sf = _tn(hi, fold) + _tn(lo2, fold)
                sf_ref[0, d, h * hpp:(h + 1) * hpp] = sf.reshape(hpp, HEAD_DIM_B, HEAD_DIM_B)


def _rwkv(u, seq, gw, s0, want_state, shift_mu, decay_w0, decay_w2, iclr_a0, iclr_a2,
          k_k, k_a, r_k, lnx_g, lnx_b):
    m = u.shape[0]
    nbat = m // seq
    ng = D_MODEL // gw
    hg = gw // HEAD_DIM_B
    c0 = 4 * D_MODEL // gw
    l0 = (4 * D_MODEL + 3 * D_MODEL) // LW
    ngate = gw // LW
    ublk = lambda off: pl.BlockSpec((seq, gw), lambda b, g: (b, off + g))
    lblk = pl.BlockSpec((seq, LW), lambda b, g: (b, l0))
    gblk = lambda q: pl.BlockSpec((seq, LW), lambda b, g: (b, l0 + 1 + g * ngate + q))
    mublk = lambda off: pl.BlockSpec((2, gw), lambda b, g: (0, off + g))
    mulblk = pl.BlockSpec((2, LW), lambda b, g: (0, 3 * D_MODEL // LW))
    vec2 = pl.BlockSpec((2, gw), lambda b, g: (0, g))
    mat2 = pl.BlockSpec((2, LORA_R, gw), lambda b, g: (0, 0, g))
    vec1 = pl.BlockSpec((1, gw), lambda b, g: (0, g))
    sblk = pl.BlockSpec((1, 2, hg, HEAD_DIM_B, HEAD_DIM_B), lambda b, g: (b, 0, g, 0, 0))
    in_specs = [ublk(c0), ublk(c0 + ng), ublk(c0 + 2 * ng), lblk] + [gblk(q) for q in range(ngate)] + [
                mublk(0), mublk(ng), mublk(2 * ng), mulblk,
                vec2, mat2, vec2, mat2, vec1, vec1, vec1, vec1, vec1]
    args = [u, u, u, u] + [u] * ngate + [shift_mu, shift_mu, shift_mu, shift_mu,
            decay_w0, decay_w2, iclr_a0, iclr_a2,
            k_k.reshape(1, D_MODEL), k_a.reshape(1, D_MODEL), r_k.reshape(1, D_MODEL),
            lnx_g.reshape(1, D_MODEL), lnx_b.reshape(1, D_MODEL)]
    if s0 is not None:
        in_specs.append(sblk)
        args.append(s0)
    out_specs = [pl.BlockSpec((seq, gw), lambda b, g: (b, g))]
    out_shape = [jax.ShapeDtypeStruct((m, D_MODEL), BF16)]
    if want_state:
        out_specs.append(sblk)
        out_shape.append(jax.ShapeDtypeStruct((nbat, 2, N_HEADS_B, HEAD_DIM_B, HEAD_DIM_B), F32))
    tw = pltpu.VMEM((seq, gw), F32)
    tw2 = pltpu.VMEM((2, seq, gw), F32)
    res = pl.pallas_call(
        functools.partial(_rwkv_kernel, seq=seq, gw=gw, has_s0=s0 is not None, want_state=want_state),
        grid=(nbat, ng),
        in_specs=in_specs,
        out_specs=out_specs,
        out_shape=out_shape,
        scratch_shapes=[tw, tw, tw, tw2, tw2, tw2, tw2, pltpu.VMEM((2, gw // PW, PW, PW), F32)],
        compiler_params=pltpu.CompilerParams(
            dimension_semantics=("arbitrary", "arbitrary"), vmem_limit_bytes=VMEM_LIMIT),
        name="rwkv_state" if want_state else "rwkv",
    )(*args)
    return res


def _outproj_kernel(oa_ref, ob_ref, x_ref, mod_ref, w_ref, fg_ref, y_ref):
    acc = _nn(oa_ref[...], w_ref[0:D_MODEL, :]) + _nn(ob_ref[...], w_ref[D_MODEL:, :])
    gate = mod_ref[0][:, 2 * D_MODEL:]
    xo = x_ref[...] + gate * acc
    y_ref[...] = xo * lax.rsqrt(jnp.mean(xo * xo, axis=-1, keepdims=True) + NORM_EPS) * fg_ref[...]


def _outproj(o_a, o_b, x, mod3, w_out_bf, final_g, mod_row):
    m = x.shape[0]
    tm = 512
    blk = pl.BlockSpec((tm, D_MODEL), lambda i: (i, 0))
    return pl.pallas_call(
        _outproj_kernel,
        grid=(m // tm,),
        in_specs=[blk, blk, blk,
                  pl.BlockSpec((1, 1, 3 * D_MODEL), lambda i: (mod_row(i * tm), 0, 0)),
                  pl.BlockSpec((2 * D_MODEL, D_MODEL), lambda i: (0, 0)),
                  pl.BlockSpec((1, D_MODEL), lambda i: (0, 0))],
        out_specs=blk,
        out_shape=jax.ShapeDtypeStruct((m, D_MODEL), F32),
        compiler_params=pltpu.CompilerParams(vmem_limit_bytes=VMEM_LIMIT),
        name="outproj",
    )(o_a, o_b, x, mod3, w_out_bf, final_g)


def kernel(x_prompt, x_sample, cache_k, cache_v, state_rwkv, c, c_ctx, norm_g, w_ada, b_ada, w_in, lam_q1, lam_k1, lam_q2, lam_k2, subln_g, shift_mu, decay_w0, decay_w2, iclr_a0, iclr_a2, k_k, k_a, r_k, lnx_g, lnx_b, w_out, final_g):
    batch, seq, d = x_prompt.shape
    dbatch, dseq, _ = x_sample.shape
    past = cache_k.shape[2]
    assert d == D_MODEL and w_in.shape == (1, D_MODEL, D_IN) and dseq % GRID_W == 0

    cvec = jnp.concatenate([c_ctx[None, :], c, jnp.zeros((8 - 1 - dbatch, d), F32)], axis=0)
    mod3 = _mod(cvec, w_ada[0], b_ada[0]).reshape(8, 1, 3 * d)
    w_in_bf = w_in[0].astype(BF16)
    w_out_bf = w_out[0].astype(BF16)
    ctx_row = lambda tok: 0
    lat_row = lambda tok: 1 + tok // dseq
    lp = (shift_mu[0], decay_w0[0], decay_w2[0], iclr_a0[0], iclr_a2[0], k_k[0], k_a[0], r_k[0],
          lnx_g[0], lnx_b[0])
    lam = (lam_q1, lam_k1, lam_q2, lam_k2, subln_g)
    fg = final_g.reshape(1, d)

    xp = x_prompt.reshape(batch * seq, d)
    u_c = _inproj(xp, mod3, norm_g, w_in_bf, ctx_row)
    oa_c, nk, nv = _attn_ctx(u_c, seq, *lam)
    ob_c, s_new = _rwkv(u_c, seq, RWKV_GW_CTX, None, True, *lp)
    y_prompt = _outproj(oa_c, ob_c, xp, mod3, w_out_bf, fg, ctx_row).reshape(batch, seq, d)

    xs = x_sample.reshape(dbatch * dseq, d)
    u_l = _inproj(xs, mod3, norm_g, w_in_bf, lat_row)
    ck = cache_k[:, 0].reshape(dbatch * past, d)
    cv = cache_v[:, 0].reshape(dbatch * past, d)
    oa_l = _attn_lat(u_l, ck, cv, dseq, past, *lam)
    (ob_l,) = _rwkv(u_l, dseq, RWKV_GW_LAT, state_rwkv[:, 0], False, *lp)
    y_sample = _outproj(oa_l, ob_l, xs, mod3, w_out_bf, fg, lat_row).reshape(dbatch, dseq, d)

    new_k = nk.reshape(batch, 1, seq, N_HEADS_A, 2, HEAD_DIM_A)
    new_v = nv.reshape(batch, 1, seq, N_HEADS_A, 2 * HEAD_DIM_A)
    new_s = s_new.reshape(batch, 1, 2, N_HEADS_B, HEAD_DIM_B, HEAD_DIM_B)
    return (y_prompt, y_sample, new_k, new_v, new_s)
```

```python
import functools
import math

import jax
import jax.numpy as jnp
from jax import lax
from jax.experimental import pallas as pl
from jax.experimental.pallas import tpu as pltpu

F32 = jnp.float32
BF16 = jnp.bfloat16
HIGHEST = lax.Precision.HIGHEST

D_MODEL = 1024
GRID_W = 64
HEAD_DIM_A = 64
N_HEADS_A = 8
HEAD_DIM_B = 64
N_HEADS_B = 16
LORA_R = 64
D_SHIFT = 3 * D_MODEL + 4 * LORA_R
D_IN = 4 * D_MODEL + D_SHIFT + D_MODEL
ROPE_BASE = 10000.0
NORM_EPS = 1e-6
SUBLN_EPS = 1e-5
LNX_EPS = 64e-5
LAM_INIT = 0.8 - 0.6 * math.exp(-0.3 * 0)
EXP_NEG_HALF = math.exp(-0.5)

GW = 256
LW = 4 * LORA_R
PW = 128
RWKV_GW_CTX = 1024
RWKV_GW_LAT = 256
CHUNK = 64
BLOCK_T = 256
VMEM_LIMIT = 48 * 1024 * 1024


def _nn(a, b):
    return jnp.dot(a, b, preferred_element_type=F32)


def _nt(a, b):
    return lax.dot_general(a, b, (((1,), (1,)), ((), ())), preferred_element_type=F32)


def _tn(a, b):
    return lax.dot_general(a, b, (((0,), (0,)), ((), ())), preferred_element_type=F32)


def _split2(x):
    hi = x.astype(BF16)
    lo = (x - hi.astype(F32)).astype(BF16)
    return hi, lo


def _split3(x):
    hi = x.astype(BF16)
    r1 = x - hi.astype(F32)
    mid = r1.astype(BF16)
    lo = (r1 - mid.astype(F32)).astype(BF16)
    return hi, mid, lo


def _silu(x):
    return x * jax.nn.sigmoid(x)


def _mod_kernel(c_ref, w_ref, b_ref, o_ref):
    s = _silu(c_ref[...])
    o_ref[...] = jnp.dot(s, w_ref[...], precision=HIGHEST, preferred_element_type=F32) + b_ref[...]


def _mod(cvec, w_ada, b_ada):
    n = w_ada.shape[1]
    tn = 512
    return pl.pallas_call(
        _mod_kernel,
        grid=(n // tn,),
        in_specs=[
            pl.BlockSpec((8, D_MODEL), lambda j: (0, 0)),
            pl.BlockSpec((D_MODEL, tn), lambda j: (0, j)),
            pl.BlockSpec((1, tn), lambda j: (0, j)),
        ],
        out_specs=pl.BlockSpec((8, tn), lambda j: (0, j)),
        out_shape=jax.ShapeDtypeStruct((8, n), F32),
        name="mod",
    )(cvec, w_ada, b_ada.reshape(1, n))


def _inproj_kernel(x_ref, mod_ref, g_ref, w_ref, o_ref, h_ref):
    @pl.when(pl.program_id(1) == 0)
    def _():
        x = x_ref[...]
        y = x * lax.rsqrt(jnp.mean(x * x, axis=-1, keepdims=True) + NORM_EPS) * g_ref[...]
        mod = mod_ref[0]
        shift = mod[:, 0:D_MODEL]
        scale = mod[:, D_MODEL:2 * D_MODEL]
        h_ref[...] = (y * (1.0 + scale) + shift).astype(BF16)

    o_ref[...] = _nn(h_ref[...], w_ref[...])


def _inproj(x, mod3, norm_g, w_in_bf, mod_row):
    m = x.shape[0]
    tm, tn = 1024, 768
    return pl.pallas_call(
        _inproj_kernel,
        grid=(m // tm, D_IN // tn),
        in_specs=[
            pl.BlockSpec((tm, D_MODEL), lambda i, j: (i, 0)),
            pl.BlockSpec((1, 1, 3 * D_MODEL), lambda i, j: (mod_row(i * tm), 0, 0)),
            pl.BlockSpec((1, D_MODEL), lambda i, j: (0, 0)),
            pl.BlockSpec((D_MODEL, tn), lambda i, j: (0, j)),
        ],
        out_specs=pl.BlockSpec((tm, tn), lambda i, j: (i, j)),
        out_shape=jax.ShapeDtypeStruct((m, D_IN), F32),
        scratch_shapes=[pltpu.VMEM((tm, D_MODEL), BF16)],
        compiler_params=pltpu.CompilerParams(
            dimension_semantics=("arbitrary", "arbitrary"), vmem_limit_bytes=VMEM_LIMIT),
        name="inproj",
    )(x, mod3, norm_g, w_in_bf)


def _lam(q1_ref, k1_ref, q2_ref, k2_ref):
    s1 = jnp.sum(q1_ref[...] * k1_ref[...], axis=-1, keepdims=True)
    s2 = jnp.sum(q2_ref[...] * k2_ref[...], axis=-1, keepdims=True)
    return jnp.exp(s1) - jnp.exp(s2) + LAM_INIT


def _diff_head(q, kb, vb, g, lam, subg):
    lane = lax.broadcasted_iota(jnp.int32, q.shape, 1)
    qs = q * (HEAD_DIM_A ** -0.5)
    probs = []
    for m in range(2):
        qm = jnp.where((lane >> 6) == m, qs, 0.0).astype(BF16)
        s = _nt(qm, kb)
        s = s - jnp.max(s, axis=-1, keepdims=True)
        p = jnp.exp(s)
        probs.append(p * (1.0 / jnp.sum(p, axis=-1, keepdims=True)))
    a = probs[0] - lam * probs[1]
    o = _nn(a.astype(BF16), vb)
    o = o * lax.rsqrt(jnp.mean(o * o, axis=-1, keepdims=True) + SUBLN_EPS) * subg
    o = o * (1.0 - LAM_INIT)
    return o * _silu(g)


def _attn_ctx_kernel(q_ref, k_ref, v_ref, g_ref, q1_ref, k1_ref, q2_ref, k2_ref, sg_ref,
                     o_ref, nk_ref, nv_ref):
    lam = _lam(q1_ref, k1_ref, q2_ref, k2_ref)
    subg = sg_ref[...]
    nk_ref[0, 0] = k_ref[...].reshape(nk_ref.shape[2:])
    nv_ref[0, 0] = v_ref[...].reshape(nv_ref.shape[2:])
    for h in range(q_ref.shape[1] // 128):
        sl = slice(h * 128, (h + 1) * 128)
        o = _diff_head(q_ref[:, sl], k_ref[:, sl].astype(BF16), v_ref[:, sl].astype(BF16),
                       g_ref[:, sl], lam, subg)
        o_ref[:, sl] = o.astype(BF16)


def _attn_ctx(u, seq, lam_q1, lam_k1, lam_q2, lam_k2, subln_g):
    m = u.shape[0]
    w = D_MODEL
    nb = D_MODEL // w
    small = pl.BlockSpec((1, HEAD_DIM_A), lambda b: (0, 0))
    blk = lambda off: pl.BlockSpec((seq, w), lambda b: (b, off))
    return pl.pallas_call(
        _attn_ctx_kernel,
        grid=(m // seq,),
        in_specs=[blk(0), blk(nb), blk(2 * nb), blk(3 * nb), small, small, small, small,
                  pl.BlockSpec((1, 2 * HEAD_DIM_A), lambda b: (0, 0))],
        out_specs=[blk(0),
                   pl.BlockSpec((1, 1, seq, N_HEADS_A, 2, HEAD_DIM_A), lambda b: (b, 0, 0, 0, 0, 0)),
                   pl.BlockSpec((1, 1, seq, N_HEADS_A, 2 * HEAD_DIM_A), lambda b: (b, 0, 0, 0, 0))],
        out_shape=[jax.ShapeDtypeStruct((m, D_MODEL), BF16),
                   jax.ShapeDtypeStruct((m // seq, 1, seq, N_HEADS_A, 2, HEAD_DIM_A), F32),
                   jax.ShapeDtypeStruct((m // seq, 1, seq, N_HEADS_A, 2 * HEAD_DIM_A), F32)],
        compiler_params=pltpu.CompilerParams(vmem_limit_bytes=VMEM_LIMIT),
        name="attn_ctx",
    )(u, u, u, u, lam_q1, lam_k1, lam_q2, lam_k2, subln_g)


def _rope(x, cos, s1, s2):
    return x * cos + pltpu.roll(x, 112, 1) * s1 + pltpu.roll(x, 16, 1) * s2


def _attn_lat_kernel(q_ref, k_ref, v_ref, g_ref, ck_ref, cv_ref, cq_ref, s1q_ref, s2q_ref,
                     ckk_ref, s1k_ref, s2k_ref, q1_ref, k1_ref, q2_ref, k2_ref, sg_ref,
                     o_ref, kall_ref, vall_ref):
    past = ck_ref.shape[0]
    nh = q_ref.shape[1] // 128

    @pl.when(pl.program_id(2) == 0)
    def _():
        kall_ref[0:past, :] = ck_ref[...].astype(BF16)
        vall_ref[0:past, :] = cv_ref[...].astype(BF16)
        vall_ref[past:, :] = v_ref[...].astype(BF16)
        for h in range(nh):
            sl = slice(h * 128, (h + 1) * 128)
            kall_ref[past:, sl] = _rope(k_ref[:, sl], ckk_ref[...], s1k_ref[...], s2k_ref[...]).astype(BF16)

    lam = _lam(q1_ref, k1_ref, q2_ref, k2_ref)
    subg = sg_ref[...]
    for h in range(nh):
        sl = slice(h * 128, (h + 1) * 128)
        q = _rope(q_ref[:, sl], cq_ref[...], s1q_ref[...], s2q_ref[...])
        o = _diff_head(q, kall_ref[:, sl], vall_ref[:, sl], g_ref[:, sl], lam, subg)
        o_ref[:, sl] = o.astype(BF16)


def _rope_tables(t):
    pos = jnp.arange(t)
    row = (pos // GRID_W).astype(F32)
    col = (pos % GRID_W).astype(F32)
    n_freq = HEAD_DIM_A // 4
    inv = ROPE_BASE ** (-jnp.arange(n_freq, dtype=F32) / n_freq)
    lane = jnp.arange(128)
    d = lane % HEAD_DIM_A
    use_col = (d // 32) == 1
    second = ((d % 32) // 16) == 1
    f = d % 16
    ang = jnp.where(use_col[None, :], col[:, None], row[:, None]) * inv[f][None, :]
    cos = jnp.cos(ang)
    sin = jnp.sin(ang)
    s1 = jnp.where(second[None, :], 0.0, -sin)
    s2 = jnp.where(second[None, :], sin, 0.0)
    return cos, s1, s2


def _attn_lat(u, cache_k2, cache_v2, seq, past, lam_q1, lam_k1, lam_q2, lam_k2, subln_g):
    m = u.shape[0]
    nbat = m // seq
    w = GW
    ng = D_MODEL // w
    tq = 256
    nq = seq // tq
    cos, s1, s2 = _rope_tables(seq)
    small = pl.BlockSpec((1, HEAD_DIM_A), lambda b, g, i: (0, 0))
    qblk = lambda off: pl.BlockSpec((tq, w), lambda b, g, i: (b * nq + i, off + g))
    kblk = lambda off: pl.BlockSpec((seq, w), lambda b, g, i: (b, off + g))
    cblk = pl.BlockSpec((past, w), lambda b, g, i: (b, g))
    tq_blk = pl.BlockSpec((tq, 128), lambda b, g, i: (i, 0))
    tk_blk = pl.BlockSpec((seq, 128), lambda b, g, i: (0, 0))
    return pl.pallas_call(
        _attn_lat_kernel,
        grid=(nbat, ng, nq),
        in_specs=[qblk(0), kblk(ng), kblk(2 * ng), qblk(3 * ng), cblk, cblk,
                  tq_blk, tq_blk, tq_blk, tk_blk, tk_blk, tk_blk,
                  small, small, small, small,
                  pl.BlockSpec((1, 2 * HEAD_DIM_A), lambda b, g, i: (0, 0))],
        out_specs=pl.BlockSpec((tq, w), lambda b, g, i: (b * nq + i, g)),
        out_shape=jax.ShapeDtypeStruct((m, D_MODEL), BF16),
        scratch_shapes=[pltpu.VMEM((past + seq, w), BF16), pltpu.VMEM((past + seq, w), BF16)],
        compiler_params=pltpu.CompilerParams(
            dimension_semantics=("arbitrary", "arbitrary", "arbitrary"), vmem_limit_bytes=VMEM_LIMIT),
        name="attn_lat",
    )(u, u, u, u, cache_k2, cache_v2, cos, s1, s2, cos, s1, s2,
      lam_q1, lam_k1, lam_q2, lam_k2, subln_g)


def _rwkv_consts():
    ri = lax.broadcasted_iota(jnp.int32, (PW, PW), 0)
    ci = lax.broadcasted_iota(jnp.int32, (PW, PW), 1)
    t = lax.broadcasted_iota(jnp.int32, (CHUNK, PW), 0)
    lane = lax.broadcasted_iota(jnp.int32, (CHUNK, PW), 1)
    j = lane & (CHUNK - 1)
    ti = lax.broadcasted_iota(jnp.int32, (CHUNK, CHUNK), 0)
    tj = lax.broadcasted_iota(jnp.int32, (CHUNK, CHUNK), 1)
    return dict(
        bmask=(ri >> 6) == (ci >> 6),
        eye=ri == ci,
        head0=lane < HEAD_DIM_B,
        eye_sbs=jnp.where(t == j, 1.0, 0.0),
        strict=(j < t, j > t),
        incl=(j <= t, j >= t),
        off=[((t >> (n + 1)) == (j >> (n + 1))) & ((t >> n) != (j >> n)) for n in range(6)],
        tri=(jnp.where(tj <= ti, 1.0, 0.0).astype(BF16), jnp.where(tj >= ti, 1.0, 0.0).astype(BF16)),
    )


def _wbd(x, cst):
    xb = x.astype(BF16)
    z = jnp.zeros_like(xb)
    return jnp.concatenate([jnp.where(cst["head0"], xb, z), jnp.where(cst["head0"], z, xb)], axis=0)


def _group_sum(x, e):
    hi, lo = _split2(x)
    ew = e.shape[0]
    parts = [_nn(hi[:, o:o + ew], e) + _nn(lo[:, o:o + ew], e) for o in range(0, x.shape[1], ew)]
    return parts[0] if len(parts) == 1 else jnp.concatenate(parts, axis=1)


def _block_step(ops, states, cst, same_rows):
    L = CHUNK
    ncb = len(ops[0])
    npair = ops[0][0][0].shape[1] // PW
    idx = [(d, c, h) for d in range(2) for c in range(ncb) for h in range(npair)]
    half = lambda a, h: a[:, h * PW:(h + 1) * PW]

    lg = {}
    for d in range(2):
        for c in range(ncb):
            hi, mid, lo = _split3(ops[d][c][0])
            tri = cst["tri"][d]
            lg[d, c] = _nn(tri, hi) + _nn(tri, mid) + _nn(tri, lo)

    ar, wbt, wkt, bh, kh, dg = {}, {}, {}, {}, {}, {}
    for d in range(2):
        for c in range(ncb):
            lw, r, kd, v, kk, b = ops[d][c]
            g = lg[d, c]
            g_l = g[0:1, :] if d == 1 else g[L - 1:L, :]
            at = -(kk * jnp.exp(g - lw))
            rt = r * jnp.exp(g)
            e_inv = jnp.exp(-g)
            e_l = jnp.exp(g_l - g)
            atrt = jnp.concatenate([at, rt], axis=0).astype(BF16)
            bt, kt = b * e_inv, kd * e_inv
            bhh, khh = (b * e_l).astype(BF16), (kd * e_l).astype(BF16)
            eg = jnp.exp(g_l)
            for h in range(npair):
                i = (d, c, h)
                ar[i] = half(atrt, h)
                wbt[i] = _wbd(half(bt, h), cst)
                wkt[i] = _wbd(half(kt, h), cst)
                bh[i] = half(bhh, h)
                kh[i] = half(khh, h)
                dg[i] = jnp.where(cst["eye"], half(eg, h), 0.0).astype(BF16)

    gb = {i: _nt(ar[i], wbt[i]) for i in idx}
    gk = {i: _nt(ar[i], wkt[i]) for i in idx}
    m_ab = {i: jnp.where(cst["strict"][i[0]], gb[i][:L], 0.0) for i in idx}
    m_rb = {i: jnp.where(cst["incl"][i[0]], gb[i][L:], 0.0).astype(BF16) for i in idx}
    m_akrk = {i: jnp.concatenate([jnp.where(cst["strict"][i[0]], gk[i][:L], 0.0),
                                  jnp.where(cst["incl"][i[0]], gk[i][L:], 0.0)], axis=0).astype(BF16)
              for i in idx}

    x = {i: cst["eye_sbs"] + jnp.where(cst["off"][0], m_ab[i], 0.0) for i in idx}
    for off in cst["off"][1:]:
        p = {i: _nn(jnp.where(off, m_ab[i], 0.0).astype(BF16), _wbd(x[i], cst)) for i in idx}
        x = {i: x[i] + _nn(x[i].astype(BF16), _wbd(p[i], cst)) for i in idx}
    xb = {i: x[i].astype(BF16) for i in idx}

    vsrc = [(0, c, h) if same_rows else (d, c, h) for d, c, h in idx]
    vhalf = {i: half(ops[i[0]][i[1]][3], i[2]) for i in dict.fromkeys(vsrc)}
    wv = {i: _wbd(vhalf[i], cst) for i in vhalf}
    vb = {i: vhalf[i].astype(BF16) for i in vhalf}
    mv = {i: _nn(m_akrk[i], wv[j]) for i, j in zip(idx, vsrc)}
    kv = {i: jnp.where(cst["bmask"], _tn(kh[i], vb[j]), 0.0) for i, j in zip(idx, vsrc)}

    s = [list(sd) for sd in states]
    ys = {}
    for step in range(ncb):
        cur = [(d, c, h) for d, c in ((0, step), (1, ncb - 1 - step)) for h in range(npair)]
        ars = {i: _nn(jnp.concatenate([ar[i], dg[i]], axis=0), s[i[0]][i[2]].astype(BF16)) for i in cur}
        u = {i: _nn(xb[i], _wbd(ars[i][:L] + mv[i][:L], cst)) for i in cur}
        for i in cur:
            d, c, h = i
            ys[i] = ars[i][L:2 * L] + _nn(m_rb[i], _wbd(u[i], cst)) + mv[i][L:]
            upd = _tn(bh[i], u[i].astype(BF16))
            s[d][h] = ars[i][2 * L:] + kv[i] + jnp.where(cst["bmask"], upd, 0.0)
    ys = [[jnp.concatenate([ys[d, c, h] for h in range(npair)], axis=1) for c in range(ncb)]
          for d in range(2)]
    return ys, s


def _rwkv_kernel(*refs, seq, gw, has_s0, want_state):
    it = iter(refs)
    r_ref, k_ref, v_ref, lo_ref = (next(it) for _ in range(4))
    gb_refs = [next(it) for _ in range(gw // LW)]
    mur_ref, muk_ref, muv_ref, mul_ref = (next(it) for _ in range(4))
    w0_ref, w2_ref, a0_ref, a2_ref = (next(it) for _ in range(4))
    kk_ref, ka_ref, rk_ref, lng_ref, lnb_ref = (next(it) for _ in range(5))
    s0_ref = next(it) if has_s0 else None
    ob_ref = next(it)
    sf_ref = next(it) if want_state else None
    r_s, v_s, kn_s, lw_s, kd_s, b_s, y_s, st_s = (next(it) for _ in range(8))

    T = seq
    cst = _rwkv_consts()
    npair = gw // PW
    hpp = PW // HEAD_DIM_B
    gi = lax.broadcasted_iota(jnp.int32, (LW, LW), 0)
    gj = lax.broadcasted_iota(jnp.int32, (LW, LW), 1)
    e = jnp.where((gi >> 6) == (gj >> 6), 1.0, 0.0).astype(BF16)
    rows_of = {w: lax.broadcasted_iota(jnp.int32, (T, w), 0) for w in {gw, LW}}

    def shifted(ref, mu_ref):
        x = ref[...]
        mu = mu_ref[...]
        row = rows_of[x.shape[1]]
        prev = jnp.where(row == 0, 0.0, pltpu.roll(x, 1, 0))
        nxt = jnp.where(row == T - 1, 0.0, pltpu.roll(x, T - 1, 0))
        return x + mu[0:1, :] * (prev - x) + mu[1:2, :] * (nxt - x)

    r = shifted(r_ref, mur_ref)
    k = shifted(k_ref, muk_ref)
    v = shifted(v_ref, muv_ref)
    lo = shifted(lo_ref, mul_ref)
    r_s[...] = r
    v_s[...] = v
    kk = k * kk_ref[...]
    kk = kk * lax.rsqrt(_group_sum(kk * kk, e) + 1e-12)
    kn_s[...] = kk
    bonus_dot = _group_sum(r * k * rk_ref[...], e)

    tanh_lo = jnp.tanh(lo).astype(BF16)
    lo_b = lo.astype(BF16)

    def padded(w, before):
        parts = [jnp.zeros((before, gw), F32)] if before else []
        parts.append(w)
        after = LW - before - LORA_R
        if after:
            parts.append(jnp.zeros((after, gw), F32))
        return jnp.concatenate(parts, axis=0).astype(BF16)

    for d in range(2):
        w2p = padded(w2_ref[d], LORA_R * d)
        a2p = padded(a2_ref[d], LORA_R * (2 + d))
        z = w0_ref[d:d + 1, :] + _nn(tanh_lo, w2p)
        lw_s[d] = -EXP_NEG_HALF * jax.nn.sigmoid(z)
        a = jax.nn.sigmoid(a0_ref[d:d + 1, :] + _nn(lo_b, a2p))
        kd_s[d] = k * (1.0 + (a - 1.0) * ka_ref[...])
        b_s[d] = kk * a

    if has_s0:
        ident = jnp.where(cst["eye"], 1.0, 0.0).astype(BF16)
        for d in range(2):
            for h in range(npair):
                z0 = s0_ref[0, d, h * hpp:(h + 1) * hpp].reshape(PW, HEAD_DIM_B)
                hi, lo2 = _split2(z0)
                zt = _tn(hi, ident) + _tn(lo2, ident)
                st_s[d, h] = jnp.where(cst["bmask"], jnp.concatenate([zt] * hpp, axis=0), 0.0)
    else:
        st_s[...] = jnp.zeros(st_s.shape, F32)

    nblk = T // BLOCK_T
    ncb = BLOCK_T // CHUNK

    def block(i, carry):
        rows = []
        for d in range(2):
            blk = i if d == 0 else nblk - 1 - i
            rows.append([])
            for c in range(ncb):
                start = blk * BLOCK_T + c * CHUNK
                if not isinstance(start, int):
                    start = pl.multiple_of(start, CHUNK)
                rows[d].append(pl.ds(start, CHUNK))
        ops = [[(lw_s[d, rw, :], r_s[rw, :], kd_s[d, rw, :], v_s[rw, :], kn_s[rw, :], b_s[d, rw, :])
                for rw in rows[d]] for d in range(2)]
        states = [[st_s[d, h] for h in range(npair)] for d in range(2)]
        ys, s_new = _block_step(ops, states, cst, nblk == 1)
        for d in range(2):
            for c in range(ncb):
                y_s[d, rows[d][c], :] = ys[d][c]
            for h in range(npair):
                st_s[d, h] = s_new[d][h]
        return carry

    if nblk == 1:
        block(0, 0)
    else:
        lax.fori_loop(0, nblk, block, 0)

    y = y_s[0] + y_s[1]
    mu = _group_sum(y, e) * (1.0 / HEAD_DIM_B)
    dlt = y - mu
    var = _group_sum(dlt * dlt, e) * (1.0 / HEAD_DIM_B)
    yn = dlt * lax.rsqrt(var + LNX_EPS) * lng_ref[...] + lnb_ref[...]
    gate = [g_ref[...] for g_ref in gb_refs]
    gate = gate[0] if len(gate) == 1 else jnp.concatenate(gate, axis=1)
    out = (yn + bonus_dot * v_s[...]) * _silu(gate)
    ob_ref[...] = out.astype(BF16)

    if want_state:
        ri = lax.broadcasted_iota(jnp.int32, (PW, HEAD_DIM_B), 0)
        ci = lax.broadcasted_iota(jnp.int32, (PW, HEAD_DIM_B), 1)
        fold = jnp.where((ri & (HEAD_DIM_B - 1)) == ci, 1.0, 0.0).astype(BF16)
        for d in range(2):
            for h in range(npair):
                hi, lo2 = _split2(st_s[d, h])
                sf = _tn(hi, fold) + _tn(lo2, fold)
                sf_ref[0, d, h * hpp:(h + 1) * hpp] = sf.reshape(hpp, HEAD_DIM_B, HEAD_DIM_B)


def _rwkv(u, seq, gw, s0, want_state, shift_mu, decay_w0, decay_w2, iclr_a0, iclr_a2,
          k_k, k_a, r_k, lnx_g, lnx_b):
    m = u.shape[0]
    nbat = m // seq
    ng = D_MODEL // gw
    hg = gw // HEAD_DIM_B
    c0 = 4 * D_MODEL // gw
    l0 = (4 * D_MODEL + 3 * D_MODEL) // LW
    ngate = gw // LW
    ublk = lambda off: pl.BlockSpec((seq, gw), lambda b, g: (b, off + g))
    lblk = pl.BlockSpec((seq, LW), lambda b, g: (b, l0))
    gblk = lambda q: pl.BlockSpec((seq, LW), lambda b, g: (b, l0 + 1 + g * ngate + q))
    mublk = lambda off: pl.BlockSpec((2, gw), lambda b, g: (0, off + g))
    mulblk = pl.BlockSpec((2, LW), lambda b, g: (0, 3 * D_MODEL // LW))
    vec2 = pl.BlockSpec((2, gw), lambda b, g: (0, g))
    mat2 = pl.BlockSpec((2, LORA_R, gw), lambda b, g: (0, 0, g))
    vec1 = pl.BlockSpec((1, gw), lambda b, g: (0, g))
    sblk = pl.BlockSpec((1, 2, hg, HEAD_DIM_B, HEAD_DIM_B), lambda b, g: (b, 0, g, 0, 0))
    in_specs = [ublk(c0), ublk(c0 + ng), ublk(c0 + 2 * ng), lblk] + [gblk(q) for q in range(ngate)] + [
                mublk(0), mublk(ng), mublk(2 * ng), mulblk,
                vec2, mat2, vec2, mat2, vec1, vec1, vec1, vec1, vec1]
    args = [u, u, u, u] + [u] * ngate + [shift_mu, shift_mu, shift_mu, shift_mu,
            decay_w0, decay_w2, iclr_a0, iclr_a2,
            k_k.reshape(1, D_MODEL), k_a.reshape(1, D_MODEL), r_k.reshape(1, D_MODEL),
            lnx_g.reshape(1, D_MODEL), lnx_b.reshape(1, D_MODEL)]
    if s0 is not None:
        in_specs.append(sblk)
        args.append(s0)
    out_specs = [pl.BlockSpec((seq, gw), lambda b, g: (b, g))]
    out_shape = [jax.ShapeDtypeStruct((m, D_MODEL), BF16)]
    if want_state:
        out_specs.append(sblk)
        out_shape.append(jax.ShapeDtypeStruct((nbat, 2, N_HEADS_B, HEAD_DIM_B, HEAD_DIM_B), F32))
    tw = pltpu.VMEM((seq, gw), F32)
    tw2 = pltpu.VMEM((2, seq, gw), F32)
    res = pl.pallas_call(
        functools.partial(_rwkv_kernel, seq=seq, gw=gw, has_s0=s0 is not None, want_state=want_state),
        grid=(nbat, ng),
        in_specs=in_specs,
        out_specs=out_specs,
        out_shape=out_shape,
        scratch_shapes=[tw, tw, tw, tw2, tw2, tw2, tw2, pltpu.VMEM((2, gw // PW, PW, PW), F32)],
        compiler_params=pltpu.CompilerParams(
            dimension_semantics=("arbitrary", "arbitrary"), vmem_limit_bytes=VMEM_LIMIT),
        name="rwkv_state" if want_state else "rwkv",
    )(*args)
    return res


def _outproj_kernel(oa_ref, ob_ref, x_ref, mod_ref, w_ref, fg_ref, y_ref):
    acc = _nn(oa_ref[...], w_ref[0:D_MODEL, :]) + _nn(ob_ref[...], w_ref[D_MODEL:, :])
    gate = mod_ref[0][:, 2 * D_MODEL:]
    xo = x_ref[...] + gate * acc
    y_ref[...] = xo * lax.rsqrt(jnp.mean(xo * xo, axis=-1, keepdims=True) + NORM_EPS) * fg_ref[...]


def _outproj(o_a, o_b, x, mod3, w_out_bf, final_g, mod_row):
    m = x.shape[0]
    tm = 512
    blk = pl.BlockSpec((tm, D_MODEL), lambda i: (i, 0))
    return pl.pallas_call(
        _outproj_kernel,
        grid=(m // tm,),
        in_specs=[blk, blk, blk,
                  pl.BlockSpec((1, 1, 3 * D_MODEL), lambda i: (mod_row(i * tm), 0, 0)),
                  pl.BlockSpec((2 * D_MODEL, D_MODEL), lambda i: (0, 0)),
                  pl.BlockSpec((1, D_MODEL), lambda i: (0, 0))],
        out_specs=blk,
        out_shape=jax.ShapeDtypeStruct((m, D_MODEL), F32),
        compiler_params=pltpu.CompilerParams(vmem_limit_bytes=VMEM_LIMIT),
        name="outproj",
    )(o_a, o_b, x, mod3, w_out_bf, final_g)


def kernel(x_prompt, x_sample, cache_k, cache_v, state_rwkv, c, c_ctx, norm_g, w_ada, b_ada, w_in, lam_q1, lam_k1, lam_q2, lam_k2, subln_g, shift_mu, decay_w0, decay_w2, iclr_a0, iclr_a2, k_k, k_a, r_k, lnx_g, lnx_b, w_out, final_g):
    batch, seq, d = x_prompt.shape
    dbatch, dseq, _ = x_sample.shape
    past = cache_k.shape[2]
    assert d == D_MODEL and w_in.shape == (1, D_MODEL, D_IN) and dseq % GRID_W == 0

    cvec = jnp.concatenate([c_ctx[None, :], c, jnp.zeros((8 - 1 - dbatch, d), F32)], axis=0)
    mod3 = _mod(cvec, w_ada[0], b_ada[0]).reshape(8, 1, 3 * d)
    w_in_bf = w_in[0].astype(BF16)
    w_out_bf = w_out[0].astype(BF16)
    ctx_row = lambda tok: 0
    lat_row = lambda tok: 1 + tok // dseq
    lp = (shift_mu[0], decay_w0[0], decay_w2[0], iclr_a0[0], iclr_a2[0], k_k[0], k_a[0], r_k[0],
          lnx_g[0], lnx_b[0])
    lam = (lam_q1, lam_k1, lam_q2, lam_k2, subln_g)
    fg = final_g.reshape(1, d)

    xp = x_prompt.reshape(batch * seq, d)
    u_c = _inproj(xp, mod3, norm_g, w_in_bf, ctx_row)
    oa_c, nk, nv = _attn_ctx(u_c, seq, *lam)
    ob_c, s_new = _rwkv(u_c, seq, RWKV_GW_CTX, None, True, *lp)
    y_prompt = _outproj(oa_c, ob_c, xp, mod3, w_out_bf, fg, ctx_row).reshape(batch, seq, d)

    xs = x_sample.reshape(dbatch * dseq, d)
    u_l = _inproj(xs, mod3, norm_g, w_in_bf, lat_row)
    ck = cache_k[:, 0].reshape(dbatch * past, d)
    cv = cache_v[:, 0].reshape(dbatch * past, d)
    oa_l = _attn_lat(u_l, ck, cv, dseq, past, *lam)
    (ob_l,) = _rwkv(u_l, dseq, RWKV_GW_LAT, state_rwkv[:, 0], False, *lp)
    y_sample = _outproj(oa_l, ob_l, xs, mod3, w_out_bf, fg, lat_row).reshape(dbatch, dseq, d)

    new_s = s_new.reshape(batch, 1, 2, N_HEADS_B, HEAD_DIM_B, HEAD_DIM_B)
    return (y_prompt, y_sample, nk, nv, new_s)
```

```python
import functools
import math

import jax
import jax.numpy as jnp
from jax import lax
from jax.experimental import pallas as pl
from jax.experimental.pallas import tpu as pltpu

F32 = jnp.float32
BF16 = jnp.bfloat16
HIGHEST = lax.Precision.HIGHEST

D_MODEL = 1024
GRID_W = 64
HEAD_DIM_A = 64
N_HEADS_A = 8
HEAD_DIM_B = 64
N_HEADS_B = 16
LORA_R = 64
D_SHIFT = 3 * D_MODEL + 4 * LORA_R
D_IN = 4 * D_MODEL + D_SHIFT + D_MODEL
ROPE_BASE = 10000.0
NORM_EPS = 1e-6
SUBLN_EPS = 1e-5
LNX_EPS = 64e-5
LAM_INIT = 0.8 - 0.6 * math.exp(-0.3 * 0)
EXP_NEG_HALF = math.exp(-0.5)

GW = 256
LW = 4 * LORA_R
PW = 128
RWKV_GW_CTX = 1024
RWKV_GW_LAT = 256
CHUNK = 64
BLOCK_T = 256
VMEM_LIMIT = 48 * 1024 * 1024
VMEM_LIMIT_BIG = 56 * 1024 * 1024
SEC_Q, SEC_K, SEC_V, SEC_GA, SEC_R, SEC_KB, SEC_VB, SEC_GB = range(8)


def _nn(a, b):
    return jnp.dot(a, b, preferred_element_type=F32)


def _nt(a, b):
    return lax.dot_general(a, b, (((1,), (1,)), ((), ())), preferred_element_type=F32)


def _tn(a, b):
    return lax.dot_general(a, b, (((0,), (0,)), ((), ())), preferred_element_type=F32)


def _split2(x):
    hi = x.astype(BF16)
    lo = (x - hi.astype(F32)).astype(BF16)
    return hi, lo


def _split3(x):
    hi = x.astype(BF16)
    r1 = x - hi.astype(F32)
    mid = r1.astype(BF16)
    lo = (r1 - mid.astype(F32)).astype(BF16)
    return hi, mid, lo


def _silu(x):
    return x * jax.nn.sigmoid(x)


def _mod_kernel(c_ref, w_ref, b_ref, o_ref):
    s = _silu(c_ref[...])
    o_ref[...] = jnp.dot(s, w_ref[...], precision=HIGHEST, preferred_element_type=F32) + b_ref[...]


def _mod(cvec, w_ada, b_ada):
    n = w_ada.shape[1]
    tn = 512
    return pl.pallas_call(
        _mod_kernel,
        grid=(n // tn,),
        in_specs=[
            pl.BlockSpec((8, D_MODEL), lambda j: (0, 0)),
            pl.BlockSpec((D_MODEL, tn), lambda j: (0, j)),
            pl.BlockSpec((1, tn), lambda j: (0, j)),
        ],
        out_specs=pl.BlockSpec((8, tn), lambda j: (0, j)),
        out_shape=jax.ShapeDtypeStruct((8, n), F32),
        name="mod",
    )(cvec, w_ada, b_ada.reshape(1, n))


def _inproj_kernel(*refs, seq, want_cache):
    x_ref, mod_ref, g_ref, w_ref, wl_ref, um_ref, ul_ref = refs[:7]
    h_ref = refs[-1]
    j = pl.program_id(1)

    @pl.when(j == 0)
    def _():
        x = x_ref[...]
        y = x * lax.rsqrt(jnp.mean(x * x, axis=-1, keepdims=True) + NORM_EPS) * g_ref[...]
        mod = mod_ref[0]
        shift = mod[:, 0:D_MODEL]
        scale = mod[:, D_MODEL:2 * D_MODEL]
        h = (y * (1.0 + scale) + shift).astype(BF16)
        h_ref[...] = h
        ul_ref[...] = _nn(h, wl_ref[...]).astype(BF16)

    acc = _nn(h_ref[...], w_ref[...])
    um_ref[...] = acc.astype(BF16)
    if want_cache:
        nk_ref, nv_ref = refs[7:9]
        nb = acc.shape[0] // seq

        @pl.when(j == SEC_K)
        def _():
            nk_ref[:, 0] = acc.reshape((nb,) + nk_ref.shape[2:])

        @pl.when(j == SEC_V)
        def _():
            nv_ref[:, 0] = acc.reshape((nb,) + nv_ref.shape[2:])


def _inproj(x, mod3, norm_g, w_main, w_lora, mod_row, seq, want_cache):
    m = x.shape[0]
    tm = 1024
    nsec = w_main.shape[1] // D_MODEL
    nb = tm // seq
    out_specs = [pl.BlockSpec((tm, D_MODEL), lambda i, j: (i, j)),
                 pl.BlockSpec((tm, LW), lambda i, j: (i, 0))]
    out_shape = [jax.ShapeDtypeStruct((m, nsec * D_MODEL), BF16), jax.ShapeDtypeStruct((m, LW), BF16)]
    if want_cache:
        out_specs += [pl.BlockSpec((nb, 1, seq, N_HEADS_A, 2, HEAD_DIM_A), lambda i, j: (i, 0, 0, 0, 0, 0)),
                      pl.BlockSpec((nb, 1, seq, N_HEADS_A, 2 * HEAD_DIM_A), lambda i, j: (i, 0, 0, 0, 0))]
        out_shape += [jax.ShapeDtypeStruct((m // seq, 1, seq, N_HEADS_A, 2, HEAD_DIM_A), F32),
                      jax.ShapeDtypeStruct((m // seq, 1, seq, N_HEADS_A, 2 * HEAD_DIM_A), F32)]
    return pl.pallas_call(
        functools.partial(_inproj_kernel, seq=seq, want_cache=want_cache),
        grid=(m // tm, nsec),
        in_specs=[
            pl.BlockSpec((tm, D_MODEL), lambda i, j: (i, 0)),
            pl.BlockSpec((1, 1, 3 * D_MODEL), lambda i, j: (mod_row(i * tm), 0, 0)),
            pl.BlockSpec((1, D_MODEL), lambda i, j: (0, 0)),
            pl.BlockSpec((D_MODEL, D_MODEL), lambda i, j: (0, j)),
            pl.BlockSpec((D_MODEL, LW), lambda i, j: (0, 0)),
        ],
        out_specs=out_specs,
        out_shape=out_shape,
        scratch_shapes=[pltpu.VMEM((tm, D_MODEL), BF16)],
        compiler_params=pltpu.CompilerParams(
            dimension_semantics=("arbitrary", "arbitrary"), vmem_limit_bytes=VMEM_LIMIT_BIG),
        name="inproj_cache" if want_cache else "inproj",
    )(x, mod3, norm_g, w_main, w_lora)


def _lam(q1_ref, k1_ref, q2_ref, k2_ref):
    s1 = jnp.sum(q1_ref[...] * k1_ref[...], axis=-1, keepdims=True)
    s2 = jnp.sum(q2_ref[...] * k2_ref[...], axis=-1, keepdims=True)
    return jnp.exp(s1) - jnp.exp(s2) + LAM_INIT


def _diff_head(q, kb, vb, g, lam, subg):
    lane = lax.broadcasted_iota(jnp.int32, q.shape, 1)
    qs = q * (HEAD_DIM_A ** -0.5)
    probs = []
    for m in range(2):
        qm = jnp.where((lane >> 6) == m, qs, 0.0).astype(BF16)
        s = _nt(qm, kb)
        s = s - jnp.max(s, axis=-1, keepdims=True)
        p = jnp.exp(s)
        probs.append(p * (1.0 / jnp.sum(p, axis=-1, keepdims=True)))
    a = probs[0] - lam * probs[1]
    o = _nn(a.astype(BF16), vb)
    o = o * lax.rsqrt(jnp.mean(o * o, axis=-1, keepdims=True) + SUBLN_EPS) * subg
    o = o * (1.0 - LAM_INIT)
    return o * _silu(g)


def _attn_ctx_kernel(q_ref, k_ref, v_ref, g_ref, q1_ref, k1_ref, q2_ref, k2_ref, sg_ref, o_ref):
    lam = _lam(q1_ref, k1_ref, q2_ref, k2_ref)
    subg = sg_ref[...]
    for h in range(q_ref.shape[1] // 128):
        sl = slice(h * 128, (h + 1) * 128)
        o = _diff_head(q_ref[:, sl].astype(F32), k_ref[:, sl], v_ref[:, sl],
                       g_ref[:, sl].astype(F32), lam, subg)
        o_ref[:, sl] = o.astype(BF16)


def _attn_ctx(um, seq, lam_q1, lam_k1, lam_q2, lam_k2, subln_g):
    m = um.shape[0]
    small = pl.BlockSpec((1, HEAD_DIM_A), lambda b: (0, 0))
    blk = lambda sec: pl.BlockSpec((seq, D_MODEL), lambda b: (b, sec))
    return pl.pallas_call(
        _attn_ctx_kernel,
        grid=(m // seq,),
        in_specs=[blk(SEC_Q), blk(SEC_K), blk(SEC_V), blk(SEC_GA), small, small, small, small,
                  pl.BlockSpec((1, 2 * HEAD_DIM_A), lambda b: (0, 0))],
        out_specs=blk(0),
        out_shape=jax.ShapeDtypeStruct((m, D_MODEL), BF16),
        compiler_params=pltpu.CompilerParams(vmem_limit_bytes=VMEM_LIMIT),
        name="attn_ctx",
    )(um, um, um, um, lam_q1, lam_k1, lam_q2, lam_k2, subln_g)


def _rope(x, cos, s1, s2):
    return x * cos + pltpu.roll(x, 112, 1) * s1 + pltpu.roll(x, 16, 1) * s2


def _attn_lat_kernel(q_ref, k_ref, v_ref, g_ref, ck_ref, cv_ref, cq_ref, s1q_ref, s2q_ref,
                     ckk_ref, s1k_ref, s2k_ref, q1_ref, k1_ref, q2_ref, k2_ref, sg_ref,
                     o_ref, kall_ref, vall_ref):
    past = cv_ref.shape[0]
    w = q_ref.shape[1]
    nh = w // 128

    @pl.when(pl.program_id(2) == 0)
    def _():
        kall_ref[0:past, :] = ck_ref[0, 0].reshape(past, w).astype(BF16)
        vall_ref[0:past, :] = cv_ref[...].astype(BF16)
        vall_ref[past:, :] = v_ref[...]
        for h in range(nh):
            sl = slice(h * 128, (h + 1) * 128)
            kall_ref[past:, sl] = _rope(k_ref[:, sl].astype(F32), ckk_ref[...], s1k_ref[...],
                                        s2k_ref[...]).astype(BF16)

    lam = _lam(q1_ref, k1_ref, q2_ref, k2_ref)
    subg = sg_ref[...]
    for h in range(nh):
        sl = slice(h * 128, (h + 1) * 128)
        q = _rope(q_ref[:, sl].astype(F32), cq_ref[...], s1q_ref[...], s2q_ref[...])
        o = _diff_head(q, kall_ref[:, sl], vall_ref[:, sl], g_ref[:, sl].astype(F32), lam, subg)
        o_ref[:, sl] = o.astype(BF16)


def _rope_tables(t):
    pos = jnp.arange(t)
    row = (pos // GRID_W).astype(F32)
    col = (pos % GRID_W).astype(F32)
    n_freq = HEAD_DIM_A // 4
    lane = jnp.arange(128)
    d = lane % HEAD_DIM_A
    use_col = (d // 32) == 1
    second = ((d % 32) // 16) == 1
    inv = ROPE_BASE ** (-(d % n_freq).astype(F32) / n_freq)
    ang = jnp.where(use_col[None, :], col[:, None], row[:, None]) * inv[None, :]
    cos = jnp.cos(ang)
    sin = jnp.sin(ang)
    s1 = jnp.where(second[None, :], 0.0, -sin)
    s2 = jnp.where(second[None, :], sin, 0.0)
    return cos, s1, s2


def _attn_lat(u, cache_k, cache_v2, seq, past, lam_q1, lam_k1, lam_q2, lam_k2, subln_g):
    m = u.shape[0]
    nbat = m // seq
    w = GW
    ng = D_MODEL // w
    tq = 256
    nq = seq // tq
    cos, s1, s2 = _rope_tables(seq)
    small = pl.BlockSpec((1, HEAD_DIM_A), lambda b, g, i: (0, 0))
    qblk = lambda sec: pl.BlockSpec((tq, w), lambda b, g, i: (b * nq + i, sec * ng + g))
    kblk = lambda sec: pl.BlockSpec((seq, w), lambda b, g, i: (b, sec * ng + g))
    ckblk = pl.BlockSpec((1, 1, past, w // 128, 2, HEAD_DIM_A), lambda b, g, i: (b, 0, 0, g, 0, 0))
    cblk = pl.BlockSpec((past, w), lambda b, g, i: (b, g))
    tq_blk = pl.BlockSpec((tq, 128), lambda b, g, i: (i, 0))
    tk_blk = pl.BlockSpec((seq, 128), lambda b, g, i: (0, 0))
    return pl.pallas_call(
        _attn_lat_kernel,
        grid=(nbat, ng, nq),
        in_specs=[qblk(SEC_Q), kblk(SEC_K), kblk(SEC_V), qblk(SEC_GA), ckblk, cblk,
                  tq_blk, tq_blk, tq_blk, tk_blk, tk_blk, tk_blk,
                  small, small, small, small,
                  pl.BlockSpec((1, 2 * HEAD_DIM_A), lambda b, g, i: (0, 0))],
        out_specs=pl.BlockSpec((tq, w), lambda b, g, i: (b * nq + i, g)),
        out_shape=jax.ShapeDtypeStruct((m, D_MODEL), BF16),
        scratch_shapes=[pltpu.VMEM((past + seq, w), BF16), pltpu.VMEM((past + seq, w), BF16)],
        compiler_params=pltpu.CompilerParams(
            dimension_semantics=("arbitrary", "arbitrary", "arbitrary"), vmem_limit_bytes=VMEM_LIMIT),
        name="attn_lat",
    )(u, u, u, u, cache_k, cache_v2, cos, s1, s2, cos, s1, s2,
      lam_q1, lam_k1, lam_q2, lam_k2, subln_g)


def _rwkv_consts():
    ri = lax.broadcasted_iota(jnp.int32, (PW, PW), 0)
    ci = lax.broadcasted_iota(jnp.int32, (PW, PW), 1)
    t = lax.broadcasted_iota(jnp.int32, (CHUNK, PW), 0)
    lane = lax.broadcasted_iota(jnp.int32, (CHUNK, PW), 1)
    j = lane & (CHUNK - 1)
    ti = lax.broadcasted_iota(jnp.int32, (CHUNK, CHUNK), 0)
    tj = lax.broadcasted_iota(jnp.int32, (CHUNK, CHUNK), 1)
    return dict(
        bmask=(ri >> 6) == (ci >> 6),
        eye=ri == ci,
        head0=lane < HEAD_DIM_B,
        eye_sbs=jnp.where(t == j, 1.0, 0.0),
        strict=(j < t, j > t),
        incl=(j <= t, j >= t),
        off=[((t >> (n + 1)) == (j >> (n + 1))) & ((t >> n) != (j >> n)) for n in range(6)],
        tri=(jnp.where(tj <= ti, 1.0, 0.0).astype(BF16), jnp.where(tj >= ti, 1.0, 0.0).astype(BF16)),
    )


def _wbd(x, cst):
    xb = x.astype(BF16)
    z = jnp.zeros_like(xb)
    return jnp.concatenate([jnp.where(cst["head0"], xb, z), jnp.where(cst["head0"], z, xb)], axis=0)


def _group_sum(x, e):
    hi, lo = _split2(x)
    ew = e.shape[0]
    parts = [_nn(hi[:, o:o + ew], e) + _nn(lo[:, o:o + ew], e) for o in range(0, x.shape[1], ew)]
    return parts[0] if len(parts) == 1 else jnp.concatenate(parts, axis=1)


def _block_step(ops, states, cst, same_rows):
    L = CHUNK
    ncb = len(ops[0])
    npair = ops[0][0][0].shape[1] // PW
    idx = [(d, c, h) for d in range(2) for c in range(ncb) for h in range(npair)]
    half = lambda a, h: a[:, h * PW:(h + 1) * PW]

    lg = {}
    for d in range(2):
        for c in range(ncb):
            hi, mid, lo = _split3(ops[d][c][0])
            tri = cst["tri"][d]
            lg[d, c] = _nn(tri, hi) + _nn(tri, mid) + _nn(tri, lo)

    ar, wbt, wkt, bh, kh, dg = {}, {}, {}, {}, {}, {}
    for d in range(2):
        for c in range(ncb):
            lw, r, kd, v, kk, b = ops[d][c]
            g = lg[d, c]
            g_l = g[0:1, :] if d == 1 else g[L - 1:L, :]
            at = -(kk * jnp.exp(g - lw))
            rt = r * jnp.exp(g)
            e_inv = jnp.exp(-g)
            e_l = jnp.exp(g_l - g)
            atrt = jnp.concatenate([at, rt], axis=0).astype(BF16)
            bt, kt = b * e_inv, kd * e_inv
            bhh, khh = (b * e_l).astype(BF16), (kd * e_l).astype(BF16)
            eg = jnp.exp(g_l)
            for h in range(npair):
                i = (d, c, h)
                ar[i] = half(atrt, h)
                wbt[i] = _wbd(half(bt, h), cst)
                wkt[i] = _wbd(half(kt, h), cst)
                bh[i] = half(bhh, h)
                kh[i] = half(khh, h)
                dg[i] = jnp.where(cst["eye"], half(eg, h), 0.0).astype(BF16)

    gb = {i: _nt(ar[i], wbt[i]) for i in idx}
    gk = {i: _nt(ar[i], wkt[i]) for i in idx}
    m_ab = {i: jnp.where(cst["strict"][i[0]], gb[i][:L], 0.0) for i in idx}
    m_rb = {i: jnp.where(cst["incl"][i[0]], gb[i][L:], 0.0).astype(BF16) for i in idx}
    m_akrk = {i: jnp.concatenate([jnp.where(cst["strict"][i[0]], gk[i][:L], 0.0),
                                  jnp.where(cst["incl"][i[0]], gk[i][L:], 0.0)], axis=0).astype(BF16)
              for i in idx}

    x = {i: cst["eye_sbs"] + jnp.where(cst["off"][0], m_ab[i], 0.0) for i in idx}
    for off in cst["off"][1:]:
        p = {i: _nn(jnp.where(off, m_ab[i], 0.0).astype(BF16), _wbd(x[i], cst)) for i in idx}
        x = {i: x[i] + _nn(x[i].astype(BF16), _wbd(p[i], cst)) for i in idx}
    xb = {i: x[i].astype(BF16) for i in idx}

    vsrc = [(0, c, h) if same_rows else (d, c, h) for d, c, h in idx]
    vhalf = {i: half(ops[i[0]][i[1]][3], i[2]) for i in dict.fromkeys(vsrc)}
    wv = {i: _wbd(vhalf[i], cst) for i in vhalf}
    vb = {i: vhalf[i].astype(BF16) for i in vhalf}
    mv = {i: _nn(m_akrk[i], wv[j]) for i, j in zip(idx, vsrc)}
    kv = {i: jnp.where(cst["bmask"], _tn(kh[i], vb[j]), 0.0) for i, j in zip(idx, vsrc)}

    s = [list(sd) for sd in states]
    ys = {}
    for step in range(ncb):
        cur = [(d, c, h) for d, c in ((0, step), (1, ncb - 1 - step)) for h in range(npair)]
        ars = {i: _nn(jnp.concatenate([ar[i], dg[i]], axis=0), s[i[0]][i[2]].astype(BF16)) for i in cur}
        u = {i: _nn(xb[i], _wbd(ars[i][:L] + mv[i][:L], cst)) for i in cur}
        for i in cur:
            d, c, h = i
            ys[i] = ars[i][L:2 * L] + _nn(m_rb[i], _wbd(u[i], cst)) + mv[i][L:]
            upd = _tn(bh[i], u[i].astype(BF16))
            s[d][h] = ars[i][2 * L:] + kv[i] + jnp.where(cst["bmask"], upd, 0.0)
    ys = [[jnp.concatenate([ys[d, c, h] for h in range(npair)], axis=1) for c in range(ncb)]
          for d in range(2)]
    return ys, s


def _rwkv_kernel(*refs, seq, gw, has_s0, want_state):
    it = iter(refs)
    r_ref, k_ref, v_ref, lo_ref, gb_ref = (next(it) for _ in range(5))
    mur_ref, muk_ref, muv_ref, mul_ref = (next(it) for _ in range(4))
    w0_ref, w2_ref, a0_ref, a2_ref = (next(it) for _ in range(4))
    kk_ref, ka_ref, rk_ref, lng_ref, lnb_ref = (next(it) for _ in range(5))
    s0_ref = next(it) if has_s0 else None
    ob_ref = next(it)
    sf_ref = next(it) if want_state else None
    r_s, v_s, kn_s, lw_s, kd_s, b_s, y_s, st_s = (next(it) for _ in range(8))

    T = seq
    cst = _rwkv_consts()
    npair = gw // PW
    hpp = PW // HEAD_DIM_B
    gi = lax.broadcasted_iota(jnp.int32, (LW, LW), 0)
    gj = lax.broadcasted_iota(jnp.int32, (LW, LW), 1)
    e = jnp.where((gi >> 6) == (gj >> 6), 1.0, 0.0).astype(BF16)
    rows_of = {w: lax.broadcasted_iota(jnp.int32, (T, w), 0) for w in {gw, LW}}

    def shifted(ref, mu_ref):
        x = ref[...].astype(F32)
        mu = mu_ref[...]
        row = rows_of[x.shape[1]]
        prev = jnp.where(row == 0, 0.0, pltpu.roll(x, 1, 0))
        nxt = jnp.where(row == T - 1, 0.0, pltpu.roll(x, T - 1, 0))
        return x + mu[0:1, :] * (prev - x) + mu[1:2, :] * (nxt - x)

    r = shifted(r_ref, mur_ref)
    k = shifted(k_ref, muk_ref)
    v = shifted(v_ref, muv_ref)
    lo = shifted(lo_ref, mul_ref)
    r_s[...] = r
    v_s[...] = v
    kk = k * kk_ref[...]
    kk = kk * lax.rsqrt(_group_sum(kk * kk, e) + 1e-12)
    kn_s[...] = kk
    bonus_dot = _group_sum(r * k * rk_ref[...], e)

    tanh_lo = jnp.tanh(lo).astype(BF16)
    lo_b = lo.astype(BF16)

    def padded(w, before):
        parts = [jnp.zeros((before, gw), F32)] if before else []
        parts.append(w)
        after = LW - before - LORA_R
        if after:
            parts.append(jnp.zeros((after, gw), F32))
        return jnp.concatenate(parts, axis=0).astype(BF16)

    for d in range(2):
        w2p = padded(w2_ref[d], LORA_R * d)
        a2p = padded(a2_ref[d], LORA_R * (2 + d))
        z = w0_ref[d:d + 1, :] + _nn(tanh_lo, w2p)
        lw_s[d] = -EXP_NEG_HALF * jax.nn.sigmoid(z)
        a = jax.nn.sigmoid(a0_ref[d:d + 1, :] + _nn(lo_b, a2p))
        kd_s[d] = k * (1.0 + (a - 1.0) * ka_ref[...])
        b_s[d] = kk * a

    if has_s0:
        ident = jnp.where(cst["eye"], 1.0, 0.0).astype(BF16)
        for d in range(2):
            for h in range(npair):
                z0 = s0_ref[0, d, h * hpp:(h + 1) * hpp].reshape(PW, HEAD_DIM_B)
                hi, lo2 = _split2(z0)
                zt = _tn(hi, ident) + _tn(lo2, ident)
                st_s[d, h] = jnp.where(cst["bmask"], jnp.concatenate([zt] * hpp, axis=0), 0.0)
    else:
        st_s[...] = jnp.zeros(st_s.shape, F32)

    nblk = T // BLOCK_T
    ncb = BLOCK_T // CHUNK

    def block(i, carry):
        rows = []
        for d in range(2):
            blk = i if d == 0 else nblk - 1 - i
            rows.append([])
            for c in range(ncb):
                start = blk * BLOCK_T + c * CHUNK
                if not isinstance(start, int):
                    start = pl.multiple_of(start, CHUNK)
                rows[d].append(pl.ds(start, CHUNK))
        ops = [[(lw_s[d, rw, :], r_s[rw, :], kd_s[d, rw, :], v_s[rw, :], kn_s[rw, :], b_s[d, rw, :])
                for rw in rows[d]] for d in range(2)]
        states = [[st_s[d, h] for h in range(npair)] for d in range(2)]
        ys, s_new = _block_step(ops, states, cst, nblk == 1)
        for d in range(2):
            for c in range(ncb):
                y_s[d, rows[d][c], :] = ys[d][c]
            for h in range(npair):
                st_s[d, h] = s_new[d][h]
        return carry

    if nblk == 1:
        block(0, 0)
    else:
        lax.fori_loop(0, nblk, block, 0)

    y = y_s[0] + y_s[1]
    mu = _group_sum(y, e) * (1.0 / HEAD_DIM_B)
    dlt = y - mu
    var = _group_sum(dlt * dlt, e) * (1.0 / HEAD_DIM_B)
    yn = dlt * lax.rsqrt(var + LNX_EPS) * lng_ref[...] + lnb_ref[...]
    out = (yn + bonus_dot * v_s[...]) * _silu(gb_ref[...].astype(F32))
    ob_ref[...] = out.astype(BF16)

    if want_state:
        ri = lax.broadcasted_iota(jnp.int32, (PW, HEAD_DIM_B), 0)
        ci = lax.broadcasted_iota(jnp.int32, (PW, HEAD_DIM_B), 1)
        fold = jnp.where((ri & (HEAD_DIM_B - 1)) == ci, 1.0, 0.0).astype(BF16)
        for d in range(2):
            for h in range(npair):
                hi, lo2 = _split2(st_s[d, h])
                sf = _tn(hi, fold) + _tn(lo2, fold)
                sf_ref[0, d, h * hpp:(h + 1) * hpp] = sf.reshape(hpp, HEAD_DIM_B, HEAD_DIM_B)


def _rwkv(u, ul, seq, gw, s0, want_state, shift_mu, decay_w0, decay_w2, iclr_a0, iclr_a2,
          k_k, k_a, r_k, lnx_g, lnx_b):
    m = u.shape[0]
    nbat = m // seq
    ng = D_MODEL // gw
    hg = gw // HEAD_DIM_B
    ublk = lambda sec: pl.BlockSpec((seq, gw), lambda b, g: (b, sec * ng + g))
    lblk = pl.BlockSpec((seq, LW), lambda b, g: (b, 0))
    mublk = lambda sec: pl.BlockSpec((2, gw), lambda b, g: (0, sec * ng + g))
    mulblk = pl.BlockSpec((2, LW), lambda b, g: (0, 3 * D_MODEL // LW))
    vec2 = pl.BlockSpec((2, gw), lambda b, g: (0, g))
    mat2 = pl.BlockSpec((2, LORA_R, gw), lambda b, g: (0, 0, g))
    vec1 = pl.BlockSpec((1, gw), lambda b, g: (0, g))
    sblk = pl.BlockSpec((1, 2, hg, HEAD_DIM_B, HEAD_DIM_B), lambda b, g: (b, 0, g, 0, 0))
    in_specs = [ublk(SEC_R), ublk(SEC_KB), ublk(SEC_VB), lblk, ublk(SEC_GB),
                mublk(0), mublk(1), mublk(2), mulblk,
                vec2, mat2, vec2, mat2, vec1, vec1, vec1, vec1, vec1]
    args = [u, u, u, ul, u, shift_mu, shift_mu, shift_mu, shift_mu,
            decay_w0, decay_w2, iclr_a0, iclr_a2,
            k_k.reshape(1, D_MODEL), k_a.reshape(1, D_MODEL), r_k.reshape(1, D_MODEL),
            lnx_g.reshape(1, D_MODEL), lnx_b.reshape(1, D_MODEL)]
    if s0 is not None:
        in_specs.append(sblk)
        args.append(s0)
    out_specs = [pl.BlockSpec((seq, gw), lambda b, g: (b, g))]
    out_shape = [jax.ShapeDtypeStruct((m, D_MODEL), BF16)]
    if want_state:
        out_specs.append(sblk)
        out_shape.append(jax.ShapeDtypeStruct((nbat, 2, N_HEADS_B, HEAD_DIM_B, HEAD_DIM_B), F32))
    tw = pltpu.VMEM((seq, gw), F32)
    tw2 = pltpu.VMEM((2, seq, gw), F32)
    res = pl.pallas_call(
        functools.partial(_rwkv_kernel, seq=seq, gw=gw, has_s0=s0 is not None, want_state=want_state),
        grid=(nbat, ng),
        in_specs=in_specs,
        out_specs=out_specs,
        out_shape=out_shape,
        scratch_shapes=[tw, tw, tw, tw2, tw2, tw2, tw2, pltpu.VMEM((2, gw // PW, PW, PW), F32)],
        compiler_params=pltpu.CompilerParams(
            dimension_semantics=("arbitrary", "arbitrary"), vmem_limit_bytes=VMEM_LIMIT),
        name="rwkv_state" if want_state else "rwkv",
    )(*args)
    return res


def _outproj_kernel(oa_ref, ob_ref, x_ref, mod_ref, w_ref, fg_ref, y_ref):
    acc = _nn(oa_ref[...], w_ref[0:D_MODEL, :]) + _nn(ob_ref[...], w_ref[D_MODEL:, :])
    gate = mod_ref[0][:, 2 * D_MODEL:]
    xo = x_ref[...] + gate * acc
    y_ref[...] = xo * lax.rsqrt(jnp.mean(xo * xo, axis=-1, keepdims=True) + NORM_EPS) * fg_ref[...]


def _outproj(o_a, o_b, x, mod3, w_out_bf, final_g, mod_row):
    m = x.shape[0]
    tm = 512
    blk = pl.BlockSpec((tm, D_MODEL), lambda i: (i, 0))
    return pl.pallas_call(
        _outproj_kernel,
        grid=(m // tm,),
        in_specs=[blk, blk, blk,
                  pl.BlockSpec((1, 1, 3 * D_MODEL), lambda i: (mod_row(i * tm), 0, 0)),
                  pl.BlockSpec((2 * D_MODEL, D_MODEL), lambda i: (0, 0)),
                  pl.BlockSpec((1, D_MODEL), lambda i: (0, 0))],
        out_specs=blk,
        out_shape=jax.ShapeDtypeStruct((m, D_MODEL), F32),
        compiler_params=pltpu.CompilerParams(vmem_limit_bytes=VMEM_LIMIT),
        name="outproj",
    )(o_a, o_b, x, mod3, w_out_bf, final_g)


def kernel(x_prompt, x_sample, cache_k, cache_v, state_rwkv, c, c_ctx, norm_g, w_ada, b_ada, w_in, lam_q1, lam_k1, lam_q2, lam_k2, subln_g, shift_mu, decay_w0, decay_w2, iclr_a0, iclr_a2, k_k, k_a, r_k, lnx_g, lnx_b, w_out, final_g):
    batch, seq, d = x_prompt.shape
    dbatch, dseq, _ = x_sample.shape
    past = cache_k.shape[2]
    assert d == D_MODEL and w_in.shape == (1, D_MODEL, D_IN) and dseq % GRID_W == 0

    cvec = jnp.concatenate([c_ctx[None, :], c, jnp.zeros((8 - 1 - dbatch, d), F32)], axis=0)
    mod3 = _mod(cvec, w_ada[0], b_ada[0]).reshape(8, 1, 3 * d)
    lo0 = 4 * D_MODEL + 3 * D_MODEL
    w_main = jnp.concatenate([w_in[0, :, :lo0], w_in[0, :, lo0 + LW:]], axis=1).astype(BF16)
    w_lora = w_in[0, :, lo0:lo0 + LW].astype(BF16)
    w_out_bf = w_out[0].astype(BF16)
    ctx_row = lambda tok: 0
    lat_row = lambda tok: 1 + tok // dseq
    lp = (shift_mu[0], decay_w0[0], decay_w2[0], iclr_a0[0], iclr_a2[0], k_k[0], k_a[0], r_k[0],
          lnx_g[0], lnx_b[0])
    lam = (lam_q1, lam_k1, lam_q2, lam_k2, subln_g)
    fg = final_g.reshape(1, d)

    xp = x_prompt.reshape(batch * seq, d)
    u_c, ul_c, nk, nv = _inproj(xp, mod3, norm_g, w_main, w_lora, ctx_row, seq, True)
    oa_c = _attn_ctx(u_c, seq, *lam)
    ob_c, s_new = _rwkv(u_c, ul_c, seq, RWKV_GW_CTX, None, True, *lp)
    y_prompt = _outproj(oa_c, ob_c, xp, mod3, w_out_bf, fg, ctx_row).reshape(batch, seq, d)

    xs = x_sample.reshape(dbatch * dseq, d)
    u_l, ul_l = _inproj(xs, mod3, norm_g, w_main, w_lora, lat_row, dseq, False)
    cv = cache_v[:, 0].reshape(dbatch * past, d)
    oa_l = _attn_lat(u_l, cache_k, cv, dseq, past, *lam)
    (ob_l,) = _rwkv(u_l, ul_l, dseq, RWKV_GW_LAT, state_rwkv[:, 0], False, *lp)
    y_sample = _outproj(oa_l, ob_l, xs, mod3, w_out_bf, fg, lat_row).reshape(dbatch, dseq, d)

    new_s = s_new.reshape(batch, 1, 2, N_HEADS_B, HEAD_DIM_B, HEAD_DIM_B)
    return (y_prompt, y_sample, nk, nv, new_s)
```

```python
import functools
import math

import jax
import jax.numpy as jnp
from jax import lax
from jax.experimental import pallas as pl
from jax.experimental.pallas import tpu as pltpu

F32 = jnp.float32
BF16 = jnp.bfloat16
HIGHEST = lax.Precision.HIGHEST

D_MODEL = 1024
GRID_W = 64
HEAD_DIM_A = 64
N_HEADS_A = 8
HEAD_DIM_B = 64
N_HEADS_B = 16
LORA_R = 64
D_SHIFT = 3 * D_MODEL + 4 * LORA_R
D_IN = 4 * D_MODEL + D_SHIFT + D_MODEL
ROPE_BASE = 10000.0
NORM_EPS = 1e-6
SUBLN_EPS = 1e-5
LNX_EPS = 64e-5
LAM_INIT = 0.8 - 0.6 * math.exp(-0.3 * 0)
EXP_NEG_HALF = math.exp(-0.5)

GW = 256
LW = 4 * LORA_R
PW = 128
RWKV_GW_CTX = 1024
RWKV_GW_LAT = 256
CHUNK = 64
BLOCK_T = 256
VMEM_LIMIT = 48 * 1024 * 1024
VMEM_LIMIT_BIG = 56 * 1024 * 1024
SEC_Q, SEC_K, SEC_V, SEC_GA, SEC_R, SEC_KB, SEC_VB, SEC_GB = range(8)


def _nn(a, b):
    return jnp.dot(a, b, preferred_element_type=F32)


def _nt(a, b):
    return lax.dot_general(a, b, (((1,), (1,)), ((), ())), preferred_element_type=F32)


def _tn(a, b):
    return lax.dot_general(a, b, (((0,), (0,)), ((), ())), preferred_element_type=F32)


def _split2(x):
    hi = x.astype(BF16)
    lo = (x - hi.astype(F32)).astype(BF16)
    return hi, lo


def _split3(x):
    hi = x.astype(BF16)
    r1 = x - hi.astype(F32)
    mid = r1.astype(BF16)
    lo = (r1 - mid.astype(F32)).astype(BF16)
    return hi, mid, lo


def _silu(x):
    return x * jax.nn.sigmoid(x)


def _mod_kernel(c_ref, w_ref, b_ref, o_ref):
    s = _silu(c_ref[...])
    o_ref[...] = jnp.dot(s, w_ref[...], precision=HIGHEST, preferred_element_type=F32) + b_ref[...]


def _mod(cvec, w_ada, b_ada):
    n = w_ada.shape[1]
    tn = 512
    return pl.pallas_call(
        _mod_kernel,
        grid=(n // tn,),
        in_specs=[
            pl.BlockSpec((8, D_MODEL), lambda j: (0, 0)),
            pl.BlockSpec((D_MODEL, tn), lambda j: (0, j)),
            pl.BlockSpec((1, tn), lambda j: (0, j)),
        ],
        out_specs=pl.BlockSpec((8, tn), lambda j: (0, j)),
        out_shape=jax.ShapeDtypeStruct((8, n), F32),
        name="mod",
    )(cvec, w_ada, b_ada.reshape(1, n))


def _inproj_kernel(*refs, seq, want_cache):
    x_ref, mod_ref, g_ref, w_ref, wgb_ref, wl_ref, um_ref, ul_ref = refs[:8]
    h_ref = refs[-1]
    j = pl.program_id(1)

    @pl.when(j == 0)
    def _():
        x = x_ref[...]
        y = x * lax.rsqrt(jnp.mean(x * x, axis=-1, keepdims=True) + NORM_EPS) * g_ref[...]
        mod = mod_ref[0]
        shift = mod[:, 0:D_MODEL]
        scale = mod[:, D_MODEL:2 * D_MODEL]
        h = (y * (1.0 + scale) + shift).astype(BF16)
        h_ref[...] = h
        ul_ref[...] = _nn(h, wl_ref[...]).astype(BF16)

    w = jnp.where(j == SEC_GB, wgb_ref[...], w_ref[...])
    acc = _nn(h_ref[...], w)
    um_ref[...] = acc.astype(BF16)
    if want_cache:
        nk_ref, nv_ref = refs[8:10]
        nb = acc.shape[0] // seq

        @pl.when(j == SEC_K)
        def _():
            nk_ref[:, 0] = acc.reshape((nb,) + nk_ref.shape[2:])

        @pl.when(j == SEC_V)
        def _():
            nv_ref[:, 0] = acc.reshape((nb,) + nv_ref.shape[2:])


def _inproj(x, mod3, norm_g, w_bf, w_gb, w_lora, mod_row, seq, want_cache):
    m = x.shape[0]
    tm = 1024
    nsec = SEC_GB + 1
    nb = tm // seq
    out_specs = [pl.BlockSpec((tm, D_MODEL), lambda i, j: (i, j)),
                 pl.BlockSpec((tm, LW), lambda i, j: (i, 0))]
    out_shape = [jax.ShapeDtypeStruct((m, nsec * D_MODEL), BF16), jax.ShapeDtypeStruct((m, LW), BF16)]
    if want_cache:
        out_specs += [pl.BlockSpec((nb, 1, seq, N_HEADS_A, 2, HEAD_DIM_A), lambda i, j: (i, 0, 0, 0, 0, 0)),
                      pl.BlockSpec((nb, 1, seq, N_HEADS_A, 2 * HEAD_DIM_A), lambda i, j: (i, 0, 0, 0, 0))]
        out_shape += [jax.ShapeDtypeStruct((m // seq, 1, seq, N_HEADS_A, 2, HEAD_DIM_A), F32),
                      jax.ShapeDtypeStruct((m // seq, 1, seq, N_HEADS_A, 2 * HEAD_DIM_A), F32)]
    return pl.pallas_call(
        functools.partial(_inproj_kernel, seq=seq, want_cache=want_cache),
        grid=(m // tm, nsec),
        in_specs=[
            pl.BlockSpec((tm, D_MODEL), lambda i, j: (i, 0)),
            pl.BlockSpec((1, 1, 3 * D_MODEL), lambda i, j: (mod_row(i * tm), 0, 0)),
            pl.BlockSpec((1, D_MODEL), lambda i, j: (0, 0)),
            pl.BlockSpec((D_MODEL, D_MODEL), lambda i, j: (0, jnp.minimum(j, SEC_GB - 1))),
            pl.BlockSpec((D_MODEL, D_MODEL), lambda i, j: (0, 0)),
            pl.BlockSpec((D_MODEL, LW), lambda i, j: (0, 0)),
        ],
        out_specs=out_specs,
        out_shape=out_shape,
        scratch_shapes=[pltpu.VMEM((tm, D_MODEL), BF16)],
        compiler_params=pltpu.CompilerParams(
            dimension_semantics=("arbitrary", "arbitrary"), vmem_limit_bytes=VMEM_LIMIT_BIG),
        name="inproj_cache" if want_cache else "inproj",
    )(x, mod3, norm_g, w_bf, w_gb, w_lora)


def _lam(q1_ref, k1_ref, q2_ref, k2_ref):
    s1 = jnp.sum(q1_ref[...] * k1_ref[...], axis=-1, keepdims=True)
    s2 = jnp.sum(q2_ref[...] * k2_ref[...], axis=-1, keepdims=True)
    return jnp.exp(s1) - jnp.exp(s2) + LAM_INIT


def _diff_head(q, kb, vb, g, lam, subg):
    lane = lax.broadcasted_iota(jnp.int32, q.shape, 1)
    qs = q * (HEAD_DIM_A ** -0.5)
    outs = []
    for m in range(2):
        qm = jnp.where((lane >> 6) == m, qs, 0.0).astype(BF16)
        s = _nt(qm, kb)
        p = jnp.exp(s - jnp.max(s, axis=-1, keepdims=True))
        outs.append(_nn(p.astype(BF16), vb) * (1.0 / jnp.sum(p, axis=-1, keepdims=True)))
    o = outs[0] - lam * outs[1]
    o = o * lax.rsqrt(jnp.mean(o * o, axis=-1, keepdims=True) + SUBLN_EPS) * subg
    o = o * (1.0 - LAM_INIT)
    return o * _silu(g)


def _attn_ctx_kernel(q_ref, k_ref, v_ref, g_ref, q1_ref, k1_ref, q2_ref, k2_ref, sg_ref, o_ref):
    lam = _lam(q1_ref, k1_ref, q2_ref, k2_ref)
    subg = sg_ref[...]
    for h in range(q_ref.shape[1] // 128):
        sl = slice(h * 128, (h + 1) * 128)
        o = _diff_head(q_ref[:, sl].astype(F32), k_ref[:, sl], v_ref[:, sl],
                       g_ref[:, sl].astype(F32), lam, subg)
        o_ref[:, sl] = o.astype(BF16)


def _attn_ctx(um, seq, lam_q1, lam_k1, lam_q2, lam_k2, subln_g):
    m = um.shape[0]
    small = pl.BlockSpec((1, HEAD_DIM_A), lambda b: (0, 0))
    blk = lambda sec: pl.BlockSpec((seq, D_MODEL), lambda b: (b, sec))
    return pl.pallas_call(
        _attn_ctx_kernel,
        grid=(m // seq,),
        in_specs=[blk(SEC_Q), blk(SEC_K), blk(SEC_V), blk(SEC_GA), small, small, small, small,
                  pl.BlockSpec((1, 2 * HEAD_DIM_A), lambda b: (0, 0))],
        out_specs=blk(0),
        out_shape=jax.ShapeDtypeStruct((m, D_MODEL), BF16),
        compiler_params=pltpu.CompilerParams(vmem_limit_bytes=VMEM_LIMIT),
        name="attn_ctx",
    )(um, um, um, um, lam_q1, lam_k1, lam_q2, lam_k2, subln_g)


def _rope(x, cos, s1, s2):
    return x * cos + pltpu.roll(x, 112, 1) * s1 + pltpu.roll(x, 16, 1) * s2


def _attn_lat_kernel(q_ref, k_ref, v_ref, g_ref, ck_ref, cv_ref, cq_ref, s1q_ref, s2q_ref,
                     ckk_ref, s1k_ref, s2k_ref, q1_ref, k1_ref, q2_ref, k2_ref, sg_ref,
                     o_ref, kall_ref, vall_ref):
    past = cv_ref.shape[0]
    w = q_ref.shape[1]
    nh = w // 128

    @pl.when(pl.program_id(2) == 0)
    def _():
        kall_ref[0:past, :] = ck_ref[0].reshape(past, w).astype(BF16)
        vall_ref[0:past, :] = cv_ref[...].astype(BF16)
        vall_ref[past:, :] = v_ref[...]
        for h in range(nh):
            sl = slice(h * 128, (h + 1) * 128)
            kall_ref[past:, sl] = _rope(k_ref[:, sl].astype(F32), ckk_ref[...], s1k_ref[...],
                                        s2k_ref[...]).astype(BF16)

    lam = _lam(q1_ref, k1_ref, q2_ref, k2_ref)
    subg = sg_ref[...]
    for h in range(nh):
        sl = slice(h * 128, (h + 1) * 128)
        q = _rope(q_ref[:, sl].astype(F32), cq_ref[...], s1q_ref[...], s2q_ref[...])
        o = _diff_head(q, kall_ref[:, sl], vall_ref[:, sl], g_ref[:, sl].astype(F32), lam, subg)
        o_ref[:, sl] = o.astype(BF16)


def _rope_tables(t):
    pos = jnp.arange(t)
    row = (pos // GRID_W).astype(F32)
    col = (pos % GRID_W).astype(F32)
    n_freq = HEAD_DIM_A // 4
    lane = jnp.arange(128)
    d = lane % HEAD_DIM_A
    use_col = (d // 32) == 1
    second = ((d % 32) // 16) == 1
    inv = ROPE_BASE ** (-(d % n_freq).astype(F32) / n_freq)
    ang = jnp.where(use_col[None, :], col[:, None], row[:, None]) * inv[None, :]
    cos = jnp.cos(ang)
    sin = jnp.sin(ang)
    s1 = jnp.where(second[None, :], 0.0, -sin)
    s2 = jnp.where(second[None, :], sin, 0.0)
    return cos, s1, s2


def _attn_lat(u, cache_k, cache_v2, seq, past, lam_q1, lam_k1, lam_q2, lam_k2, subln_g):
    m = u.shape[0]
    nbat = m // seq
    w = GW
    ng = D_MODEL // w
    tq = 256
    nq = seq // tq
    cos, s1, s2 = _rope_tables(seq)
    small = pl.BlockSpec((1, HEAD_DIM_A), lambda b, g, i: (0, 0))
    qblk = lambda sec: pl.BlockSpec((tq, w), lambda b, g, i: (b * nq + i, sec * ng + g))
    kblk = lambda sec: pl.BlockSpec((seq, w), lambda b, g, i: (b, sec * ng + g))
    ckblk = pl.BlockSpec((1, past, w // 128, 2, HEAD_DIM_A), lambda b, g, i: (b, 0, g, 0, 0))
    cblk = pl.BlockSpec((past, w), lambda b, g, i: (b, g))
    tq_blk = pl.BlockSpec((tq, 128), lambda b, g, i: (i, 0))
    tk_blk = pl.BlockSpec((seq, 128), lambda b, g, i: (0, 0))
    return pl.pallas_call(
        _attn_lat_kernel,
        grid=(nbat, ng, nq),
        in_specs=[qblk(SEC_Q), kblk(SEC_K), kblk(SEC_V), qblk(SEC_GA), ckblk, cblk,
                  tq_blk, tq_blk, tq_blk, tk_blk, tk_blk, tk_blk,
                  small, small, small, small,
                  pl.BlockSpec((1, 2 * HEAD_DIM_A), lambda b, g, i: (0, 0))],
        out_specs=pl.BlockSpec((tq, w), lambda b, g, i: (b * nq + i, g)),
        out_shape=jax.ShapeDtypeStruct((m, D_MODEL), BF16),
        scratch_shapes=[pltpu.VMEM((past + seq, w), BF16), pltpu.VMEM((past + seq, w), BF16)],
        compiler_params=pltpu.CompilerParams(
            dimension_semantics=("arbitrary", "arbitrary", "arbitrary"), vmem_limit_bytes=VMEM_LIMIT),
        name="attn_lat",
    )(u, u, u, u, cache_k, cache_v2, cos, s1, s2, cos, s1, s2,
      lam_q1, lam_k1, lam_q2, lam_k2, subln_g)


def _rwkv_consts():
    ri = lax.broadcasted_iota(jnp.int32, (PW, PW), 0)
    ci = lax.broadcasted_iota(jnp.int32, (PW, PW), 1)
    t = lax.broadcasted_iota(jnp.int32, (CHUNK, PW), 0)
    lane = lax.broadcasted_iota(jnp.int32, (CHUNK, PW), 1)
    j = lane & (CHUNK - 1)
    ti = lax.broadcasted_iota(jnp.int32, (CHUNK, CHUNK), 0)
    tj = lax.broadcasted_iota(jnp.int32, (CHUNK, CHUNK), 1)
    return dict(
        bmask=(ri >> 6) == (ci >> 6),
        eye=ri == ci,
        head0=lane < HEAD_DIM_B,
        eye_sbs=jnp.where(t == j, 1.0, 0.0),
        strict=(j < t, j > t),
        incl=(j <= t, j >= t),
        off=[((t >> (n + 1)) == (j >> (n + 1))) & ((t >> n) != (j >> n)) for n in range(6)],
        tri=(jnp.where(tj <= ti, 1.0, 0.0).astype(BF16), jnp.where(tj >= ti, 1.0, 0.0).astype(BF16)),
    )


def _wbd(x, cst):
    xb = x.astype(BF16)
    z = jnp.zeros_like(xb)
    return jnp.concatenate([jnp.where(cst["head0"], xb, z), jnp.where(cst["head0"], z, xb)], axis=0)


def _group_sum(x, e):
    hi, lo = _split2(x)
    ew = e.shape[0]
    parts = [_nn(hi[:, o:o + ew], e) + _nn(lo[:, o:o + ew], e) for o in range(0, x.shape[1], ew)]
    return parts[0] if len(parts) == 1 else jnp.concatenate(parts, axis=1)


def _block_step(ops, states, cst, same_rows):
    L = CHUNK
    ncb = len(ops[0])
    npair = ops[0][0][0].shape[1] // PW
    idx = [(d, c, h) for d in range(2) for c in range(ncb) for h in range(npair)]
    half = lambda a, h: a[:, h * PW:(h + 1) * PW]

    lg = {}
    for d in range(2):
        for c in range(ncb):
            hi, mid, lo = _split3(ops[d][c][0])
            tri = cst["tri"][d]
            lg[d, c] = _nn(tri, hi) + _nn(tri, mid) + _nn(tri, lo)

    ar, wbk, khbh, dg = {}, {}, {}, {}
    for d in range(2):
        for c in range(ncb):
            lw, r, kd, v, kk, b = ops[d][c]
            g = lg[d, c]
            g_l = g[0:1, :] if d == 1 else g[L - 1:L, :]
            at = -(kk * jnp.exp(g - lw))
            rt = r * jnp.exp(g)
            e_inv = jnp.exp(-g)
            e_l = jnp.exp(g_l - g)
            atrt = jnp.concatenate([at, rt], axis=0).astype(BF16)
            bt, kt = b * e_inv, kd * e_inv
            bhh, khh = (b * e_l).astype(BF16), (kd * e_l).astype(BF16)
            eg = jnp.exp(g_l)
            for h in range(npair):
                i = (d, c, h)
                ar[i] = half(atrt, h)
                wbk[i] = jnp.concatenate([_wbd(half(bt, h), cst), _wbd(half(kt, h), cst)], axis=0)
                khbh[i] = jnp.concatenate([half(khh, h), half(bhh, h)], axis=0)
                dg[i] = jnp.where(cst["eye"], half(eg, h), 0.0).astype(BF16)

    gbk = {i: _nt(ar[i], wbk[i]) for i in idx}
    m_ab = {i: jnp.where(cst["strict"][i[0]], gbk[i][:L, :PW], 0.0) for i in idx}
    m_rb = {i: jnp.where(cst["incl"][i[0]], gbk[i][L:, :PW], 0.0).astype(BF16) for i in idx}
    m_akrk = {i: jnp.concatenate([jnp.where(cst["strict"][i[0]], gbk[i][:L, PW:], 0.0),
                                  jnp.where(cst["incl"][i[0]], gbk[i][L:, PW:], 0.0)], axis=0).astype(BF16)
              for i in idx}

    x = {i: cst["eye_sbs"] + jnp.where(cst["off"][0], m_ab[i], 0.0) for i in idx}
    for off in cst["off"][1:]:
        p = {i: _nn(jnp.where(off, m_ab[i], 0.0).astype(BF16), _wbd(x[i], cst)) for i in idx}
        x = {i: x[i] + _nn(x[i].astype(BF16), _wbd(p[i], cst)) for i in idx}
    xb = {i: x[i].astype(BF16) for i in idx}

    vsrc = [(0, c, h) if same_rows else (d, c, h) for d, c, h in idx]
    vhalf = {i: half(ops[i[0]][i[1]][3], i[2]) for i in dict.fromkeys(vsrc)}
    wv = {i: _wbd(vhalf[i], cst) for i in vhalf}
    vb = {i: vhalf[i].astype(BF16) for i in vhalf}
    vof = dict(zip(idx, vsrc))
    lk = {i: jnp.concatenate([jnp.concatenate([ar[i], m_akrk[i]], axis=1),
                              jnp.concatenate([dg[i], jnp.zeros_like(dg[i])], axis=1)], axis=0)
          for i in idx}

    s = [list(sd) for sd in states]
    ys = {}
    for step in range(ncb):
        cur = [(d, c, h) for d, c in ((0, step), (1, ncb - 1 - step)) for h in range(npair)]
        ars = {i: _nn(lk[i], jnp.concatenate([s[i[0]][i[2]].astype(BF16), wv[vof[i]]], axis=0)) for i in cur}
        u = {i: _nn(xb[i], _wbd(ars[i][:L], cst)) for i in cur}
        for i in cur:
            d, c, h = i
            ys[i] = ars[i][L:2 * L] + _nn(m_rb[i], _wbd(u[i], cst))
            upd = _tn(khbh[i], jnp.concatenate([vb[vof[i]], u[i].astype(BF16)], axis=0))
            s[d][h] = ars[i][2 * L:] + jnp.where(cst["bmask"], upd, 0.0)
    ys = [[jnp.concatenate([ys[d, c, h] for h in range(npair)], axis=1) for c in range(ncb)]
          for d in range(2)]
    return ys, s


def _rwkv_kernel(*refs, seq, gw, has_s0, want_state):
    it = iter(refs)
    r_ref, k_ref, v_ref, lo_ref, gb_ref = (next(it) for _ in range(5))
    mur_ref, muk_ref, muv_ref, mul_ref = (next(it) for _ in range(4))
    w0_ref, w2_ref, a0_ref, a2_ref = (next(it) for _ in range(4))
    kk_ref, ka_ref, rk_ref, lng_ref, lnb_ref = (next(it) for _ in range(5))
    s0_ref = next(it) if has_s0 else None
    ob_ref = next(it)
    sf_ref = next(it) if want_state else None
    r_s, v_s, kn_s, lw_s, kd_s, b_s, y_s, st_s = (next(it) for _ in range(8))

    T = seq
    cst = _rwkv_consts()
    npair = gw // PW
    hpp = PW // HEAD_DIM_B
    gi = lax.broadcasted_iota(jnp.int32, (LW, LW), 0)
    gj = lax.broadcasted_iota(jnp.int32, (LW, LW), 1)
    e = jnp.where((gi >> 6) == (gj >> 6), 1.0, 0.0).astype(BF16)
    rows_of = {w: lax.broadcasted_iota(jnp.int32, (T, w), 0) for w in {gw, LW}}

    def shifted(ref, mu_ref):
        x = ref[...].astype(F32)
        mu = mu_ref[...]
        row = rows_of[x.shape[1]]
        prev = jnp.where(row == 0, 0.0, pltpu.roll(x, 1, 0))
        nxt = jnp.where(row == T - 1, 0.0, pltpu.roll(x, T - 1, 0))
        return x + mu[0:1, :] * (prev - x) + mu[1:2, :] * (nxt - x)

    r = shifted(r_ref, mur_ref)
    k = shifted(k_ref, muk_ref)
    v = shifted(v_ref, muv_ref)
    lo = shifted(lo_ref, mul_ref)
    r_s[...] = r
    v_s[...] = v
    kk = k * kk_ref[...]
    kk = kk * lax.rsqrt(_group_sum(kk * kk, e) + 1e-12)
    kn_s[...] = kk
    bonus_dot = _group_sum(r * k * rk_ref[...], e)

    tanh_lo = jnp.tanh(lo).astype(BF16)
    lo_b = lo.astype(BF16)

    def padded(w, before):
        parts = [jnp.zeros((before, gw), F32)] if before else []
        parts.append(w)
        after = LW - before - LORA_R
        if after:
            parts.append(jnp.zeros((after, gw), F32))
        return jnp.concatenate(parts, axis=0).astype(BF16)

    for d in range(2):
        w2p = padded(w2_ref[d], LORA_R * d)
        a2p = padded(a2_ref[d], LORA_R * (2 + d))
        z = w0_ref[d:d + 1, :] + _nn(tanh_lo, w2p)
        lw_s[d] = -EXP_NEG_HALF * jax.nn.sigmoid(z)
        a = jax.nn.sigmoid(a0_ref[d:d + 1, :] + _nn(lo_b, a2p))
        kd_s[d] = k * (1.0 + (a - 1.0) * ka_ref[...])
        b_s[d] = kk * a

    if has_s0:
        ident = jnp.where(cst["eye"], 1.0, 0.0).astype(BF16)
        for d in range(2):
            for h in range(npair):
                z0 = s0_ref[0, d, h * hpp:(h + 1) * hpp].reshape(PW, HEAD_DIM_B)
                hi, lo2 = _split2(z0)
                zt = _tn(hi, ident) + _tn(lo2, ident)
                st_s[d, h] = jnp.where(cst["bmask"], jnp.concatenate([zt] * hpp, axis=0), 0.0)
    else:
        st_s[...] = jnp.zeros(st_s.shape, F32)

    nblk = T // BLOCK_T
    ncb = BLOCK_T // CHUNK

    def block(i, carry):
        rows = []
        for d in range(2):
            blk = i if d == 0 else nblk - 1 - i
            rows.append([])
            for c in range(ncb):
                start = blk * BLOCK_T + c * CHUNK
                if not isinstance(start, int):
                    start = pl.multiple_of(start, CHUNK)
                rows[d].append(pl.ds(start, CHUNK))
        ops = [[(lw_s[d, rw, :], r_s[rw, :], kd_s[d, rw, :], v_s[rw, :], kn_s[rw, :], b_s[d, rw, :])
                for rw in rows[d]] for d in range(2)]
        states = [[st_s[d, h] for h in range(npair)] for d in range(2)]
        ys, s_new = _block_step(ops, states, cst, nblk == 1)
        for d in range(2):
            for c in range(ncb):
                y_s[d, rows[d][c], :] = ys[d][c]
            for h in range(npair):
                st_s[d, h] = s_new[d][h]
        return carry

    if nblk == 1:
        block(0, 0)
    else:
        lax.fori_loop(0, nblk, block, 0)

    y = y_s[0] + y_s[1]
    mu = _group_sum(y, e) * (1.0 / HEAD_DIM_B)
    dlt = y - mu
    var = _group_sum(dlt * dlt, e) * (1.0 / HEAD_DIM_B)
    yn = dlt * lax.rsqrt(var + LNX_EPS) * lng_ref[...] + lnb_ref[...]
    out = (yn + bonus_dot * v_s[...]) * _silu(gb_ref[...].astype(F32))
    ob_ref[...] = out.astype(BF16)

    if want_state:
        ri = lax.broadcasted_iota(jnp.int32, (PW, HEAD_DIM_B), 0)
        ci = lax.broadcasted_iota(jnp.int32, (PW, HEAD_DIM_B), 1)
        fold = jnp.where((ri & (HEAD_DIM_B - 1)) == ci, 1.0, 0.0).astype(BF16)
        for d in range(2):
            for h in range(npair):
                hi, lo2 = _split2(st_s[d, h])
                sf = _tn(hi, fold) + _tn(lo2, fold)
                sf_ref[0, d, h * hpp:(h + 1) * hpp] = sf.reshape(hpp, HEAD_DIM_B, HEAD_DIM_B)


def _rwkv(u, ul, seq, gw, s0, want_state, shift_mu, decay_w0, decay_w2, iclr_a0, iclr_a2,
          k_k, k_a, r_k, lnx_g, lnx_b):
    m = u.shape[0]
    nbat = m // seq
    ng = D_MODEL // gw
    hg = gw // HEAD_DIM_B
    ublk = lambda sec: pl.BlockSpec((seq, gw), lambda b, g: (b, sec * ng + g))
    lblk = pl.BlockSpec((seq, LW), lambda b, g: (b, 0))
    mublk = lambda sec: pl.BlockSpec((2, gw), lambda b, g: (0, sec * ng + g))
    mulblk = pl.BlockSpec((2, LW), lambda b, g: (0, 3 * D_MODEL // LW))
    vec2 = pl.BlockSpec((2, gw), lambda b, g: (0, g))
    mat2 = pl.BlockSpec((2, LORA_R, gw), lambda b, g: (0, 0, g))
    vec1 = pl.BlockSpec((1, gw), lambda b, g: (0, g))
    sblk = pl.BlockSpec((1, 2, hg, HEAD_DIM_B, HEAD_DIM_B), lambda b, g: (b, 0, g, 0, 0))
    in_specs = [ublk(SEC_R), ublk(SEC_KB), ublk(SEC_VB), lblk, ublk(SEC_GB),
                mublk(0), mublk(1), mublk(2), mulblk,
                vec2, mat2, vec2, mat2, vec1, vec1, vec1, vec1, vec1]
    args = [u, u, u, ul, u, shift_mu, shift_mu, shift_mu, shift_mu,
            decay_w0, decay_w2, iclr_a0, iclr_a2,
            k_k.reshape(1, D_MODEL), k_a.reshape(1, D_MODEL), r_k.reshape(1, D_MODEL),
            lnx_g.reshape(1, D_MODEL), lnx_b.reshape(1, D_MODEL)]
    if s0 is not None:
        in_specs.append(sblk)
        args.append(s0)
    out_specs = [pl.BlockSpec((seq, gw), lambda b, g: (b, g))]
    out_shape = [jax.ShapeDtypeStruct((m, D_MODEL), BF16)]
    if want_state:
        out_specs.append(sblk)
        out_shape.append(jax.ShapeDtypeStruct((nbat, 2, N_HEADS_B, HEAD_DIM_B, HEAD_DIM_B), F32))
    tw = pltpu.VMEM((seq, gw), F32)
    tw2 = pltpu.VMEM((2, seq, gw), F32)
    res = pl.pallas_call(
        functools.partial(_rwkv_kernel, seq=seq, gw=gw, has_s0=s0 is not None, want_state=want_state),
        grid=(nbat, ng),
        in_specs=in_specs,
        out_specs=out_specs,
        out_shape=out_shape,
        scratch_shapes=[tw, tw, tw, tw2, tw2, tw2, tw2, pltpu.VMEM((2, gw // PW, PW, PW), F32)],
        compiler_params=pltpu.CompilerParams(
            dimension_semantics=("arbitrary", "arbitrary"), vmem_limit_bytes=VMEM_LIMIT),
        name="rwkv_state" if want_state else "rwkv",
    )(*args)
    return res


def _outproj_kernel(oa_ref, ob_ref, x_ref, mod_ref, w_ref, fg_ref, y_ref):
    acc = _nn(oa_ref[...], w_ref[0:D_MODEL, :]) + _nn(ob_ref[...], w_ref[D_MODEL:, :])
    gate = mod_ref[0][:, 2 * D_MODEL:]
    xo = x_ref[...] + gate * acc
    y_ref[...] = xo * lax.rsqrt(jnp.mean(xo * xo, axis=-1, keepdims=True) + NORM_EPS) * fg_ref[...]


def _outproj(o_a, o_b, x, mod3, w_out_bf, final_g, mod_row):
    m = x.shape[0]
    tm = 512
    blk = pl.BlockSpec((tm, D_MODEL), lambda i: (i, 0))
    return pl.pallas_call(
        _outproj_kernel,
        grid=(m // tm,),
        in_specs=[blk, blk, blk,
                  pl.BlockSpec((1, 1, 3 * D_MODEL), lambda i: (mod_row(i * tm), 0, 0)),
                  pl.BlockSpec((2 * D_MODEL, D_MODEL), lambda i: (0, 0)),
                  pl.BlockSpec((1, D_MODEL), lambda i: (0, 0))],
        out_specs=blk,
        out_shape=jax.ShapeDtypeStruct((m, D_MODEL), F32),
        compiler_params=pltpu.CompilerParams(vmem_limit_bytes=VMEM_LIMIT),
        name="outproj",
    )(o_a, o_b, x, mod3, w_out_bf, final_g)


def kernel(x_prompt, x_sample, cache_k, cache_v, state_rwkv, c, c_ctx, norm_g, w_ada, b_ada, w_in, lam_q1, lam_k1, lam_q2, lam_k2, subln_g, shift_mu, decay_w0, decay_w2, iclr_a0, iclr_a2, k_k, k_a, r_k, lnx_g, lnx_b, w_out, final_g):
    batch, seq, d = x_prompt.shape
    dbatch, dseq, _ = x_sample.shape
    past = cache_k.shape[2]
    assert d == D_MODEL and w_in.shape == (1, D_MODEL, D_IN) and dseq % GRID_W == 0

    cvec = jnp.concatenate([c_ctx[None, :], c, jnp.zeros((8 - 1 - dbatch, d), F32)], axis=0)
    mod3 = _mod(cvec, w_ada[0], b_ada[0]).reshape(8, 1, 3 * d)
    lo0 = 4 * D_MODEL + 3 * D_MODEL
    w_bf = w_in[0].astype(BF16)
    w_lora = w_bf[:, lo0:lo0 + LW]
    w_gb = w_bf[:, lo0 + LW:]
    w_out_bf = w_out[0].astype(BF16)
    ctx_row = lambda tok: 0
    lat_row = lambda tok: 1 + tok // dseq
    lp = (shift_mu[0], decay_w0[0], decay_w2[0], iclr_a0[0], iclr_a2[0], k_k[0], k_a[0], r_k[0],
          lnx_g[0], lnx_b[0])
    lam = (lam_q1, lam_k1, lam_q2, lam_k2, subln_g)
    fg = final_g.reshape(1, d)

    xp = x_prompt.reshape(batch * seq, d)
    u_c, ul_c, nk, nv = _inproj(xp, mod3, norm_g, w_bf, w_gb, w_lora, ctx_row, seq, True)
    oa_c = _attn_ctx(u_c, seq, *lam)
    ob_c, s_new = _rwkv(u_c, ul_c, seq, RWKV_GW_CTX, None, True, *lp)
    y_prompt = _outproj(oa_c, ob_c, xp, mod3, w_out_bf, fg, ctx_row).reshape(batch, seq, d)

    xs = x_sample.reshape(dbatch * dseq, d)
    u_l, ul_l = _inproj(xs, mod3, norm_g, w_bf, w_gb, w_lora, lat_row, dseq, False)
    cv = cache_v[:, 0].reshape(dbatch * past, d)
    oa_l = _attn_lat(u_l, cache_k[:, 0], cv, dseq, past, *lam)
    (ob_l,) = _rwkv(u_l, ul_l, dseq, RWKV_GW_LAT, state_rwkv[:, 0], False, *lp)
    y_sample = _outproj(oa_l, ob_l, xs, mod3, w_out_bf, fg, lat_row).reshape(dbatch, dseq, d)

    new_s = s_new.reshape(batch, 1, 2, N_HEADS_B, HEAD_DIM_B, HEAD_DIM_B)
    return (y_prompt, y_sample, nk, nv, new_s)
```

```python
import functools
import math

import jax
import jax.numpy as jnp
from jax import lax
from jax.experimental import pallas as pl
from jax.experimental.pallas import tpu as pltpu

F32 = jnp.float32
BF16 = jnp.bfloat16
HIGHEST = lax.Precision.HIGHEST

D_MODEL = 1024
GRID_W = 64
HEAD_DIM_A = 64
N_HEADS_A = 8
HEAD_DIM_B = 64
N_HEADS_B = 16
LORA_R = 64
D_SHIFT = 3 * D_MODEL + 4 * LORA_R
D_IN = 4 * D_MODEL + D_SHIFT + D_MODEL
ROPE_BASE = 10000.0
NORM_EPS = 1e-6
SUBLN_EPS = 1e-5
LNX_EPS = 64e-5
LAM_INIT = 0.8 - 0.6 * math.exp(-0.3 * 0)
EXP_NEG_HALF = math.exp(-0.5)
LOG2E = math.log2(math.e)

GW = 256
LW = 4 * LORA_R
PW = 128
RWKV_GW_CTX = 1024
RWKV_GW_LAT = 512
CHUNK = 64
BLOCK_T = 256
VMEM_LIMIT = 48 * 1024 * 1024
VMEM_LIMIT_BIG = 56 * 1024 * 1024
SEC_Q, SEC_K, SEC_V, SEC_GA, SEC_R, SEC_KB, SEC_VB, SEC_GB = range(8)


def _nn(a, b):
    return jnp.dot(a, b, preferred_element_type=F32)


def _nt(a, b):
    return lax.dot_general(a, b, (((1,), (1,)), ((), ())), preferred_element_type=F32)


def _tn(a, b):
    return lax.dot_general(a, b, (((0,), (0,)), ((), ())), preferred_element_type=F32)


def _split2(x):
    hi = x.astype(BF16)
    lo = (x - hi.astype(F32)).astype(BF16)
    return hi, lo


def _split3(x):
    hi = x.astype(BF16)
    r1 = x - hi.astype(F32)
    mid = r1.astype(BF16)
    lo = (r1 - mid.astype(F32)).astype(BF16)
    return hi, mid, lo


def _silu(x):
    return x * jax.nn.sigmoid(x)


def _mod_kernel(c_ref, w_ref, b_ref, o_ref):
    s = _silu(c_ref[...])
    o_ref[...] = jnp.dot(s, w_ref[...], precision=HIGHEST, preferred_element_type=F32) + b_ref[...]


def _mod(cvec, w_ada, b_ada):
    n = w_ada.shape[1]
    tn = 512
    return pl.pallas_call(
        _mod_kernel,
        grid=(n // tn,),
        in_specs=[
            pl.BlockSpec((8, D_MODEL), lambda j: (0, 0)),
            pl.BlockSpec((D_MODEL, tn), lambda j: (0, j)),
            pl.BlockSpec((1, tn), lambda j: (0, j)),
        ],
        out_specs=pl.BlockSpec((8, tn), lambda j: (0, j)),
        out_shape=jax.ShapeDtypeStruct((8, n), F32),
        name="mod",
    )(cvec, w_ada, b_ada.reshape(1, n))


def _inproj_kernel(*refs, seq, want_cache):
    x_ref, mod_ref, g_ref, w_ref, wgb_ref, wl_ref, um_ref, ul_ref = refs[:8]
    h_ref = refs[-1]
    j = pl.program_id(1)

    @pl.when(j == 0)
    def _():
        x = x_ref[...]
        y = x * lax.rsqrt(jnp.mean(x * x, axis=-1, keepdims=True) + NORM_EPS) * g_ref[...]
        mod = mod_ref[0]
        shift = mod[:, 0:D_MODEL]
        scale = mod[:, D_MODEL:2 * D_MODEL]
        h = (y * (1.0 + scale) + shift).astype(BF16)
        h_ref[...] = h
        ul_ref[...] = _nn(h, wl_ref[...]).astype(BF16)

    w = jnp.where(j == SEC_GB, wgb_ref[...], w_ref[...])
    acc = _nn(h_ref[...], w)
    um_ref[...] = acc.astype(BF16)
    if want_cache:
        nk_ref, nv_ref = refs[8:10]
        nb = acc.shape[0] // seq

        @pl.when(j == SEC_K)
        def _():
            nk_ref[:, 0] = acc.reshape((nb,) + nk_ref.shape[2:])

        @pl.when(j == SEC_V)
        def _():
            nv_ref[:, 0] = acc.reshape((nb,) + nv_ref.shape[2:])


def _inproj(x, mod3, norm_g, w_bf, w_gb, w_lora, mod_row, seq, want_cache):
    m = x.shape[0]
    tm = 1024
    nsec = SEC_GB + 1
    nb = tm // seq
    out_specs = [pl.BlockSpec((tm, D_MODEL), lambda i, j: (i, j)),
                 pl.BlockSpec((tm, LW), lambda i, j: (i, 0))]
    out_shape = [jax.ShapeDtypeStruct((m, nsec * D_MODEL), BF16), jax.ShapeDtypeStruct((m, LW), BF16)]
    if want_cache:
        out_specs += [pl.BlockSpec((nb, 1, seq, N_HEADS_A, 2, HEAD_DIM_A), lambda i, j: (i, 0, 0, 0, 0, 0)),
                      pl.BlockSpec((nb, 1, seq, N_HEADS_A, 2 * HEAD_DIM_A), lambda i, j: (i, 0, 0, 0, 0))]
        out_shape += [jax.ShapeDtypeStruct((m // seq, 1, seq, N_HEADS_A, 2, HEAD_DIM_A), F32),
                      jax.ShapeDtypeStruct((m // seq, 1, seq, N_HEADS_A, 2 * HEAD_DIM_A), F32)]
    return pl.pallas_call(
        functools.partial(_inproj_kernel, seq=seq, want_cache=want_cache),
        grid=(m // tm, nsec),
        in_specs=[
            pl.BlockSpec((tm, D_MODEL), lambda i, j: (i, 0)),
            pl.BlockSpec((1, 1, 3 * D_MODEL), lambda i, j: (mod_row(i * tm), 0, 0)),
            pl.BlockSpec((1, D_MODEL), lambda i, j: (0, 0)),
            pl.BlockSpec((D_MODEL, D_MODEL), lambda i, j: (0, jnp.minimum(j, SEC_GB - 1))),
            pl.BlockSpec((D_MODEL, D_MODEL), lambda i, j: (0, 0)),
            pl.BlockSpec((D_MODEL, LW), lambda i, j: (0, 0)),
        ],
        out_specs=out_specs,
        out_shape=out_shape,
        scratch_shapes=[pltpu.VMEM((tm, D_MODEL), BF16)],
        compiler_params=pltpu.CompilerParams(
            dimension_semantics=("arbitrary", "arbitrary"), vmem_limit_bytes=VMEM_LIMIT_BIG),
        name="inproj_cache" if want_cache else "inproj",
    )(x, mod3, norm_g, w_bf, w_gb, w_lora)


def _lam(q1_ref, k1_ref, q2_ref, k2_ref):
    s1 = jnp.sum(q1_ref[...] * k1_ref[...], axis=-1, keepdims=True)
    s2 = jnp.sum(q2_ref[...] * k2_ref[...], axis=-1, keepdims=True)
    return jnp.exp(s1) - jnp.exp(s2) + LAM_INIT


def _diff_head(q, kb, vb, g, lam, subg):
    lane = lax.broadcasted_iota(jnp.int32, q.shape, 1)
    qs = q * (HEAD_DIM_A ** -0.5 * LOG2E)
    outs = []
    for m in range(2):
        qm = jnp.where((lane >> 6) == m, qs, 0.0).astype(BF16)
        s = _nt(qm, kb)
        p = jnp.exp2(s - jnp.max(s, axis=-1, keepdims=True))
        outs.append(_nn(p.astype(BF16), vb) * (1.0 / jnp.sum(p, axis=-1, keepdims=True)))
    o = outs[0] - lam * outs[1]
    o = o * lax.rsqrt(jnp.mean(o * o, axis=-1, keepdims=True) + SUBLN_EPS) * subg
    o = o * (1.0 - LAM_INIT)
    return o * _silu(g)


def _attn_ctx_kernel(q_ref, k_ref, v_ref, g_ref, q1_ref, k1_ref, q2_ref, k2_ref, sg_ref, o_ref):
    lam = _lam(q1_ref, k1_ref, q2_ref, k2_ref)
    subg = sg_ref[...]
    for h in range(q_ref.shape[1] // 128):
        sl = slice(h * 128, (h + 1) * 128)
        o = _diff_head(q_ref[:, sl].astype(F32), k_ref[:, sl], v_ref[:, sl],
                       g_ref[:, sl].astype(F32), lam, subg)
        o_ref[:, sl] = o.astype(BF16)


def _attn_ctx(um, seq, lam_q1, lam_k1, lam_q2, lam_k2, subln_g):
    m = um.shape[0]
    small = pl.BlockSpec((1, HEAD_DIM_A), lambda b: (0, 0))
    blk = lambda sec: pl.BlockSpec((seq, D_MODEL), lambda b: (b, sec))
    return pl.pallas_call(
        _attn_ctx_kernel,
        grid=(m // seq,),
        in_specs=[blk(SEC_Q), blk(SEC_K), blk(SEC_V), blk(SEC_GA), small, small, small, small,
                  pl.BlockSpec((1, 2 * HEAD_DIM_A), lambda b: (0, 0))],
        out_specs=blk(0),
        out_shape=jax.ShapeDtypeStruct((m, D_MODEL), BF16),
        compiler_params=pltpu.CompilerParams(vmem_limit_bytes=VMEM_LIMIT),
        name="attn_ctx",
    )(um, um, um, um, lam_q1, lam_k1, lam_q2, lam_k2, subln_g)


def _rope(x, cos, s1, s2):
    return x * cos + pltpu.roll(x, 112, 1) * s1 + pltpu.roll(x, 16, 1) * s2


def _attn_lat_kernel(q_ref, k_ref, v_ref, g_ref, ck_ref, cv_ref, cq_ref, s1q_ref, s2q_ref,
                     ckk_ref, s1k_ref, s2k_ref, q1_ref, k1_ref, q2_ref, k2_ref, sg_ref,
                     o_ref, kall_ref, vall_ref):
    past = cv_ref.shape[0]
    w = q_ref.shape[1]
    nh = w // 128

    @pl.when(pl.program_id(2) == 0)
    def _():
        kall_ref[0:past, :] = ck_ref[0].reshape(past, w).astype(BF16)
        vall_ref[0:past, :] = cv_ref[...].astype(BF16)
        vall_ref[past:, :] = v_ref[...]
        for h in range(nh):
            sl = slice(h * 128, (h + 1) * 128)
            kall_ref[past:, sl] = _rope(k_ref[:, sl].astype(F32), ckk_ref[...], s1k_ref[...],
                                        s2k_ref[...]).astype(BF16)

    lam = _lam(q1_ref, k1_ref, q2_ref, k2_ref)
    subg = sg_ref[...]
    for h in range(nh):
        sl = slice(h * 128, (h + 1) * 128)
        q = _rope(q_ref[:, sl].astype(F32), cq_ref[...], s1q_ref[...], s2q_ref[...])
        o = _diff_head(q, kall_ref[:, sl], vall_ref[:, sl], g_ref[:, sl].astype(F32), lam, subg)
        o_ref[:, sl] = o.astype(BF16)


def _rope_tables(t):
    pos = jnp.arange(t)
    row = (pos // GRID_W).astype(F32)
    col = (pos % GRID_W).astype(F32)
    n_freq = HEAD_DIM_A // 4
    lane = jnp.arange(128)
    d = lane % HEAD_DIM_A
    use_col = (d // 32) == 1
    second = ((d % 32) // 16) == 1
    inv = ROPE_BASE ** (-(d % n_freq).astype(F32) / n_freq)
    ang = jnp.where(use_col[None, :], col[:, None], row[:, None]) * inv[None, :]
    cos = jnp.cos(ang)
    sin = jnp.sin(ang)
    s1 = jnp.where(second[None, :], 0.0, -sin)
    s2 = jnp.where(second[None, :], sin, 0.0)
    return cos, s1, s2


def _attn_lat(u, cache_k, cache_v2, seq, past, lam_q1, lam_k1, lam_q2, lam_k2, subln_g):
    m = u.shape[0]
    nbat = m // seq
    w = GW
    ng = D_MODEL // w
    tq = 256
    nq = seq // tq
    cos, s1, s2 = _rope_tables(seq)
    small = pl.BlockSpec((1, HEAD_DIM_A), lambda b, g, i: (0, 0))
    qblk = lambda sec: pl.BlockSpec((tq, w), lambda b, g, i: (b * nq + i, sec * ng + g))
    kblk = lambda sec: pl.BlockSpec((seq, w), lambda b, g, i: (b, sec * ng + g))
    ckblk = pl.BlockSpec((1, past, w // 128, 2, HEAD_DIM_A), lambda b, g, i: (b, 0, g, 0, 0))
    cblk = pl.BlockSpec((past, w), lambda b, g, i: (b, g))
    tq_blk = pl.BlockSpec((tq, 128), lambda b, g, i: (i, 0))
    tk_blk = pl.BlockSpec((seq, 128), lambda b, g, i: (0, 0))
    return pl.pallas_call(
        _attn_lat_kernel,
        grid=(nbat, ng, nq),
        in_specs=[qblk(SEC_Q), kblk(SEC_K), kblk(SEC_V), qblk(SEC_GA), ckblk, cblk,
                  tq_blk, tq_blk, tq_blk, tk_blk, tk_blk, tk_blk,
                  small, small, small, small,
                  pl.BlockSpec((1, 2 * HEAD_DIM_A), lambda b, g, i: (0, 0))],
        out_specs=pl.BlockSpec((tq, w), lambda b, g, i: (b * nq + i, g)),
        out_shape=jax.ShapeDtypeStruct((m, D_MODEL), BF16),
        scratch_shapes=[pltpu.VMEM((past + seq, w), BF16), pltpu.VMEM((past + seq, w), BF16)],
        compiler_params=pltpu.CompilerParams(
            dimension_semantics=("arbitrary", "arbitrary", "arbitrary"), vmem_limit_bytes=VMEM_LIMIT),
        name="attn_lat",
    )(u, u, u, u, cache_k, cache_v2, cos, s1, s2, cos, s1, s2,
      lam_q1, lam_k1, lam_q2, lam_k2, subln_g)


def _rwkv_consts():
    ri = lax.broadcasted_iota(jnp.int32, (PW, PW), 0)
    ci = lax.broadcasted_iota(jnp.int32, (PW, PW), 1)
    t = lax.broadcasted_iota(jnp.int32, (CHUNK, PW), 0)
    lane = lax.broadcasted_iota(jnp.int32, (CHUNK, PW), 1)
    j = lane & (CHUNK - 1)
    ti = lax.broadcasted_iota(jnp.int32, (CHUNK, CHUNK), 0)
    tj = lax.broadcasted_iota(jnp.int32, (CHUNK, CHUNK), 1)
    return dict(
        bmask=(ri >> 6) == (ci >> 6),
        eye=ri == ci,
        head0=lane < HEAD_DIM_B,
        eye_sbs=jnp.where(t == j, 1.0, 0.0),
        strict=(j < t, j > t),
        incl=(j <= t, j >= t),
        off=[((t >> (n + 1)) == (j >> (n + 1))) & ((t >> n) != (j >> n)) for n in range(6)],
        tri=(jnp.where(tj <= ti, 1.0, 0.0).astype(BF16), jnp.where(tj >= ti, 1.0, 0.0).astype(BF16)),
    )


def _wbd(x, cst):
    xb = x.astype(BF16)
    z = jnp.zeros_like(xb)
    return jnp.concatenate([jnp.where(cst["head0"], xb, z), jnp.where(cst["head0"], z, xb)], axis=0)


def _group_sum(x, e):
    hi, lo = _split2(x)
    ew = e.shape[0]
    parts = [_nn(hi[:, o:o + ew], e) + _nn(lo[:, o:o + ew], e) for o in range(0, x.shape[1], ew)]
    return parts[0] if len(parts) == 1 else jnp.concatenate(parts, axis=1)


def _pair_pipeline(h, ops, lg, states, cst, same_rows, out):
    L = CHUNK
    ncb = len(ops[0])
    idx = [(d, c) for d in range(2) for c in range(ncb)]
    half = lambda a: a[:, h * PW:(h + 1) * PW]

    ar, wbk, khbh, dg = {}, {}, {}, {}
    for d, c in idx:
        lw, r, kd, v, kk, b = (half(a) for a in ops[d][c])
        g = half(lg[d, c])
        g_l = g[0:1, :] if d == 1 else g[L - 1:L, :]
        at = -(kk * jnp.exp2(g - lw))
        rt = r * jnp.exp2(g)
        e_inv = jnp.exp2(-g)
        eg = jnp.exp2(g_l)
        bt, kt = b * e_inv, kd * e_inv
        ar[d, c] = jnp.concatenate([at, rt], axis=0).astype(BF16)
        wbk[d, c] = jnp.concatenate([_wbd(bt, cst), _wbd(kt, cst)], axis=0)
        khbh[d, c] = jnp.concatenate([(kt * eg).astype(BF16), (bt * eg).astype(BF16)], axis=0)
        dg[d, c] = jnp.where(cst["eye"], eg, 0.0).astype(BF16)
    yield

    gbk = {i: _nt(ar[i], wbk[i]) for i in idx}
    yield
    m_ab = {i: jnp.where(cst["strict"][i[0]], gbk[i][:L, :PW], 0.0) for i in idx}
    m_rb = {i: jnp.where(cst["incl"][i[0]], gbk[i][L:, :PW], 0.0).astype(BF16) for i in idx}
    m_akrk = {i: jnp.concatenate([jnp.where(cst["strict"][i[0]], gbk[i][:L, PW:], 0.0),
                                  jnp.where(cst["incl"][i[0]], gbk[i][L:, PW:], 0.0)], axis=0).astype(BF16)
              for i in idx}

    x = {i: cst["eye_sbs"] + jnp.where(cst["off"][0], m_ab[i], 0.0) for i in idx}
    for off in cst["off"][1:]:
        p = {i: _nn(jnp.where(off, m_ab[i], 0.0).astype(BF16), _wbd(x[i], cst)) for i in idx}
        yield
        x = {i: x[i] + _nn(x[i].astype(BF16), _wbd(p[i], cst)) for i in idx}
        yield
    xb = {i: x[i].astype(BF16) for i in idx}

    vsrc = {(d, c): (0, c) if same_rows else (d, c) for d, c in idx}
    vhalf = {i: half(ops[i[0]][i[1]][3]) for i in dict.fromkeys(vsrc.values())}
    wv = {i: _wbd(vhalf[i], cst) for i in vhalf}
    vb = {i: vhalf[i].astype(BF16) for i in vhalf}
    lk = {i: jnp.concatenate([jnp.concatenate([ar[i], m_akrk[i]], axis=1),
                              jnp.concatenate([dg[i], jnp.zeros_like(dg[i])], axis=1)], axis=0)
          for i in idx}

    s = [states[0][h], states[1][h]]
    for step in range(ncb):
        cur = [(0, step), (1, ncb - 1 - step)]
        ars = {i: _nn(lk[i], jnp.concatenate([s[i[0]].astype(BF16), wv[vsrc[i]]], axis=0)) for i in cur}
        yield
        u = {i: _nn(xb[i], _wbd(ars[i][:L], cst)) for i in cur}
        yield
        for i in cur:
            d, c = i
            out["y"][d, c, h] = ars[i][L:2 * L] + _nn(m_rb[i], _wbd(u[i], cst))
            upd = _tn(khbh[i], jnp.concatenate([vb[vsrc[i]], u[i].astype(BF16)], axis=0))
            s[d] = ars[i][2 * L:] + jnp.where(cst["bmask"], upd, 0.0)
        yield
    out["s"][0][h], out["s"][1][h] = s


def _block_step(ops, states, cst, same_rows):
    ncb = len(ops[0])
    npair = ops[0][0][0].shape[1] // PW

    lg = {}
    for d in range(2):
        for c in range(ncb):
            hi, mid, lo = _split3(ops[d][c][0])
            tri = cst["tri"][d]
            lg[d, c] = _nn(tri, hi) + _nn(tri, mid) + _nn(tri, lo)

    out = {"y": {}, "s": [[None] * npair, [None] * npair]}
    pending = [_pair_pipeline(h, ops, lg, states, cst, same_rows, out) for h in range(npair)]
    active = pending
    while active:
        active = [gen for gen in active if next(gen, "done") != "done"]
    ys = [[jnp.concatenate([out["y"][d, c, h] for h in range(npair)], axis=1) if npair > 1
           else out["y"][d, c, 0] for c in range(ncb)] for d in range(2)]
    return ys, out["s"]


def _rwkv_kernel(*refs, seq, gw, has_s0, want_state):
    it = iter(refs)
    r_ref, k_ref, v_ref, lo_ref, gb_ref = (next(it) for _ in range(5))
    mur_ref, muk_ref, muv_ref, mul_ref = (next(it) for _ in range(4))
    w0_ref, w2_ref, a0_ref, a2_ref = (next(it) for _ in range(4))
    kk_ref, ka_ref, rk_ref, lng_ref, lnb_ref = (next(it) for _ in range(5))
    s0_ref = next(it) if has_s0 else None
    ob_ref = next(it)
    sf_ref = next(it) if want_state else None
    r_s, v_s, kn_s, lw_s, kd_s, b_s, y_s, st_s = (next(it) for _ in range(8))

    T = seq
    cst = _rwkv_consts()
    npair = gw // PW
    hpp = PW // HEAD_DIM_B
    gi = lax.broadcasted_iota(jnp.int32, (LW, LW), 0)
    gj = lax.broadcasted_iota(jnp.int32, (LW, LW), 1)
    e = jnp.where((gi >> 6) == (gj >> 6), 1.0, 0.0).astype(BF16)
    rows_of = {w: lax.broadcasted_iota(jnp.int32, (T, w), 0) for w in {gw, LW}}

    def shifted(ref, mu_ref):
        x = ref[...].astype(F32)
        mu = mu_ref[...]
        row = rows_of[x.shape[1]]
        prev = jnp.where(row == 0, 0.0, pltpu.roll(x, 1, 0))
        nxt = jnp.where(row == T - 1, 0.0, pltpu.roll(x, T - 1, 0))
        return x + mu[0:1, :] * (prev - x) + mu[1:2, :] * (nxt - x)

    r = shifted(r_ref, mur_ref)
    k = shifted(k_ref, muk_ref)
    v = shifted(v_ref, muv_ref)
    lo = shifted(lo_ref, mul_ref)
    r_s[...] = r
    v_s[...] = v
    kk = k * kk_ref[...]
    kk = kk * lax.rsqrt(_group_sum(kk * kk, e) + 1e-12)
    kn_s[...] = kk
    bonus_dot = _group_sum(r * k * rk_ref[...], e)

    tanh_lo = jnp.tanh(lo).astype(BF16)
    lo_b = lo.astype(BF16)

    def padded(w, before):
        parts = [jnp.zeros((before, gw), F32)] if before else []
        parts.append(w)
        after = LW - before - LORA_R
        if after:
            parts.append(jnp.zeros((after, gw), F32))
        return jnp.concatenate(parts, axis=0).astype(BF16)

    for d in range(2):
        w2p = padded(w2_ref[d], LORA_R * d)
        a2p = padded(a2_ref[d], LORA_R * (2 + d))
        z = w0_ref[d:d + 1, :] + _nn(tanh_lo, w2p)
        lw_s[d] = -(EXP_NEG_HALF * LOG2E) * jax.nn.sigmoid(z)
        a = jax.nn.sigmoid(a0_ref[d:d + 1, :] + _nn(lo_b, a2p))
        kd_s[d] = k * (1.0 + (a - 1.0) * ka_ref[...])
        b_s[d] = kk * a

    if has_s0:
        ident = jnp.where(cst["eye"], 1.0, 0.0).astype(BF16)
        for d in range(2):
            for h in range(npair):
                z0 = s0_ref[0, d, h * hpp:(h + 1) * hpp].reshape(PW, HEAD_DIM_B)
                hi, lo2 = _split2(z0)
                zt = _tn(hi, ident) + _tn(lo2, ident)
                st_s[d, h] = jnp.where(cst["bmask"], jnp.concatenate([zt] * hpp, axis=0), 0.0)
    else:
        st_s[...] = jnp.zeros(st_s.shape, F32)

    nblk = T // BLOCK_T
    ncb = BLOCK_T // CHUNK

    def block(i, carry):
        rows = []
        for d in range(2):
            blk = i if d == 0 else nblk - 1 - i
            rows.append([])
            for c in range(ncb):
                start = blk * BLOCK_T + c * CHUNK
                if not isinstance(start, int):
                    start = pl.multiple_of(start, CHUNK)
                rows[d].append(pl.ds(start, CHUNK))
        ops = [[(lw_s[d, rw, :], r_s[rw, :], kd_s[d, rw, :], v_s[rw, :], kn_s[rw, :], b_s[d, rw, :])
                for rw in rows[d]] for d in range(2)]
        states = [[st_s[d, h] for h in range(npair)] for d in range(2)]
        ys, s_new = _block_step(ops, states, cst, nblk == 1)
        for d in range(2):
            for c in range(ncb):
                y_s[d, rows[d][c], :] = ys[d][c]
            for h in range(npair):
                st_s[d, h] = s_new[d][h]
        return carry

    if nblk == 1:
        block(0, 0)
    else:
        lax.fori_loop(0, nblk, block, 0)

    y = y_s[0] + y_s[1]
    mu = _group_sum(y, e) * (1.0 / HEAD_DIM_B)
    dlt = y - mu
    var = _group_sum(dlt * dlt, e) * (1.0 / HEAD_DIM_B)
    yn = dlt * lax.rsqrt(var + LNX_EPS) * lng_ref[...] + lnb_ref[...]
    out = (yn + bonus_dot * v_s[...]) * _silu(gb_ref[...].astype(F32))
    ob_ref[...] = out.astype(BF16)

    if want_state:
        ri = lax.broadcasted_iota(jnp.int32, (PW, HEAD_DIM_B), 0)
        ci = lax.broadcasted_iota(jnp.int32, (PW, HEAD_DIM_B), 1)
        fold = jnp.where((ri & (HEAD_DIM_B - 1)) == ci, 1.0, 0.0).astype(BF16)
        for d in range(2):
            for h in range(npair):
                hi, lo2 = _split2(st_s[d, h])
                sf = _tn(hi, fold) + _tn(lo2, fold)
                sf_ref[0, d, h * hpp:(h + 1) * hpp] = sf.reshape(hpp, HEAD_DIM_B, HEAD_DIM_B)


def _rwkv(u, ul, seq, gw, s0, want_state, shift_mu, decay_w0, decay_w2, iclr_a0, iclr_a2,
          k_k, k_a, r_k, lnx_g, lnx_b):
    m = u.shape[0]
    nbat = m // seq
    ng = D_MODEL // gw
    hg = gw // HEAD_DIM_B
    ublk = lambda sec: pl.BlockSpec((seq, gw), lambda b, g: (b, sec * ng + g))
    lblk = pl.BlockSpec((seq, LW), lambda b, g: (b, 0))
    mublk = lambda sec: pl.BlockSpec((2, gw), lambda b, g: (0, sec * ng + g))
    mulblk = pl.BlockSpec((2, LW), lambda b, g: (0, 3 * D_MODEL // LW))
    vec2 = pl.BlockSpec((2, gw), lambda b, g: (0, g))
    mat2 = pl.BlockSpec((2, LORA_R, gw), lambda b, g: (0, 0, g))
    vec1 = pl.BlockSpec((1, gw), lambda b, g: (0, g))
    sblk = pl.BlockSpec((1, 2, hg, HEAD_DIM_B, HEAD_DIM_B), lambda b, g: (b, 0, g, 0, 0))
    in_specs = [ublk(SEC_R), ublk(SEC_KB), ublk(SEC_VB), lblk, ublk(SEC_GB),
                mublk(0), mublk(1), mublk(2), mulblk,
                vec2, mat2, vec2, mat2, vec1, vec1, vec1, vec1, vec1]
    args = [u, u, u, ul, u, shift_mu, shift_mu, shift_mu, shift_mu,
            decay_w0, decay_w2, iclr_a0, iclr_a2,
            k_k.reshape(1, D_MODEL), k_a.reshape(1, D_MODEL), r_k.reshape(1, D_MODEL),
            lnx_g.reshape(1, D_MODEL), lnx_b.reshape(1, D_MODEL)]
    if s0 is not None:
        in_specs.append(sblk)
        args.append(s0)
    out_specs = [pl.BlockSpec((seq, gw), lambda b, g: (b, g))]
    out_shape = [jax.ShapeDtypeStruct((m, D_MODEL), BF16)]
    if want_state:
        out_specs.append(sblk)
        out_shape.append(jax.ShapeDtypeStruct((nbat, 2, N_HEADS_B, HEAD_DIM_B, HEAD_DIM_B), F32))
    tw = pltpu.VMEM((seq, gw), F32)
    tw2 = pltpu.VMEM((2, seq, gw), F32)
    res = pl.pallas_call(
        functools.partial(_rwkv_kernel, seq=seq, gw=gw, has_s0=s0 is not None, want_state=want_state),
        grid=(nbat, ng),
        in_specs=in_specs,
        out_specs=out_specs,
        out_shape=out_shape,
        scratch_shapes=[tw, tw, tw, tw2, tw2, tw2, tw2, pltpu.VMEM((2, gw // PW, PW, PW), F32)],
        compiler_params=pltpu.CompilerParams(
            dimension_semantics=("arbitrary", "arbitrary"), vmem_limit_bytes=VMEM_LIMIT_BIG),
        name="rwkv_state" if want_state else "rwkv",
    )(*args)
    return res


def _outproj_kernel(oa_ref, ob_ref, x_ref, mod_ref, w_ref, fg_ref, y_ref):
    acc = _nn(oa_ref[...], w_ref[0:D_MODEL, :]) + _nn(ob_ref[...], w_ref[D_MODEL:, :])
    gate = mod_ref[0][:, 2 * D_MODEL:]
    xo = x_ref[...] + gate * acc
    y_ref[...] = xo * lax.rsqrt(jnp.mean(xo * xo, axis=-1, keepdims=True) + NORM_EPS) * fg_ref[...]


def _outproj(o_a, o_b, x, mod3, w_out_bf, final_g, mod_row):
    m = x.shape[0]
    tm = 512
    blk = pl.BlockSpec((tm, D_MODEL), lambda i: (i, 0))
    return pl.pallas_call(
        _outproj_kernel,
        grid=(m // tm,),
        in_specs=[blk, blk, blk,
                  pl.BlockSpec((1, 1, 3 * D_MODEL), lambda i: (mod_row(i * tm), 0, 0)),
                  pl.BlockSpec((2 * D_MODEL, D_MODEL), lambda i: (0, 0)),
                  pl.BlockSpec((1, D_MODEL), lambda i: (0, 0))],
        out_specs=blk,
        out_shape=jax.ShapeDtypeStruct((m, D_MODEL), F32),
        compiler_params=pltpu.CompilerParams(vmem_limit_bytes=VMEM_LIMIT),
        name="outproj",
    )(o_a, o_b, x, mod3, w_out_bf, final_g)


def kernel(x_prompt, x_sample, cache_k, cache_v, state_rwkv, c, c_ctx, norm_g, w_ada, b_ada, w_in, lam_q1, lam_k1, lam_q2, lam_k2, subln_g, shift_mu, decay_w0, decay_w2, iclr_a0, iclr_a2, k_k, k_a, r_k, lnx_g, lnx_b, w_out, final_g):
    batch, seq, d = x_prompt.shape
    dbatch, dseq, _ = x_sample.shape
    past = cache_k.shape[2]
    assert d == D_MODEL and w_in.shape == (1, D_MODEL, D_IN) and dseq % GRID_W == 0

    cvec = jnp.concatenate([c_ctx[None, :], c, jnp.zeros((8 - 1 - dbatch, d), F32)], axis=0)
    mod3 = _mod(cvec, w_ada[0], b_ada[0]).reshape(8, 1, 3 * d)
    lo0 = 4 * D_MODEL + 3 * D_MODEL
    w_bf = w_in[0].astype(BF16)
    w_lora = w_bf[:, lo0:lo0 + LW]
    w_gb = w_bf[:, lo0 + LW:]
    w_out_bf = w_out[0].astype(BF16)
    ctx_row = lambda tok: 0
    lat_row = lambda tok: 1 + tok // dseq
    lp = (shift_mu[0], decay_w0[0], decay_w2[0], iclr_a0[0], iclr_a2[0], k_k[0], k_a[0], r_k[0],
          lnx_g[0], lnx_b[0])
    lam = (lam_q1, lam_k1, lam_q2, lam_k2, subln_g)
    fg = final_g.reshape(1, d)

    xp = x_prompt.reshape(batch * seq, d)
    u_c, ul_c, nk, nv = _inproj(xp, mod3, norm_g, w_bf, w_gb, w_lora, ctx_row, seq, True)
    oa_c = _attn_ctx(u_c, seq, *lam)
    ob_c, s_new = _rwkv(u_c, ul_c, seq, RWKV_GW_CTX, None, True, *lp)
    y_prompt = _outproj(oa_c, ob_c, xp, mod3, w_out_bf, fg, ctx_row).reshape(batch, seq, d)

    xs = x_sample.reshape(dbatch * dseq, d)
    u_l, ul_l = _inproj(xs, mod3, norm_g, w_bf, w_gb, w_lora, lat_row, dseq, False)
    cv = cache_v[:, 0].reshape(dbatch * past, d)
    oa_l = _attn_lat(u_l, cache_k[:, 0], cv, dseq, past, *lam)
    (ob_l,) = _rwkv(u_l, ul_l, dseq, RWKV_GW_LAT, state_rwkv[:, 0], False, *lp)
    y_sample = _outproj(oa_l, ob_l, xs, mod3, w_out_bf, fg, lat_row).reshape(dbatch, dseq, d)

    new_s = s_new.reshape(batch, 1, 2, N_HEADS_B, HEAD_DIM_B, HEAD_DIM_B)
    return (y_prompt, y_sample, nk, nv, new_s)
```

```python
import functools
import math

import jax
import jax.numpy as jnp
from jax import lax
from jax.experimental import pallas as pl
from jax.experimental.pallas import tpu as pltpu

F32 = jnp.float32
BF16 = jnp.bfloat16
HIGHEST = lax.Precision.HIGHEST

D_MODEL = 1024
GRID_W = 64
HEAD_DIM_A = 64
N_HEADS_A = 8
HEAD_DIM_B = 64
N_HEADS_B = 16
LORA_R = 64
D_SHIFT = 3 * D_MODEL + 4 * LORA_R
D_IN = 4 * D_MODEL + D_SHIFT + D_MODEL
ROPE_BASE = 10000.0
NORM_EPS = 1e-6
SUBLN_EPS = 1e-5
LNX_EPS = 64e-5
LAM_INIT = 0.8 - 0.6 * math.exp(-0.3 * 0)
EXP_NEG_HALF = math.exp(-0.5)
LOG2E = math.log2(math.e)

GW = 256
LW = 4 * LORA_R
PW = 128
RWKV_GW_CTX = 1024
RWKV_GW_LAT = 512
CHUNK = 64
BLOCK_T = 256
VMEM_LIMIT = 48 * 1024 * 1024
VMEM_LIMIT_BIG = 56 * 1024 * 1024
SEC_Q, SEC_K, SEC_V, SEC_GA, SEC_R, SEC_KB, SEC_VB, SEC_GB = range(8)


def _nn(a, b):
    return jnp.dot(a, b, preferred_element_type=F32)


def _nt(a, b):
    return lax.dot_general(a, b, (((1,), (1,)), ((), ())), preferred_element_type=F32)


def _tn(a, b):
    return lax.dot_general(a, b, (((0,), (0,)), ((), ())), preferred_element_type=F32)


def _split2(x):
    hi = x.astype(BF16)
    lo = (x - hi.astype(F32)).astype(BF16)
    return hi, lo


def _split3(x):
    hi = x.astype(BF16)
    r1 = x - hi.astype(F32)
    mid = r1.astype(BF16)
    lo = (r1 - mid.astype(F32)).astype(BF16)
    return hi, mid, lo


def _silu(x):
    return x * jax.nn.sigmoid(x)


def _mod_kernel(c_ref, w_ref, b_ref, o_ref):
    s = _silu(c_ref[...])
    o_ref[...] = jnp.dot(s, w_ref[...], precision=HIGHEST, preferred_element_type=F32) + b_ref[...]


def _mod(cvec, w_ada, b_ada):
    n = w_ada.shape[1]
    tn = 512
    return pl.pallas_call(
        _mod_kernel,
        grid=(n // tn,),
        in_specs=[
            pl.BlockSpec((8, D_MODEL), lambda j: (0, 0)),
            pl.BlockSpec((D_MODEL, tn), lambda j: (0, j)),
            pl.BlockSpec((1, tn), lambda j: (0, j)),
        ],
        out_specs=pl.BlockSpec((8, tn), lambda j: (0, j)),
        out_shape=jax.ShapeDtypeStruct((8, n), F32),
        name="mod",
    )(cvec, w_ada, b_ada.reshape(1, n))


def _inproj_kernel(*refs, seq, want_cache):
    x_ref, mod_ref, g_ref, w_ref, wgb_ref, wl_ref, um_ref, ul_ref = refs[:8]
    h_ref = refs[-1]
    j = pl.program_id(1)

    @pl.when(j == 0)
    def _():
        x = x_ref[...]
        y = x * lax.rsqrt(jnp.mean(x * x, axis=-1, keepdims=True) + NORM_EPS) * g_ref[...]
        mod = mod_ref[0]
        shift = mod[:, 0:D_MODEL]
        scale = mod[:, D_MODEL:2 * D_MODEL]
        h = (y * (1.0 + scale) + shift).astype(BF16)
        h_ref[...] = h
        ul_ref[...] = _nn(h, wl_ref[...]).astype(BF16)

    w = jnp.where(j == SEC_GB, wgb_ref[...], w_ref[...])
    acc = _nn(h_ref[...], w)
    um_ref[...] = acc.astype(BF16)
    if want_cache:
        nk_ref, nv_ref = refs[8:10]
        nb = acc.shape[0] // seq

        @pl.when(j == SEC_K)
        def _():
            nk_ref[:, 0] = acc.reshape((nb,) + nk_ref.shape[2:])

        @pl.when(j == SEC_V)
        def _():
            nv_ref[:, 0] = acc.reshape((nb,) + nv_ref.shape[2:])


def _inproj(x, mod3, norm_g, w_bf, w_gb, w_lora, mod_row, seq, want_cache):
    m = x.shape[0]
    tm = 1024
    nsec = SEC_GB + 1
    nb = tm // seq
    out_specs = [pl.BlockSpec((tm, D_MODEL), lambda i, j: (i, j)),
                 pl.BlockSpec((tm, LW), lambda i, j: (i, 0))]
    out_shape = [jax.ShapeDtypeStruct((m, nsec * D_MODEL), BF16), jax.ShapeDtypeStruct((m, LW), BF16)]
    if want_cache:
        out_specs += [pl.BlockSpec((nb, 1, seq, N_HEADS_A, 2, HEAD_DIM_A), lambda i, j: (i, 0, 0, 0, 0, 0)),
                      pl.BlockSpec((nb, 1, seq, N_HEADS_A, 2 * HEAD_DIM_A), lambda i, j: (i, 0, 0, 0, 0))]
        out_shape += [jax.ShapeDtypeStruct((m // seq, 1, seq, N_HEADS_A, 2, HEAD_DIM_A), F32),
                      jax.ShapeDtypeStruct((m // seq, 1, seq, N_HEADS_A, 2 * HEAD_DIM_A), F32)]
    return pl.pallas_call(
        functools.partial(_inproj_kernel, seq=seq, want_cache=want_cache),
        grid=(m // tm, nsec),
        in_specs=[
            pl.BlockSpec((tm, D_MODEL), lambda i, j: (i, 0)),
            pl.BlockSpec((1, 1, 3 * D_MODEL), lambda i, j: (mod_row(i * tm), 0, 0)),
            pl.BlockSpec((1, D_MODEL), lambda i, j: (0, 0)),
            pl.BlockSpec((D_MODEL, D_MODEL), lambda i, j: (0, jnp.minimum(j, SEC_GB - 1))),
            pl.BlockSpec((D_MODEL, D_MODEL), lambda i, j: (0, 0)),
            pl.BlockSpec((D_MODEL, LW), lambda i, j: (0, 0)),
        ],
        out_specs=out_specs,
        out_shape=out_shape,
        scratch_shapes=[pltpu.VMEM((tm, D_MODEL), BF16)],
        compiler_params=pltpu.CompilerParams(
            dimension_semantics=("arbitrary", "arbitrary"), vmem_limit_bytes=VMEM_LIMIT_BIG),
        name="inproj_cache" if want_cache else "inproj",
    )(x, mod3, norm_g, w_bf, w_gb, w_lora)


def _lam(q1_ref, k1_ref, q2_ref, k2_ref):
    s1 = jnp.sum(q1_ref[...] * k1_ref[...], axis=-1, keepdims=True)
    s2 = jnp.sum(q2_ref[...] * k2_ref[...], axis=-1, keepdims=True)
    return jnp.exp(s1) - jnp.exp(s2) + LAM_INIT


def _diff_head(q, kb, vb, g, lam, subg, kct=None):
    lane = lax.broadcasted_iota(jnp.int32, q.shape, 1)
    qs = q * (HEAD_DIM_A ** -0.5 * LOG2E)
    outs = []
    for m in range(2):
        qm = jnp.where((lane >> 6) == m, qs, 0.0).astype(BF16)
        s = _nt(qm, kb)
        if kct is not None:
            s = jnp.concatenate([_nn(qm, kct), s], axis=1)
        p = jnp.exp2(s - jnp.max(s, axis=-1, keepdims=True))
        outs.append(_nn(p.astype(BF16), vb) * (1.0 / jnp.sum(p, axis=-1, keepdims=True)))
    o = outs[0] - lam * outs[1]
    o = o * lax.rsqrt(jnp.mean(o * o, axis=-1, keepdims=True) + SUBLN_EPS) * subg
    o = o * (1.0 - LAM_INIT)
    return o * _silu(g)


def _attn_ctx_kernel(q_ref, k_ref, v_ref, g_ref, q1_ref, k1_ref, q2_ref, k2_ref, sg_ref, o_ref):
    lam = _lam(q1_ref, k1_ref, q2_ref, k2_ref)
    subg = sg_ref[...]
    for h in range(q_ref.shape[1] // 128):
        sl = slice(h * 128, (h + 1) * 128)
        o = _diff_head(q_ref[:, sl].astype(F32), k_ref[:, sl], v_ref[:, sl],
                       g_ref[:, sl].astype(F32), lam, subg)
        o_ref[:, sl] = o.astype(BF16)


def _attn_ctx(um, seq, lam_q1, lam_k1, lam_q2, lam_k2, subln_g):
    m = um.shape[0]
    small = pl.BlockSpec((1, HEAD_DIM_A), lambda b: (0, 0))
    blk = lambda sec: pl.BlockSpec((seq, D_MODEL), lambda b: (b, sec))
    return pl.pallas_call(
        _attn_ctx_kernel,
        grid=(m // seq,),
        in_specs=[blk(SEC_Q), blk(SEC_K), blk(SEC_V), blk(SEC_GA), small, small, small, small,
                  pl.BlockSpec((1, 2 * HEAD_DIM_A), lambda b: (0, 0))],
        out_specs=blk(0),
        out_shape=jax.ShapeDtypeStruct((m, D_MODEL), BF16),
        compiler_params=pltpu.CompilerParams(vmem_limit_bytes=VMEM_LIMIT),
        name="attn_ctx",
    )(um, um, um, um, lam_q1, lam_k1, lam_q2, lam_k2, subln_g)


def _rope(x, cos, s1, s2):
    return x * cos + pltpu.roll(x, 112, 1) * s1 + pltpu.roll(x, 16, 1) * s2


def _attn_lat_kernel(q_ref, k_ref, v_ref, g_ref, ck_ref, cv_ref, cq_ref, s1q_ref, s2q_ref,
                     ckk_ref, s1k_ref, s2k_ref, q1_ref, k1_ref, q2_ref, k2_ref, sg_ref,
                     o_ref, kct_ref, krot_ref, vall_ref):
    past = cv_ref.shape[0]
    w = q_ref.shape[1]
    nh = w // 128

    @pl.when(pl.program_id(2) == 0)
    def _():
        vall_ref[0:past, :] = cv_ref[...].astype(BF16)
        vall_ref[past:, :] = v_ref[...]
        for h in range(nh):
            sl = slice(h * 128, (h + 1) * 128)
            kct_ref[h] = ck_ref[0, h].reshape(128, past).astype(BF16)
            krot_ref[:, sl] = _rope(k_ref[:, sl].astype(F32), ckk_ref[...], s1k_ref[...],
                                    s2k_ref[...]).astype(BF16)

    lam = _lam(q1_ref, k1_ref, q2_ref, k2_ref)
    subg = sg_ref[...]
    for h in range(nh):
        sl = slice(h * 128, (h + 1) * 128)
        q = _rope(q_ref[:, sl].astype(F32), cq_ref[...], s1q_ref[...], s2q_ref[...])
        o = _diff_head(q, krot_ref[:, sl], vall_ref[:, sl], g_ref[:, sl].astype(F32), lam, subg,
                       kct=kct_ref[h])
        o_ref[:, sl] = o.astype(BF16)


def _rope_tables(t):
    pos = jnp.arange(t)
    row = (pos // GRID_W).astype(F32)
    col = (pos % GRID_W).astype(F32)
    n_freq = HEAD_DIM_A // 4
    lane = jnp.arange(128)
    d = lane % HEAD_DIM_A
    use_col = (d // 32) == 1
    second = ((d % 32) // 16) == 1
    inv = ROPE_BASE ** (-(d % n_freq).astype(F32) / n_freq)
    ang = jnp.where(use_col[None, :], col[:, None], row[:, None]) * inv[None, :]
    cos = jnp.cos(ang)
    sin = jnp.sin(ang)
    s1 = jnp.where(second[None, :], 0.0, -sin)
    s2 = jnp.where(second[None, :], sin, 0.0)
    return cos, s1, s2


def _attn_lat(u, cache_kt, cache_v2, seq, past, lam_q1, lam_k1, lam_q2, lam_k2, subln_g):
    m = u.shape[0]
    nbat = m // seq
    w = GW
    ng = D_MODEL // w
    tq = 256
    nq = seq // tq
    cos, s1, s2 = _rope_tables(seq)
    small = pl.BlockSpec((1, HEAD_DIM_A), lambda b, g, i: (0, 0))
    qblk = lambda sec: pl.BlockSpec((tq, w), lambda b, g, i: (b * nq + i, sec * ng + g))
    kblk = lambda sec: pl.BlockSpec((seq, w), lambda b, g, i: (b, sec * ng + g))
    ckblk = pl.BlockSpec((1, w // 128, 2, HEAD_DIM_A, past), lambda b, g, i: (b, g, 0, 0, 0))
    cblk = pl.BlockSpec((past, w), lambda b, g, i: (b, g))
    tq_blk = pl.BlockSpec((tq, 128), lambda b, g, i: (i, 0))
    tk_blk = pl.BlockSpec((seq, 128), lambda b, g, i: (0, 0))
    return pl.pallas_call(
        _attn_lat_kernel,
        grid=(nbat, ng, nq),
        in_specs=[qblk(SEC_Q), kblk(SEC_K), kblk(SEC_V), qblk(SEC_GA), ckblk, cblk,
                  tq_blk, tq_blk, tq_blk, tk_blk, tk_blk, tk_blk,
                  small, small, small, small,
                  pl.BlockSpec((1, 2 * HEAD_DIM_A), lambda b, g, i: (0, 0))],
        out_specs=pl.BlockSpec((tq, w), lambda b, g, i: (b * nq + i, g)),
        out_shape=jax.ShapeDtypeStruct((m, D_MODEL), BF16),
        scratch_shapes=[pltpu.VMEM((w // 128, 128, past), BF16), pltpu.VMEM((seq, w), BF16),
                        pltpu.VMEM((past + seq, w), BF16)],
        compiler_params=pltpu.CompilerParams(
            dimension_semantics=("arbitrary", "arbitrary", "arbitrary"), vmem_limit_bytes=VMEM_LIMIT),
        name="attn_lat",
    )(u, u, u, u, cache_kt, cache_v2, cos, s1, s2, cos, s1, s2,
      lam_q1, lam_k1, lam_q2, lam_k2, subln_g)


def _rwkv_consts():
    ri = lax.broadcasted_iota(jnp.int32, (PW, PW), 0)
    ci = lax.broadcasted_iota(jnp.int32, (PW, PW), 1)
    t = lax.broadcasted_iota(jnp.int32, (CHUNK, PW), 0)
    lane = lax.broadcasted_iota(jnp.int32, (CHUNK, PW), 1)
    j = lane & (CHUNK - 1)
    ti = lax.broadcasted_iota(jnp.int32, (CHUNK, CHUNK), 0)
    tj = lax.broadcasted_iota(jnp.int32, (CHUNK, CHUNK), 1)
    return dict(
        bmask=(ri >> 6) == (ci >> 6),
        eye=ri == ci,
        head0=lane < HEAD_DIM_B,
        eye_sbs=jnp.where(t == j, 1.0, 0.0),
        strict=(j < t, j > t),
        incl=(j <= t, j >= t),
        off=[((t >> (n + 1)) == (j >> (n + 1))) & ((t >> n) != (j >> n)) for n in range(6)],
        tri=(jnp.where(tj <= ti, 1.0, 0.0).astype(BF16), jnp.where(tj >= ti, 1.0, 0.0).astype(BF16)),
    )


def _wbd(x, cst):
    xb = x.astype(BF16)
    z = jnp.zeros_like(xb)
    return jnp.concatenate([jnp.where(cst["head0"], xb, z), jnp.where(cst["head0"], z, xb)], axis=0)


def _group_sum(x, e):
    hi, lo = _split2(x)
    ew = e.shape[0]
    parts = [_nn(hi[:, o:o + ew], e) + _nn(lo[:, o:o + ew], e) for o in range(0, x.shape[1], ew)]
    return parts[0] if len(parts) == 1 else jnp.concatenate(parts, axis=1)


def _pair_pipeline(h, ops, lg, states, cst, same_rows, out):
    L = CHUNK
    ncb = len(ops[0])
    idx = [(d, c) for d in range(2) for c in range(ncb)]
    half = lambda a: a[:, h * PW:(h + 1) * PW]

    ar, wbk, khbh, dg = {}, {}, {}, {}
    for d, c in idx:
        lw, r, kd, v, kk, b = (half(a) for a in ops[d][c])
        g = half(lg[d, c])
        g_l = g[0:1, :] if d == 1 else g[L - 1:L, :]
        at = -(kk * jnp.exp2(g - lw))
        rt = r * jnp.exp2(g)
        e_inv = jnp.exp2(-g)
        eg = jnp.exp2(g_l)
        bt, kt = b * e_inv, kd * e_inv
        ar[d, c] = jnp.concatenate([at, rt], axis=0).astype(BF16)
        wbk[d, c] = jnp.concatenate([_wbd(bt, cst), _wbd(kt, cst)], axis=0)
        khbh[d, c] = jnp.concatenate([(kt * eg).astype(BF16), (bt * eg).astype(BF16)], axis=0)
        dg[d, c] = jnp.where(cst["eye"], eg, 0.0).astype(BF16)
    yield

    gbk = {i: _nt(ar[i], wbk[i]) for i in idx}
    yield
    m_ab = {i: jnp.where(cst["strict"][i[0]], gbk[i][:L, :PW], 0.0) for i in idx}
    m_rb = {i: jnp.where(cst["incl"][i[0]], gbk[i][L:, :PW], 0.0).astype(BF16) for i in idx}
    m_akrk = {i: jnp.concatenate([jnp.where(cst["strict"][i[0]], gbk[i][:L, PW:], 0.0),
                                  jnp.where(cst["incl"][i[0]], gbk[i][L:, PW:], 0.0)], axis=0).astype(BF16)
              for i in idx}

    x = {i: cst["eye_sbs"] + jnp.where(cst["off"][0], m_ab[i], 0.0) for i in idx}
    for off in cst["off"][1:]:
        p = {i: _nn(jnp.where(off, m_ab[i], 0.0).astype(BF16), _wbd(x[i], cst)) for i in idx}
        yield
        x = {i: x[i] + _nn(x[i].astype(BF16), _wbd(p[i], cst)) for i in idx}
        yield
    xb = {i: x[i].astype(BF16) for i in idx}

    vsrc = {(d, c): (0, c) if same_rows else (d, c) for d, c in idx}
    vhalf = {i: half(ops[i[0]][i[1]][3]) for i in dict.fromkeys(vsrc.values())}
    wv = {i: _wbd(vhalf[i], cst) for i in vhalf}
    vb = {i: vhalf[i].astype(BF16) for i in vhalf}
    lk = {i: jnp.concatenate([jnp.concatenate([ar[i], m_akrk[i]], axis=1),
                              jnp.concatenate([dg[i], jnp.zeros_like(dg[i])], axis=1)], axis=0)
          for i in idx}

    s = [states[0][h], states[1][h]]
    for step in range(ncb):
        cur = [(0, step), (1, ncb - 1 - step)]
        ars = {i: _nn(lk[i], jnp.concatenate([s[i[0]].astype(BF16), wv[vsrc[i]]], axis=0)) for i in cur}
        yield
        u = {i: _nn(xb[i], _wbd(ars[i][:L], cst)) for i in cur}
        yield
        for i in cur:
            d, c = i
            out["y"][d, c, h] = ars[i][L:2 * L] + _nn(m_rb[i], _wbd(u[i], cst))
            upd = _tn(khbh[i], jnp.concatenate([vb[vsrc[i]], u[i].astype(BF16)], axis=0))
            s[d] = ars[i][2 * L:] + jnp.where(cst["bmask"], upd, 0.0)
        yield
    out["s"][0][h], out["s"][1][h] = s


def _block_step(ops, states, cst, same_rows):
    ncb = len(ops[0])
    npair = ops[0][0][0].shape[1] // PW

    lg = {}
    for d in range(2):
        for c in range(ncb):
            hi, mid, lo = _split3(ops[d][c][0])
            tri = cst["tri"][d]
            lg[d, c] = _nn(tri, hi) + _nn(tri, mid) + _nn(tri, lo)

    out = {"y": {}, "s": [[None] * npair, [None] * npair]}
    pending = [_pair_pipeline(h, ops, lg, states, cst, same_rows, out) for h in range(npair)]
    active = pending
    while active:
        active = [gen for gen in active if next(gen, "done") != "done"]
    ys = [[jnp.concatenate([out["y"][d, c, h] for h in range(npair)], axis=1) if npair > 1
           else out["y"][d, c, 0] for c in range(ncb)] for d in range(2)]
    return ys, out["s"]


def _rwkv_kernel(*refs, seq, gw, has_s0, want_state):
    it = iter(refs)
    r_ref, k_ref, v_ref, lo_ref, gb_ref = (next(it) for _ in range(5))
    mur_ref, muk_ref, muv_ref, mul_ref = (next(it) for _ in range(4))
    w0_ref, w2_ref, a0_ref, a2_ref = (next(it) for _ in range(4))
    kk_ref, ka_ref, rk_ref, lng_ref, lnb_ref = (next(it) for _ in range(5))
    s0_ref = next(it) if has_s0 else None
    ob_ref = next(it)
    sf_ref = next(it) if want_state else None
    r_s, v_s, kn_s, lw_s, kd_s, b_s, y_s, st_s = (next(it) for _ in range(8))

    T = seq
    cst = _rwkv_consts()
    npair = gw // PW
    hpp = PW // HEAD_DIM_B
    gi = lax.broadcasted_iota(jnp.int32, (LW, LW), 0)
    gj = lax.broadcasted_iota(jnp.int32, (LW, LW), 1)
    e = jnp.where((gi >> 6) == (gj >> 6), 1.0, 0.0).astype(BF16)
    rows_of = {w: lax.broadcasted_iota(jnp.int32, (T, w), 0) for w in {gw, LW}}

    def shifted(ref, mu_ref):
        x = ref[...].astype(F32)
        mu = mu_ref[...]
        row = rows_of[x.shape[1]]
        prev = jnp.where(row == 0, 0.0, pltpu.roll(x, 1, 0))
        nxt = jnp.where(row == T - 1, 0.0, pltpu.roll(x, T - 1, 0))
        return x + mu[0:1, :] * (prev - x) + mu[1:2, :] * (nxt - x)

    r = shifted(r_ref, mur_ref)
    k = shifted(k_ref, muk_ref)
    v = shifted(v_ref, muv_ref)
    lo = shifted(lo_ref, mul_ref)
    r_s[...] = r
    v_s[...] = v
    kk = k * kk_ref[...]
    kk = kk * lax.rsqrt(_group_sum(kk * kk, e) + 1e-12)
    kn_s[...] = kk
    bonus_dot = _group_sum(r * k * rk_ref[...], e)

    tanh_lo = jnp.tanh(lo).astype(BF16)
    lo_b = lo.astype(BF16)

    def padded(w, before):
        parts = [jnp.zeros((before, gw), F32)] if before else []
        parts.append(w)
        after = LW - before - LORA_R
        if after:
            parts.append(jnp.zeros((after, gw), F32))
        return jnp.concatenate(parts, axis=0).astype(BF16)

    for d in range(2):
        w2p = padded(w2_ref[d], LORA_R * d)
        a2p = padded(a2_ref[d], LORA_R * (2 + d))
        z = w0_ref[d:d + 1, :] + _nn(tanh_lo, w2p)
        lw_s[d] = -(EXP_NEG_HALF * LOG2E) * jax.nn.sigmoid(z)
        a = jax.nn.sigmoid(a0_ref[d:d + 1, :] + _nn(lo_b, a2p))
        kd_s[d] = k * (1.0 + (a - 1.0) * ka_ref[...])
        b_s[d] = kk * a

    if has_s0:
        ident = jnp.where(cst["eye"], 1.0, 0.0).astype(BF16)
        for d in range(2):
            for h in range(npair):
                z0 = s0_ref[0, d, h * hpp:(h + 1) * hpp].reshape(PW, HEAD_DIM_B)
                hi, lo2 = _split2(z0)
                zt = _tn(hi, ident) + _tn(lo2, ident)
                st_s[d, h] = jnp.where(cst["bmask"], jnp.concatenate([zt] * hpp, axis=0), 0.0)
    else:
        st_s[...] = jnp.zeros(st_s.shape, F32)

    nblk = T // BLOCK_T
    ncb = BLOCK_T // CHUNK

    def block(i, carry):
        rows = []
        for d in range(2):
            blk = i if d == 0 else nblk - 1 - i
            rows.append([])
            for c in range(ncb):
                start = blk * BLOCK_T + c * CHUNK
                if not isinstance(start, int):
                    start = pl.multiple_of(start, CHUNK)
                rows[d].append(pl.ds(start, CHUNK))
        ops = [[(lw_s[d, rw, :], r_s[rw, :], kd_s[d, rw, :], v_s[rw, :], kn_s[rw, :], b_s[d, rw, :])
                for rw in rows[d]] for d in range(2)]
        states = [[st_s[d, h] for h in range(npair)] for d in range(2)]
        ys, s_new = _block_step(ops, states, cst, nblk == 1)
        for d in range(2):
            for c in range(ncb):
                y_s[d, rows[d][c], :] = ys[d][c]
            for h in range(npair):
                st_s[d, h] = s_new[d][h]
        return carry

    if nblk == 1:
        block(0, 0)
    else:
        lax.fori_loop(0, nblk, block, 0)

    y = y_s[0] + y_s[1]
    mu = _group_sum(y, e) * (1.0 / HEAD_DIM_B)
    dlt = y - mu
    var = _group_sum(dlt * dlt, e) * (1.0 / HEAD_DIM_B)
    yn = dlt * lax.rsqrt(var + LNX_EPS) * lng_ref[...] + lnb_ref[...]
    out = (yn + bonus_dot * v_s[...]) * _silu(gb_ref[...].astype(F32))
    ob_ref[...] = out.astype(BF16)

    if want_state:
        ri = lax.broadcasted_iota(jnp.int32, (PW, HEAD_DIM_B), 0)
        ci = lax.broadcasted_iota(jnp.int32, (PW, HEAD_DIM_B), 1)
        fold = jnp.where((ri & (HEAD_DIM_B - 1)) == ci, 1.0, 0.0).astype(BF16)
        for d in range(2):
            for h in range(npair):
                hi, lo2 = _split2(st_s[d, h])
                sf = _tn(hi, fold) + _tn(lo2, fold)
                sf_ref[0, d, h * hpp:(h + 1) * hpp] = sf.reshape(hpp, HEAD_DIM_B, HEAD_DIM_B)


def _rwkv(u, ul, seq, gw, s0, want_state, shift_mu, decay_w0, decay_w2, iclr_a0, iclr_a2,
          k_k, k_a, r_k, lnx_g, lnx_b):
    m = u.shape[0]
    nbat = m // seq
    ng = D_MODEL // gw
    hg = gw // HEAD_DIM_B
    ublk = lambda sec: pl.BlockSpec((seq, gw), lambda b, g: (b, sec * ng + g))
    lblk = pl.BlockSpec((seq, LW), lambda b, g: (b, 0))
    mublk = lambda sec: pl.BlockSpec((2, gw), lambda b, g: (0, sec * ng + g))
    mulblk = pl.BlockSpec((2, LW), lambda b, g: (0, 3 * D_MODEL // LW))
    vec2 = pl.BlockSpec((2, gw), lambda b, g: (0, g))
    mat2 = pl.BlockSpec((2, LORA_R, gw), lambda b, g: (0, 0, g))
    vec1 = pl.BlockSpec((1, gw), lambda b, g: (0, g))
    sblk = pl.BlockSpec((1, 2, hg, HEAD_DIM_B, HEAD_DIM_B), lambda b, g: (b, 0, g, 0, 0))
    in_specs = [ublk(SEC_R), ublk(SEC_KB), ublk(SEC_VB), lblk, ublk(SEC_GB),
                mublk(0), mublk(1), mublk(2), mulblk,
                vec2, mat2, vec2, mat2, vec1, vec1, vec1, vec1, vec1]
    args = [u, u, u, ul, u, shift_mu, shift_mu, shift_mu, shift_mu,
            decay_w0, decay_w2, iclr_a0, iclr_a2,
            k_k.reshape(1, D_MODEL), k_a.reshape(1, D_MODEL), r_k.reshape(1, D_MODEL),
            lnx_g.reshape(1, D_MODEL), lnx_b.reshape(1, D_MODEL)]
    if s0 is not None:
        in_specs.append(sblk)
        args.append(s0)
    out_specs = [pl.BlockSpec((seq, gw), lambda b, g: (b, g))]
    out_shape = [jax.ShapeDtypeStruct((m, D_MODEL), BF16)]
    if want_state:
        out_specs.append(sblk)
        out_shape.append(jax.ShapeDtypeStruct((nbat, 2, N_HEADS_B, HEAD_DIM_B, HEAD_DIM_B), F32))
    tw = pltpu.VMEM((seq, gw), F32)
    tw2 = pltpu.VMEM((2, seq, gw), F32)
    res = pl.pallas_call(
        functools.partial(_rwkv_kernel, seq=seq, gw=gw, has_s0=s0 is not None, want_state=want_state),
        grid=(nbat, ng),
        in_specs=in_specs,
        out_specs=out_specs,
        out_shape=out_shape,
        scratch_shapes=[tw, tw, tw, tw2, tw2, tw2, tw2, pltpu.VMEM((2, gw // PW, PW, PW), F32)],
        compiler_params=pltpu.CompilerParams(
            dimension_semantics=("arbitrary", "arbitrary"), vmem_limit_bytes=VMEM_LIMIT_BIG),
        name="rwkv_state" if want_state else "rwkv",
    )(*args)
    return res


def _outproj_kernel(oa_ref, ob_ref, x_ref, mod_ref, w_ref, fg_ref, y_ref):
    acc = _nn(oa_ref[...], w_ref[0:D_MODEL, :]) + _nn(ob_ref[...], w_ref[D_MODEL:, :])
    gate = mod_ref[0][:, 2 * D_MODEL:]
    xo = x_ref[...] + gate * acc
    y_ref[...] = xo * lax.rsqrt(jnp.mean(xo * xo, axis=-1, keepdims=True) + NORM_EPS) * fg_ref[...]


def _outproj(o_a, o_b, x, mod3, w_out_bf, final_g, mod_row):
    m = x.shape[0]
    tm = 512
    blk = pl.BlockSpec((tm, D_MODEL), lambda i: (i, 0))
    return pl.pallas_call(
        _outproj_kernel,
        grid=(m // tm,),
        in_specs=[blk, blk, blk,
                  pl.BlockSpec((1, 1, 3 * D_MODEL), lambda i: (mod_row(i * tm), 0, 0)),
                  pl.BlockSpec((2 * D_MODEL, D_MODEL), lambda i: (0, 0)),
                  pl.BlockSpec((1, D_MODEL), lambda i: (0, 0))],
        out_specs=blk,
        out_shape=jax.ShapeDtypeStruct((m, D_MODEL), F32),
        compiler_params=pltpu.CompilerParams(vmem_limit_bytes=VMEM_LIMIT),
        name="outproj",
    )(o_a, o_b, x, mod3, w_out_bf, final_g)


def kernel(x_prompt, x_sample, cache_k, cache_v, state_rwkv, c, c_ctx, norm_g, w_ada, b_ada, w_in, lam_q1, lam_k1, lam_q2, lam_k2, subln_g, shift_mu, decay_w0, decay_w2, iclr_a0, iclr_a2, k_k, k_a, r_k, lnx_g, lnx_b, w_out, final_g):
    batch, seq, d = x_prompt.shape
    dbatch, dseq, _ = x_sample.shape
    past = cache_k.shape[2]
    assert d == D_MODEL and w_in.shape == (1, D_MODEL, D_IN) and dseq % GRID_W == 0

    cvec = jnp.concatenate([c_ctx[None, :], c, jnp.zeros((8 - 1 - dbatch, d), F32)], axis=0)
    mod3 = _mod(cvec, w_ada[0], b_ada[0]).reshape(8, 1, 3 * d)
    lo0 = 4 * D_MODEL + 3 * D_MODEL
    w_bf = w_in[0].astype(BF16)
    w_lora = w_bf[:, lo0:lo0 + LW]
    w_gb = w_bf[:, lo0 + LW:]
    w_out_bf = w_out[0].astype(BF16)
    ctx_row = lambda tok: 0
    lat_row = lambda tok: 1 + tok // dseq
    lp = (shift_mu[0], decay_w0[0], decay_w2[0], iclr_a0[0], iclr_a2[0], k_k[0], k_a[0], r_k[0],
          lnx_g[0], lnx_b[0])
    lam = (lam_q1, lam_k1, lam_q2, lam_k2, subln_g)
    fg = final_g.reshape(1, d)

    xp = x_prompt.reshape(batch * seq, d)
    u_c, ul_c, nk, nv = _inproj(xp, mod3, norm_g, w_bf, w_gb, w_lora, ctx_row, seq, True)
    oa_c = _attn_ctx(u_c, seq, *lam)
    ob_c, s_new = _rwkv(u_c, ul_c, seq, RWKV_GW_CTX, None, True, *lp)
    y_prompt = _outproj(oa_c, ob_c, xp, mod3, w_out_bf, fg, ctx_row).reshape(batch, seq, d)

    xs = x_sample.reshape(dbatch * dseq, d)
    u_l, ul_l = _inproj(xs, mod3, norm_g, w_bf, w_gb, w_lora, lat_row, dseq, False)
    cv = cache_v[:, 0].reshape(dbatch * past, d)
    ckt = jnp.transpose(cache_k[:, 0], (0, 2, 3, 4, 1))
    oa_l = _attn_lat(u_l, ckt, cv, dseq, past, *lam)
    (ob_l,) = _rwkv(u_l, ul_l, dseq, RWKV_GW_LAT, state_rwkv[:, 0], False, *lp)
    y_sample = _outproj(oa_l, ob_l, xs, mod3, w_out_bf, fg, lat_row).reshape(dbatch, dseq, d)

    new_s = s_new.reshape(batch, 1, 2, N_HEADS_B, HEAD_DIM_B, HEAD_DIM_B)
    return (y_prompt, y_sample, nk, nv, new_s)
```

```python
import functools
import math

import jax
import jax.numpy as jnp
from jax import lax
from jax.experimental import pallas as pl
from jax.experimental.pallas import tpu as pltpu

F32 = jnp.float32
BF16 = jnp.bfloat16
HIGHEST = lax.Precision.HIGHEST

D_MODEL = 1024
GRID_W = 64
HEAD_DIM_A = 64
N_HEADS_A = 8
HEAD_DIM_B = 64
N_HEADS_B = 16
LORA_R = 64
D_SHIFT = 3 * D_MODEL + 4 * LORA_R
D_IN = 4 * D_MODEL + D_SHIFT + D_MODEL
ROPE_BASE = 10000.0
NORM_EPS = 1e-6
SUBLN_EPS = 1e-5
LNX_EPS = 64e-5
LAM_INIT = 0.8 - 0.6 * math.exp(-0.3 * 0)
EXP_NEG_HALF = math.exp(-0.5)
LOG2E = math.log2(math.e)

GW = 256
LW = 4 * LORA_R
PW = 128
RWKV_GW_CTX = 1024
RWKV_GW_LAT = 512
CHUNK = 64
BLOCK_T = 256
VMEM_LIMIT = 48 * 1024 * 1024
VMEM_LIMIT_BIG = 56 * 1024 * 1024
SEC_Q, SEC_K, SEC_V, SEC_GA, SEC_R, SEC_KB, SEC_VB, SEC_GB = range(8)


def _nn(a, b):
    return jnp.dot(a, b, preferred_element_type=F32)


def _nt(a, b):
    return lax.dot_general(a, b, (((1,), (1,)), ((), ())), preferred_element_type=F32)


def _tn(a, b):
    return lax.dot_general(a, b, (((0,), (0,)), ((), ())), preferred_element_type=F32)


def _split2(x):
    hi = x.astype(BF16)
    lo = (x - hi.astype(F32)).astype(BF16)
    return hi, lo


def _silu(x):
    return x * jax.nn.sigmoid(x)


def _mod_kernel(c_ref, w_ref, b_ref, o_ref):
    s = _silu(c_ref[...])
    o_ref[...] = jnp.dot(s, w_ref[...], precision=HIGHEST, preferred_element_type=F32) + b_ref[...]


def _mod(cvec, w_ada, b_ada):
    n = w_ada.shape[1]
    tn = 512
    return pl.pallas_call(
        _mod_kernel,
        grid=(n // tn,),
        in_specs=[
            pl.BlockSpec((8, D_MODEL), lambda j: (0, 0)),
            pl.BlockSpec((D_MODEL, tn), lambda j: (0, j)),
            pl.BlockSpec((1, tn), lambda j: (0, j)),
        ],
        out_specs=pl.BlockSpec((8, tn), lambda j: (0, j)),
        out_shape=jax.ShapeDtypeStruct((8, n), F32),
        name="mod",
    )(cvec, w_ada, b_ada.reshape(1, n))


def _inproj_kernel(*refs, seq, want_cache):
    x_ref, mod_ref, g_ref, w_ref, wgb_ref, wl_ref, um_ref, ul_ref = refs[:8]
    h_ref = refs[-1]
    j = pl.program_id(1)

    @pl.when(j == 0)
    def _():
        x = x_ref[...]
        y = x * lax.rsqrt(jnp.mean(x * x, axis=-1, keepdims=True) + NORM_EPS) * g_ref[...]
        mod = mod_ref[0]
        shift = mod[:, 0:D_MODEL]
        scale = mod[:, D_MODEL:2 * D_MODEL]
        h = (y * (1.0 + scale) + shift).astype(BF16)
        h_ref[...] = h
        ul_ref[...] = _nn(h, wl_ref[...]).astype(BF16)

    w = jnp.where(j == SEC_GB, wgb_ref[...], w_ref[...])
    acc = _nn(h_ref[...], w)
    um_ref[...] = acc.astype(BF16)
    if want_cache:
        nk_ref, nv_ref = refs[8:10]
        nb = acc.shape[0] // seq

        @pl.when(j == SEC_K)
        def _():
            nk_ref[:, 0] = acc.reshape((nb,) + nk_ref.shape[2:])

        @pl.when(j == SEC_V)
        def _():
            nv_ref[:, 0] = acc.reshape((nb,) + nv_ref.shape[2:])


def _inproj(x, mod3, norm_g, w_bf, w_gb, w_lora, mod_row, seq, want_cache):
    m = x.shape[0]
    tm = 1024
    nsec = SEC_GB + 1
    nb = tm // seq
    out_specs = [pl.BlockSpec((tm, D_MODEL), lambda i, j: (i, j)),
                 pl.BlockSpec((tm, LW), lambda i, j: (i, 0))]
    out_shape = [jax.ShapeDtypeStruct((m, nsec * D_MODEL), BF16), jax.ShapeDtypeStruct((m, LW), BF16)]
    if want_cache:
        out_specs += [pl.BlockSpec((nb, 1, seq, N_HEADS_A, 2, HEAD_DIM_A), lambda i, j: (i, 0, 0, 0, 0, 0)),
                      pl.BlockSpec((nb, 1, seq, N_HEADS_A, 2 * HEAD_DIM_A), lambda i, j: (i, 0, 0, 0, 0))]
        out_shape += [jax.ShapeDtypeStruct((m // seq, 1, seq, N_HEADS_A, 2, HEAD_DIM_A), F32),
                      jax.ShapeDtypeStruct((m // seq, 1, seq, N_HEADS_A, 2 * HEAD_DIM_A), F32)]
    return pl.pallas_call(
        functools.partial(_inproj_kernel, seq=seq, want_cache=want_cache),
        grid=(m // tm, nsec),
        in_specs=[
            pl.BlockSpec((tm, D_MODEL), lambda i, j: (i, 0)),
            pl.BlockSpec((1, 1, 3 * D_MODEL), lambda i, j: (mod_row(i * tm), 0, 0)),
            pl.BlockSpec((1, D_MODEL), lambda i, j: (0, 0)),
            pl.BlockSpec((D_MODEL, D_MODEL), lambda i, j: (0, jnp.minimum(j, SEC_GB - 1))),
            pl.BlockSpec((D_MODEL, D_MODEL), lambda i, j: (0, 0)),
            pl.BlockSpec((D_MODEL, LW), lambda i, j: (0, 0)),
        ],
        out_specs=out_specs,
        out_shape=out_shape,
        scratch_shapes=[pltpu.VMEM((tm, D_MODEL), BF16)],
        compiler_params=pltpu.CompilerParams(
            dimension_semantics=("arbitrary", "arbitrary"), vmem_limit_bytes=VMEM_LIMIT_BIG),
        name="inproj_cache" if want_cache else "inproj",
    )(x, mod3, norm_g, w_bf, w_gb, w_lora)


def _lam(q1_ref, k1_ref, q2_ref, k2_ref):
    s1 = jnp.sum(q1_ref[...] * k1_ref[...], axis=-1, keepdims=True)
    s2 = jnp.sum(q2_ref[...] * k2_ref[...], axis=-1, keepdims=True)
    return jnp.exp(s1) - jnp.exp(s2) + LAM_INIT


def _diff_head(q, kb, vb, g, lam, subg, kct=None):
    lane = lax.broadcasted_iota(jnp.int32, q.shape, 1)
    qs = q * (HEAD_DIM_A ** -0.5 * LOG2E)
    v1 = jnp.concatenate([vb, jnp.ones_like(vb)], axis=1)
    outs = []
    for m in range(2):
        qm = jnp.where((lane >> 6) == m, qs, 0.0).astype(BF16)
        s = _nt(qm, kb)
        if kct is not None:
            s = jnp.concatenate([_nn(qm, kct), s], axis=1)
        p = jnp.exp2(s - jnp.max(s, axis=-1, keepdims=True)).astype(BF16)
        pv = _nn(p, v1)
        outs.append(pv[:, :128] / pv[:, 128:])
    o = outs[0] - lam * outs[1]
    o = o * lax.rsqrt(jnp.mean(o * o, axis=-1, keepdims=True) + SUBLN_EPS) * subg
    o = o * (1.0 - LAM_INIT)
    return o * _silu(g)


def _attn_ctx_kernel(q_ref, k_ref, v_ref, g_ref, q1_ref, k1_ref, q2_ref, k2_ref, sg_ref, o_ref):
    lam = _lam(q1_ref, k1_ref, q2_ref, k2_ref)
    subg = sg_ref[...]
    for h in range(q_ref.shape[1] // 128):
        sl = slice(h * 128, (h + 1) * 128)
        o = _diff_head(q_ref[:, sl].astype(F32), k_ref[:, sl], v_ref[:, sl],
                       g_ref[:, sl].astype(F32), lam, subg)
        o_ref[:, sl] = o.astype(BF16)


def _attn_ctx(um, seq, lam_q1, lam_k1, lam_q2, lam_k2, subln_g):
    m = um.shape[0]
    small = pl.BlockSpec((1, HEAD_DIM_A), lambda b: (0, 0))
    blk = lambda sec: pl.BlockSpec((seq, D_MODEL), lambda b: (b, sec))
    return pl.pallas_call(
        _attn_ctx_kernel,
        grid=(m // seq,),
        in_specs=[blk(SEC_Q), blk(SEC_K), blk(SEC_V), blk(SEC_GA), small, small, small, small,
                  pl.BlockSpec((1, 2 * HEAD_DIM_A), lambda b: (0, 0))],
        out_specs=blk(0),
        out_shape=jax.ShapeDtypeStruct((m, D_MODEL), BF16),
        compiler_params=pltpu.CompilerParams(vmem_limit_bytes=VMEM_LIMIT),
        name="attn_ctx",
    )(um, um, um, um, lam_q1, lam_k1, lam_q2, lam_k2, subln_g)


def _rope(x, cos, s1, s2):
    return x * cos + pltpu.roll(x, 112, 1) * s1 + pltpu.roll(x, 16, 1) * s2


def _attn_lat_kernel(q_ref, k_ref, v_ref, g_ref, ck_ref, cv_ref, cq_ref, s1q_ref, s2q_ref,
                     ckk_ref, s1k_ref, s2k_ref, q1_ref, k1_ref, q2_ref, k2_ref, sg_ref,
                     o_ref, kct_ref, krot_ref, vall_ref):
    past = cv_ref.shape[0]
    w = q_ref.shape[1]
    nh = w // 128

    @pl.when(pl.program_id(2) == 0)
    def _():
        vall_ref[0:past, :] = cv_ref[...].astype(BF16)
        vall_ref[past:, :] = v_ref[...]
        for h in range(nh):
            sl = slice(h * 128, (h + 1) * 128)
            kct_ref[h] = ck_ref[0, h].reshape(128, past).astype(BF16)
            krot_ref[:, sl] = _rope(k_ref[:, sl].astype(F32), ckk_ref[...], s1k_ref[...],
                                    s2k_ref[...]).astype(BF16)

    lam = _lam(q1_ref, k1_ref, q2_ref, k2_ref)
    subg = sg_ref[...]
    for h in range(nh):
        sl = slice(h * 128, (h + 1) * 128)
        q = _rope(q_ref[:, sl].astype(F32), cq_ref[...], s1q_ref[...], s2q_ref[...])
        o = _diff_head(q, krot_ref[:, sl], vall_ref[:, sl], g_ref[:, sl].astype(F32), lam, subg,
                       kct=kct_ref[h])
        o_ref[:, sl] = o.astype(BF16)


def _rope_tables(t):
    pos = jnp.arange(t)
    row = (pos // GRID_W).astype(F32)
    col = (pos % GRID_W).astype(F32)
    n_freq = HEAD_DIM_A // 4
    lane = jnp.arange(128)
    d = lane % HEAD_DIM_A
    use_col = (d // 32) == 1
    second = ((d % 32) // 16) == 1
    inv = ROPE_BASE ** (-(d % n_freq).astype(F32) / n_freq)
    ang = jnp.where(use_col[None, :], col[:, None], row[:, None]) * inv[None, :]
    cos = jnp.cos(ang)
    sin = jnp.sin(ang)
    s1 = jnp.where(second[None, :], 0.0, -sin)
    s2 = jnp.where(second[None, :], sin, 0.0)
    return cos, s1, s2


def _attn_lat(u, cache_kt, cache_v2, seq, past, lam_q1, lam_k1, lam_q2, lam_k2, subln_g):
    m = u.shape[0]
    nbat = m // seq
    w = GW
    ng = D_MODEL // w
    tq = 256
    nq = seq // tq
    cos, s1, s2 = _rope_tables(seq)
    small = pl.BlockSpec((1, HEAD_DIM_A), lambda b, g, i: (0, 0))
    qblk = lambda sec: pl.BlockSpec((tq, w), lambda b, g, i: (b * nq + i, sec * ng + g))
    kblk = lambda sec: pl.BlockSpec((seq, w), lambda b, g, i: (b, sec * ng + g))
    ckblk = pl.BlockSpec((1, w // 128, 2, HEAD_DIM_A, past), lambda b, g, i: (b, g, 0, 0, 0))
    cblk = pl.BlockSpec((past, w), lambda b, g, i: (b, g))
    tq_blk = pl.BlockSpec((tq, 128), lambda b, g, i: (i, 0))
    tk_blk = pl.BlockSpec((seq, 128), lambda b, g, i: (0, 0))
    return pl.pallas_call(
        _attn_lat_kernel,
        grid=(nbat, ng, nq),
        in_specs=[qblk(SEC_Q), kblk(SEC_K), kblk(SEC_V), qblk(SEC_GA), ckblk, cblk,
                  tq_blk, tq_blk, tq_blk, tk_blk, tk_blk, tk_blk,
                  small, small, small, small,
                  pl.BlockSpec((1, 2 * HEAD_DIM_A), lambda b, g, i: (0, 0))],
        out_specs=pl.BlockSpec((tq, w), lambda b, g, i: (b * nq + i, g)),
        out_shape=jax.ShapeDtypeStruct((m, D_MODEL), BF16),
        scratch_shapes=[pltpu.VMEM((w // 128, 128, past), BF16), pltpu.VMEM((seq, w), BF16),
                        pltpu.VMEM((past + seq, w), BF16)],
        compiler_params=pltpu.CompilerParams(
            dimension_semantics=("arbitrary", "arbitrary", "arbitrary"), vmem_limit_bytes=VMEM_LIMIT),
        name="attn_lat",
    )(u, u, u, u, cache_kt, cache_v2, cos, s1, s2, cos, s1, s2,
      lam_q1, lam_k1, lam_q2, lam_k2, subln_g)


def _rwkv_consts():
    ri = lax.broadcasted_iota(jnp.int32, (PW, PW), 0)
    ci = lax.broadcasted_iota(jnp.int32, (PW, PW), 1)
    t = lax.broadcasted_iota(jnp.int32, (CHUNK, PW), 0)
    lane = lax.broadcasted_iota(jnp.int32, (CHUNK, PW), 1)
    j = lane & (CHUNK - 1)
    ti = lax.broadcasted_iota(jnp.int32, (CHUNK, CHUNK), 0)
    tj = lax.broadcasted_iota(jnp.int32, (CHUNK, CHUNK), 1)
    return dict(
        bmask=(ri >> 6) == (ci >> 6),
        eye=ri == ci,
        head0=lane < HEAD_DIM_B,
        eye_sbs=jnp.where(t == j, 1.0, 0.0),
        strict=(j < t, j > t),
        incl=(j <= t, j >= t),
        off=[((t >> (n + 1)) == (j >> (n + 1))) & ((t >> n) != (j >> n)) for n in range(6)],
        tri=(jnp.where(tj <= ti, 1.0, 0.0).astype(BF16), jnp.where(tj >= ti, 1.0, 0.0).astype(BF16)),
    )


def _wbd(x, cst):
    xb = x.astype(BF16)
    z = jnp.zeros_like(xb)
    return jnp.concatenate([jnp.where(cst["head0"], xb, z), jnp.where(cst["head0"], z, xb)], axis=0)


def _group_sum(x, e):
    hi, lo = _split2(x)
    ew = e.shape[0]
    parts = [_nn(hi[:, o:o + ew], e) + _nn(lo[:, o:o + ew], e) for o in range(0, x.shape[1], ew)]
    return parts[0] if len(parts) == 1 else jnp.concatenate(parts, axis=1)


def _pair_pipeline(h, ops, lg, states, cst, same_rows, out):
    L = CHUNK
    ncb = len(ops[0])
    idx = [(d, c) for d in range(2) for c in range(ncb)]
    half = lambda a: a[:, h * PW:(h + 1) * PW]

    ar, wbk, khbh, dg = {}, {}, {}, {}
    for d, c in idx:
        lw, r, kd, v, kk, b = (half(a) for a in ops[d][c])
        g = half(lg[d, c])
        g_l = g[0:1, :] if d == 1 else g[L - 1:L, :]
        at = -(kk * jnp.exp2(g - lw))
        rt = r * jnp.exp2(g)
        e_inv = jnp.exp2(-g)
        eg = jnp.exp2(g_l)
        bt, kt = b * e_inv, kd * e_inv
        ar[d, c] = jnp.concatenate([at, rt], axis=0).astype(BF16)
        wbk[d, c] = jnp.concatenate([_wbd(bt, cst), _wbd(kt, cst)], axis=0)
        khbh[d, c] = jnp.concatenate([(kt * eg).astype(BF16), (bt * eg).astype(BF16)], axis=0)
        dg[d, c] = jnp.where(cst["eye"], eg, 0.0).astype(BF16)
    yield

    gbk = {i: _nt(ar[i], wbk[i]) for i in idx}
    yield
    m_ab = {i: jnp.where(cst["strict"][i[0]], gbk[i][:L, :PW], 0.0) for i in idx}
    m_rb = {i: jnp.where(cst["incl"][i[0]], gbk[i][L:, :PW], 0.0).astype(BF16) for i in idx}
    m_akrk = {i: jnp.concatenate([jnp.where(cst["strict"][i[0]], gbk[i][:L, PW:], 0.0),
                                  jnp.where(cst["incl"][i[0]], gbk[i][L:, PW:], 0.0)], axis=0).astype(BF16)
              for i in idx}

    x = {i: cst["eye_sbs"] + jnp.where(cst["off"][0], m_ab[i], 0.0) for i in idx}
    for off in cst["off"][1:]:
        p = {i: _nn(jnp.where(off, m_ab[i], 0.0).astype(BF16), _wbd(x[i], cst)) for i in idx}
        yield
        x = {i: x[i] + _nn(x[i].astype(BF16), _wbd(p[i], cst)) for i in idx}
        yield
    xb = {i: x[i].astype(BF16) for i in idx}

    vsrc = {(d, c): (0, c) if same_rows else (d, c) for d, c in idx}
    vhalf = {i: half(ops[i[0]][i[1]][3]) for i in dict.fromkeys(vsrc.values())}
    wv = {i: _wbd(vhalf[i], cst) for i in vhalf}
    vb = {i: vhalf[i].astype(BF16) for i in vhalf}
    lk = {i: jnp.concatenate([jnp.concatenate([ar[i], m_akrk[i]], axis=1),
                              jnp.concatenate([dg[i], jnp.zeros_like(dg[i])], axis=1)], axis=0)
          for i in idx}

    s = [states[0][h], states[1][h]]
    for step in range(ncb):
        cur = [(0, step), (1, ncb - 1 - step)]
        ars = {i: _nn(lk[i], jnp.concatenate([s[i[0]].astype(BF16), wv[vsrc[i]]], axis=0)) for i in cur}
        yield
        u = {i: _nn(xb[i], _wbd(ars[i][:L], cst)) for i in cur}
        yield
        for i in cur:
            d, c = i
            out["y"][d, c, h] = ars[i][L:2 * L] + _nn(m_rb[i], _wbd(u[i], cst))
            upd = _tn(khbh[i], jnp.concatenate([vb[vsrc[i]], u[i].astype(BF16)], axis=0))
            s[d] = ars[i][2 * L:] + jnp.where(cst["bmask"], upd, 0.0)
        yield
    out["s"][0][h], out["s"][1][h] = s


def _block_step(ops, states, cst, same_rows):
    ncb = len(ops[0])
    npair = ops[0][0][0].shape[1] // PW

    lg = {}
    for d in range(2):
        for c in range(ncb):
            hi, mid = _split2(ops[d][c][0])
            tri = cst["tri"][d]
            lg[d, c] = _nn(tri, hi) + _nn(tri, mid)

    out = {"y": {}, "s": [[None] * npair, [None] * npair]}
    pending = [_pair_pipeline(h, ops, lg, states, cst, same_rows, out) for h in range(npair)]
    active = pending
    while active:
        active = [gen for gen in active if next(gen, "done") != "done"]
    ys = [[jnp.concatenate([out["y"][d, c, h] for h in range(npair)], axis=1) if npair > 1
           else out["y"][d, c, 0] for c in range(ncb)] for d in range(2)]
    return ys, out["s"]


def _rwkv_kernel(*refs, seq, gw, has_s0, want_state):
    it = iter(refs)
    r_ref, k_ref, v_ref, lo_ref, gb_ref = (next(it) for _ in range(5))
    mur_ref, muk_ref, muv_ref, mul_ref = (next(it) for _ in range(4))
    w0_ref, w2_ref, a0_ref, a2_ref = (next(it) for _ in range(4))
    kk_ref, ka_ref, rk_ref, lng_ref, lnb_ref = (next(it) for _ in range(5))
    s0_ref = next(it) if has_s0 else None
    ob_ref = next(it)
    sf_ref = next(it) if want_state else None
    r_s, v_s, kn_s, lw_s, kd_s, b_s, y_s, st_s = (next(it) for _ in range(8))

    T = seq
    cst = _rwkv_consts()
    npair = gw // PW
    hpp = PW // HEAD_DIM_B
    gi = lax.broadcasted_iota(jnp.int32, (LW, LW), 0)
    gj = lax.broadcasted_iota(jnp.int32, (LW, LW), 1)
    e = jnp.where((gi >> 6) == (gj >> 6), 1.0, 0.0).astype(BF16)
    def shifted(ref, mu_ref):
        x = ref[...].astype(F32)
        mu = mu_ref[...]
        row8 = lax.broadcasted_iota(jnp.int32, (8, x.shape[1]), 0)
        prev = pltpu.roll(x, 1, 0)
        nxt = pltpu.roll(x, T - 1, 0)
        prev = jnp.concatenate([jnp.where(row8 == 0, 0.0, prev[0:8]), prev[8:]], axis=0)
        nxt = jnp.concatenate([nxt[:T - 8], jnp.where(row8 == 7, 0.0, nxt[T - 8:])], axis=0)
        return x * (1.0 - mu[0:1, :] - mu[1:2, :]) + mu[0:1, :] * prev + mu[1:2, :] * nxt

    r = shifted(r_ref, mur_ref)
    k = shifted(k_ref, muk_ref)
    v = shifted(v_ref, muv_ref)
    lo = shifted(lo_ref, mul_ref)
    r_s[...] = r
    v_s[...] = v
    kk = k * kk_ref[...]
    kk = kk * lax.rsqrt(_group_sum(kk * kk, e) + 1e-12)
    kn_s[...] = kk
    bonus_dot = _group_sum(r * k * rk_ref[...], e)

    tanh_lo = jnp.tanh(lo).astype(BF16)
    lo_b = lo.astype(BF16)

    def padded(w, before):
        parts = [jnp.zeros((before, gw), F32)] if before else []
        parts.append(w)
        after = LW - before - LORA_R
        if after:
            parts.append(jnp.zeros((after, gw), F32))
        return jnp.concatenate(parts, axis=0).astype(BF16)

    for d in range(2):
        w2p = padded(w2_ref[d], LORA_R * d)
        a2p = padded(a2_ref[d], LORA_R * (2 + d))
        z = w0_ref[d:d + 1, :] + _nn(tanh_lo, w2p)
        lw_s[d] = -(EXP_NEG_HALF * LOG2E) * jax.nn.sigmoid(z)
        a = jax.nn.sigmoid(a0_ref[d:d + 1, :] + _nn(lo_b, a2p))
        kd_s[d] = k * (1.0 + (a - 1.0) * ka_ref[...])
        b_s[d] = kk * a

    if has_s0:
        ident = jnp.where(cst["eye"], 1.0, 0.0).astype(BF16)
        for d in range(2):
            for h in range(npair):
                z0 = s0_ref[0, d, h * hpp:(h + 1) * hpp].reshape(PW, HEAD_DIM_B)
                hi, lo2 = _split2(z0)
                zt = _tn(hi, ident) + _tn(lo2, ident)
                st_s[d, h] = jnp.where(cst["bmask"], jnp.concatenate([zt] * hpp, axis=0), 0.0)
    else:
        st_s[...] = jnp.zeros(st_s.shape, F32)

    nblk = T // BLOCK_T
    ncb = BLOCK_T // CHUNK

    def block(i, carry):
        rows = []
        for d in range(2):
            blk = i if d == 0 else nblk - 1 - i
            rows.append([])
            for c in range(ncb):
                start = blk * BLOCK_T + c * CHUNK
                if not isinstance(start, int):
                    start = pl.multiple_of(start, CHUNK)
                rows[d].append(pl.ds(start, CHUNK))
        ops = [[(lw_s[d, rw, :], r_s[rw, :], kd_s[d, rw, :], v_s[rw, :], kn_s[rw, :], b_s[d, rw, :])
                for rw in rows[d]] for d in range(2)]
        states = [[st_s[d, h] for h in range(npair)] for d in range(2)]
        ys, s_new = _block_step(ops, states, cst, nblk == 1)
        for d in range(2):
            for c in range(ncb):
                y_s[d, rows[d][c], :] = ys[d][c]
            for h in range(npair):
                st_s[d, h] = s_new[d][h]
        return carry

    if nblk == 1:
        block(0, 0)
    else:
        lax.fori_loop(0, nblk, block, 0)

    y = y_s[0] + y_s[1]
    mu = _group_sum(y, e) * (1.0 / HEAD_DIM_B)
    dlt = y - mu
    var = _group_sum(dlt * dlt, e) * (1.0 / HEAD_DIM_B)
    yn = dlt * lax.rsqrt(var + LNX_EPS) * lng_ref[...] + lnb_ref[...]
    out = (yn + bonus_dot * v_s[...]) * _silu(gb_ref[...].astype(F32))
    ob_ref[...] = out.astype(BF16)

    if want_state:
        ri = lax.broadcasted_iota(jnp.int32, (PW, HEAD_DIM_B), 0)
        ci = lax.broadcasted_iota(jnp.int32, (PW, HEAD_DIM_B), 1)
        fold = jnp.where((ri & (HEAD_DIM_B - 1)) == ci, 1.0, 0.0).astype(BF16)
        for d in range(2):
            for h in range(npair):
                hi, lo2 = _split2(st_s[d, h])
                sf = _tn(hi, fold) + _tn(lo2, fold)
                sf_ref[0, d, h * hpp:(h + 1) * hpp] = sf.reshape(hpp, HEAD_DIM_B, HEAD_DIM_B)


def _rwkv(u, ul, seq, gw, s0, want_state, shift_mu, decay_w0, decay_w2, iclr_a0, iclr_a2,
          k_k, k_a, r_k, lnx_g, lnx_b):
    m = u.shape[0]
    nbat = m // seq
    ng = D_MODEL // gw
    hg = gw // HEAD_DIM_B
    ublk = lambda sec: pl.BlockSpec((seq, gw), lambda b, g: (b, sec * ng + g))
    lblk = pl.BlockSpec((seq, LW), lambda b, g: (b, 0))
    mublk = lambda sec: pl.BlockSpec((2, gw), lambda b, g: (0, sec * ng + g))
    mulblk = pl.BlockSpec((2, LW), lambda b, g: (0, 3 * D_MODEL // LW))
    vec2 = pl.BlockSpec((2, gw), lambda b, g: (0, g))
    mat2 = pl.BlockSpec((2, LORA_R, gw), lambda b, g: (0, 0, g))
    vec1 = pl.BlockSpec((1, gw), lambda b, g: (0, g))
    sblk = pl.BlockSpec((1, 2, hg, HEAD_DIM_B, HEAD_DIM_B), lambda b, g: (b, 0, g, 0, 0))
    in_specs = [ublk(SEC_R), ublk(SEC_KB), ublk(SEC_VB), lblk, ublk(SEC_GB),
                mublk(0), mublk(1), mublk(2), mulblk,
                vec2, mat2, vec2, mat2, vec1, vec1, vec1, vec1, vec1]
    args = [u, u, u, ul, u, shift_mu, shift_mu, shift_mu, shift_mu,
            decay_w0, decay_w2, iclr_a0, iclr_a2,
            k_k.reshape(1, D_MODEL), k_a.reshape(1, D_MODEL), r_k.reshape(1, D_MODEL),
            lnx_g.reshape(1, D_MODEL), lnx_b.reshape(1, D_MODEL)]
    if s0 is not None:
        in_specs.append(sblk)
        args.append(s0)
    out_specs = [pl.BlockSpec((seq, gw), lambda b, g: (b, g))]
    out_shape = [jax.ShapeDtypeStruct((m, D_MODEL), BF16)]
    if want_state:
        out_specs.append(sblk)
        out_shape.append(jax.ShapeDtypeStruct((nbat, 2, N_HEADS_B, HEAD_DIM_B, HEAD_DIM_B), F32))
    tw = pltpu.VMEM((seq, gw), F32)
    tw2 = pltpu.VMEM((2, seq, gw), F32)
    res = pl.pallas_call(
        functools.partial(_rwkv_kernel, seq=seq, gw=gw, has_s0=s0 is not None, want_state=want_state),
        grid=(nbat, ng),
        in_specs=in_specs,
        out_specs=out_specs,
        out_shape=out_shape,
        scratch_shapes=[tw, tw, tw, tw2, tw2, tw2, tw2, pltpu.VMEM((2, gw // PW, PW, PW), F32)],
        compiler_params=pltpu.CompilerParams(
            dimension_semantics=("arbitrary", "arbitrary"), vmem_limit_bytes=VMEM_LIMIT_BIG),
        name="rwkv_state" if want_state else "rwkv",
    )(*args)
    return res


def _outproj_kernel(oa_ref, ob_ref, x_ref, mod_ref, w_ref, fg_ref, y_ref):
    acc = _nn(oa_ref[...], w_ref[0:D_MODEL, :]) + _nn(ob_ref[...], w_ref[D_MODEL:, :])
    gate = mod_ref[0][:, 2 * D_MODEL:]
    xo = x_ref[...] + gate * acc
    y_ref[...] = xo * lax.rsqrt(jnp.mean(xo * xo, axis=-1, keepdims=True) + NORM_EPS) * fg_ref[...]


def _outproj(o_a, o_b, x, mod3, w_out_bf, final_g, mod_row):
    m = x.shape[0]
    tm = 512
    blk = pl.BlockSpec((tm, D_MODEL), lambda i: (i, 0))
    return pl.pallas_call(
        _outproj_kernel,
        grid=(m // tm,),
        in_specs=[blk, blk, blk,
                  pl.BlockSpec((1, 1, 3 * D_MODEL), lambda i: (mod_row(i * tm), 0, 0)),
                  pl.BlockSpec((2 * D_MODEL, D_MODEL), lambda i: (0, 0)),
                  pl.BlockSpec((1, D_MODEL), lambda i: (0, 0))],
        out_specs=blk,
        out_shape=jax.ShapeDtypeStruct((m, D_MODEL), F32),
        compiler_params=pltpu.CompilerParams(vmem_limit_bytes=VMEM_LIMIT),
        name="outproj",
    )(o_a, o_b, x, mod3, w_out_bf, final_g)


def kernel(x_prompt, x_sample, cache_k, cache_v, state_rwkv, c, c_ctx, norm_g, w_ada, b_ada, w_in, lam_q1, lam_k1, lam_q2, lam_k2, subln_g, shift_mu, decay_w0, decay_w2, iclr_a0, iclr_a2, k_k, k_a, r_k, lnx_g, lnx_b, w_out, final_g):
    batch, seq, d = x_prompt.shape
    dbatch, dseq, _ = x_sample.shape
    past = cache_k.shape[2]
    assert d == D_MODEL and w_in.shape == (1, D_MODEL, D_IN) and dseq % GRID_W == 0

    cvec = jnp.concatenate([c_ctx[None, :], c, jnp.zeros((8 - 1 - dbatch, d), F32)], axis=0)
    mod3 = _mod(cvec, w_ada[0], b_ada[0]).reshape(8, 1, 3 * d)
    lo0 = 4 * D_MODEL + 3 * D_MODEL
    w_bf = w_in[0].astype(BF16)
    w_lora = w_bf[:, lo0:lo0 + LW]
    w_gb = w_bf[:, lo0 + LW:]
    w_out_bf = w_out[0].astype(BF16)
    ctx_row = lambda tok: 0
    lat_row = lambda tok: 1 + tok // dseq
    lp = (shift_mu[0], decay_w0[0], decay_w2[0], iclr_a0[0], iclr_a2[0], k_k[0], k_a[0], r_k[0],
          lnx_g[0], lnx_b[0])
    lam = (lam_q1, lam_k1, lam_q2, lam_k2, subln_g)
    fg = final_g.reshape(1, d)

    xp = x_prompt.reshape(batch * seq, d)
    u_c, ul_c, nk, nv = _inproj(xp, mod3, norm_g, w_bf, w_gb, w_lora, ctx_row, seq, True)
    oa_c = _attn_ctx(u_c, seq, *lam)
    ob_c, s_new = _rwkv(u_c, ul_c, seq, RWKV_GW_CTX, None, True, *lp)
    y_prompt = _outproj(oa_c, ob_c, xp, mod3, w_out_bf, fg, ctx_row).reshape(batch, seq, d)

    xs = x_sample.reshape(dbatch * dseq, d)
    u_l, ul_l = _inproj(xs, mod3, norm_g, w_bf, w_gb, w_lora, lat_row, dseq, False)
    cv = cache_v[:, 0].reshape(dbatch * past, d)
    ckt = jnp.transpose(cache_k[:, 0], (0, 2, 3, 4, 1))
    oa_l = _attn_lat(u_l, ckt, cv, dseq, past, *lam)
    (ob_l,) = _rwkv(u_l, ul_l, dseq, RWKV_GW_LAT, state_rwkv[:, 0], False, *lp)
    y_sample = _outproj(oa_l, ob_l, xs, mod3, w_out_bf, fg, lat_row).reshape(dbatch, dseq, d)

    new_s = s_new.reshape(batch, 1, 2, N_HEADS_B, HEAD_DIM_B, HEAD_DIM_B)
    return (y_prompt, y_sample, nk, nv, new_s)
```

```python
import functools
import math

import jax
import jax.numpy as jnp
from jax import lax
from jax.experimental import pallas as pl
from jax.experimental.pallas import tpu as pltpu

F32 = jnp.float32
BF16 = jnp.bfloat16
HIGHEST = lax.Precision.HIGHEST

D_MODEL = 1024
GRID_W = 64
HEAD_DIM_A = 64
N_HEADS_A = 8
HEAD_DIM_B = 64
N_HEADS_B = 16
LORA_R = 64
D_SHIFT = 3 * D_MODEL + 4 * LORA_R
D_IN = 4 * D_MODEL + D_SHIFT + D_MODEL
ROPE_BASE = 10000.0
NORM_EPS = 1e-6
SUBLN_EPS = 1e-5
LNX_EPS = 64e-5
LAM_INIT = 0.8 - 0.6 * math.exp(-0.3 * 0)
EXP_NEG_HALF = math.exp(-0.5)
LOG2E = math.log2(math.e)

GW = 256
LW = 4 * LORA_R
PW = 128
RWKV_GW_CTX = 1024
RWKV_GW_LAT = 512
CHUNK = 64
BLOCK_T = 256
VMEM_LIMIT = 48 * 1024 * 1024
VMEM_LIMIT_BIG = 56 * 1024 * 1024
SEC_Q, SEC_K, SEC_V, SEC_GA, SEC_R, SEC_KB, SEC_VB, SEC_GB = range(8)


def _nn(a, b):
    return jnp.dot(a, b, preferred_element_type=F32)


def _nt(a, b):
    return lax.dot_general(a, b, (((1,), (1,)), ((), ())), preferred_element_type=F32)


def _tn(a, b):
    return lax.dot_general(a, b, (((0,), (0,)), ((), ())), preferred_element_type=F32)


def _split2(x):
    hi = x.astype(BF16)
    lo = (x - hi.astype(F32)).astype(BF16)
    return hi, lo


def _silu(x):
    return x * jax.nn.sigmoid(x)


def _mod_kernel(c_ref, w_ref, b_ref, o_ref):
    s = _silu(c_ref[...])
    o_ref[...] = jnp.dot(s, w_ref[...], precision=HIGHEST, preferred_element_type=F32) + b_ref[...]


def _mod(cvec, w_ada, b_ada):
    n = w_ada.shape[1]
    tn = 512
    return pl.pallas_call(
        _mod_kernel,
        grid=(n // tn,),
        in_specs=[
            pl.BlockSpec((8, D_MODEL), lambda j: (0, 0)),
            pl.BlockSpec((D_MODEL, tn), lambda j: (0, j)),
            pl.BlockSpec((1, tn), lambda j: (0, j)),
        ],
        out_specs=pl.BlockSpec((8, tn), lambda j: (0, j)),
        out_shape=jax.ShapeDtypeStruct((8, n), F32),
        name="mod",
    )(cvec, w_ada, b_ada.reshape(1, n))


def _inproj_kernel(*refs, seq, want_cache):
    x_ref, mod_ref, g_ref, w_ref, wgb_ref, wl_ref, um_ref, ul_ref = refs[:8]
    h_ref = refs[-1]
    j = pl.program_id(1)

    @pl.when(j == 0)
    def _():
        x = x_ref[...]
        y = x * lax.rsqrt(jnp.mean(x * x, axis=-1, keepdims=True) + NORM_EPS) * g_ref[...]
        mod = mod_ref[0]
        shift = mod[:, 0:D_MODEL]
        scale = mod[:, D_MODEL:2 * D_MODEL]
        h = (y * (1.0 + scale) + shift).astype(BF16)
        h_ref[...] = h
        ul_ref[...] = _nn(h, wl_ref[...]).astype(BF16)

    w = jnp.where(j == SEC_GB, wgb_ref[...], w_ref[...])
    acc = _nn(h_ref[...], w)
    um_ref[...] = acc.astype(BF16)
    if want_cache:
        nk_ref, nv_ref = refs[8:10]
        nb = acc.shape[0] // seq

        @pl.when(j == SEC_K)
        def _():
            nk_ref[:, 0] = acc.reshape((nb,) + nk_ref.shape[2:])

        @pl.when(j == SEC_V)
        def _():
            nv_ref[:, 0] = acc.reshape((nb,) + nv_ref.shape[2:])


def _inproj(x, mod3, norm_g, w_bf, w_gb, w_lora, mod_row, seq, want_cache):
    m = x.shape[0]
    tm = 1024
    nsec = SEC_GB + 1
    nb = tm // seq
    out_specs = [pl.BlockSpec((tm, D_MODEL), lambda i, j: (i, j)),
                 pl.BlockSpec((tm, LW), lambda i, j: (i, 0))]
    out_shape = [jax.ShapeDtypeStruct((m, nsec * D_MODEL), BF16), jax.ShapeDtypeStruct((m, LW), BF16)]
    if want_cache:
        out_specs += [pl.BlockSpec((nb, 1, seq, N_HEADS_A, 2, HEAD_DIM_A), lambda i, j: (i, 0, 0, 0, 0, 0)),
                      pl.BlockSpec((nb, 1, seq, N_HEADS_A, 2 * HEAD_DIM_A), lambda i, j: (i, 0, 0, 0, 0))]
        out_shape += [jax.ShapeDtypeStruct((m // seq, 1, seq, N_HEADS_A, 2, HEAD_DIM_A), F32),
                      jax.ShapeDtypeStruct((m // seq, 1, seq, N_HEADS_A, 2 * HEAD_DIM_A), F32)]
    return pl.pallas_call(
        functools.partial(_inproj_kernel, seq=seq, want_cache=want_cache),
        grid=(m // tm, nsec),
        in_specs=[
            pl.BlockSpec((tm, D_MODEL), lambda i, j: (i, 0)),
            pl.BlockSpec((1, 1, 3 * D_MODEL), lambda i, j: (mod_row(i * tm), 0, 0)),
            pl.BlockSpec((1, D_MODEL), lambda i, j: (0, 0)),
            pl.BlockSpec((D_MODEL, D_MODEL), lambda i, j: (0, jnp.minimum(j, SEC_GB - 1))),
            pl.BlockSpec((D_MODEL, D_MODEL), lambda i, j: (0, 0)),
            pl.BlockSpec((D_MODEL, LW), lambda i, j: (0, 0)),
        ],
        out_specs=out_specs,
        out_shape=out_shape,
        scratch_shapes=[pltpu.VMEM((tm, D_MODEL), BF16)],
        compiler_params=pltpu.CompilerParams(
            dimension_semantics=("arbitrary", "arbitrary"), vmem_limit_bytes=VMEM_LIMIT_BIG),
        name="inproj_cache" if want_cache else "inproj",
    )(x, mod3, norm_g, w_bf, w_gb, w_lora)


def _lam(q1_ref, k1_ref, q2_ref, k2_ref):
    s1 = jnp.sum(q1_ref[...] * k1_ref[...], axis=-1, keepdims=True)
    s2 = jnp.sum(q2_ref[...] * k2_ref[...], axis=-1, keepdims=True)
    return jnp.exp(s1) - jnp.exp(s2) + LAM_INIT


def _diff_head(q, kb, vb, g, lam, subg, kct=None):
    lane = lax.broadcasted_iota(jnp.int32, q.shape, 1)
    qs = q * (HEAD_DIM_A ** -0.5 * LOG2E)
    v1 = jnp.concatenate([vb, jnp.ones_like(vb)], axis=1)
    outs = []
    for m in range(2):
        qm = jnp.where((lane >> 6) == m, qs, 0.0).astype(BF16)
        s = _nt(qm, kb)
        if kct is not None:
            s = jnp.concatenate([_nn(qm, kct), s], axis=1)
        p = jnp.exp2(s - jnp.max(s, axis=-1, keepdims=True)).astype(BF16)
        pv = _nn(p, v1)
        outs.append(pv[:, :128] / pv[:, 128:])
    o = outs[0] - lam * outs[1]
    o = o * lax.rsqrt(jnp.mean(o * o, axis=-1, keepdims=True) + SUBLN_EPS) * subg
    o = o * (1.0 - LAM_INIT)
    return o * _silu(g)


def _attn_ctx_kernel(q_ref, k_ref, v_ref, g_ref, q1_ref, k1_ref, q2_ref, k2_ref, sg_ref, o_ref):
    lam = _lam(q1_ref, k1_ref, q2_ref, k2_ref)
    subg = sg_ref[...]
    for h in range(q_ref.shape[1] // 128):
        sl = slice(h * 128, (h + 1) * 128)
        o = _diff_head(q_ref[:, sl].astype(F32), k_ref[:, sl], v_ref[:, sl],
                       g_ref[:, sl].astype(F32), lam, subg)
        o_ref[:, sl] = o.astype(BF16)


def _attn_ctx(um, seq, lam_q1, lam_k1, lam_q2, lam_k2, subln_g):
    m = um.shape[0]
    small = pl.BlockSpec((1, HEAD_DIM_A), lambda b: (0, 0))
    blk = lambda sec: pl.BlockSpec((seq, D_MODEL), lambda b: (b, sec))
    return pl.pallas_call(
        _attn_ctx_kernel,
        grid=(m // seq,),
        in_specs=[blk(SEC_Q), blk(SEC_K), blk(SEC_V), blk(SEC_GA), small, small, small, small,
                  pl.BlockSpec((1, 2 * HEAD_DIM_A), lambda b: (0, 0))],
        out_specs=blk(0),
        out_shape=jax.ShapeDtypeStruct((m, D_MODEL), BF16),
        compiler_params=pltpu.CompilerParams(vmem_limit_bytes=VMEM_LIMIT),
        name="attn_ctx",
    )(um, um, um, um, lam_q1, lam_k1, lam_q2, lam_k2, subln_g)


def _rope(x, cos, s1, s2):
    return x * cos + pltpu.roll(x, 112, 1) * s1 + pltpu.roll(x, 16, 1) * s2


def _attn_lat_kernel(q_ref, k_ref, v_ref, g_ref, ck_ref, cv_ref, cq_ref, s1q_ref, s2q_ref,
                     ckk_ref, s1k_ref, s2k_ref, q1_ref, k1_ref, q2_ref, k2_ref, sg_ref,
                     o_ref, kct_ref, krot_ref, vall_ref):
    past = cv_ref.shape[0]
    w = q_ref.shape[1]
    nh = w // 128

    @pl.when(pl.program_id(2) == 0)
    def _():
        vall_ref[0:past, :] = cv_ref[...].astype(BF16)
        vall_ref[past:, :] = v_ref[...]
        for h in range(nh):
            sl = slice(h * 128, (h + 1) * 128)
            kct_ref[h] = ck_ref[0, h].reshape(128, past).astype(BF16)
            krot_ref[:, sl] = _rope(k_ref[:, sl].astype(F32), ckk_ref[...], s1k_ref[...],
                                    s2k_ref[...]).astype(BF16)

    lam = _lam(q1_ref, k1_ref, q2_ref, k2_ref)
    subg = sg_ref[...]
    for h in range(nh):
        sl = slice(h * 128, (h + 1) * 128)
        q = _rope(q_ref[:, sl].astype(F32), cq_ref[...], s1q_ref[...], s2q_ref[...])
        o = _diff_head(q, krot_ref[:, sl], vall_ref[:, sl], g_ref[:, sl].astype(F32), lam, subg,
                       kct=kct_ref[h])
        o_ref[:, sl] = o.astype(BF16)


def _rope_tables(t):
    pos = jnp.arange(t)
    row = (pos // GRID_W).astype(F32)
    col = (pos % GRID_W).astype(F32)
    n_freq = HEAD_DIM_A // 4
    lane = jnp.arange(128)
    d = lane % HEAD_DIM_A
    use_col = (d // 32) == 1
    second = ((d % 32) // 16) == 1
    inv = ROPE_BASE ** (-(d % n_freq).astype(F32) / n_freq)
    ang = jnp.where(use_col[None, :], col[:, None], row[:, None]) * inv[None, :]
    cos = jnp.cos(ang)
    sin = jnp.sin(ang)
    s1 = jnp.where(second[None, :], 0.0, -sin)
    s2 = jnp.where(second[None, :], sin, 0.0)
    return cos, s1, s2


def _attn_lat(u, cache_kt, cache_v2, seq, past, lam_q1, lam_k1, lam_q2, lam_k2, subln_g):
    m = u.shape[0]
    nbat = m // seq
    w = GW
    ng = D_MODEL // w
    tq = 512
    nq = seq // tq
    cos, s1, s2 = _rope_tables(seq)
    small = pl.BlockSpec((1, HEAD_DIM_A), lambda b, g, i: (0, 0))
    qblk = lambda sec: pl.BlockSpec((tq, w), lambda b, g, i: (b * nq + i, sec * ng + g))
    kblk = lambda sec: pl.BlockSpec((seq, w), lambda b, g, i: (b, sec * ng + g))
    ckblk = pl.BlockSpec((1, w // 128, 2, HEAD_DIM_A, past), lambda b, g, i: (b, g, 0, 0, 0))
    cblk = pl.BlockSpec((past, w), lambda b, g, i: (b, g))
    tq_blk = pl.BlockSpec((tq, 128), lambda b, g, i: (i, 0))
    tk_blk = pl.BlockSpec((seq, 128), lambda b, g, i: (0, 0))
    return pl.pallas_call(
        _attn_lat_kernel,
        grid=(nbat, ng, nq),
        in_specs=[qblk(SEC_Q), kblk(SEC_K), kblk(SEC_V), qblk(SEC_GA), ckblk, cblk,
                  tq_blk, tq_blk, tq_blk, tk_blk, tk_blk, tk_blk,
                  small, small, small, small,
                  pl.BlockSpec((1, 2 * HEAD_DIM_A), lambda b, g, i: (0, 0))],
        out_specs=pl.BlockSpec((tq, w), lambda b, g, i: (b * nq + i, g)),
        out_shape=jax.ShapeDtypeStruct((m, D_MODEL), BF16),
        scratch_shapes=[pltpu.VMEM((w // 128, 128, past), BF16), pltpu.VMEM((seq, w), BF16),
                        pltpu.VMEM((past + seq, w), BF16)],
        compiler_params=pltpu.CompilerParams(
            dimension_semantics=("arbitrary", "arbitrary", "arbitrary"), vmem_limit_bytes=VMEM_LIMIT),
        name="attn_lat",
    )(u, u, u, u, cache_kt, cache_v2, cos, s1, s2, cos, s1, s2,
      lam_q1, lam_k1, lam_q2, lam_k2, subln_g)


def _rwkv_consts():
    ri = lax.broadcasted_iota(jnp.int32, (PW, PW), 0)
    ci = lax.broadcasted_iota(jnp.int32, (PW, PW), 1)
    t = lax.broadcasted_iota(jnp.int32, (CHUNK, PW), 0)
    lane = lax.broadcasted_iota(jnp.int32, (CHUNK, PW), 1)
    j = lane & (CHUNK - 1)
    ti = lax.broadcasted_iota(jnp.int32, (CHUNK, CHUNK), 0)
    tj = lax.broadcasted_iota(jnp.int32, (CHUNK, CHUNK), 1)
    return dict(
        bmask=(ri >> 6) == (ci >> 6),
        eye=ri == ci,
        head0=lane < HEAD_DIM_B,
        eye_sbs=jnp.where(t == j, 1.0, 0.0),
        strict=(j < t, j > t),
        incl=(j <= t, j >= t),
        off=[((t >> (n + 1)) == (j >> (n + 1))) & ((t >> n) != (j >> n)) for n in range(6)],
        tri=(jnp.where(tj <= ti, 1.0, 0.0).astype(BF16), jnp.where(tj >= ti, 1.0, 0.0).astype(BF16)),
    )


def _wbd(x, cst):
    xb = x.astype(BF16)
    z = jnp.zeros_like(xb)
    return jnp.concatenate([jnp.where(cst["head0"], xb, z), jnp.where(cst["head0"], z, xb)], axis=0)


def _group_sum(x, e, exact=True):
    ew = e.shape[0]
    cols = range(0, x.shape[1], ew)
    if exact:
        hi, lo = _split2(x)
        parts = [_nn(hi[:, o:o + ew], e) + _nn(lo[:, o:o + ew], e) for o in cols]
    else:
        xb = x.astype(BF16)
        parts = [_nn(xb[:, o:o + ew], e) for o in cols]
    return parts[0] if len(parts) == 1 else jnp.concatenate(parts, axis=1)


def _pair_pipeline(h, ops, lg, states, cst, same_rows, out):
    L = CHUNK
    ncb = len(ops[0])
    idx = [(d, c) for d in range(2) for c in range(ncb)]
    half = lambda a: a[:, h * PW:(h + 1) * PW]

    ar, wbk, khbh, dg = {}, {}, {}, {}
    for d, c in idx:
        lw, r, kd, v, kk, b = (half(a) for a in ops[d][c])
        g = half(lg[d, c])
        g_l = g[0:1, :] if d == 1 else g[L - 1:L, :]
        at = -(kk * jnp.exp2(g - lw))
        rt = r * jnp.exp2(g)
        e_inv = jnp.exp2(-g)
        eg = jnp.exp2(g_l)
        bt, kt = b * e_inv, kd * e_inv
        ar[d, c] = jnp.concatenate([at, rt], axis=0).astype(BF16)
        wbk[d, c] = jnp.concatenate([_wbd(bt, cst), _wbd(kt, cst)], axis=0)
        khbh[d, c] = jnp.concatenate([(kt * eg).astype(BF16), (bt * eg).astype(BF16)], axis=0)
        dg[d, c] = jnp.where(cst["eye"], eg, 0.0).astype(BF16)
    yield

    gbk = {i: _nt(ar[i], wbk[i]) for i in idx}
    yield
    m_ab = {i: jnp.where(cst["strict"][i[0]], gbk[i][:L, :PW], 0.0) for i in idx}
    m_rb = {i: jnp.where(cst["incl"][i[0]], gbk[i][L:, :PW], 0.0).astype(BF16) for i in idx}
    m_akrk = {i: jnp.concatenate([jnp.where(cst["strict"][i[0]], gbk[i][:L, PW:], 0.0),
                                  jnp.where(cst["incl"][i[0]], gbk[i][L:, PW:], 0.0)], axis=0).astype(BF16)
              for i in idx}

    x = {i: cst["eye_sbs"] + jnp.where(cst["off"][0], m_ab[i], 0.0) for i in idx}
    for off in cst["off"][1:]:
        p = {i: _nn(jnp.where(off, m_ab[i], 0.0).astype(BF16), _wbd(x[i], cst)) for i in idx}
        yield
        x = {i: x[i] + _nn(x[i].astype(BF16), _wbd(p[i], cst)) for i in idx}
        yield
    xb = {i: x[i].astype(BF16) for i in idx}

    vsrc = {(d, c): (0, c) if same_rows else (d, c) for d, c in idx}
    vhalf = {i: half(ops[i[0]][i[1]][3]) for i in dict.fromkeys(vsrc.values())}
    wv = {i: _wbd(vhalf[i], cst) for i in vhalf}
    vb = {i: vhalf[i].astype(BF16) for i in vhalf}
    lk = {i: jnp.concatenate([jnp.concatenate([ar[i], m_akrk[i]], axis=1),
                              jnp.concatenate([dg[i], jnp.zeros_like(dg[i])], axis=1)], axis=0)
          for i in idx}

    s = [states[0][h], states[1][h]]
    for step in range(ncb):
        cur = [(0, step), (1, ncb - 1 - step)]
        ars = {i: _nn(lk[i], jnp.concatenate([s[i[0]].astype(BF16), wv[vsrc[i]]], axis=0)) for i in cur}
        yield
        u = {i: _nn(xb[i], _wbd(ars[i][:L], cst)) for i in cur}
        yield
        for i in cur:
            d, c = i
            out["y"][d, c, h] = ars[i][L:2 * L] + _nn(m_rb[i], _wbd(u[i], cst))
            upd = _tn(khbh[i], jnp.concatenate([vb[vsrc[i]], u[i].astype(BF16)], axis=0))
            s[d] = ars[i][2 * L:] + jnp.where(cst["bmask"], upd, 0.0)
        yield
    out["s"][0][h], out["s"][1][h] = s


def _block_step(ops, states, cst, same_rows):
    ncb = len(ops[0])
    npair = ops[0][0][0].shape[1] // PW

    lg = {}
    for d in range(2):
        for c in range(ncb):
            hi, mid = _split2(ops[d][c][0])
            tri = cst["tri"][d]
            lg[d, c] = _nn(tri, hi) + _nn(tri, mid)

    out = {"y": {}, "s": [[None] * npair, [None] * npair]}
    pending = [_pair_pipeline(h, ops, lg, states, cst, same_rows, out) for h in range(npair)]
    active = pending
    while active:
        active = [gen for gen in active if next(gen, "done") != "done"]
    ys = [[jnp.concatenate([out["y"][d, c, h] for h in range(npair)], axis=1) if npair > 1
           else out["y"][d, c, 0] for c in range(ncb)] for d in range(2)]
    return ys, out["s"]


def _rwkv_kernel(*refs, seq, gw, has_s0, want_state):
    it = iter(refs)
    r_ref, k_ref, v_ref, lo_ref, gb_ref = (next(it) for _ in range(5))
    mur_ref, muk_ref, muv_ref, mul_ref = (next(it) for _ in range(4))
    w0_ref, w2_ref, a0_ref, a2_ref = (next(it) for _ in range(4))
    kk_ref, ka_ref, rk_ref, lng_ref, lnb_ref = (next(it) for _ in range(5))
    s0_ref = next(it) if has_s0 else None
    ob_ref = next(it)
    sf_ref = next(it) if want_state else None
    r_s, v_s, kn_s, lw_s, kd_s, b_s, y_s, st_s = (next(it) for _ in range(8))

    T = seq
    cst = _rwkv_consts()
    npair = gw // PW
    hpp = PW // HEAD_DIM_B
    gi = lax.broadcasted_iota(jnp.int32, (LW, LW), 0)
    gj = lax.broadcasted_iota(jnp.int32, (LW, LW), 1)
    e = jnp.where((gi >> 6) == (gj >> 6), 1.0, 0.0).astype(BF16)
    def shifted(ref, mu_ref):
        x = ref[...].astype(F32)
        mu = mu_ref[...]
        row8 = lax.broadcasted_iota(jnp.int32, (8, x.shape[1]), 0)
        prev = pltpu.roll(x, 1, 0)
        nxt = pltpu.roll(x, T - 1, 0)
        prev = jnp.concatenate([jnp.where(row8 == 0, 0.0, prev[0:8]), prev[8:]], axis=0)
        nxt = jnp.concatenate([nxt[:T - 8], jnp.where(row8 == 7, 0.0, nxt[T - 8:])], axis=0)
        return x * (1.0 - mu[0:1, :] - mu[1:2, :]) + mu[0:1, :] * prev + mu[1:2, :] * nxt

    r = shifted(r_ref, mur_ref)
    k = shifted(k_ref, muk_ref)
    v = shifted(v_ref, muv_ref)
    lo = shifted(lo_ref, mul_ref)
    r_s[...] = r
    v_s[...] = v
    kk = k * kk_ref[...]
    kk = kk * lax.rsqrt(_group_sum(kk * kk, e, exact=False) + 1e-12)
    kn_s[...] = kk
    bonus_dot = _group_sum(r * k * rk_ref[...], e, exact=False)

    tanh_lo = jnp.tanh(lo).astype(BF16)
    lo_b = lo.astype(BF16)

    def padded(w, before):
        parts = [jnp.zeros((before, gw), F32)] if before else []
        parts.append(w)
        after = LW - before - LORA_R
        if after:
            parts.append(jnp.zeros((after, gw), F32))
        return jnp.concatenate(parts, axis=0).astype(BF16)

    for d in range(2):
        w2p = padded(w2_ref[d], LORA_R * d)
        a2p = padded(a2_ref[d], LORA_R * (2 + d))
        z = w0_ref[d:d + 1, :] + _nn(tanh_lo, w2p)
        lw_s[d] = -(EXP_NEG_HALF * LOG2E) * jax.nn.sigmoid(z)
        a = jax.nn.sigmoid(a0_ref[d:d + 1, :] + _nn(lo_b, a2p))
        kd_s[d] = k * (1.0 + (a - 1.0) * ka_ref[...])
        b_s[d] = kk * a

    if has_s0:
        ident = jnp.where(cst["eye"], 1.0, 0.0).astype(BF16)
        for d in range(2):
            for h in range(npair):
                z0 = s0_ref[0, d, h * hpp:(h + 1) * hpp].reshape(PW, HEAD_DIM_B)
                hi, lo2 = _split2(z0)
                zt = _tn(hi, ident) + _tn(lo2, ident)
                st_s[d, h] = jnp.where(cst["bmask"], jnp.concatenate([zt] * hpp, axis=0), 0.0)
    else:
        st_s[...] = jnp.zeros(st_s.shape, F32)

    nblk = T // BLOCK_T
    ncb = BLOCK_T // CHUNK

    def block(i, carry):
        rows = []
        for d in range(2):
            blk = i if d == 0 else nblk - 1 - i
            rows.append([])
            for c in range(ncb):
                start = blk * BLOCK_T + c * CHUNK
                if not isinstance(start, int):
                    start = pl.multiple_of(start, CHUNK)
                rows[d].append(pl.ds(start, CHUNK))
        ops = [[(lw_s[d, rw, :], r_s[rw, :], kd_s[d, rw, :], v_s[rw, :], kn_s[rw, :], b_s[d, rw, :])
                for rw in rows[d]] for d in range(2)]
        states = [[st_s[d, h] for h in range(npair)] for d in range(2)]
        ys, s_new = _block_step(ops, states, cst, nblk == 1)
        for d in range(2):
            for c in range(ncb):
                y_s[d, rows[d][c], :] = ys[d][c]
            for h in range(npair):
                st_s[d, h] = s_new[d][h]
        return carry

    if nblk == 1:
        block(0, 0)
    else:
        lax.fori_loop(0, nblk, block, 0)

    y = y_s[0] + y_s[1]
    mu = _group_sum(y, e) * (1.0 / HEAD_DIM_B)
    dlt = y - mu
    var = _group_sum(dlt * dlt, e, exact=False) * (1.0 / HEAD_DIM_B)
    yn = dlt * lax.rsqrt(var + LNX_EPS) * lng_ref[...] + lnb_ref[...]
    out = (yn + bonus_dot * v_s[...]) * _silu(gb_ref[...].astype(F32))
    ob_ref[...] = out.astype(BF16)

    if want_state:
        ri = lax.broadcasted_iota(jnp.int32, (PW, HEAD_DIM_B), 0)
        ci = lax.broadcasted_iota(jnp.int32, (PW, HEAD_DIM_B), 1)
        fold = jnp.where((ri & (HEAD_DIM_B - 1)) == ci, 1.0, 0.0).astype(BF16)
        for d in range(2):
            for h in range(npair):
                hi, lo2 = _split2(st_s[d, h])
                sf = _tn(hi, fold) + _tn(lo2, fold)
                sf_ref[0, d, h * hpp:(h + 1) * hpp] = sf.reshape(hpp, HEAD_DIM_B, HEAD_DIM_B)


def _rwkv(u, ul, seq, gw, s0, want_state, shift_mu, decay_w0, decay_w2, iclr_a0, iclr_a2,
          k_k, k_a, r_k, lnx_g, lnx_b):
    m = u.shape[0]
    nbat = m // seq
    ng = D_MODEL // gw
    hg = gw // HEAD_DIM_B
    ublk = lambda sec: pl.BlockSpec((seq, gw), lambda b, g: (b, sec * ng + g))
    lblk = pl.BlockSpec((seq, LW), lambda b, g: (b, 0))
    mublk = lambda sec: pl.BlockSpec((2, gw), lambda b, g: (0, sec * ng + g))
    mulblk = pl.BlockSpec((2, LW), lambda b, g: (0, 3 * D_MODEL // LW))
    vec2 = pl.BlockSpec((2, gw), lambda b, g: (0, g))
    mat2 = pl.BlockSpec((2, LORA_R, gw), lambda b, g: (0, 0, g))
    vec1 = pl.BlockSpec((1, gw), lambda b, g: (0, g))
    sblk = pl.BlockSpec((1, 2, hg, HEAD_DIM_B, HEAD_DIM_B), lambda b, g: (b, 0, g, 0, 0))
    in_specs = [ublk(SEC_R), ublk(SEC_KB), ublk(SEC_VB), lblk, ublk(SEC_GB),
                mublk(0), mublk(1), mublk(2), mulblk,
                vec2, mat2, vec2, mat2, vec1, vec1, vec1, vec1, vec1]
    args = [u, u, u, ul, u, shift_mu, shift_mu, shift_mu, shift_mu,
            decay_w0, decay_w2, iclr_a0, iclr_a2,
            k_k.reshape(1, D_MODEL), k_a.reshape(1, D_MODEL), r_k.reshape(1, D_MODEL),
            lnx_g.reshape(1, D_MODEL), lnx_b.reshape(1, D_MODEL)]
    if s0 is not None:
        in_specs.append(sblk)
        args.append(s0)
    out_specs = [pl.BlockSpec((seq, gw), lambda b, g: (b, g))]
    out_shape = [jax.ShapeDtypeStruct((m, D_MODEL), BF16)]
    if want_state:
        out_specs.append(sblk)
        out_shape.append(jax.ShapeDtypeStruct((nbat, 2, N_HEADS_B, HEAD_DIM_B, HEAD_DIM_B), F32))
    tw = pltpu.VMEM((seq, gw), F32)
    tw2 = pltpu.VMEM((2, seq, gw), F32)
    res = pl.pallas_call(
        functools.partial(_rwkv_kernel, seq=seq, gw=gw, has_s0=s0 is not None, want_state=want_state),
        grid=(nbat, ng),
        in_specs=in_specs,
        out_specs=out_specs,
        out_shape=out_shape,
        scratch_shapes=[tw, tw, tw, tw2, tw2, tw2, tw2, pltpu.VMEM((2, gw // PW, PW, PW), F32)],
        compiler_params=pltpu.CompilerParams(
            dimension_semantics=("arbitrary", "arbitrary"), vmem_limit_bytes=VMEM_LIMIT_BIG),
        name="rwkv_state" if want_state else "rwkv",
    )(*args)
    return res


def _outproj_kernel(oa_ref, ob_ref, x_ref, mod_ref, w_ref, fg_ref, y_ref):
    acc = _nn(oa_ref[...], w_ref[0:D_MODEL, :]) + _nn(ob_ref[...], w_ref[D_MODEL:, :])
    gate = mod_ref[0][:, 2 * D_MODEL:]
    xo = x_ref[...] + gate * acc
    y_ref[...] = xo * lax.rsqrt(jnp.mean(xo * xo, axis=-1, keepdims=True) + NORM_EPS) * fg_ref[...]


def _outproj(o_a, o_b, x, mod3, w_out_bf, final_g, mod_row):
    m = x.shape[0]
    tm = 512
    blk = pl.BlockSpec((tm, D_MODEL), lambda i: (i, 0))
    return pl.pallas_call(
        _outproj_kernel,
        grid=(m // tm,),
        in_specs=[blk, blk, blk,
                  pl.BlockSpec((1, 1, 3 * D_MODEL), lambda i: (mod_row(i * tm), 0, 0)),
                  pl.BlockSpec((2 * D_MODEL, D_MODEL), lambda i: (0, 0)),
                  pl.BlockSpec((1, D_MODEL), lambda i: (0, 0))],
        out_specs=blk,
        out_shape=jax.ShapeDtypeStruct((m, D_MODEL), F32),
        compiler_params=pltpu.CompilerParams(vmem_limit_bytes=VMEM_LIMIT),
        name="outproj",
    )(o_a, o_b, x, mod3, w_out_bf, final_g)


def kernel(x_prompt, x_sample, cache_k, cache_v, state_rwkv, c, c_ctx, norm_g, w_ada, b_ada, w_in, lam_q1, lam_k1, lam_q2, lam_k2, subln_g, shift_mu, decay_w0, decay_w2, iclr_a0, iclr_a2, k_k, k_a, r_k, lnx_g, lnx_b, w_out, final_g):
    batch, seq, d = x_prompt.shape
    dbatch, dseq, _ = x_sample.shape
    past = cache_k.shape[2]
    assert d == D_MODEL and w_in.shape == (1, D_MODEL, D_IN) and dseq % GRID_W == 0

    cvec = jnp.concatenate([c_ctx[None, :], c, jnp.zeros((8 - 1 - dbatch, d), F32)], axis=0)
    mod3 = _mod(cvec, w_ada[0], b_ada[0]).reshape(8, 1, 3 * d)
    lo0 = 4 * D_MODEL + 3 * D_MODEL
    w_bf = w_in[0].astype(BF16)
    w_lora = w_bf[:, lo0:lo0 + LW]
    w_gb = w_bf[:, lo0 + LW:]
    w_out_bf = w_out[0].astype(BF16)
    ctx_row = lambda tok: 0
    lat_row = lambda tok: 1 + tok // dseq
    lp = (shift_mu[0], decay_w0[0], decay_w2[0], iclr_a0[0], iclr_a2[0], k_k[0], k_a[0], r_k[0],
          lnx_g[0], lnx_b[0])
    lam = (lam_q1, lam_k1, lam_q2, lam_k2, subln_g)
    fg = final_g.reshape(1, d)

    xp = x_prompt.reshape(batch * seq, d)
    u_c, ul_c, nk, nv = _inproj(xp, mod3, norm_g, w_bf, w_gb, w_lora, ctx_row, seq, True)
    oa_c = _attn_ctx(u_c, seq, *lam)
    ob_c, s_new = _rwkv(u_c, ul_c, seq, RWKV_GW_CTX, None, True, *lp)
    y_prompt = _outproj(oa_c, ob_c, xp, mod3, w_out_bf, fg, ctx_row).reshape(batch, seq, d)

    xs = x_sample.reshape(dbatch * dseq, d)
    u_l, ul_l = _inproj(xs, mod3, norm_g, w_bf, w_gb, w_lora, lat_row, dseq, False)
    cv = cache_v[:, 0].reshape(dbatch * past, d)
    ckt = jnp.transpose(cache_k[:, 0], (0, 2, 3, 4, 1))
    oa_l = _attn_lat(u_l, ckt, cv, dseq, past, *lam)
    (ob_l,) = _rwkv(u_l, ul_l, dseq, RWKV_GW_LAT, state_rwkv[:, 0], False, *lp)
    y_sample = _outproj(oa_l, ob_l, xs, mod3, w_out_bf, fg, lat_row).reshape(dbatch, dseq, d)

    new_s = s_new.reshape(batch, 1, 2, N_HEADS_B, HEAD_DIM_B, HEAD_DIM_B)
    return (y_prompt, y_sample, nk, nv, new_s)
```

```python
import functools
import math

import jax
import jax.numpy as jnp
from jax import lax
from jax.experimental import pallas as pl
from jax.experimental.pallas import tpu as pltpu

F32 = jnp.float32
BF16 = jnp.bfloat16
HIGHEST = lax.Precision.HIGHEST

D_MODEL = 1024
GRID_W = 64
HEAD_DIM_A = 64
N_HEADS_A = 8
HEAD_DIM_B = 64
N_HEADS_B = 16
LORA_R = 64
D_SHIFT = 3 * D_MODEL + 4 * LORA_R
D_IN = 4 * D_MODEL + D_SHIFT + D_MODEL
ROPE_BASE = 10000.0
NORM_EPS = 1e-6
SUBLN_EPS = 1e-5
LNX_EPS = 64e-5
LAM_INIT = 0.8 - 0.6 * math.exp(-0.3 * 0)
EXP_NEG_HALF = math.exp(-0.5)
LOG2E = math.log2(math.e)

HL = 2 * HEAD_DIM_A
HD_SHIFT = 6
INPROJ_TM = 1024
OUTPROJ_TM = 1024
ATTN_TQ = 512
MOD_TN = 512
GW = 256
LW = 4 * LORA_R
PW = 128
RWKV_GW_CTX = 1024
RWKV_GW_LAT = 512
CHUNK = 64
BLOCK_T = 256
VMEM_LIMIT = 48 * 1024 * 1024
VMEM_LIMIT_BIG = 56 * 1024 * 1024
SEC_Q, SEC_K, SEC_V, SEC_GA, SEC_R, SEC_KB, SEC_VB, SEC_GB = range(8)


def _nn(a, b):
    return jnp.dot(a, b, preferred_element_type=F32)


def _nt(a, b):
    return lax.dot_general(a, b, (((1,), (1,)), ((), ())), preferred_element_type=F32)


def _tn(a, b):
    return lax.dot_general(a, b, (((0,), (0,)), ((), ())), preferred_element_type=F32)


def _split2(x):
    hi = x.astype(BF16)
    lo = (x - hi.astype(F32)).astype(BF16)
    return hi, lo


def _sigmoid(x):
    return 0.5 * jnp.tanh(0.5 * x) + 0.5


def _silu(x):
    return x * _sigmoid(x)


def _mod_kernel(c_ref, w_ref, b_ref, o_ref):
    s = _silu(c_ref[...])
    o_ref[...] = jnp.dot(s, w_ref[...], precision=HIGHEST, preferred_element_type=F32) + b_ref[...]


def _mod(cvec, w_ada, b_ada):
    n = w_ada.shape[1]
    tn = MOD_TN
    return pl.pallas_call(
        _mod_kernel,
        grid=(n // tn,),
        in_specs=[
            pl.BlockSpec((8, D_MODEL), lambda j: (0, 0)),
            pl.BlockSpec((D_MODEL, tn), lambda j: (0, j)),
            pl.BlockSpec((1, tn), lambda j: (0, j)),
        ],
        out_specs=pl.BlockSpec((8, tn), lambda j: (0, j)),
        out_shape=jax.ShapeDtypeStruct((8, n), F32),
        name="mod",
    )(cvec, w_ada, b_ada.reshape(1, n))


def _inproj_kernel(*refs, seq, want_cache):
    x_ref, mod_ref, g_ref, w_ref, wgb_ref, wl_ref, um_ref, ul_ref = refs[:8]
    h_ref = refs[-1]
    j = pl.program_id(1)

    @pl.when(j == 0)
    def _():
        x = x_ref[...]
        y = x * lax.rsqrt(jnp.mean(x * x, axis=-1, keepdims=True) + NORM_EPS) * g_ref[...]
        mod = mod_ref[0]
        shift = mod[:, 0:D_MODEL]
        scale = mod[:, D_MODEL:2 * D_MODEL]
        h = (y * (1.0 + scale) + shift).astype(BF16)
        h_ref[...] = h
        ul_ref[...] = _nn(h, wl_ref[...]).astype(BF16)

    w = jnp.where(j == SEC_GB, wgb_ref[...], w_ref[...])
    acc = _nn(h_ref[...], w)
    um_ref[...] = acc.astype(BF16)
    if want_cache:
        nk_ref, nv_ref = refs[8:10]
        nb = acc.shape[0] // seq

        @pl.when(j == SEC_K)
        def _():
            nk_ref[:, 0] = acc.reshape((nb,) + nk_ref.shape[2:])

        @pl.when(j == SEC_V)
        def _():
            nv_ref[:, 0] = acc.reshape((nb,) + nv_ref.shape[2:])


def _inproj(x, mod3, norm_g, w_bf, w_gb, w_lora, mod_row, seq, want_cache):
    m = x.shape[0]
    tm = INPROJ_TM
    nsec = SEC_GB + 1
    nb = tm // seq
    out_specs = [pl.BlockSpec((tm, D_MODEL), lambda i, j: (i, j)),
                 pl.BlockSpec((tm, LW), lambda i, j: (i, 0))]
    out_shape = [jax.ShapeDtypeStruct((m, nsec * D_MODEL), BF16), jax.ShapeDtypeStruct((m, LW), BF16)]
    if want_cache:
        out_specs += [pl.BlockSpec((nb, 1, seq, N_HEADS_A, 2, HEAD_DIM_A), lambda i, j: (i, 0, 0, 0, 0, 0)),
                      pl.BlockSpec((nb, 1, seq, N_HEADS_A, 2 * HEAD_DIM_A), lambda i, j: (i, 0, 0, 0, 0))]
        out_shape += [jax.ShapeDtypeStruct((m // seq, 1, seq, N_HEADS_A, 2, HEAD_DIM_A), F32),
                      jax.ShapeDtypeStruct((m // seq, 1, seq, N_HEADS_A, 2 * HEAD_DIM_A), F32)]
    return pl.pallas_call(
        functools.partial(_inproj_kernel, seq=seq, want_cache=want_cache),
        grid=(m // tm, nsec),
        in_specs=[
            pl.BlockSpec((tm, D_MODEL), lambda i, j: (i, 0)),
            pl.BlockSpec((1, 1, 3 * D_MODEL), lambda i, j: (mod_row(i * tm), 0, 0)),
            pl.BlockSpec((1, D_MODEL), lambda i, j: (0, 0)),
            pl.BlockSpec((D_MODEL, D_MODEL), lambda i, j: (0, jnp.minimum(j, SEC_GB - 1))),
            pl.BlockSpec((D_MODEL, D_MODEL), lambda i, j: (0, 0)),
            pl.BlockSpec((D_MODEL, LW), lambda i, j: (0, 0)),
        ],
        out_specs=out_specs,
        out_shape=out_shape,
        scratch_shapes=[pltpu.VMEM((tm, D_MODEL), BF16)],
        compiler_params=pltpu.CompilerParams(
            dimension_semantics=("arbitrary", "arbitrary"), vmem_limit_bytes=VMEM_LIMIT_BIG),
        name="inproj_cache" if want_cache else "inproj",
    )(x, mod3, norm_g, w_bf, w_gb, w_lora)


def _lam(q1_ref, k1_ref, q2_ref, k2_ref):
    s1 = jnp.sum(q1_ref[...] * k1_ref[...], axis=-1, keepdims=True)
    s2 = jnp.sum(q2_ref[...] * k2_ref[...], axis=-1, keepdims=True)
    return jnp.exp(s1) - jnp.exp(s2) + LAM_INIT


def _diff_head(q, kb, vb, g, lam, subg, kct=None):
    lane = lax.broadcasted_iota(jnp.int32, q.shape, 1)
    qs = q * (HEAD_DIM_A ** -0.5 * LOG2E)
    v1 = jnp.concatenate([vb, jnp.ones_like(vb)], axis=1)
    outs = []
    for m in range(2):
        qm = jnp.where((lane >> HD_SHIFT) == m, qs, 0.0).astype(BF16)
        s = _nt(qm, kb)
        if kct is not None:
            s = jnp.concatenate([_nn(qm, kct), s], axis=1)
        p = jnp.exp2(s - jnp.max(s, axis=-1, keepdims=True)).astype(BF16)
        pv = _nn(p, v1)
        outs.append(pv[:, :HL] / pv[:, HL:])
    o = outs[0] - lam * outs[1]
    o = o * lax.rsqrt(jnp.mean(o * o, axis=-1, keepdims=True) + SUBLN_EPS) * subg
    o = o * (1.0 - LAM_INIT)
    return o * _silu(g)


def _attn_ctx_kernel(q_ref, k_ref, v_ref, g_ref, q1_ref, k1_ref, q2_ref, k2_ref, sg_ref, o_ref):
    lam = _lam(q1_ref, k1_ref, q2_ref, k2_ref)
    subg = sg_ref[...]
    for h in range(q_ref.shape[1] // HL):
        sl = slice(h * HL, (h + 1) * HL)
        o = _diff_head(q_ref[:, sl].astype(F32), k_ref[:, sl], v_ref[:, sl],
                       g_ref[:, sl].astype(F32), lam, subg)
        o_ref[:, sl] = o.astype(BF16)


def _attn_ctx(um, seq, lam_q1, lam_k1, lam_q2, lam_k2, subln_g):
    m = um.shape[0]
    small = pl.BlockSpec((1, HEAD_DIM_A), lambda b: (0, 0))
    blk = lambda sec: pl.BlockSpec((seq, D_MODEL), lambda b: (b, sec))
    return pl.pallas_call(
        _attn_ctx_kernel,
        grid=(m // seq,),
        in_specs=[blk(SEC_Q), blk(SEC_K), blk(SEC_V), blk(SEC_GA), small, small, small, small,
                  pl.BlockSpec((1, 2 * HEAD_DIM_A), lambda b: (0, 0))],
        out_specs=blk(0),
        out_shape=jax.ShapeDtypeStruct((m, D_MODEL), BF16),
        compiler_params=pltpu.CompilerParams(vmem_limit_bytes=VMEM_LIMIT),
        name="attn_ctx",
    )(um, um, um, um, lam_q1, lam_k1, lam_q2, lam_k2, subln_g)


def _rope(x, cos, s1, s2):
    return x * cos + pltpu.roll(x, 112, 1) * s1 + pltpu.roll(x, 16, 1) * s2


def _attn_lat_kernel(q_ref, k_ref, v_ref, g_ref, ck_ref, cv_ref, cq_ref, s1q_ref, s2q_ref,
                     ckk_ref, s1k_ref, s2k_ref, q1_ref, k1_ref, q2_ref, k2_ref, sg_ref,
                     o_ref, kct_ref, krot_ref, vall_ref):
    past = cv_ref.shape[0]
    w = q_ref.shape[1]
    nh = w // HL

    @pl.when(pl.program_id(2) == 0)
    def _():
        vall_ref[0:past, :] = cv_ref[...].astype(BF16)
        vall_ref[past:, :] = v_ref[...]
        for h in range(nh):
            sl = slice(h * HL, (h + 1) * HL)
            kct_ref[h] = ck_ref[0, h].reshape(HL, past).astype(BF16)
            krot_ref[:, sl] = _rope(k_ref[:, sl].astype(F32), ckk_ref[...], s1k_ref[...],
                                    s2k_ref[...]).astype(BF16)

    lam = _lam(q1_ref, k1_ref, q2_ref, k2_ref)
    subg = sg_ref[...]
    for h in range(nh):
        sl = slice(h * HL, (h + 1) * HL)
        q = _rope(q_ref[:, sl].astype(F32), cq_ref[...], s1q_ref[...], s2q_ref[...])
        o = _diff_head(q, krot_ref[:, sl], vall_ref[:, sl], g_ref[:, sl].astype(F32), lam, subg,
                       kct=kct_ref[h])
        o_ref[:, sl] = o.astype(BF16)


def _rope_tables(t):
    pos = jnp.arange(t)
    row = (pos // GRID_W).astype(F32)
    col = (pos % GRID_W).astype(F32)
    n_freq = HEAD_DIM_A // 4
    lane = jnp.arange(HL)
    d = lane % HEAD_DIM_A
    use_col = (d // 32) == 1
    second = ((d % 32) // 16) == 1
    inv = ROPE_BASE ** (-(d % n_freq).astype(F32) / n_freq)
    ang = jnp.where(use_col[None, :], col[:, None], row[:, None]) * inv[None, :]
    cos = jnp.cos(ang)
    sin = jnp.sin(ang)
    s1 = jnp.where(second[None, :], 0.0, -sin)
    s2 = jnp.where(second[None, :], sin, 0.0)
    return cos, s1, s2


def _attn_lat(u, cache_kt, cache_v2, seq, past, lam_q1, lam_k1, lam_q2, lam_k2, subln_g):
    m = u.shape[0]
    nbat = m // seq
    w = GW
    ng = D_MODEL // w
    tq = ATTN_TQ
    nq = seq // tq
    cos, s1, s2 = _rope_tables(seq)
    small = pl.BlockSpec((1, HEAD_DIM_A), lambda b, g, i: (0, 0))
    qblk = lambda sec: pl.BlockSpec((tq, w), lambda b, g, i: (b * nq + i, sec * ng + g))
    kblk = lambda sec: pl.BlockSpec((seq, w), lambda b, g, i: (b, sec * ng + g))
    ckblk = pl.BlockSpec((1, w // HL, 2, HEAD_DIM_A, past), lambda b, g, i: (b, g, 0, 0, 0))
    cblk = pl.BlockSpec((past, w), lambda b, g, i: (b, g))
    tq_blk = pl.BlockSpec((tq, HL), lambda b, g, i: (i, 0))
    tk_blk = pl.BlockSpec((seq, HL), lambda b, g, i: (0, 0))
    return pl.pallas_call(
        _attn_lat_kernel,
        grid=(nbat, ng, nq),
        in_specs=[qblk(SEC_Q), kblk(SEC_K), kblk(SEC_V), qblk(SEC_GA), ckblk, cblk,
                  tq_blk, tq_blk, tq_blk, tk_blk, tk_blk, tk_blk,
                  small, small, small, small,
                  pl.BlockSpec((1, 2 * HEAD_DIM_A), lambda b, g, i: (0, 0))],
        out_specs=pl.BlockSpec((tq, w), lambda b, g, i: (b * nq + i, g)),
        out_shape=jax.ShapeDtypeStruct((m, D_MODEL), BF16),
        scratch_shapes=[pltpu.VMEM((w // HL, HL, past), BF16), pltpu.VMEM((seq, w), BF16),
                        pltpu.VMEM((past + seq, w), BF16)],
        compiler_params=pltpu.CompilerParams(
            dimension_semantics=("arbitrary", "arbitrary", "arbitrary"), vmem_limit_bytes=VMEM_LIMIT),
        name="attn_lat",
    )(u, u, u, u, cache_kt, cache_v2, cos, s1, s2, cos, s1, s2,
      lam_q1, lam_k1, lam_q2, lam_k2, subln_g)


def _rwkv_consts():
    ri = lax.broadcasted_iota(jnp.int32, (PW, PW), 0)
    ci = lax.broadcasted_iota(jnp.int32, (PW, PW), 1)
    t = lax.broadcasted_iota(jnp.int32, (CHUNK, PW), 0)
    lane = lax.broadcasted_iota(jnp.int32, (CHUNK, PW), 1)
    j = lane & (CHUNK - 1)
    ti = lax.broadcasted_iota(jnp.int32, (CHUNK, CHUNK), 0)
    tj = lax.broadcasted_iota(jnp.int32, (CHUNK, CHUNK), 1)
    return dict(
        bmask=(ri >> HD_SHIFT) == (ci >> HD_SHIFT),
        eye=ri == ci,
        head0=lane < HEAD_DIM_B,
        eye_sbs=jnp.where(t == j, 1.0, 0.0),
        strict=(j < t, j > t),
        incl=(j <= t, j >= t),
        off=[((t >> (n + 1)) == (j >> (n + 1))) & ((t >> n) != (j >> n)) for n in range(6)],
        tri=(jnp.where(tj <= ti, 1.0, 0.0).astype(BF16), jnp.where(tj >= ti, 1.0, 0.0).astype(BF16)),
    )


def _wbd(x, cst):
    xb = x.astype(BF16)
    z = jnp.zeros_like(xb)
    return jnp.concatenate([jnp.where(cst["head0"], xb, z), jnp.where(cst["head0"], z, xb)], axis=0)


def _group_sum(x, e, exact=True):
    ew = e.shape[0]
    cols = range(0, x.shape[1], ew)
    if exact:
        hi, lo = _split2(x)
        parts = [_nn(hi[:, o:o + ew], e) + _nn(lo[:, o:o + ew], e) for o in cols]
    else:
        xb = x.astype(BF16)
        parts = [_nn(xb[:, o:o + ew], e) for o in cols]
    return parts[0] if len(parts) == 1 else jnp.concatenate(parts, axis=1)


def _pair_pipeline(h, ops, lg, states, cst, same_rows, out):
    L = CHUNK
    ncb = len(ops[0])
    idx = [(d, c) for d in range(2) for c in range(ncb)]
    half = lambda a: a[:, h * PW:(h + 1) * PW]

    ar, wbk, khbh, dg = {}, {}, {}, {}
    for d, c in idx:
        lw, r, kd, v, kk, b = (half(a) for a in ops[d][c])
        g = half(lg[d, c])
        g_l = g[0:1, :] if d == 1 else g[L - 1:L, :]
        at = -(kk * jnp.exp2(g - lw))
        rt = r * jnp.exp2(g)
        e_inv = jnp.exp2(-g)
        eg = jnp.exp2(g_l)
        bt, kt = b * e_inv, kd * e_inv
        ar[d, c] = jnp.concatenate([at, rt], axis=0).astype(BF16)
        wbk[d, c] = jnp.concatenate([_wbd(bt, cst), _wbd(kt, cst)], axis=0)
        khbh[d, c] = jnp.concatenate([(kt * eg).astype(BF16), (bt * eg).astype(BF16)], axis=0)
        dg[d, c] = jnp.where(cst["eye"], eg, 0.0).astype(BF16)
    yield

    gbk = {i: _nt(ar[i], wbk[i]) for i in idx}
    yield
    m_ab = {i: jnp.where(cst["strict"][i[0]], gbk[i][:L, :PW], 0.0) for i in idx}
    gb16 = {i: gbk[i].astype(BF16) for i in idx}
    zb = jnp.zeros((L, PW), BF16)
    m_abb = {i: jnp.where(cst["strict"][i[0]], gb16[i][:L, :PW], zb) for i in idx}
    m_rb = {i: jnp.where(cst["incl"][i[0]], gb16[i][L:, :PW], zb) for i in idx}
    m_akrk = {i: jnp.concatenate([jnp.where(cst["strict"][i[0]], gb16[i][:L, PW:], zb),
                                  jnp.where(cst["incl"][i[0]], gb16[i][L:, PW:], zb)], axis=0)
              for i in idx}

    x = {i: cst["eye_sbs"] + jnp.where(cst["off"][0], m_ab[i], 0.0) for i in idx}
    for off in cst["off"][1:]:
        p = {i: _nn(jnp.where(off, m_abb[i], zb), _wbd(x[i], cst)) for i in idx}
        yield
        x = {i: x[i] + _nn(x[i].astype(BF16), _wbd(p[i], cst)) for i in idx}
        yield
    xb = {i: x[i].astype(BF16) for i in idx}

    vsrc = {(d, c): (0, c) if same_rows else (d, c) for d, c in idx}
    vhalf = {i: half(ops[i[0]][i[1]][3]) for i in dict.fromkeys(vsrc.values())}
    wv = {i: _wbd(vhalf[i], cst) for i in vhalf}
    vb = {i: vhalf[i].astype(BF16) for i in vhalf}
    lk = {i: jnp.concatenate([jnp.concatenate([ar[i], m_akrk[i]], axis=1),
                              jnp.concatenate([dg[i], jnp.zeros_like(dg[i])], axis=1)], axis=0)
          for i in idx}

    s = [states[0][h], states[1][h]]
    for step in range(ncb):
        cur = [(0, step), (1, ncb - 1 - step)]
        ars = {i: _nn(lk[i], jnp.concatenate([s[i[0]].astype(BF16), wv[vsrc[i]]], axis=0)) for i in cur}
        yield
        u = {i: _nn(xb[i], _wbd(ars[i][:L], cst)) for i in cur}
        yield
        for i in cur:
            d, c = i
            out["y"][d, c, h] = ars[i][L:2 * L] + _nn(m_rb[i], _wbd(u[i], cst))
            upd = _tn(khbh[i], jnp.concatenate([vb[vsrc[i]], u[i].astype(BF16)], axis=0))
            s[d] = ars[i][2 * L:] + jnp.where(cst["bmask"], upd, 0.0)
        yield
    out["s"][0][h], out["s"][1][h] = s


def _block_step(ops, states, cst, same_rows):
    ncb = len(ops[0])
    npair = ops[0][0][0].shape[1] // PW

    lg = {}
    for d in range(2):
        for c in range(ncb):
            hi, mid = _split2(ops[d][c][0])
            tri = cst["tri"][d]
            lg[d, c] = _nn(tri, hi) + _nn(tri, mid)

    out = {"y": {}, "s": [[None] * npair, [None] * npair]}
    pending = [_pair_pipeline(h, ops, lg, states, cst, same_rows, out) for h in range(npair)]
    active = pending
    while active:
        active = [gen for gen in active if next(gen, "done") != "done"]
    ys = [[jnp.concatenate([out["y"][d, c, h] for h in range(npair)], axis=1) if npair > 1
           else out["y"][d, c, 0] for c in range(ncb)] for d in range(2)]
    return ys, out["s"]


def _rwkv_kernel(*refs, seq, gw, has_s0, want_state):
    it = iter(refs)
    r_ref, k_ref, v_ref, lo_ref, gb_ref = (next(it) for _ in range(5))
    mur_ref, muk_ref, muv_ref, mul_ref = (next(it) for _ in range(4))
    w0_ref, w2_ref, a0_ref, a2_ref = (next(it) for _ in range(4))
    kk_ref, ka_ref, rk_ref, lng_ref, lnb_ref = (next(it) for _ in range(5))
    s0_ref = next(it) if has_s0 else None
    ob_ref = next(it)
    sf_ref = next(it) if want_state else None
    r_s, v_s, kn_s, lw_s, kd_s, b_s, y_s, st_s = (next(it) for _ in range(8))

    T = seq
    cst = _rwkv_consts()
    npair = gw // PW
    hpp = PW // HEAD_DIM_B
    gi = lax.broadcasted_iota(jnp.int32, (LW, LW), 0)
    gj = lax.broadcasted_iota(jnp.int32, (LW, LW), 1)
    e = jnp.where((gi >> HD_SHIFT) == (gj >> HD_SHIFT), 1.0, 0.0).astype(BF16)
    def shifted(ref, mu_ref):
        x = ref[...].astype(F32)
        mu = mu_ref[...]
        row8 = lax.broadcasted_iota(jnp.int32, (8, x.shape[1]), 0)
        prev = pltpu.roll(x, 1, 0)
        nxt = pltpu.roll(x, T - 1, 0)
        prev = jnp.concatenate([jnp.where(row8 == 0, 0.0, prev[0:8]), prev[8:]], axis=0)
        nxt = jnp.concatenate([nxt[:T - 8], jnp.where(row8 == 7, 0.0, nxt[T - 8:])], axis=0)
        return x * (1.0 - mu[0:1, :] - mu[1:2, :]) + mu[0:1, :] * prev + mu[1:2, :] * nxt

    r = shifted(r_ref, mur_ref)
    k = shifted(k_ref, muk_ref)
    v = shifted(v_ref, muv_ref)
    lo = shifted(lo_ref, mul_ref)
    r_s[...] = r
    v_s[...] = v
    kk = k * kk_ref[...]
    kk = kk * lax.rsqrt(_group_sum(kk * kk, e, exact=False) + 1e-12)
    kn_s[...] = kk
    bonus_dot = _group_sum(r * k * rk_ref[...], e, exact=False)

    tanh_lo = jnp.tanh(lo).astype(BF16)
    lo_b = lo.astype(BF16)
    kka = k * ka_ref[...]

    def padded(w, before):
        parts = [jnp.zeros((before, gw), F32)] if before else []
        parts.append(w)
        after = LW - before - LORA_R
        if after:
            parts.append(jnp.zeros((after, gw), F32))
        return jnp.concatenate(parts, axis=0).astype(BF16)

    for d in range(2):
        w2p = padded(w2_ref[d], LORA_R * d)
        a2p = padded(a2_ref[d], LORA_R * (2 + d))
        z = w0_ref[d:d + 1, :] + _nn(tanh_lo, w2p)
        half_c = -0.5 * EXP_NEG_HALF * LOG2E
        lw_s[d] = half_c * jnp.tanh(0.5 * z) + half_c
        a = _sigmoid(a0_ref[d:d + 1, :] + _nn(lo_b, a2p))
        kd_s[d] = k + kka * (a - 1.0)
        b_s[d] = kk * a

    if has_s0:
        ident = jnp.where(cst["eye"], 1.0, 0.0).astype(BF16)
        for d in range(2):
            for h in range(npair):
                z0 = s0_ref[0, d, h * hpp:(h + 1) * hpp].reshape(PW, HEAD_DIM_B)
                hi, lo2 = _split2(z0)
                zt = _tn(hi, ident) + _tn(lo2, ident)
                st_s[d, h] = jnp.where(cst["bmask"], jnp.concatenate([zt] * hpp, axis=0), 0.0)
    else:
        st_s[...] = jnp.zeros(st_s.shape, F32)

    nblk = T // BLOCK_T
    ncb = BLOCK_T // CHUNK

    def block(i, carry):
        rows = []
        for d in range(2):
            blk = i if d == 0 else nblk - 1 - i
            rows.append([])
            for c in range(ncb):
                start = blk * BLOCK_T + c * CHUNK
                if not isinstance(start, int):
                    start = pl.multiple_of(start, CHUNK)
                rows[d].append(pl.ds(start, CHUNK))
        ops = [[(lw_s[d, rw, :], r_s[rw, :], kd_s[d, rw, :], v_s[rw, :], kn_s[rw, :], b_s[d, rw, :])
                for rw in rows[d]] for d in range(2)]
        states = [[st_s[d, h] for h in range(npair)] for d in range(2)]
        ys, s_new = _block_step(ops, states, cst, nblk == 1)
        for d in range(2):
            for c in range(ncb):
                y_s[d, rows[d][c], :] = ys[d][c]
            for h in range(npair):
                st_s[d, h] = s_new[d][h]
        return carry

    if nblk == 1:
        block(0, 0)
    else:
        lax.fori_loop(0, nblk, block, 0)

    y = y_s[0] + y_s[1]
    mu = _group_sum(y, e) * (1.0 / HEAD_DIM_B)
    dlt = y - mu
    var = _group_sum(dlt * dlt, e, exact=False) * (1.0 / HEAD_DIM_B)
    yn = dlt * lax.rsqrt(var + LNX_EPS) * lng_ref[...] + lnb_ref[...]
    out = (yn + bonus_dot * v_s[...]) * _silu(gb_ref[...].astype(F32))
    ob_ref[...] = out.astype(BF16)

    if want_state:
        ri = lax.broadcasted_iota(jnp.int32, (PW, HEAD_DIM_B), 0)
        ci = lax.broadcasted_iota(jnp.int32, (PW, HEAD_DIM_B), 1)
        fold = jnp.where((ri & (HEAD_DIM_B - 1)) == ci, 1.0, 0.0).astype(BF16)
        for d in range(2):
            for h in range(npair):
                hi, lo2 = _split2(st_s[d, h])
                sf = _tn(hi, fold) + _tn(lo2, fold)
                sf_ref[0, d, h * hpp:(h + 1) * hpp] = sf.reshape(hpp, HEAD_DIM_B, HEAD_DIM_B)


def _rwkv(u, ul, seq, gw, s0, want_state, shift_mu, decay_w0, decay_w2, iclr_a0, iclr_a2,
          k_k, k_a, r_k, lnx_g, lnx_b):
    m = u.shape[0]
    nbat = m // seq
    ng = D_MODEL // gw
    hg = gw // HEAD_DIM_B
    ublk = lambda sec: pl.BlockSpec((seq, gw), lambda b, g: (b, sec * ng + g))
    lblk = pl.BlockSpec((seq, LW), lambda b, g: (b, 0))
    mublk = lambda sec: pl.BlockSpec((2, gw), lambda b, g: (0, sec * ng + g))
    mulblk = pl.BlockSpec((2, LW), lambda b, g: (0, 3 * D_MODEL // LW))
    vec2 = pl.BlockSpec((2, gw), lambda b, g: (0, g))
    mat2 = pl.BlockSpec((2, LORA_R, gw), lambda b, g: (0, 0, g))
    vec1 = pl.BlockSpec((1, gw), lambda b, g: (0, g))
    sblk = pl.BlockSpec((1, 2, hg, HEAD_DIM_B, HEAD_DIM_B), lambda b, g: (b, 0, g, 0, 0))
    in_specs = [ublk(SEC_R), ublk(SEC_KB), ublk(SEC_VB), lblk, ublk(SEC_GB),
                mublk(0), mublk(1), mublk(2), mulblk,
                vec2, mat2, vec2, mat2, vec1, vec1, vec1, vec1, vec1]
    args = [u, u, u, ul, u, shift_mu, shift_mu, shift_mu, shift_mu,
            decay_w0, decay_w2, iclr_a0, iclr_a2,
            k_k.reshape(1, D_MODEL), k_a.reshape(1, D_MODEL), r_k.reshape(1, D_MODEL),
            lnx_g.reshape(1, D_MODEL), lnx_b.reshape(1, D_MODEL)]
    if s0 is not None:
        in_specs.append(sblk)
        args.append(s0)
    out_specs = [pl.BlockSpec((seq, gw), lambda b, g: (b, g))]
    out_shape = [jax.ShapeDtypeStruct((m, D_MODEL), BF16)]
    if want_state:
        out_specs.append(sblk)
        out_shape.append(jax.ShapeDtypeStruct((nbat, 2, N_HEADS_B, HEAD_DIM_B, HEAD_DIM_B), F32))
    tw = pltpu.VMEM((seq, gw), F32)
    tw2 = pltpu.VMEM((2, seq, gw), F32)
    res = pl.pallas_call(
        functools.partial(_rwkv_kernel, seq=seq, gw=gw, has_s0=s0 is not None, want_state=want_state),
        grid=(nbat, ng),
        in_specs=in_specs,
        out_specs=out_specs,
        out_shape=out_shape,
        scratch_shapes=[tw, tw, tw, tw2, tw2, tw2, tw2, pltpu.VMEM((2, gw // PW, PW, PW), F32)],
        compiler_params=pltpu.CompilerParams(
            dimension_semantics=("arbitrary", "arbitrary"), vmem_limit_bytes=VMEM_LIMIT_BIG),
        name="rwkv_state" if want_state else "rwkv",
    )(*args)
    return res


def _outproj_kernel(oa_ref, ob_ref, x_ref, mod_ref, w_ref, fg_ref, y_ref):
    acc = _nn(oa_ref[...], w_ref[0:D_MODEL, :]) + _nn(ob_ref[...], w_ref[D_MODEL:, :])
    gate = mod_ref[0][:, 2 * D_MODEL:]
    xo = x_ref[...] + gate * acc
    y_ref[...] = xo * lax.rsqrt(jnp.mean(xo * xo, axis=-1, keepdims=True) + NORM_EPS) * fg_ref[...]


def _outproj(o_a, o_b, x, mod3, w_out_bf, final_g, mod_row):
    m = x.shape[0]
    tm = OUTPROJ_TM
    blk = pl.BlockSpec((tm, D_MODEL), lambda i: (i, 0))
    return pl.pallas_call(
        _outproj_kernel,
        grid=(m // tm,),
        in_specs=[blk, blk, blk,
                  pl.BlockSpec((1, 1, 3 * D_MODEL), lambda i: (mod_row(i * tm), 0, 0)),
                  pl.BlockSpec((2 * D_MODEL, D_MODEL), lambda i: (0, 0)),
                  pl.BlockSpec((1, D_MODEL), lambda i: (0, 0))],
        out_specs=blk,
        out_shape=jax.ShapeDtypeStruct((m, D_MODEL), F32),
        compiler_params=pltpu.CompilerParams(vmem_limit_bytes=VMEM_LIMIT),
        name="outproj",
    )(o_a, o_b, x, mod3, w_out_bf, final_g)


def kernel(x_prompt, x_sample, cache_k, cache_v, state_rwkv, c, c_ctx, norm_g, w_ada, b_ada, w_in, lam_q1, lam_k1, lam_q2, lam_k2, subln_g, shift_mu, decay_w0, decay_w2, iclr_a0, iclr_a2, k_k, k_a, r_k, lnx_g, lnx_b, w_out, final_g):
    batch, seq, d = x_prompt.shape
    dbatch, dseq, _ = x_sample.shape
    past = cache_k.shape[2]
    assert d == D_MODEL and w_in.shape == (1, D_MODEL, D_IN) and dseq % GRID_W == 0

    cvec = jnp.concatenate([c_ctx[None, :], c, jnp.zeros((8 - 1 - dbatch, d), F32)], axis=0)
    mod3 = _mod(cvec, w_ada[0], b_ada[0]).reshape(8, 1, 3 * d)
    lo0 = 4 * D_MODEL + 3 * D_MODEL
    w_bf = w_in[0].astype(BF16)
    w_lora = w_bf[:, lo0:lo0 + LW]
    w_gb = w_bf[:, lo0 + LW:]
    w_out_bf = w_out[0].astype(BF16)
    ctx_row = lambda tok: 0
    lat_row = lambda tok: 1 + tok // dseq
    lp = (shift_mu[0], decay_w0[0], decay_w2[0], iclr_a0[0], iclr_a2[0], k_k[0], k_a[0], r_k[0],
          lnx_g[0], lnx_b[0])
    lam = (lam_q1, lam_k1, lam_q2, lam_k2, subln_g)
    fg = final_g.reshape(1, d)

    xp = x_prompt.reshape(batch * seq, d)
    u_c, ul_c, nk, nv = _inproj(xp, mod3, norm_g, w_bf, w_gb, w_lora, ctx_row, seq, True)
    oa_c = _attn_ctx(u_c, seq, *lam)
    ob_c, s_new = _rwkv(u_c, ul_c, seq, RWKV_GW_CTX, None, True, *lp)
    y_prompt = _outproj(oa_c, ob_c, xp, mod3, w_out_bf, fg, ctx_row).reshape(batch, seq, d)

    xs = x_sample.reshape(dbatch * dseq, d)
    u_l, ul_l = _inproj(xs, mod3, norm_g, w_bf, w_gb, w_lora, lat_row, dseq, False)
    cv = cache_v[:, 0].reshape(dbatch * past, d)
    ckt = jnp.transpose(cache_k[:, 0], (0, 2, 3, 4, 1))
    oa_l = _attn_lat(u_l, ckt, cv, dseq, past, *lam)
    (ob_l,) = _rwkv(u_l, ul_l, dseq, RWKV_GW_LAT, state_rwkv[:, 0], False, *lp)
    y_sample = _outproj(oa_l, ob_l, xs, mod3, w_out_bf, fg, lat_row).reshape(dbatch, dseq, d)

    new_s = s_new.reshape(batch, 1, 2, N_HEADS_B, HEAD_DIM_B, HEAD_DIM_B)
    return (y_prompt, y_sample, nk, nv, new_s)
```

```python
import functools
import math

import jax
import jax.numpy as jnp
from jax import lax
from jax.experimental import pallas as pl
from jax.experimental.pallas import tpu as pltpu

F32 = jnp.float32
BF16 = jnp.bfloat16
HIGHEST = lax.Precision.HIGHEST

D_MODEL = 1024
GRID_W = 64
HEAD_DIM_A = 64
N_HEADS_A = 8
HEAD_DIM_B = 64
N_HEADS_B = 16
LORA_R = 64
D_SHIFT = 3 * D_MODEL + 4 * LORA_R
D_IN = 4 * D_MODEL + D_SHIFT + D_MODEL
ROPE_BASE = 10000.0
NORM_EPS = 1e-6
SUBLN_EPS = 1e-5
LNX_EPS = 64e-5
LAM_INIT = 0.8 - 0.6 * math.exp(-0.3 * 0)
EXP_NEG_HALF = math.exp(-0.5)
LOG2E = math.log2(math.e)

HL = 2 * HEAD_DIM_A
HD_SHIFT = 6
INPROJ_TM = 1024
OUTPROJ_TM = 1024
ATTN_TQ = 512
MOD_TN = 512
GW = 256
LW = 4 * LORA_R
PW = 128
RWKV_GW_CTX = 1024
RWKV_GW_LAT = 512
CHUNK = 64
BLOCK_T = 256
VMEM_LIMIT = 48 * 1024 * 1024
VMEM_LIMIT_BIG = 56 * 1024 * 1024
SEC_Q, SEC_K, SEC_V, SEC_GA, SEC_R, SEC_KB, SEC_VB, SEC_GB = range(8)


def _nn(a, b):
    return jnp.dot(a, b, preferred_element_type=F32)


def _nt(a, b):
    return lax.dot_general(a, b, (((1,), (1,)), ((), ())), preferred_element_type=F32)


def _tn(a, b):
    return lax.dot_general(a, b, (((0,), (0,)), ((), ())), preferred_element_type=F32)


def _split2(x):
    hi = x.astype(BF16)
    lo = (x - hi.astype(F32)).astype(BF16)
    return hi, lo


def _sigmoid(x):
    return 0.5 * jnp.tanh(0.5 * x) + 0.5


def _silu(x):
    return x * _sigmoid(x)


def _mod_kernel(c_ref, w_ref, b_ref, o_ref):
    s = _silu(c_ref[...])
    o_ref[...] = jnp.dot(s, w_ref[...], precision=HIGHEST, preferred_element_type=F32) + b_ref[...]


def _mod(cvec, w_ada, b_ada):
    n = w_ada.shape[1]
    tn = MOD_TN
    return pl.pallas_call(
        _mod_kernel,
        grid=(n // tn,),
        in_specs=[
            pl.BlockSpec((8, D_MODEL), lambda j: (0, 0)),
            pl.BlockSpec((D_MODEL, tn), lambda j: (0, j)),
            pl.BlockSpec((1, tn), lambda j: (0, j)),
        ],
        out_specs=pl.BlockSpec((8, tn), lambda j: (0, j)),
        out_shape=jax.ShapeDtypeStruct((8, n), F32),
        name="mod",
    )(cvec, w_ada, b_ada.reshape(1, n))


def _token_shift(x, mu, seq):
    row8 = lax.broadcasted_iota(jnp.int32, (8, x.shape[1]), 0)
    keep = 1.0 - mu[0:1, :] - mu[1:2, :]
    outs = []
    for b in range(x.shape[0] // seq):
        xs = x[b * seq:(b + 1) * seq]
        prev = pltpu.roll(xs, 1, 0)
        nxt = pltpu.roll(xs, seq - 1, 0)
        prev = jnp.concatenate([jnp.where(row8 == 0, 0.0, prev[0:8]), prev[8:]], axis=0)
        nxt = jnp.concatenate([nxt[:seq - 8], jnp.where(row8 == 7, 0.0, nxt[seq - 8:])], axis=0)
        outs.append(xs * keep + mu[0:1, :] * prev + mu[1:2, :] * nxt)
    return outs[0] if len(outs) == 1 else jnp.concatenate(outs, axis=0)


def _inproj_kernel(*refs, seq, want_cache):
    x_ref, mod_ref, g_ref, w_ref, wgb_ref, wl_ref, mu_ref, mul_ref, um_ref, ul_ref = refs[:10]
    h_ref = refs[-1]
    j = pl.program_id(1)

    @pl.when(j == 0)
    def _():
        x = x_ref[...]
        y = x * lax.rsqrt(jnp.mean(x * x, axis=-1, keepdims=True) + NORM_EPS) * g_ref[...]
        mod = mod_ref[0]
        shift = mod[:, 0:D_MODEL]
        scale = mod[:, D_MODEL:2 * D_MODEL]
        h = (y * (1.0 + scale) + shift).astype(BF16)
        h_ref[...] = h
        ul_ref[...] = _token_shift(_nn(h, wl_ref[...]), mul_ref[...], seq).astype(BF16)

    rwkv_sec = (j >= SEC_R) & (j <= SEC_VB)

    @pl.when(rwkv_sec)
    def _():
        acc = _nn(h_ref[...], w_ref[...])
        um_ref[...] = _token_shift(acc, mu_ref[...], seq).astype(BF16)

    @pl.when(jnp.logical_not(rwkv_sec))
    def _():
        w = jnp.where(j == SEC_GB, wgb_ref[...], w_ref[...])
        acc = _nn(h_ref[...], w)
        um_ref[...] = acc.astype(BF16)
        if want_cache:
            nk_ref, nv_ref = refs[10:12]
            nb = acc.shape[0] // seq

            @pl.when(j == SEC_K)
            def _():
                nk_ref[:, 0] = acc.reshape((nb,) + nk_ref.shape[2:])

            @pl.when(j == SEC_V)
            def _():
                nv_ref[:, 0] = acc.reshape((nb,) + nv_ref.shape[2:])


def _inproj(x, mod3, norm_g, w_bf, w_gb, w_lora, shift_mu, mod_row, seq, want_cache):
    m = x.shape[0]
    tm = INPROJ_TM
    nsec = SEC_GB + 1
    nb = tm // seq
    out_specs = [pl.BlockSpec((tm, D_MODEL), lambda i, j: (i, j)),
                 pl.BlockSpec((tm, LW), lambda i, j: (i, 0))]
    out_shape = [jax.ShapeDtypeStruct((m, nsec * D_MODEL), BF16), jax.ShapeDtypeStruct((m, LW), BF16)]
    if want_cache:
        out_specs += [pl.BlockSpec((nb, 1, seq, N_HEADS_A, 2, HEAD_DIM_A), lambda i, j: (i, 0, 0, 0, 0, 0)),
                      pl.BlockSpec((nb, 1, seq, N_HEADS_A, 2 * HEAD_DIM_A), lambda i, j: (i, 0, 0, 0, 0))]
        out_shape += [jax.ShapeDtypeStruct((m // seq, 1, seq, N_HEADS_A, 2, HEAD_DIM_A), F32),
                      jax.ShapeDtypeStruct((m // seq, 1, seq, N_HEADS_A, 2 * HEAD_DIM_A), F32)]
    return pl.pallas_call(
        functools.partial(_inproj_kernel, seq=seq, want_cache=want_cache),
        grid=(m // tm, nsec),
        in_specs=[
            pl.BlockSpec((tm, D_MODEL), lambda i, j: (i, 0)),
            pl.BlockSpec((1, 1, 3 * D_MODEL), lambda i, j: (mod_row(i * tm), 0, 0)),
            pl.BlockSpec((1, D_MODEL), lambda i, j: (0, 0)),
            pl.BlockSpec((D_MODEL, D_MODEL), lambda i, j: (0, jnp.minimum(j, SEC_GB - 1))),
            pl.BlockSpec((D_MODEL, D_MODEL), lambda i, j: (0, 0)),
            pl.BlockSpec((D_MODEL, LW), lambda i, j: (0, 0)),
            pl.BlockSpec((2, D_MODEL), lambda i, j: (0, jnp.clip(j - SEC_R, 0, SEC_VB - SEC_R))),
            pl.BlockSpec((2, LW), lambda i, j: (0, 3 * D_MODEL // LW)),
        ],
        out_specs=out_specs,
        out_shape=out_shape,
        scratch_shapes=[pltpu.VMEM((tm, D_MODEL), BF16)],
        compiler_params=pltpu.CompilerParams(
            dimension_semantics=("arbitrary", "arbitrary"), vmem_limit_bytes=VMEM_LIMIT_BIG),
        name="inproj_cache" if want_cache else "inproj",
    )(x, mod3, norm_g, w_bf, w_gb, w_lora, shift_mu, shift_mu)


def _lam(q1_ref, k1_ref, q2_ref, k2_ref):
    s1 = jnp.sum(q1_ref[...] * k1_ref[...], axis=-1, keepdims=True)
    s2 = jnp.sum(q2_ref[...] * k2_ref[...], axis=-1, keepdims=True)
    return jnp.exp(s1) - jnp.exp(s2) + LAM_INIT


def _diff_head(q, kb, vb, g, lam, subg, kct=None):
    lane = lax.broadcasted_iota(jnp.int32, q.shape, 1)
    qs = q * (HEAD_DIM_A ** -0.5 * LOG2E)
    v1 = jnp.concatenate([vb, jnp.ones_like(vb)], axis=1)
    outs = []
    for m in range(2):
        qm = jnp.where((lane >> HD_SHIFT) == m, qs, 0.0).astype(BF16)
        s = _nt(qm, kb)
        if kct is not None:
            s = jnp.concatenate([_nn(qm, kct), s], axis=1)
        p = jnp.exp2(s - jnp.max(s, axis=-1, keepdims=True)).astype(BF16)
        pv = _nn(p, v1)
        outs.append(pv[:, :HL] / pv[:, HL:])
    o = outs[0] - lam * outs[1]
    o = o * lax.rsqrt(jnp.mean(o * o, axis=-1, keepdims=True) + SUBLN_EPS) * subg
    o = o * (1.0 - LAM_INIT)
    return o * _silu(g)


def _attn_ctx_kernel(q_ref, k_ref, v_ref, g_ref, q1_ref, k1_ref, q2_ref, k2_ref, sg_ref, o_ref):
    lam = _lam(q1_ref, k1_ref, q2_ref, k2_ref)
    subg = sg_ref[...]
    for h in range(q_ref.shape[1] // HL):
        sl = slice(h * HL, (h + 1) * HL)
        o = _diff_head(q_ref[:, sl].astype(F32), k_ref[:, sl], v_ref[:, sl],
                       g_ref[:, sl].astype(F32), lam, subg)
        o_ref[:, sl] = o.astype(BF16)


def _attn_ctx(um, seq, lam_q1, lam_k1, lam_q2, lam_k2, subln_g):
    m = um.shape[0]
    small = pl.BlockSpec((1, HEAD_DIM_A), lambda b: (0, 0))
    blk = lambda sec: pl.BlockSpec((seq, D_MODEL), lambda b: (b, sec))
    return pl.pallas_call(
        _attn_ctx_kernel,
        grid=(m // seq,),
        in_specs=[blk(SEC_Q), blk(SEC_K), blk(SEC_V), blk(SEC_GA), small, small, small, small,
                  pl.BlockSpec((1, 2 * HEAD_DIM_A), lambda b: (0, 0))],
        out_specs=blk(0),
        out_shape=jax.ShapeDtypeStruct((m, D_MODEL), BF16),
        compiler_params=pltpu.CompilerParams(vmem_limit_bytes=VMEM_LIMIT),
        name="attn_ctx",
    )(um, um, um, um, lam_q1, lam_k1, lam_q2, lam_k2, subln_g)


def _rope(x, cos, s1, s2):
    return x * cos + pltpu.roll(x, 112, 1) * s1 + pltpu.roll(x, 16, 1) * s2


def _attn_lat_kernel(q_ref, k_ref, v_ref, g_ref, ck_ref, cv_ref, cq_ref, s1q_ref, s2q_ref,
                     ckk_ref, s1k_ref, s2k_ref, q1_ref, k1_ref, q2_ref, k2_ref, sg_ref,
                     o_ref, kct_ref, krot_ref, vall_ref):
    past = cv_ref.shape[0]
    w = q_ref.shape[1]
    nh = w // HL

    @pl.when(pl.program_id(2) == 0)
    def _():
        vall_ref[0:past, :] = cv_ref[...].astype(BF16)
        vall_ref[past:, :] = v_ref[...]
        for h in range(nh):
            sl = slice(h * HL, (h + 1) * HL)
            kct_ref[h] = ck_ref[0, h].reshape(HL, past).astype(BF16)
            krot_ref[:, sl] = _rope(k_ref[:, sl].astype(F32), ckk_ref[...], s1k_ref[...],
                                    s2k_ref[...]).astype(BF16)

    lam = _lam(q1_ref, k1_ref, q2_ref, k2_ref)
    subg = sg_ref[...]
    for h in range(nh):
        sl = slice(h * HL, (h + 1) * HL)
        q = _rope(q_ref[:, sl].astype(F32), cq_ref[...], s1q_ref[...], s2q_ref[...])
        o = _diff_head(q, krot_ref[:, sl], vall_ref[:, sl], g_ref[:, sl].astype(F32), lam, subg,
                       kct=kct_ref[h])
        o_ref[:, sl] = o.astype(BF16)


def _rope_tables(t):
    pos = jnp.arange(t)
    row = (pos // GRID_W).astype(F32)
    col = (pos % GRID_W).astype(F32)
    n_freq = HEAD_DIM_A // 4
    lane = jnp.arange(HL)
    d = lane % HEAD_DIM_A
    use_col = (d // 32) == 1
    second = ((d % 32) // 16) == 1
    inv = ROPE_BASE ** (-(d % n_freq).astype(F32) / n_freq)
    ang = jnp.where(use_col[None, :], col[:, None], row[:, None]) * inv[None, :]
    cos = jnp.cos(ang)
    sin = jnp.sin(ang)
    s1 = jnp.where(second[None, :], 0.0, -sin)
    s2 = jnp.where(second[None, :], sin, 0.0)
    return cos, s1, s2


def _attn_lat(u, cache_kt, cache_v2, seq, past, lam_q1, lam_k1, lam_q2, lam_k2, subln_g):
    m = u.shape[0]
    nbat = m // seq
    w = GW
    ng = D_MODEL // w
    tq = ATTN_TQ
    nq = seq // tq
    cos, s1, s2 = _rope_tables(seq)
    small = pl.BlockSpec((1, HEAD_DIM_A), lambda b, g, i: (0, 0))
    qblk = lambda sec: pl.BlockSpec((tq, w), lambda b, g, i: (b * nq + i, sec * ng + g))
    kblk = lambda sec: pl.BlockSpec((seq, w), lambda b, g, i: (b, sec * ng + g))
    ckblk = pl.BlockSpec((1, w // HL, 2, HEAD_DIM_A, past), lambda b, g, i: (b, g, 0, 0, 0))
    cblk = pl.BlockSpec((past, w), lambda b, g, i: (b, g))
    tq_blk = pl.BlockSpec((tq, HL), lambda b, g, i: (i, 0))
    tk_blk = pl.BlockSpec((seq, HL), lambda b, g, i: (0, 0))
    return pl.pallas_call(
        _attn_lat_kernel,
        grid=(nbat, ng, nq),
        in_specs=[qblk(SEC_Q), kblk(SEC_K), kblk(SEC_V), qblk(SEC_GA), ckblk, cblk,
                  tq_blk, tq_blk, tq_blk, tk_blk, tk_blk, tk_blk,
                  small, small, small, small,
                  pl.BlockSpec((1, 2 * HEAD_DIM_A), lambda b, g, i: (0, 0))],
        out_specs=pl.BlockSpec((tq, w), lambda b, g, i: (b * nq + i, g)),
        out_shape=jax.ShapeDtypeStruct((m, D_MODEL), BF16),
        scratch_shapes=[pltpu.VMEM((w // HL, HL, past), BF16), pltpu.VMEM((seq, w), BF16),
                        pltpu.VMEM((past + seq, w), BF16)],
        compiler_params=pltpu.CompilerParams(
            dimension_semantics=("arbitrary", "arbitrary", "arbitrary"), vmem_limit_bytes=VMEM_LIMIT),
        name="attn_lat",
    )(u, u, u, u, cache_kt, cache_v2, cos, s1, s2, cos, s1, s2,
      lam_q1, lam_k1, lam_q2, lam_k2, subln_g)


def _rwkv_consts():
    ri = lax.broadcasted_iota(jnp.int32, (PW, PW), 0)
    ci = lax.broadcasted_iota(jnp.int32, (PW, PW), 1)
    t = lax.broadcasted_iota(jnp.int32, (CHUNK, PW), 0)
    lane = lax.broadcasted_iota(jnp.int32, (CHUNK, PW), 1)
    j = lane & (CHUNK - 1)
    ti = lax.broadcasted_iota(jnp.int32, (CHUNK, CHUNK), 0)
    tj = lax.broadcasted_iota(jnp.int32, (CHUNK, CHUNK), 1)
    return dict(
        bmask=(ri >> HD_SHIFT) == (ci >> HD_SHIFT),
        eye=ri == ci,
        head0=lane < HEAD_DIM_B,
        eye_sbs=jnp.where(t == j, 1.0, 0.0),
        strict=(j < t, j > t),
        incl=(j <= t, j >= t),
        off=[((t >> (n + 1)) == (j >> (n + 1))) & ((t >> n) != (j >> n)) for n in range(6)],
        tri=(jnp.where(tj <= ti, 1.0, 0.0).astype(BF16), jnp.where(tj >= ti, 1.0, 0.0).astype(BF16)),
    )


def _wbd(x, cst):
    xb = x.astype(BF16)
    z = jnp.zeros_like(xb)
    return jnp.concatenate([jnp.where(cst["head0"], xb, z), jnp.where(cst["head0"], z, xb)], axis=0)


def _group_sum(x, e, exact=True):
    ew = e.shape[0]
    cols = range(0, x.shape[1], ew)
    if exact:
        hi, lo = _split2(x)
        parts = [_nn(hi[:, o:o + ew], e) + _nn(lo[:, o:o + ew], e) for o in cols]
    else:
        xb = x.astype(BF16)
        parts = [_nn(xb[:, o:o + ew], e) for o in cols]
    return parts[0] if len(parts) == 1 else jnp.concatenate(parts, axis=1)


def _pair_pipeline(h, ops, lg, states, cst, same_rows, out):
    L = CHUNK
    ncb = len(ops[0])
    idx = [(d, c) for d in range(2) for c in range(ncb)]
    half = lambda a: a[:, h * PW:(h + 1) * PW]

    ar, wbk, khbh, dg = {}, {}, {}, {}
    for d, c in idx:
        lw, r, kd, v, kk, b = (half(a) for a in ops[d][c])
        g = half(lg[d, c])
        g_l = g[0:1, :] if d == 1 else g[L - 1:L, :]
        at = -(kk * jnp.exp2(g - lw))
        rt = r * jnp.exp2(g)
        e_inv = jnp.exp2(-g)
        eg = jnp.exp2(g_l)
        bt, kt = b * e_inv, kd * e_inv
        ar[d, c] = jnp.concatenate([at, rt], axis=0).astype(BF16)
        wbk[d, c] = jnp.concatenate([_wbd(bt, cst), _wbd(kt, cst)], axis=0)
        khbh[d, c] = jnp.concatenate([(kt * eg).astype(BF16), (bt * eg).astype(BF16)], axis=0)
        dg[d, c] = jnp.where(cst["eye"], eg, 0.0).astype(BF16)
    yield

    gbk = {i: _nt(ar[i], wbk[i]) for i in idx}
    yield
    m_ab = {i: jnp.where(cst["strict"][i[0]], gbk[i][:L, :PW], 0.0) for i in idx}
    gb16 = {i: gbk[i].astype(BF16) for i in idx}
    zb = jnp.zeros((L, PW), BF16)
    m_abb = {i: jnp.where(cst["strict"][i[0]], gb16[i][:L, :PW], zb) for i in idx}
    m_rb = {i: jnp.where(cst["incl"][i[0]], gb16[i][L:, :PW], zb) for i in idx}
    m_akrk = {i: jnp.concatenate([jnp.where(cst["strict"][i[0]], gb16[i][:L, PW:], zb),
                                  jnp.where(cst["incl"][i[0]], gb16[i][L:, PW:], zb)], axis=0)
              for i in idx}

    x = {i: cst["eye_sbs"] + jnp.where(cst["off"][0], m_ab[i], 0.0) for i in idx}
    for off in cst["off"][1:]:
        p = {i: _nn(jnp.where(off, m_abb[i], zb), _wbd(x[i], cst)) for i in idx}
        yield
        x = {i: x[i] + _nn(x[i].astype(BF16), _wbd(p[i], cst)) for i in idx}
        yield
    xb = {i: x[i].astype(BF16) for i in idx}

    vsrc = {(d, c): (0, c) if same_rows else (d, c) for d, c in idx}
    vhalf = {i: half(ops[i[0]][i[1]][3]) for i in dict.fromkeys(vsrc.values())}
    wv = {i: _wbd(vhalf[i], cst) for i in vhalf}
    vb = {i: vhalf[i].astype(BF16) for i in vhalf}
    lk = {i: jnp.concatenate([jnp.concatenate([ar[i], m_akrk[i]], axis=1),
                              jnp.concatenate([dg[i], jnp.zeros_like(dg[i])], axis=1)], axis=0)
          for i in idx}

    s = [states[0][h], states[1][h]]
    for step in range(ncb):
        cur = [(0, step), (1, ncb - 1 - step)]
        ars = {i: _nn(lk[i], jnp.concatenate([s[i[0]].astype(BF16), wv[vsrc[i]]], axis=0)) for i in cur}
        yield
        u = {i: _nn(xb[i], _wbd(ars[i][:L], cst)) for i in cur}
        yield
        for i in cur:
            d, c = i
            out["y"][d, c, h] = ars[i][L:2 * L] + _nn(m_rb[i], _wbd(u[i], cst))
            upd = _tn(khbh[i], jnp.concatenate([vb[vsrc[i]], u[i].astype(BF16)], axis=0))
            s[d] = ars[i][2 * L:] + jnp.where(cst["bmask"], upd, 0.0)
        yield
    out["s"][0][h], out["s"][1][h] = s


def _block_step(ops, states, cst, same_rows):
    ncb = len(ops[0])
    npair = ops[0][0][0].shape[1] // PW

    lg = {}
    for d in range(2):
        for c in range(ncb):
            hi, mid = _split2(ops[d][c][0])
            tri = cst["tri"][d]
            lg[d, c] = _nn(tri, hi) + _nn(tri, mid)

    out = {"y": {}, "s": [[None] * npair, [None] * npair]}
    pending = [_pair_pipeline(h, ops, lg, states, cst, same_rows, out) for h in range(npair)]
    active = pending
    while active:
        active = [gen for gen in active if next(gen, "done") != "done"]
    ys = [[jnp.concatenate([out["y"][d, c, h] for h in range(npair)], axis=1) if npair > 1
           else out["y"][d, c, 0] for c in range(ncb)] for d in range(2)]
    return ys, out["s"]


def _rwkv_kernel(*refs, seq, gw, has_s0, want_state):
    it = iter(refs)
    r_ref, k_ref, v_ref, lo_ref, gb_ref = (next(it) for _ in range(5))
    w0_ref, w2_ref, a0_ref, a2_ref = (next(it) for _ in range(4))
    kk_ref, ka_ref, rk_ref, lng_ref, lnb_ref = (next(it) for _ in range(5))
    s0_ref = next(it) if has_s0 else None
    ob_ref = next(it)
    sf_ref = next(it) if want_state else None
    r_s, v_s, kn_s, lw_s, kd_s, b_s, y_s, st_s = (next(it) for _ in range(8))

    T = seq
    cst = _rwkv_consts()
    npair = gw // PW
    hpp = PW // HEAD_DIM_B
    gi = lax.broadcasted_iota(jnp.int32, (LW, LW), 0)
    gj = lax.broadcasted_iota(jnp.int32, (LW, LW), 1)
    e = jnp.where((gi >> HD_SHIFT) == (gj >> HD_SHIFT), 1.0, 0.0).astype(BF16)

    r = r_ref[...].astype(F32)
    k = k_ref[...].astype(F32)
    v = v_ref[...].astype(F32)
    lo = lo_ref[...].astype(F32)
    r_s[...] = r
    v_s[...] = v
    kk = k * kk_ref[...]
    kk = kk * lax.rsqrt(_group_sum(kk * kk, e, exact=False) + 1e-12)
    kn_s[...] = kk
    bonus_dot = _group_sum(r * k * rk_ref[...], e, exact=False)

    tanh_lo = jnp.tanh(lo).astype(BF16)
    lo_b = lo.astype(BF16)
    kka = k * ka_ref[...]

    def padded(w, before):
        parts = [jnp.zeros((before, gw), F32)] if before else []
        parts.append(w)
        after = LW - before - LORA_R
        if after:
            parts.append(jnp.zeros((after, gw), F32))
        return jnp.concatenate(parts, axis=0).astype(BF16)

    for d in range(2):
        w2p = padded(w2_ref[d], LORA_R * d)
        a2p = padded(a2_ref[d], LORA_R * (2 + d))
        z = w0_ref[d:d + 1, :] + _nn(tanh_lo, w2p)
        half_c = -0.5 * EXP_NEG_HALF * LOG2E
        lw_s[d] = half_c * jnp.tanh(0.5 * z) + half_c
        a = _sigmoid(a0_ref[d:d + 1, :] + _nn(lo_b, a2p))
        kd_s[d] = k + kka * (a - 1.0)
        b_s[d] = kk * a

    if has_s0:
        ident = jnp.where(cst["eye"], 1.0, 0.0).astype(BF16)
        for d in range(2):
            for h in range(npair):
                z0 = s0_ref[0, d, h * hpp:(h + 1) * hpp].reshape(PW, HEAD_DIM_B)
                hi, lo2 = _split2(z0)
                zt = _tn(hi, ident) + _tn(lo2, ident)
                st_s[d, h] = jnp.where(cst["bmask"], jnp.concatenate([zt] * hpp, axis=0), 0.0)
    else:
        st_s[...] = jnp.zeros(st_s.shape, F32)

    nblk = T // BLOCK_T
    ncb = BLOCK_T // CHUNK

    def block(i, carry):
        rows = []
        for d in range(2):
            blk = i if d == 0 else nblk - 1 - i
            rows.append([])
            for c in range(ncb):
                start = blk * BLOCK_T + c * CHUNK
                if not isinstance(start, int):
                    start = pl.multiple_of(start, CHUNK)
                rows[d].append(pl.ds(start, CHUNK))
        ops = [[(lw_s[d, rw, :], r_s[rw, :], kd_s[d, rw, :], v_s[rw, :], kn_s[rw, :], b_s[d, rw, :])
                for rw in rows[d]] for d in range(2)]
        states = [[st_s[d, h] for h in range(npair)] for d in range(2)]
        ys, s_new = _block_step(ops, states, cst, nblk == 1)
        for d in range(2):
            for c in range(ncb):
                y_s[d, rows[d][c], :] = ys[d][c]
            for h in range(npair):
                st_s[d, h] = s_new[d][h]
        return carry

    if nblk == 1:
        block(0, 0)
    else:
        lax.fori_loop(0, nblk, block, 0)

    y = y_s[0] + y_s[1]
    mu = _group_sum(y, e) * (1.0 / HEAD_DIM_B)
    dlt = y - mu
    var = _group_sum(dlt * dlt, e, exact=False) * (1.0 / HEAD_DIM_B)
    yn = dlt * lax.rsqrt(var + LNX_EPS) * lng_ref[...] + lnb_ref[...]
    out = (yn + bonus_dot * v_s[...]) * _silu(gb_ref[...].astype(F32))
    ob_ref[...] = out.astype(BF16)

    if want_state:
        ri = lax.broadcasted_iota(jnp.int32, (PW, HEAD_DIM_B), 0)
        ci = lax.broadcasted_iota(jnp.int32, (PW, HEAD_DIM_B), 1)
        fold = jnp.where((ri & (HEAD_DIM_B - 1)) == ci, 1.0, 0.0).astype(BF16)
        for d in range(2):
            for h in range(npair):
                hi, lo2 = _split2(st_s[d, h])
                sf = _tn(hi, fold) + _tn(lo2, fold)
                sf_ref[0, d, h * hpp:(h + 1) * hpp] = sf.reshape(hpp, HEAD_DIM_B, HEAD_DIM_B)


def _rwkv(u, ul, seq, gw, s0, want_state, decay_w0, decay_w2, iclr_a0, iclr_a2,
          k_k, k_a, r_k, lnx_g, lnx_b):
    m = u.shape[0]
    nbat = m // seq
    ng = D_MODEL // gw
    hg = gw // HEAD_DIM_B
    ublk = lambda sec: pl.BlockSpec((seq, gw), lambda b, g: (b, sec * ng + g))
    lblk = pl.BlockSpec((seq, LW), lambda b, g: (b, 0))
    vec2 = pl.BlockSpec((2, gw), lambda b, g: (0, g))
    mat2 = pl.BlockSpec((2, LORA_R, gw), lambda b, g: (0, 0, g))
    vec1 = pl.BlockSpec((1, gw), lambda b, g: (0, g))
    sblk = pl.BlockSpec((1, 2, hg, HEAD_DIM_B, HEAD_DIM_B), lambda b, g: (b, 0, g, 0, 0))
    in_specs = [ublk(SEC_R), ublk(SEC_KB), ublk(SEC_VB), lblk, ublk(SEC_GB),
                vec2, mat2, vec2, mat2, vec1, vec1, vec1, vec1, vec1]
    args = [u, u, u, ul, u,
            decay_w0, decay_w2, iclr_a0, iclr_a2,
            k_k.reshape(1, D_MODEL), k_a.reshape(1, D_MODEL), r_k.reshape(1, D_MODEL),
            lnx_g.reshape(1, D_MODEL), lnx_b.reshape(1, D_MODEL)]
    if s0 is not None:
        in_specs.append(sblk)
        args.append(s0)
    out_specs = [pl.BlockSpec((seq, gw), lambda b, g: (b, g))]
    out_shape = [jax.ShapeDtypeStruct((m, D_MODEL), BF16)]
    if want_state:
        out_specs.append(sblk)
        out_shape.append(jax.ShapeDtypeStruct((nbat, 2, N_HEADS_B, HEAD_DIM_B, HEAD_DIM_B), F32))
    tw = pltpu.VMEM((seq, gw), F32)
    tw2 = pltpu.VMEM((2, seq, gw), F32)
    res = pl.pallas_call(
        functools.partial(_rwkv_kernel, seq=seq, gw=gw, has_s0=s0 is not None, want_state=want_state),
        grid=(nbat, ng),
        in_specs=in_specs,
        out_specs=out_specs,
        out_shape=out_shape,
        scratch_shapes=[tw, tw, tw, tw2, tw2, tw2, tw2, pltpu.VMEM((2, gw // PW, PW, PW), F32)],
        compiler_params=pltpu.CompilerParams(
            dimension_semantics=("arbitrary", "arbitrary"), vmem_limit_bytes=VMEM_LIMIT_BIG),
        name="rwkv_state" if want_state else "rwkv",
    )(*args)
    return res


def _outproj_kernel(oa_ref, ob_ref, x_ref, mod_ref, w_ref, fg_ref, y_ref):
    acc = _nn(oa_ref[...], w_ref[0:D_MODEL, :]) + _nn(ob_ref[...], w_ref[D_MODEL:, :])
    gate = mod_ref[0][:, 2 * D_MODEL:]
    xo = x_ref[...] + gate * acc
    y_ref[...] = xo * lax.rsqrt(jnp.mean(xo * xo, axis=-1, keepdims=True) + NORM_EPS) * fg_ref[...]


def _outproj(o_a, o_b, x, mod3, w_out_bf, final_g, mod_row):
    m = x.shape[0]
    tm = OUTPROJ_TM
    blk = pl.BlockSpec((tm, D_MODEL), lambda i: (i, 0))
    return pl.pallas_call(
        _outproj_kernel,
        grid=(m // tm,),
        in_specs=[blk, blk, blk,
                  pl.BlockSpec((1, 1, 3 * D_MODEL), lambda i: (mod_row(i * tm), 0, 0)),
                  pl.BlockSpec((2 * D_MODEL, D_MODEL), lambda i: (0, 0)),
                  pl.BlockSpec((1, D_MODEL), lambda i: (0, 0))],
        out_specs=blk,
        out_shape=jax.ShapeDtypeStruct((m, D_MODEL), F32),
        compiler_params=pltpu.CompilerParams(vmem_limit_bytes=VMEM_LIMIT),
        name="outproj",
    )(o_a, o_b, x, mod3, w_out_bf, final_g)


def kernel(x_prompt, x_sample, cache_k, cache_v, state_rwkv, c, c_ctx, norm_g, w_ada, b_ada, w_in, lam_q1, lam_k1, lam_q2, lam_k2, subln_g, shift_mu, decay_w0, decay_w2, iclr_a0, iclr_a2, k_k, k_a, r_k, lnx_g, lnx_b, w_out, final_g):
    batch, seq, d = x_prompt.shape
    dbatch, dseq, _ = x_sample.shape
    past = cache_k.shape[2]
    assert d == D_MODEL and w_in.shape == (1, D_MODEL, D_IN) and dseq % GRID_W == 0

    cvec = jnp.concatenate([c_ctx[None, :], c, jnp.zeros((8 - 1 - dbatch, d), F32)], axis=0)
    mod3 = _mod(cvec, w_ada[0], b_ada[0]).reshape(8, 1, 3 * d)
    lo0 = 4 * D_MODEL + 3 * D_MODEL
    w_bf = w_in[0].astype(BF16)
    w_lora = w_bf[:, lo0:lo0 + LW]
    w_gb = w_bf[:, lo0 + LW:]
    w_out_bf = w_out[0].astype(BF16)
    ctx_row = lambda tok: 0
    lat_row = lambda tok: 1 + tok // dseq
    lp = (decay_w0[0], decay_w2[0], iclr_a0[0], iclr_a2[0], k_k[0], k_a[0], r_k[0],
          lnx_g[0], lnx_b[0])
    mu = shift_mu[0]
    lam = (lam_q1, lam_k1, lam_q2, lam_k2, subln_g)
    fg = final_g.reshape(1, d)

    xp = x_prompt.reshape(batch * seq, d)
    u_c, ul_c, nk, nv = _inproj(xp, mod3, norm_g, w_bf, w_gb, w_lora, mu, ctx_row, seq, True)
    oa_c = _attn_ctx(u_c, seq, *lam)
    ob_c, s_new = _rwkv(u_c, ul_c, seq, RWKV_GW_CTX, None, True, *lp)
    y_prompt = _outproj(oa_c, ob_c, xp, mod3, w_out_bf, fg, ctx_row).reshape(batch, seq, d)

    xs = x_sample.reshape(dbatch * dseq, d)
    u_l, ul_l = _inproj(xs, mod3, norm_g, w_bf, w_gb, w_lora, mu, lat_row, dseq, False)
    cv = cache_v[:, 0].reshape(dbatch * past, d)
    ckt = jnp.transpose(cache_k[:, 0], (0, 2, 3, 4, 1))
    oa_l = _attn_lat(u_l, ckt, cv, dseq, past, *lam)
    (ob_l,) = _rwkv(u_l, ul_l, dseq, RWKV_GW_LAT, state_rwkv[:, 0], False, *lp)
    y_sample = _outproj(oa_l, ob_l, xs, mod3, w_out_bf, fg, lat_row).reshape(dbatch, dseq, d)

    new_s = s_new.reshape(batch, 1, 2, N_HEADS_B, HEAD_DIM_B, HEAD_DIM_B)
    return (y_prompt, y_sample, nk, nv, new_s)
```

```python
import functools
import math

import jax
import jax.numpy as jnp
from jax import lax
from jax.experimental import pallas as pl
from jax.experimental.pallas import tpu as pltpu

F32 = jnp.float32
BF16 = jnp.bfloat16
HIGHEST = lax.Precision.HIGHEST

D_MODEL = 1024
GRID_W = 64
HEAD_DIM_A = 64
N_HEADS_A = 8
HEAD_DIM_B = 64
N_HEADS_B = 16
LORA_R = 64
D_SHIFT = 3 * D_MODEL + 4 * LORA_R
D_IN = 4 * D_MODEL + D_SHIFT + D_MODEL
ROPE_BASE = 10000.0
NORM_EPS = 1e-6
SUBLN_EPS = 1e-5
LNX_EPS = 64e-5
LAM_INIT = 0.8 - 0.6 * math.exp(-0.3 * 0)
EXP_NEG_HALF = math.exp(-0.5)
LOG2E = math.log2(math.e)

HL = 2 * HEAD_DIM_A
HD_SHIFT = 6
INPROJ_TM = 1024
W_SLOTS = 3
OUTPROJ_TM = 1024
ATTN_TQ = 512
MOD_TN = 512
GW = 256
LW = 4 * LORA_R
PW = 128
RWKV_GW_CTX = 1024
RWKV_GW_LAT = 512
CHUNK = 64
BLOCK_T = 256
VMEM_LIMIT = 48 * 1024 * 1024
VMEM_LIMIT_BIG = 56 * 1024 * 1024
SEC_Q, SEC_K, SEC_V, SEC_GA, SEC_R, SEC_KB, SEC_VB, SEC_GB = range(8)


def _nn(a, b):
    return jnp.dot(a, b, preferred_element_type=F32)


def _nt(a, b):
    return lax.dot_general(a, b, (((1,), (1,)), ((), ())), preferred_element_type=F32)


def _tn(a, b):
    return lax.dot_general(a, b, (((0,), (0,)), ((), ())), preferred_element_type=F32)


def _split2(x):
    hi = x.astype(BF16)
    lo = (x - hi.astype(F32)).astype(BF16)
    return hi, lo


def _sigmoid(x):
    return 0.5 * jnp.tanh(0.5 * x) + 0.5


def _silu(x):
    return x * _sigmoid(x)


def _mod_kernel(c_ref, w_ref, b_ref, o_ref):
    s = _silu(c_ref[...])
    o_ref[...] = jnp.dot(s, w_ref[...], precision=HIGHEST, preferred_element_type=F32) + b_ref[...]


def _mod(cvec, w_ada, b_ada):
    n = w_ada.shape[1]
    tn = MOD_TN
    return pl.pallas_call(
        _mod_kernel,
        grid=(n // tn,),
        in_specs=[
            pl.BlockSpec((8, D_MODEL), lambda j: (0, 0)),
            pl.BlockSpec((D_MODEL, tn), lambda j: (0, j)),
            pl.BlockSpec((1, tn), lambda j: (0, j)),
        ],
        out_specs=pl.BlockSpec((8, tn), lambda j: (0, j)),
        out_shape=jax.ShapeDtypeStruct((8, n), F32),
        name="mod",
    )(cvec, w_ada, b_ada.reshape(1, n))


def _token_shift(x, mu, seq):
    row8 = lax.broadcasted_iota(jnp.int32, (8, x.shape[1]), 0)
    keep = 1.0 - mu[0:1, :] - mu[1:2, :]
    outs = []
    for b in range(x.shape[0] // seq):
        xs = x[b * seq:(b + 1) * seq]
        prev = pltpu.roll(xs, 1, 0)
        nxt = pltpu.roll(xs, seq - 1, 0)
        prev = jnp.concatenate([jnp.where(row8 == 0, 0.0, prev[0:8]), prev[8:]], axis=0)
        nxt = jnp.concatenate([nxt[:seq - 8], jnp.where(row8 == 7, 0.0, nxt[seq - 8:])], axis=0)
        outs.append(xs * keep + mu[0:1, :] * prev + mu[1:2, :] * nxt)
    return outs[0] if len(outs) == 1 else jnp.concatenate(outs, axis=0)


def _inproj_kernel(*refs, seq, want_cache):
    x_ref, mod_ref, g_ref, w_hbm, wl_ref, mu_ref, mul_ref, um_ref, ul_ref = refs[:9]
    h_ref, wbuf_ref, wsem = refs[-3:]
    j = pl.program_id(1)
    nsec = pl.num_programs(1)
    step = pl.program_id(0) * nsec + j
    nstep = pl.num_programs(0) * nsec

    def w_copy(s):
        sec = s % nsec
        col = pl.multiple_of(jnp.where(sec == SEC_GB, D_IN - D_MODEL, sec * D_MODEL), HL)
        slot = s % W_SLOTS
        return pltpu.make_async_copy(w_hbm.at[:, pl.ds(col, D_MODEL)], wbuf_ref.at[slot], wsem.at[slot])

    @pl.when(step == 0)
    def _():
        w_copy(step).start()
        w_copy(step + 1).start()

    @pl.when(step + 2 < nstep)
    def _():
        w_copy(step + 2).start()

    @pl.when(j == 0)
    def _():
        x = x_ref[...]
        y = x * lax.rsqrt(jnp.mean(x * x, axis=-1, keepdims=True) + NORM_EPS) * g_ref[...]
        mod = mod_ref[0]
        shift = mod[:, 0:D_MODEL]
        scale = mod[:, D_MODEL:2 * D_MODEL]
        h = (y * (1.0 + scale) + shift).astype(BF16)
        h_ref[...] = h
        ul_ref[...] = _token_shift(_nn(h, wl_ref[...]), mul_ref[...], seq).astype(BF16)

    w_copy(step).wait()
    w_ref = wbuf_ref.at[step % W_SLOTS]
    rwkv_sec = (j >= SEC_R) & (j <= SEC_VB)

    @pl.when(rwkv_sec)
    def _():
        acc = _nn(h_ref[...], w_ref[...])
        um_ref[...] = _token_shift(acc, mu_ref[...], seq).astype(BF16)

    @pl.when(jnp.logical_not(rwkv_sec))
    def _():
        acc = _nn(h_ref[...], w_ref[...])
        um_ref[...] = acc.astype(BF16)
        if want_cache:
            nk_ref, nv_ref = refs[9:11]
            nb = acc.shape[0] // seq

            @pl.when(j == SEC_K)
            def _():
                nk_ref[:, 0] = acc.reshape((nb,) + nk_ref.shape[2:])

            @pl.when(j == SEC_V)
            def _():
                nv_ref[:, 0] = acc.reshape((nb,) + nv_ref.shape[2:])


def _inproj(x, mod3, norm_g, w_bf, w_lora, shift_mu, mod_row, seq, want_cache):
    m = x.shape[0]
    tm = INPROJ_TM
    nsec = SEC_GB + 1
    nb = tm // seq
    out_specs = [pl.BlockSpec((tm, D_MODEL), lambda i, j: (i, j)),
                 pl.BlockSpec((tm, LW), lambda i, j: (i, 0))]
    out_shape = [jax.ShapeDtypeStruct((m, nsec * D_MODEL), BF16), jax.ShapeDtypeStruct((m, LW), BF16)]
    if want_cache:
        held = lambda i, j, sec: jnp.maximum(i - jnp.where(j < sec, 1, 0), 0)
        out_specs += [pl.BlockSpec((nb, 1, seq, N_HEADS_A, 2, HEAD_DIM_A),
                                   lambda i, j: (held(i, j, SEC_K), 0, 0, 0, 0, 0)),
                      pl.BlockSpec((nb, 1, seq, N_HEADS_A, 2 * HEAD_DIM_A),
                                   lambda i, j: (held(i, j, SEC_V), 0, 0, 0, 0))]
        out_shape += [jax.ShapeDtypeStruct((m // seq, 1, seq, N_HEADS_A, 2, HEAD_DIM_A), F32),
                      jax.ShapeDtypeStruct((m // seq, 1, seq, N_HEADS_A, 2 * HEAD_DIM_A), F32)]
    return pl.pallas_call(
        functools.partial(_inproj_kernel, seq=seq, want_cache=want_cache),
        grid=(m // tm, nsec),
        in_specs=[
            pl.BlockSpec((tm, D_MODEL), lambda i, j: (i, 0)),
            pl.BlockSpec((1, 1, 3 * D_MODEL), lambda i, j: (mod_row(i * tm), 0, 0)),
            pl.BlockSpec((1, D_MODEL), lambda i, j: (0, 0)),
            pl.BlockSpec(memory_space=pl.ANY),
            pl.BlockSpec((D_MODEL, LW), lambda i, j: (0, 0)),
            pl.BlockSpec((2, D_MODEL), lambda i, j: (0, jnp.clip(j - SEC_R, 0, SEC_VB - SEC_R))),
            pl.BlockSpec((2, LW), lambda i, j: (0, 3 * D_MODEL // LW)),
        ],
        out_specs=out_specs,
        out_shape=out_shape,
        scratch_shapes=[pltpu.VMEM((tm, D_MODEL), BF16),
                        pltpu.VMEM((W_SLOTS, D_MODEL, D_MODEL), BF16),
                        pltpu.SemaphoreType.DMA((W_SLOTS,))],
        compiler_params=pltpu.CompilerParams(
            dimension_semantics=("arbitrary", "arbitrary"), vmem_limit_bytes=VMEM_LIMIT_BIG),
        name="inproj_cache" if want_cache else "inproj",
    )(x, mod3, norm_g, w_bf, w_lora, shift_mu, shift_mu)


def _lam(q1_ref, k1_ref, q2_ref, k2_ref):
    s1 = jnp.sum(q1_ref[...] * k1_ref[...], axis=-1, keepdims=True)
    s2 = jnp.sum(q2_ref[...] * k2_ref[...], axis=-1, keepdims=True)
    return jnp.exp(s1) - jnp.exp(s2) + LAM_INIT


def _diff_head(q, kb, vb, g, lam, subg, kct=None):
    lane = lax.broadcasted_iota(jnp.int32, q.shape, 1)
    qs = q * (HEAD_DIM_A ** -0.5 * LOG2E)
    v1 = jnp.concatenate([vb, jnp.ones_like(vb)], axis=1)
    outs = []
    for m in range(2):
        qm = jnp.where((lane >> HD_SHIFT) == m, qs, 0.0).astype(BF16)
        s = _nt(qm, kb)
        if kct is not None:
            s = jnp.concatenate([_nn(qm, kct), s], axis=1)
        p = jnp.exp2(s - jnp.max(s, axis=-1, keepdims=True)).astype(BF16)
        pv = _nn(p, v1)
        outs.append(pv[:, :HL] / pv[:, HL:])
    o = outs[0] - lam * outs[1]
    o = o * lax.rsqrt(jnp.mean(o * o, axis=-1, keepdims=True) + SUBLN_EPS) * subg
    o = o * (1.0 - LAM_INIT)
    return o * _silu(g)


def _attn_ctx_kernel(q_ref, k_ref, v_ref, g_ref, q1_ref, k1_ref, q2_ref, k2_ref, sg_ref, o_ref):
    lam = _lam(q1_ref, k1_ref, q2_ref, k2_ref)
    subg = sg_ref[...]
    for h in range(q_ref.shape[1] // HL):
        sl = slice(h * HL, (h + 1) * HL)
        o = _diff_head(q_ref[:, sl].astype(F32), k_ref[:, sl], v_ref[:, sl],
                       g_ref[:, sl].astype(F32), lam, subg)
        o_ref[:, sl] = o.astype(BF16)


def _attn_ctx(um, seq, lam_q1, lam_k1, lam_q2, lam_k2, subln_g):
    m = um.shape[0]
    small = pl.BlockSpec((1, HEAD_DIM_A), lambda b: (0, 0))
    blk = lambda sec: pl.BlockSpec((seq, D_MODEL), lambda b: (b, sec))
    return pl.pallas_call(
        _attn_ctx_kernel,
        grid=(m // seq,),
        in_specs=[blk(SEC_Q), blk(SEC_K), blk(SEC_V), blk(SEC_GA), small, small, small, small,
                  pl.BlockSpec((1, 2 * HEAD_DIM_A), lambda b: (0, 0))],
        out_specs=blk(0),
        out_shape=jax.ShapeDtypeStruct((m, D_MODEL), BF16),
        compiler_params=pltpu.CompilerParams(vmem_limit_bytes=VMEM_LIMIT),
        name="attn_ctx",
    )(um, um, um, um, lam_q1, lam_k1, lam_q2, lam_k2, subln_g)


def _rope(x, cos, s1, s2):
    return x * cos + pltpu.roll(x, 112, 1) * s1 + pltpu.roll(x, 16, 1) * s2


def _attn_lat_kernel(q_ref, k_ref, v_ref, g_ref, ck_ref, cv_ref, cq_ref, s1q_ref, s2q_ref,
                     ckk_ref, s1k_ref, s2k_ref, q1_ref, k1_ref, q2_ref, k2_ref, sg_ref,
                     o_ref, kct_ref, krot_ref, vall_ref):
    past = cv_ref.shape[0]
    w = q_ref.shape[1]
    nh = w // HL

    @pl.when(pl.program_id(2) == 0)
    def _():
        vall_ref[0:past, :] = cv_ref[...].astype(BF16)
        vall_ref[past:, :] = v_ref[...]
        for h in range(nh):
            sl = slice(h * HL, (h + 1) * HL)
            kct_ref[h] = ck_ref[0, h].reshape(HL, past).astype(BF16)
            krot_ref[:, sl] = _rope(k_ref[:, sl].astype(F32), ckk_ref[...], s1k_ref[...],
                                    s2k_ref[...]).astype(BF16)

    lam = _lam(q1_ref, k1_ref, q2_ref, k2_ref)
    subg = sg_ref[...]
    for h in range(nh):
        sl = slice(h * HL, (h + 1) * HL)
        q = _rope(q_ref[:, sl].astype(F32), cq_ref[...], s1q_ref[...], s2q_ref[...])
        o = _diff_head(q, krot_ref[:, sl], vall_ref[:, sl], g_ref[:, sl].astype(F32), lam, subg,
                       kct=kct_ref[h])
        o_ref[:, sl] = o.astype(BF16)


def _rope_tables(t):
    pos = jnp.arange(t)
    row = (pos // GRID_W).astype(F32)
    col = (pos % GRID_W).astype(F32)
    n_freq = HEAD_DIM_A // 4
    lane = jnp.arange(HL)
    d = lane % HEAD_DIM_A
    use_col = (d // 32) == 1
    second = ((d % 32) // 16) == 1
    inv = ROPE_BASE ** (-(d % n_freq).astype(F32) / n_freq)
    ang = jnp.where(use_col[None, :], col[:, None], row[:, None]) * inv[None, :]
    cos = jnp.cos(ang)
    sin = jnp.sin(ang)
    s1 = jnp.where(second[None, :], 0.0, -sin)
    s2 = jnp.where(second[None, :], sin, 0.0)
    return cos, s1, s2


def _attn_lat(u, cache_kt, cache_v2, seq, past, lam_q1, lam_k1, lam_q2, lam_k2, subln_g):
    m = u.shape[0]
    nbat = m // seq
    w = GW
    ng = D_MODEL // w
    tq = ATTN_TQ
    nq = seq // tq
    cos, s1, s2 = _rope_tables(seq)
    small = pl.BlockSpec((1, HEAD_DIM_A), lambda b, g, i: (0, 0))
    qblk = lambda sec: pl.BlockSpec((tq, w), lambda b, g, i: (b * nq + i, sec * ng + g))
    kblk = lambda sec: pl.BlockSpec((seq, w), lambda b, g, i: (b, sec * ng + g))
    ckblk = pl.BlockSpec((1, w // HL, 2, HEAD_DIM_A, past), lambda b, g, i: (b, g, 0, 0, 0))
    cblk = pl.BlockSpec((past, w), lambda b, g, i: (b, g))
    tq_blk = pl.BlockSpec((tq, HL), lambda b, g, i: (i, 0))
    tk_blk = pl.BlockSpec((seq, HL), lambda b, g, i: (0, 0))
    return pl.pallas_call(
        _attn_lat_kernel,
        grid=(nbat, ng, nq),
        in_specs=[qblk(SEC_Q), kblk(SEC_K), kblk(SEC_V), qblk(SEC_GA), ckblk, cblk,
                  tq_blk, tq_blk, tq_blk, tk_blk, tk_blk, tk_blk,
                  small, small, small, small,
                  pl.BlockSpec((1, 2 * HEAD_DIM_A), lambda b, g, i: (0, 0))],
        out_specs=pl.BlockSpec((tq, w), lambda b, g, i: (b * nq + i, g)),
        out_shape=jax.ShapeDtypeStruct((m, D_MODEL), BF16),
        scratch_shapes=[pltpu.VMEM((w // HL, HL, past), BF16), pltpu.VMEM((seq, w), BF16),
                        pltpu.VMEM((past + seq, w), BF16)],
        compiler_params=pltpu.CompilerParams(
            dimension_semantics=("arbitrary", "arbitrary", "arbitrary"), vmem_limit_bytes=VMEM_LIMIT),
        name="attn_lat",
    )(u, u, u, u, cache_kt, cache_v2, cos, s1, s2, cos, s1, s2,
      lam_q1, lam_k1, lam_q2, lam_k2, subln_g)


def _rwkv_consts():
    ri = lax.broadcasted_iota(jnp.int32, (PW, PW), 0)
    ci = lax.broadcasted_iota(jnp.int32, (PW, PW), 1)
    t = lax.broadcasted_iota(jnp.int32, (CHUNK, PW), 0)
    lane = lax.broadcasted_iota(jnp.int32, (CHUNK, PW), 1)
    j = lane & (CHUNK - 1)
    ti = lax.broadcasted_iota(jnp.int32, (CHUNK, CHUNK), 0)
    tj = lax.broadcasted_iota(jnp.int32, (CHUNK, CHUNK), 1)
    return dict(
        bmask=(ri >> HD_SHIFT) == (ci >> HD_SHIFT),
        eye=ri == ci,
        head0=lane < HEAD_DIM_B,
        eye_sbs=jnp.where(t == j, 1.0, 0.0),
        strict=(j < t, j > t),
        incl=(j <= t, j >= t),
        off=[((t >> (n + 1)) == (j >> (n + 1))) & ((t >> n) != (j >> n)) for n in range(6)],
        tri=(jnp.where(tj <= ti, 1.0, 0.0).astype(BF16), jnp.where(tj >= ti, 1.0, 0.0).astype(BF16)),
    )


def _wbd(x, cst):
    xb = x.astype(BF16)
    z = jnp.zeros_like(xb)
    return jnp.concatenate([jnp.where(cst["head0"], xb, z), jnp.where(cst["head0"], z, xb)], axis=0)


def _group_sum(x, e, exact=True):
    ew = e.shape[0]
    cols = range(0, x.shape[1], ew)
    if exact:
        hi, lo = _split2(x)
        parts = [_nn(hi[:, o:o + ew], e) + _nn(lo[:, o:o + ew], e) for o in cols]
    else:
        xb = x.astype(BF16)
        parts = [_nn(xb[:, o:o + ew], e) for o in cols]
    return parts[0] if len(parts) == 1 else jnp.concatenate(parts, axis=1)


def _pair_pipeline(h, ops, lg, states, cst, same_rows, out):
    L = CHUNK
    ncb = len(ops[0])
    idx = [(d, c) for d in range(2) for c in range(ncb)]
    half = lambda a: a[:, h * PW:(h + 1) * PW]

    ar, wbk, khbh, dg = {}, {}, {}, {}
    for d, c in idx:
        lw, r, kd, v, kk, b = (half(a) for a in ops[d][c])
        g = half(lg[d, c])
        g_l = g[0:1, :] if d == 1 else g[L - 1:L, :]
        at = -(kk * jnp.exp2(g - lw))
        rt = r * jnp.exp2(g)
        e_inv = jnp.exp2(-g)
        eg = jnp.exp2(g_l)
        bt, kt = b * e_inv, kd * e_inv
        ar[d, c] = jnp.concatenate([at, rt], axis=0).astype(BF16)
        wbk[d, c] = jnp.concatenate([_wbd(bt, cst), _wbd(kt, cst)], axis=0)
        khbh[d, c] = jnp.concatenate([(kt * eg).astype(BF16), (bt * eg).astype(BF16)], axis=0)
        dg[d, c] = jnp.where(cst["eye"], eg, 0.0).astype(BF16)
    yield

    gbk = {i: _nt(ar[i], wbk[i]) for i in idx}
    yield
    m_ab = {i: jnp.where(cst["strict"][i[0]], gbk[i][:L, :PW], 0.0) for i in idx}
    gb16 = {i: gbk[i].astype(BF16) for i in idx}
    zb = jnp.zeros((L, PW), BF16)
    m_abb = {i: jnp.where(cst["strict"][i[0]], gb16[i][:L, :PW], zb) for i in idx}
    m_rb = {i: jnp.where(cst["incl"][i[0]], gb16[i][L:, :PW], zb) for i in idx}
    m_akrk = {i: jnp.concatenate([jnp.where(cst["strict"][i[0]], gb16[i][:L, PW:], zb),
                                  jnp.where(cst["incl"][i[0]], gb16[i][L:, PW:], zb)], axis=0)
              for i in idx}

    x = {i: cst["eye_sbs"] + jnp.where(cst["off"][0], m_ab[i], 0.0) for i in idx}
    for off in cst["off"][1:]:
        p = {i: _nn(jnp.where(off, m_abb[i], zb), _wbd(x[i], cst)) for i in idx}
        yield
        x = {i: x[i] + _nn(x[i].astype(BF16), _wbd(p[i], cst)) for i in idx}
        yield
    xb = {i: x[i].astype(BF16) for i in idx}

    vsrc = {(d, c): (0, c) if same_rows else (d, c) for d, c in idx}
    vhalf = {i: half(ops[i[0]][i[1]][3]) for i in dict.fromkeys(vsrc.values())}
    wv = {i: _wbd(vhalf[i], cst) for i in vhalf}
    vb = {i: vhalf[i].astype(BF16) for i in vhalf}
    lk = {i: jnp.concatenate([jnp.concatenate([ar[i], m_akrk[i]], axis=1),
                              jnp.concatenate([dg[i], jnp.zeros_like(dg[i])], axis=1)], axis=0)
          for i in idx}

    s = [states[0][h], states[1][h]]
    for step in range(ncb):
        cur = [(0, step), (1, ncb - 1 - step)]
        ars = {i: _nn(lk[i], jnp.concatenate([s[i[0]].astype(BF16), wv[vsrc[i]]], axis=0)) for i in cur}
        yield
        u = {i: _nn(xb[i], _wbd(ars[i][:L], cst)) for i in cur}
        yield
        for i in cur:
            d, c = i
            out["y"][d, c, h] = ars[i][L:2 * L] + _nn(m_rb[i], _wbd(u[i], cst))
            upd = _tn(khbh[i], jnp.concatenate([vb[vsrc[i]], u[i].astype(BF16)], axis=0))
            s[d] = ars[i][2 * L:] + jnp.where(cst["bmask"], upd, 0.0)
        yield
    out["s"][0][h], out["s"][1][h] = s


def _block_step(ops, states, cst, same_rows):
    ncb = len(ops[0])
    npair = ops[0][0][0].shape[1] // PW

    lg = {}
    for d in range(2):
        for c in range(ncb):
            hi, mid = _split2(ops[d][c][0])
            tri = cst["tri"][d]
            lg[d, c] = _nn(tri, hi) + _nn(tri, mid)

    out = {"y": {}, "s": [[None] * npair, [None] * npair]}
    pending = [_pair_pipeline(h, ops, lg, states, cst, same_rows, out) for h in range(npair)]
    active = pending
    while active:
        active = [gen for gen in active if next(gen, "done") != "done"]
    ys = [[jnp.concatenate([out["y"][d, c, h] for h in range(npair)], axis=1) if npair > 1
           else out["y"][d, c, 0] for c in range(ncb)] for d in range(2)]
    return ys, out["s"]


def _rwkv_kernel(*refs, seq, gw, has_s0, want_state):
    it = iter(refs)
    r_ref, k_ref, v_ref, lo_ref, gb_ref = (next(it) for _ in range(5))
    w0_ref, w2_ref, a0_ref, a2_ref = (next(it) for _ in range(4))
    kk_ref, ka_ref, rk_ref, lng_ref, lnb_ref = (next(it) for _ in range(5))
    s0_ref = next(it) if has_s0 else None
    ob_ref = next(it)
    sf_ref = next(it) if want_state else None
    r_s, v_s, kn_s, lw_s, kd_s, b_s, y_s, st_s = (next(it) for _ in range(8))

    T = seq
    cst = _rwkv_consts()
    npair = gw // PW
    hpp = PW // HEAD_DIM_B
    gi = lax.broadcasted_iota(jnp.int32, (LW, LW), 0)
    gj = lax.broadcasted_iota(jnp.int32, (LW, LW), 1)
    e = jnp.where((gi >> HD_SHIFT) == (gj >> HD_SHIFT), 1.0, 0.0).astype(BF16)

    r = r_ref[...].astype(F32)
    k = k_ref[...].astype(F32)
    v = v_ref[...].astype(F32)
    lo = lo_ref[...].astype(F32)
    r_s[...] = r
    v_s[...] = v
    kk = k * kk_ref[...]
    kk = kk * lax.rsqrt(_group_sum(kk * kk, e, exact=False) + 1e-12)
    kn_s[...] = kk
    bonus_dot = _group_sum(r * k * rk_ref[...], e, exact=False)

    tanh_lo = jnp.tanh(lo).astype(BF16)
    lo_b = lo.astype(BF16)
    kka = k * ka_ref[...]

    def padded(w, before):
        parts = [jnp.zeros((before, gw), F32)] if before else []
        parts.append(w)
        after = LW - before - LORA_R
        if after:
            parts.append(jnp.zeros((after, gw), F32))
        return jnp.concatenate(parts, axis=0).astype(BF16)

    for d in range(2):
        w2p = padded(w2_ref[d], LORA_R * d)
        a2p = padded(a2_ref[d], LORA_R * (2 + d))
        z = w0_ref[d:d + 1, :] + _nn(tanh_lo, w2p)
        half_c = -0.5 * EXP_NEG_HALF * LOG2E
        lw_s[d] = half_c * jnp.tanh(0.5 * z) + half_c
        a = _sigmoid(a0_ref[d:d + 1, :] + _nn(lo_b, a2p))
        kd_s[d] = k + kka * (a - 1.0)
        b_s[d] = kk * a

    if has_s0:
        ident = jnp.where(cst["eye"], 1.0, 0.0).astype(BF16)
        for d in range(2):
            for h in range(npair):
                z0 = s0_ref[0, d, h * hpp:(h + 1) * hpp].reshape(PW, HEAD_DIM_B)
                hi, lo2 = _split2(z0)
                zt = _tn(hi, ident) + _tn(lo2, ident)
                st_s[d, h] = jnp.where(cst["bmask"], jnp.concatenate([zt] * hpp, axis=0), 0.0)
    else:
        st_s[...] = jnp.zeros(st_s.shape, F32)

    nblk = T // BLOCK_T
    ncb = BLOCK_T // CHUNK

    def block(i, carry):
        rows = []
        for d in range(2):
            blk = i if d == 0 else nblk - 1 - i
            rows.append([])
            for c in range(ncb):
                start = blk * BLOCK_T + c * CHUNK
                if not isinstance(start, int):
                    start = pl.multiple_of(start, CHUNK)
                rows[d].append(pl.ds(start, CHUNK))
        ops = [[(lw_s[d, rw, :], r_s[rw, :], kd_s[d, rw, :], v_s[rw, :], kn_s[rw, :], b_s[d, rw, :])
                for rw in rows[d]] for d in range(2)]
        states = [[st_s[d, h] for h in range(npair)] for d in range(2)]
        ys, s_new = _block_step(ops, states, cst, nblk == 1)
        for d in range(2):
            for c in range(ncb):
                y_s[d, rows[d][c], :] = ys[d][c]
            for h in range(npair):
                st_s[d, h] = s_new[d][h]
        return carry

    if nblk == 1:
        block(0, 0)
    else:
        lax.fori_loop(0, nblk, block, 0)

    y = y_s[0] + y_s[1]
    mu = _group_sum(y, e) * (1.0 / HEAD_DIM_B)
    dlt = y - mu
    var = _group_sum(dlt * dlt, e, exact=False) * (1.0 / HEAD_DIM_B)
    yn = dlt * lax.rsqrt(var + LNX_EPS) * lng_ref[...] + lnb_ref[...]
    out = (yn + bonus_dot * v_s[...]) * _silu(gb_ref[...].astype(F32))
    ob_ref[...] = out.astype(BF16)

    if want_state:
        ri = lax.broadcasted_iota(jnp.int32, (PW, HEAD_DIM_B), 0)
        ci = lax.broadcasted_iota(jnp.int32, (PW, HEAD_DIM_B), 1)
        fold = jnp.where((ri & (HEAD_DIM_B - 1)) == ci, 1.0, 0.0).astype(BF16)
        for d in range(2):
            for h in range(npair):
                hi, lo2 = _split2(st_s[d, h])
                sf = _tn(hi, fold) + _tn(lo2, fold)
                sf_ref[0, d, h * hpp:(h + 1) * hpp] = sf.reshape(hpp, HEAD_DIM_B, HEAD_DIM_B)


def _rwkv(u, ul, seq, gw, s0, want_state, decay_w0, decay_w2, iclr_a0, iclr_a2,
          k_k, k_a, r_k, lnx_g, lnx_b):
    m = u.shape[0]
    nbat = m // seq
    ng = D_MODEL // gw
    hg = gw // HEAD_DIM_B
    ublk = lambda sec: pl.BlockSpec((seq, gw), lambda b, g: (b, sec * ng + g))
    lblk = pl.BlockSpec((seq, LW), lambda b, g: (b, 0))
    vec2 = pl.BlockSpec((2, gw), lambda b, g: (0, g))
    mat2 = pl.BlockSpec((2, LORA_R, gw), lambda b, g: (0, 0, g))
    vec1 = pl.BlockSpec((1, gw), lambda b, g: (0, g))
    sblk = pl.BlockSpec((1, 2, hg, HEAD_DIM_B, HEAD_DIM_B), lambda b, g: (b, 0, g, 0, 0))
    in_specs = [ublk(SEC_R), ublk(SEC_KB), ublk(SEC_VB), lblk, ublk(SEC_GB),
                vec2, mat2, vec2, mat2, vec1, vec1, vec1, vec1, vec1]
    args = [u, u, u, ul, u,
            decay_w0, decay_w2, iclr_a0, iclr_a2,
            k_k.reshape(1, D_MODEL), k_a.reshape(1, D_MODEL), r_k.reshape(1, D_MODEL),
            lnx_g.reshape(1, D_MODEL), lnx_b.reshape(1, D_MODEL)]
    if s0 is not None:
        in_specs.append(sblk)
        args.append(s0)
    out_specs = [pl.BlockSpec((seq, gw), lambda b, g: (b, g))]
    out_shape = [jax.ShapeDtypeStruct((m, D_MODEL), BF16)]
    if want_state:
        out_specs.append(sblk)
        out_shape.append(jax.ShapeDtypeStruct((nbat, 2, N_HEADS_B, HEAD_DIM_B, HEAD_DIM_B), F32))
    tw = pltpu.VMEM((seq, gw), F32)
    tw2 = pltpu.VMEM((2, seq, gw), F32)
    res = pl.pallas_call(
        functools.partial(_rwkv_kernel, seq=seq, gw=gw, has_s0=s0 is not None, want_state=want_state),
        grid=(nbat, ng),
        in_specs=in_specs,
        out_specs=out_specs,
        out_shape=out_shape,
        scratch_shapes=[tw, tw, tw, tw2, tw2, tw2, tw2, pltpu.VMEM((2, gw // PW, PW, PW), F32)],
        compiler_params=pltpu.CompilerParams(
            dimension_semantics=("arbitrary", "arbitrary"), vmem_limit_bytes=VMEM_LIMIT_BIG),
        name="rwkv_state" if want_state else "rwkv",
    )(*args)
    return res


def _outproj_kernel(oa_ref, ob_ref, x_ref, mod_ref, w_ref, fg_ref, y_ref):
    acc = _nn(oa_ref[...], w_ref[0:D_MODEL, :]) + _nn(ob_ref[...], w_ref[D_MODEL:, :])
    gate = mod_ref[0][:, 2 * D_MODEL:]
    xo = x_ref[...] + gate * acc
    y_ref[...] = xo * lax.rsqrt(jnp.mean(xo * xo, axis=-1, keepdims=True) + NORM_EPS) * fg_ref[...]


def _outproj(o_a, o_b, x, mod3, w_out_bf, final_g, mod_row):
    m = x.shape[0]
    tm = OUTPROJ_TM
    blk = pl.BlockSpec((tm, D_MODEL), lambda i: (i, 0))
    return pl.pallas_call(
        _outproj_kernel,
        grid=(m // tm,),
        in_specs=[blk, blk, blk,
                  pl.BlockSpec((1, 1, 3 * D_MODEL), lambda i: (mod_row(i * tm), 0, 0)),
                  pl.BlockSpec((2 * D_MODEL, D_MODEL), lambda i: (0, 0)),
                  pl.BlockSpec((1, D_MODEL), lambda i: (0, 0))],
        out_specs=blk,
        out_shape=jax.ShapeDtypeStruct((m, D_MODEL), F32),
        compiler_params=pltpu.CompilerParams(vmem_limit_bytes=VMEM_LIMIT),
        name="outproj",
    )(o_a, o_b, x, mod3, w_out_bf, final_g)


def kernel(x_prompt, x_sample, cache_k, cache_v, state_rwkv, c, c_ctx, norm_g, w_ada, b_ada, w_in, lam_q1, lam_k1, lam_q2, lam_k2, subln_g, shift_mu, decay_w0, decay_w2, iclr_a0, iclr_a2, k_k, k_a, r_k, lnx_g, lnx_b, w_out, final_g):
    batch, seq, d = x_prompt.shape
    dbatch, dseq, _ = x_sample.shape
    past = cache_k.shape[2]
    assert d == D_MODEL and w_in.shape == (1, D_MODEL, D_IN) and dseq % GRID_W == 0

    cvec = jnp.concatenate([c_ctx[None, :], c, jnp.zeros((8 - 1 - dbatch, d), F32)], axis=0)
    mod3 = _mod(cvec, w_ada[0], b_ada[0]).reshape(8, 1, 3 * d)
    lo0 = 4 * D_MODEL + 3 * D_MODEL
    w_bf = w_in[0].astype(BF16)
    w_lora = w_bf[:, lo0:lo0 + LW]
    w_out_bf = w_out[0].astype(BF16)
    ctx_row = lambda tok: 0
    lat_row = lambda tok: 1 + tok // dseq
    lp = (decay_w0[0], decay_w2[0], iclr_a0[0], iclr_a2[0], k_k[0], k_a[0], r_k[0],
          lnx_g[0], lnx_b[0])
    mu = shift_mu[0]
    lam = (lam_q1, lam_k1, lam_q2, lam_k2, subln_g)
    fg = final_g.reshape(1, d)

    xp = x_prompt.reshape(batch * seq, d)
    u_c, ul_c, nk, nv = _inproj(xp, mod3, norm_g, w_bf, w_lora, mu, ctx_row, seq, True)
    oa_c = _attn_ctx(u_c, seq, *lam)
    ob_c, s_new = _rwkv(u_c, ul_c, seq, RWKV_GW_CTX, None, True, *lp)
    y_prompt = _outproj(oa_c, ob_c, xp, mod3, w_out_bf, fg, ctx_row).reshape(batch, seq, d)

    xs = x_sample.reshape(dbatch * dseq, d)
    u_l, ul_l = _inproj(xs, mod3, norm_g, w_bf, w_lora, mu, lat_row, dseq, False)
    cv = cache_v[:, 0].reshape(dbatch * past, d)
    ckt = jnp.transpose(cache_k[:, 0], (0, 2, 3, 4, 1))
    oa_l = _attn_lat(u_l, ckt, cv, dseq, past, *lam)
    (ob_l,) = _rwkv(u_l, ul_l, dseq, RWKV_GW_LAT, state_rwkv[:, 0], False, *lp)
    y_sample = _outproj(oa_l, ob_l, xs, mod3, w_out_bf, fg, lat_row).reshape(dbatch, dseq, d)

    new_s = s_new.reshape(batch, 1, 2, N_HEADS_B, HEAD_DIM_B, HEAD_DIM_B)
    return (y_prompt, y_sample, nk, nv, new_s)
```

```python
import functools
import math

import jax
import jax.numpy as jnp
from jax import lax
from jax.experimental import pallas as pl
from jax.experimental.pallas import tpu as pltpu

F32 = jnp.float32
BF16 = jnp.bfloat16
HIGHEST = lax.Precision.HIGHEST

D_MODEL = 1024
GRID_W = 64
HEAD_DIM_A = 64
N_HEADS_A = 8
HEAD_DIM_B = 64
N_HEADS_B = 16
LORA_R = 64
D_SHIFT = 3 * D_MODEL + 4 * LORA_R
D_IN = 4 * D_MODEL + D_SHIFT + D_MODEL
ROPE_BASE = 10000.0
NORM_EPS = 1e-6
SUBLN_EPS = 1e-5
LNX_EPS = 64e-5
LAM_INIT = 0.8 - 0.6 * math.exp(-0.3 * 0)
EXP_NEG_HALF = math.exp(-0.5)
LOG2E = math.log2(math.e)

HL = 2 * HEAD_DIM_A
HD_SHIFT = 6
INPROJ_TM = 1024
W_SLOTS = 3
OUTPROJ_TM = 1024
ATTN_TQ = 512
MOD_TN = 512
GW = 256
LW = 4 * LORA_R
PW = 128
RWKV_GW_CTX = 1024
RWKV_GW_LAT = 512
CHUNK = 64
BLOCK_T = 256
VMEM_LIMIT = 48 * 1024 * 1024
VMEM_LIMIT_BIG = 56 * 1024 * 1024
SEC_Q, SEC_K, SEC_V, SEC_GA, SEC_R, SEC_KB, SEC_VB, SEC_GB = range(8)


def _nn(a, b):
    return jnp.dot(a, b, preferred_element_type=F32)


def _nt(a, b):
    return lax.dot_general(a, b, (((1,), (1,)), ((), ())), preferred_element_type=F32)


def _tn(a, b):
    return lax.dot_general(a, b, (((0,), (0,)), ((), ())), preferred_element_type=F32)


def _split2(x):
    hi = x.astype(BF16)
    lo = (x - hi.astype(F32)).astype(BF16)
    return hi, lo


def _sigmoid(x):
    return 0.5 * jnp.tanh(0.5 * x) + 0.5


def _silu(x):
    return x * _sigmoid(x)


def _mod_kernel(c_ref, w_ref, b_ref, o_ref):
    s = _silu(c_ref[...])
    o_ref[...] = jnp.dot(s, w_ref[...], precision=HIGHEST, preferred_element_type=F32) + b_ref[...]


def _mod(cvec, w_ada, b_ada):
    n = w_ada.shape[1]
    tn = MOD_TN
    return pl.pallas_call(
        _mod_kernel,
        grid=(n // tn,),
        in_specs=[
            pl.BlockSpec((8, D_MODEL), lambda j: (0, 0)),
            pl.BlockSpec((D_MODEL, tn), lambda j: (0, j)),
            pl.BlockSpec((1, tn), lambda j: (0, j)),
        ],
        out_specs=pl.BlockSpec((8, tn), lambda j: (0, j)),
        out_shape=jax.ShapeDtypeStruct((8, n), F32),
        name="mod",
    )(cvec, w_ada, b_ada.reshape(1, n))


def _token_shift(x, mu, seq):
    row8 = lax.broadcasted_iota(jnp.int32, (8, x.shape[1]), 0)
    keep = 1.0 - mu[0:1, :] - mu[1:2, :]
    outs = []
    for b in range(x.shape[0] // seq):
        xs = x[b * seq:(b + 1) * seq]
        prev = pltpu.roll(xs, 1, 0)
        nxt = pltpu.roll(xs, seq - 1, 0)
        prev = jnp.concatenate([jnp.where(row8 == 0, 0.0, prev[0:8]), prev[8:]], axis=0)
        nxt = jnp.concatenate([nxt[:seq - 8], jnp.where(row8 == 7, 0.0, nxt[seq - 8:])], axis=0)
        outs.append(xs * keep + mu[0:1, :] * prev + mu[1:2, :] * nxt)
    return outs[0] if len(outs) == 1 else jnp.concatenate(outs, axis=0)


def _inproj_kernel(*refs, seq, want_cache):
    x_ref, mod_ref, g_ref, w_hbm, wl_ref, mu_ref, mul_ref, um_ref, ul_ref = refs[:9]
    h_ref, wbuf_ref, wsem = refs[-3:]
    j = pl.program_id(1)
    nsec = pl.num_programs(1)
    step = pl.program_id(0) * nsec + j
    nstep = pl.num_programs(0) * nsec

    def w_copy(s):
        sec = s % nsec
        col = pl.multiple_of(jnp.where(sec == SEC_GB, D_IN - D_MODEL, sec * D_MODEL), HL)
        slot = s % W_SLOTS
        return pltpu.make_async_copy(w_hbm.at[:, pl.ds(col, D_MODEL)], wbuf_ref.at[slot], wsem.at[slot])

    @pl.when(step == 0)
    def _():
        w_copy(step).start()
        w_copy(step + 1).start()

    @pl.when(step + 2 < nstep)
    def _():
        w_copy(step + 2).start()

    @pl.when(j == 0)
    def _():
        x = x_ref[...]
        y = x * lax.rsqrt(jnp.mean(x * x, axis=-1, keepdims=True) + NORM_EPS) * g_ref[...]
        mod = mod_ref[0]
        shift = mod[:, 0:D_MODEL]
        scale = mod[:, D_MODEL:2 * D_MODEL]
        h = (y * (1.0 + scale) + shift).astype(BF16)
        h_ref[...] = h
        ul_ref[...] = _token_shift(_nn(h, wl_ref[...]), mul_ref[...], seq).astype(BF16)

    w_copy(step).wait()
    w_ref = wbuf_ref.at[step % W_SLOTS]
    rwkv_sec = (j >= SEC_R) & (j <= SEC_VB)

    @pl.when(rwkv_sec)
    def _():
        acc = _nn(h_ref[...], w_ref[...])
        um_ref[...] = _token_shift(acc, mu_ref[...], seq).astype(BF16)

    @pl.when(jnp.logical_not(rwkv_sec))
    def _():
        acc = _nn(h_ref[...], w_ref[...])
        um_ref[...] = acc.astype(BF16)
        if want_cache:
            nk_ref, nv_ref = refs[9:11]
            nb = acc.shape[0] // seq

            @pl.when(j == SEC_K)
            def _():
                nk_ref[:, 0] = acc.reshape((nb,) + nk_ref.shape[2:])

            @pl.when(j == SEC_V)
            def _():
                nv_ref[:, 0] = acc.reshape((nb,) + nv_ref.shape[2:])


def _inproj(x, mod3, norm_g, w_bf, w_lora, shift_mu, mod_row, seq, want_cache):
    m = x.shape[0]
    tm = INPROJ_TM
    nsec = SEC_GB + 1
    nb = tm // seq
    out_specs = [pl.BlockSpec((tm, D_MODEL), lambda i, j: (i, j)),
                 pl.BlockSpec((tm, LW), lambda i, j: (i, 0))]
    out_shape = [jax.ShapeDtypeStruct((m, nsec * D_MODEL), BF16), jax.ShapeDtypeStruct((m, LW), BF16)]
    if want_cache:
        held = lambda i, j, sec: jnp.maximum(i - jnp.where(j < sec, 1, 0), 0)
        out_specs += [pl.BlockSpec((nb, 1, seq, N_HEADS_A, 2, HEAD_DIM_A),
                                   lambda i, j: (held(i, j, SEC_K), 0, 0, 0, 0, 0)),
                      pl.BlockSpec((nb, 1, seq, N_HEADS_A, 2 * HEAD_DIM_A),
                                   lambda i, j: (held(i, j, SEC_V), 0, 0, 0, 0))]
        out_shape += [jax.ShapeDtypeStruct((m // seq, 1, seq, N_HEADS_A, 2, HEAD_DIM_A), F32),
                      jax.ShapeDtypeStruct((m // seq, 1, seq, N_HEADS_A, 2 * HEAD_DIM_A), F32)]
    return pl.pallas_call(
        functools.partial(_inproj_kernel, seq=seq, want_cache=want_cache),
        grid=(m // tm, nsec),
        in_specs=[
            pl.BlockSpec((tm, D_MODEL), lambda i, j: (i, 0)),
            pl.BlockSpec((1, 1, 3 * D_MODEL), lambda i, j: (mod_row(i * tm), 0, 0)),
            pl.BlockSpec((1, D_MODEL), lambda i, j: (0, 0)),
            pl.BlockSpec(memory_space=pl.ANY),
            pl.BlockSpec((D_MODEL, LW), lambda i, j: (0, 0)),
            pl.BlockSpec((2, D_MODEL), lambda i, j: (0, jnp.clip(j - SEC_R, 0, SEC_VB - SEC_R))),
            pl.BlockSpec((2, LW), lambda i, j: (0, 3 * D_MODEL // LW)),
        ],
        out_specs=out_specs,
        out_shape=out_shape,
        scratch_shapes=[pltpu.VMEM((tm, D_MODEL), BF16),
                        pltpu.VMEM((W_SLOTS, D_MODEL, D_MODEL), BF16),
                        pltpu.SemaphoreType.DMA((W_SLOTS,))],
        compiler_params=pltpu.CompilerParams(
            dimension_semantics=("arbitrary", "arbitrary"), vmem_limit_bytes=VMEM_LIMIT_BIG),
        name="inproj_cache" if want_cache else "inproj",
    )(x, mod3, norm_g, w_bf, w_lora, shift_mu, shift_mu)


def _lam(q1_ref, k1_ref, q2_ref, k2_ref):
    s1 = jnp.sum(q1_ref[...] * k1_ref[...], axis=-1, keepdims=True)
    s2 = jnp.sum(q2_ref[...] * k2_ref[...], axis=-1, keepdims=True)
    return jnp.exp(s1) - jnp.exp(s2) + LAM_INIT


def _diff_head(q, kb, vb, g, lam, subg, kct=None):
    lane = lax.broadcasted_iota(jnp.int32, q.shape, 1)
    qs = q * (HEAD_DIM_A ** -0.5 * LOG2E)
    v1 = jnp.concatenate([vb, jnp.ones_like(vb)], axis=1)
    outs = []
    for m in range(2):
        qm = jnp.where((lane >> HD_SHIFT) == m, qs, 0.0).astype(BF16)
        s = _nt(qm, kb)
        if kct is not None:
            s = jnp.concatenate([_nn(qm, kct), s], axis=1)
        p = jnp.exp2(s - jnp.max(s, axis=-1, keepdims=True)).astype(BF16)
        pv = _nn(p, v1)
        outs.append(pv[:, :HL] / pv[:, HL:])
    o = outs[0] - lam * outs[1]
    o = o * lax.rsqrt(jnp.mean(o * o, axis=-1, keepdims=True) + SUBLN_EPS) * subg
    o = o * (1.0 - LAM_INIT)
    return o * _silu(g)


def _attn_ctx_kernel(q_ref, k_ref, v_ref, g_ref, q1_ref, k1_ref, q2_ref, k2_ref, sg_ref, o_ref):
    lam = _lam(q1_ref, k1_ref, q2_ref, k2_ref)
    subg = sg_ref[...]
    for h in range(q_ref.shape[1] // HL):
        sl = slice(h * HL, (h + 1) * HL)
        o = _diff_head(q_ref[:, sl].astype(F32), k_ref[:, sl], v_ref[:, sl],
                       g_ref[:, sl].astype(F32), lam, subg)
        o_ref[:, sl] = o.astype(BF16)


def _attn_ctx(um, seq, lam_q1, lam_k1, lam_q2, lam_k2, subln_g):
    m = um.shape[0]
    small = pl.BlockSpec((1, HEAD_DIM_A), lambda b: (0, 0))
    blk = lambda sec: pl.BlockSpec((seq, D_MODEL), lambda b: (b, sec))
    return pl.pallas_call(
        _attn_ctx_kernel,
        grid=(m // seq,),
        in_specs=[blk(SEC_Q), blk(SEC_K), blk(SEC_V), blk(SEC_GA), small, small, small, small,
                  pl.BlockSpec((1, 2 * HEAD_DIM_A), lambda b: (0, 0))],
        out_specs=blk(0),
        out_shape=jax.ShapeDtypeStruct((m, D_MODEL), BF16),
        compiler_params=pltpu.CompilerParams(vmem_limit_bytes=VMEM_LIMIT),
        name="attn_ctx",
    )(um, um, um, um, lam_q1, lam_k1, lam_q2, lam_k2, subln_g)


def _rope(x, cos, s1, s2):
    return x * cos + pltpu.roll(x, 112, 1) * s1 + pltpu.roll(x, 16, 1) * s2


def _attn_lat_kernel(q_ref, k_ref, v_ref, g_ref, ck_ref, cv_ref, cq_ref, s1q_ref, s2q_ref,
                     ckk_ref, s1k_ref, s2k_ref, q1_ref, k1_ref, q2_ref, k2_ref, sg_ref,
                     o_ref, kct_ref, krot_ref, vall_ref):
    past = cv_ref.shape[0]
    w = q_ref.shape[1]
    nh = w // HL

    @pl.when(pl.program_id(2) == 0)
    def _():
        vall_ref[0:past, :] = cv_ref[...].astype(BF16)
        vall_ref[past:, :] = v_ref[...]
        for h in range(nh):
            sl = slice(h * HL, (h + 1) * HL)
            kct_ref[h] = ck_ref[0, h].reshape(HL, past).astype(BF16)
            krot_ref[:, sl] = _rope(k_ref[:, sl].astype(F32), ckk_ref[...], s1k_ref[...],
                                    s2k_ref[...]).astype(BF16)

    lam = _lam(q1_ref, k1_ref, q2_ref, k2_ref)
    subg = sg_ref[...]
    for h in range(nh):
        sl = slice(h * HL, (h + 1) * HL)
        q = _rope(q_ref[:, sl].astype(F32), cq_ref[...], s1q_ref[...], s2q_ref[...])
        o = _diff_head(q, krot_ref[:, sl], vall_ref[:, sl], g_ref[:, sl].astype(F32), lam, subg,
                       kct=kct_ref[h])
        o_ref[:, sl] = o.astype(BF16)


def _rope_tables(t):
    pos = jnp.arange(t)
    row = (pos // GRID_W).astype(F32)
    col = (pos % GRID_W).astype(F32)
    n_freq = HEAD_DIM_A // 4
    lane = jnp.arange(HL)
    d = lane % HEAD_DIM_A
    use_col = (d // 32) == 1
    second = ((d % 32) // 16) == 1
    inv = ROPE_BASE ** (-(d % n_freq).astype(F32) / n_freq)
    ang = jnp.where(use_col[None, :], col[:, None], row[:, None]) * inv[None, :]
    cos = jnp.cos(ang)
    sin = jnp.sin(ang)
    s1 = jnp.where(second[None, :], 0.0, -sin)
    s2 = jnp.where(second[None, :], sin, 0.0)
    return cos, s1, s2


def _attn_lat(u, cache_kt, cache_v2, seq, past, lam_q1, lam_k1, lam_q2, lam_k2, subln_g):
    m = u.shape[0]
    nbat = m // seq
    w = GW
    ng = D_MODEL // w
    tq = ATTN_TQ
    nq = seq // tq
    cos, s1, s2 = _rope_tables(seq)
    small = pl.BlockSpec((1, HEAD_DIM_A), lambda b, g, i: (0, 0))
    qblk = lambda sec: pl.BlockSpec((tq, w), lambda b, g, i: (b * nq + i, sec * ng + g))
    kblk = lambda sec: pl.BlockSpec((seq, w), lambda b, g, i: (b, sec * ng + g))
    ckblk = pl.BlockSpec((1, w // HL, 2, HEAD_DIM_A, past), lambda b, g, i: (b, g, 0, 0, 0))
    cblk = pl.BlockSpec((past, w), lambda b, g, i: (b, g))
    tq_blk = pl.BlockSpec((tq, HL), lambda b, g, i: (i, 0))
    tk_blk = pl.BlockSpec((seq, HL), lambda b, g, i: (0, 0))
    return pl.pallas_call(
        _attn_lat_kernel,
        grid=(nbat, ng, nq),
        in_specs=[qblk(SEC_Q), kblk(SEC_K), kblk(SEC_V), qblk(SEC_GA), ckblk, cblk,
                  tq_blk, tq_blk, tq_blk, tk_blk, tk_blk, tk_blk,
                  small, small, small, small,
                  pl.BlockSpec((1, 2 * HEAD_DIM_A), lambda b, g, i: (0, 0))],
        out_specs=pl.BlockSpec((tq, w), lambda b, g, i: (b * nq + i, g)),
        out_shape=jax.ShapeDtypeStruct((m, D_MODEL), BF16),
        scratch_shapes=[pltpu.VMEM((w // HL, HL, past), BF16), pltpu.VMEM((seq, w), BF16),
                        pltpu.VMEM((past + seq, w), BF16)],
        compiler_params=pltpu.CompilerParams(
            dimension_semantics=("arbitrary", "arbitrary", "arbitrary"), vmem_limit_bytes=VMEM_LIMIT),
        name="attn_lat",
    )(u, u, u, u, cache_kt, cache_v2, cos, s1, s2, cos, s1, s2,
      lam_q1, lam_k1, lam_q2, lam_k2, subln_g)


def _rwkv_consts():
    ri = lax.broadcasted_iota(jnp.int32, (PW, PW), 0)
    ci = lax.broadcasted_iota(jnp.int32, (PW, PW), 1)
    t = lax.broadcasted_iota(jnp.int32, (CHUNK, PW), 0)
    lane = lax.broadcasted_iota(jnp.int32, (CHUNK, PW), 1)
    j = lane & (CHUNK - 1)
    ti = lax.broadcasted_iota(jnp.int32, (CHUNK, CHUNK), 0)
    tj = lax.broadcasted_iota(jnp.int32, (CHUNK, CHUNK), 1)
    return dict(
        bmask=(ri >> HD_SHIFT) == (ci >> HD_SHIFT),
        eye=ri == ci,
        head0=lane < HEAD_DIM_B,
        eye_sbs=jnp.where(t == j, 1.0, 0.0),
        strict=(j < t, j > t),
        incl=(j <= t, j >= t),
        off=[((t >> (n + 1)) == (j >> (n + 1))) & ((t >> n) != (j >> n)) for n in range(6)],
        tri=(jnp.where(tj <= ti, 1.0, 0.0).astype(BF16), jnp.where(tj >= ti, 1.0, 0.0).astype(BF16)),
    )


def _wbd(x, cst):
    xb = x.astype(BF16)
    z = jnp.zeros_like(xb)
    return jnp.concatenate([jnp.where(cst["head0"], xb, z), jnp.where(cst["head0"], z, xb)], axis=0)


def _group_sum(x, e, exact=True):
    ew = e.shape[0]
    cols = range(0, x.shape[1], ew)
    if exact:
        hi, lo = _split2(x)
        parts = [_nn(hi[:, o:o + ew], e) + _nn(lo[:, o:o + ew], e) for o in cols]
    else:
        xb = x.astype(BF16)
        parts = [_nn(xb[:, o:o + ew], e) for o in cols]
    return parts[0] if len(parts) == 1 else jnp.concatenate(parts, axis=1)


def _pair_pipeline(h, ops, lg, states, cst, same_rows, out):
    L = CHUNK
    ncb = len(ops[0])
    idx = [(d, c) for d in range(2) for c in range(ncb)]
    half = lambda a: a[:, h * PW:(h + 1) * PW]

    ar, wbk, khbh, dg = {}, {}, {}, {}
    for d, c in idx:
        lw, r, kd, v, kk, b = (half(a) for a in ops[d][c])
        g = half(lg[d, c])
        g_l = g[0:1, :] if d == 1 else g[L - 1:L, :]
        at = -(kk * jnp.exp2(g - lw))
        rt = r * jnp.exp2(g)
        e_inv = jnp.exp2(-g)
        eg = jnp.exp2(g_l)
        bt, kt = b * e_inv, kd * e_inv
        ar[d, c] = jnp.concatenate([at, rt], axis=0).astype(BF16)
        wbk[d, c] = jnp.concatenate([_wbd(bt, cst), _wbd(kt, cst)], axis=0)
        khbh[d, c] = jnp.concatenate([(kt * eg).astype(BF16), (bt * eg).astype(BF16)], axis=0)
        dg[d, c] = jnp.where(cst["eye"], eg, 0.0).astype(BF16)
    yield

    gbk = {i: _nt(ar[i], wbk[i]) for i in idx}
    yield
    m_ab = {i: jnp.where(cst["strict"][i[0]], gbk[i][:L, :PW], 0.0) for i in idx}
    gb16 = {i: gbk[i].astype(BF16) for i in idx}
    zb = jnp.zeros((L, PW), BF16)
    m_abb = {i: jnp.where(cst["strict"][i[0]], gb16[i][:L, :PW], zb) for i in idx}
    m_rb = {i: jnp.where(cst["incl"][i[0]], gb16[i][L:, :PW], zb) for i in idx}
    m_akrk = {i: jnp.concatenate([jnp.where(cst["strict"][i[0]], gb16[i][:L, PW:], zb),
                                  jnp.where(cst["incl"][i[0]], gb16[i][L:, PW:], zb)], axis=0)
              for i in idx}

    x = {i: cst["eye_sbs"] + jnp.where(cst["off"][0], m_ab[i], 0.0) for i in idx}
    for off in cst["off"][1:]:
        p = {i: _nn(jnp.where(off, m_abb[i], zb), _wbd(x[i], cst)) for i in idx}
        yield
        x = {i: x[i] + _nn(x[i].astype(BF16), _wbd(p[i], cst)) for i in idx}
        yield
    xb = {i: x[i].astype(BF16) for i in idx}

    vsrc = {(d, c): (0, c) if same_rows else (d, c) for d, c in idx}
    vhalf = {i: half(ops[i[0]][i[1]][3]) for i in dict.fromkeys(vsrc.values())}
    wv = {i: _wbd(vhalf[i], cst) for i in vhalf}
    vb = {i: vhalf[i].astype(BF16) for i in vhalf}
    lk = {i: jnp.concatenate([jnp.concatenate([ar[i], m_akrk[i]], axis=1),
                              jnp.concatenate([dg[i], jnp.zeros_like(dg[i])], axis=1)], axis=0)
          for i in idx}

    s = [states[0][h], states[1][h]]
    for step in range(ncb):
        cur = [(0, step), (1, ncb - 1 - step)]
        ars = {i: _nn(lk[i], jnp.concatenate([s[i[0]].astype(BF16), wv[vsrc[i]]], axis=0)) for i in cur}
        yield
        u = {i: _nn(xb[i], _wbd(ars[i][:L], cst)) for i in cur}
        yield
        for i in cur:
            d, c = i
            out["y"][d, c, h] = ars[i][L:2 * L] + _nn(m_rb[i], _wbd(u[i], cst))
            upd = _tn(khbh[i], jnp.concatenate([vb[vsrc[i]], u[i].astype(BF16)], axis=0))
            s[d] = ars[i][2 * L:] + jnp.where(cst["bmask"], upd, 0.0)
        yield
    out["s"][0][h], out["s"][1][h] = s


def _block_step(ops, states, cst, same_rows):
    ncb = len(ops[0])
    npair = ops[0][0][0].shape[1] // PW

    lg = {}
    for d in range(2):
        for c in range(ncb):
            hi, mid = _split2(ops[d][c][0])
            tri = cst["tri"][d]
            lg[d, c] = _nn(tri, hi) + _nn(tri, mid)

    out = {"y": {}, "s": [[None] * npair, [None] * npair]}
    pending = [_pair_pipeline(h, ops, lg, states, cst, same_rows, out) for h in range(npair)]
    active = pending
    while active:
        active = [gen for gen in active if next(gen, "done") != "done"]
    ys = [[jnp.concatenate([out["y"][d, c, h] for h in range(npair)], axis=1) if npair > 1
           else out["y"][d, c, 0] for c in range(ncb)] for d in range(2)]
    return ys, out["s"]


def _rwkv_kernel(*refs, seq, gw, has_s0, want_state):
    it = iter(refs)
    r_ref, k_ref, v_ref, lo_ref, gb_ref = (next(it) for _ in range(5))
    w0_ref, w2_ref, a0_ref, a2_ref = (next(it) for _ in range(4))
    kk_ref, ka_ref, rk_ref, lng_ref, lnb_ref = (next(it) for _ in range(5))
    s0_ref = next(it) if has_s0 else None
    ob_ref = next(it)
    sf_ref = next(it) if want_state else None
    r_s, v_s, kn_s, lw_s, kd_s, b_s, y_s, st_s = (next(it) for _ in range(8))

    T = seq
    cst = _rwkv_consts()
    npair = gw // PW
    hpp = PW // HEAD_DIM_B
    gi = lax.broadcasted_iota(jnp.int32, (LW, LW), 0)
    gj = lax.broadcasted_iota(jnp.int32, (LW, LW), 1)
    e = jnp.where((gi >> HD_SHIFT) == (gj >> HD_SHIFT), 1.0, 0.0).astype(BF16)

    r = r_ref[...].astype(F32)
    k = k_ref[...].astype(F32)
    v = v_ref[...].astype(F32)
    lo = lo_ref[...].astype(F32)
    r_s[...] = r
    v_s[...] = v
    kk = k * kk_ref[...]
    kk = kk * lax.rsqrt(_group_sum(kk * kk, e, exact=False) + 1e-12)
    kn_s[...] = kk
    bonus_dot = _group_sum(r * k * rk_ref[...], e, exact=False)

    tanh_lo = jnp.tanh(lo).astype(BF16)
    lo_b = lo.astype(BF16)
    kka = k * ka_ref[...]

    def padded(w, before):
        parts = [jnp.zeros((before, gw), F32)] if before else []
        parts.append(w)
        after = LW - before - LORA_R
        if after:
            parts.append(jnp.zeros((after, gw), F32))
        return jnp.concatenate(parts, axis=0).astype(BF16)

    for d in range(2):
        w2p = padded(w2_ref[d], LORA_R * d)
        a2p = padded(a2_ref[d], LORA_R * (2 + d))
        z = w0_ref[d:d + 1, :] + _nn(tanh_lo, w2p)
        half_c = -0.5 * EXP_NEG_HALF * LOG2E
        lw_s[d] = half_c * jnp.tanh(0.5 * z) + half_c
        a = _sigmoid(a0_ref[d:d + 1, :] + _nn(lo_b, a2p))
        kd_s[d] = k + kka * (a - 1.0)
        b_s[d] = kk * a

    if has_s0:
        ident = jnp.where(cst["eye"], 1.0, 0.0).astype(BF16)
        for d in range(2):
            for h in range(npair):
                z0 = s0_ref[0, d, h * hpp:(h + 1) * hpp].reshape(PW, HEAD_DIM_B)
                hi, lo2 = _split2(z0)
                zt = _tn(hi, ident) + _tn(lo2, ident)
                st_s[d, h] = jnp.where(cst["bmask"], jnp.concatenate([zt] * hpp, axis=0), 0.0)
    else:
        st_s[...] = jnp.zeros(st_s.shape, F32)

    nblk = T // BLOCK_T
    ncb = BLOCK_T // CHUNK

    def block(i, carry):
        rows = []
        for d in range(2):
            blk = i if d == 0 else nblk - 1 - i
            rows.append([])
            for c in range(ncb):
                start = blk * BLOCK_T + c * CHUNK
                if not isinstance(start, int):
                    start = pl.multiple_of(start, CHUNK)
                rows[d].append(pl.ds(start, CHUNK))
        ops = [[(lw_s[d, rw, :], r_s[rw, :], kd_s[d, rw, :], v_s[rw, :], kn_s[rw, :], b_s[d, rw, :])
                for rw in rows[d]] for d in range(2)]
        states = [[st_s[d, h] for h in range(npair)] for d in range(2)]
        ys, s_new = _block_step(ops, states, cst, nblk == 1)
        for d in range(2):
            for c in range(ncb):
                y_s[d, rows[d][c], :] = ys[d][c]
            for h in range(npair):
                st_s[d, h] = s_new[d][h]
        return carry

    if nblk == 1:
        block(0, 0)
    else:
        lax.fori_loop(0, nblk, block, 0)

    y = y_s[0] + y_s[1]
    mu = _group_sum(y, e) * (1.0 / HEAD_DIM_B)
    dlt = y - mu
    var = _group_sum(dlt * dlt, e, exact=False) * (1.0 / HEAD_DIM_B)
    yn = dlt * lax.rsqrt(var + LNX_EPS) * lng_ref[...] + lnb_ref[...]
    out = (yn + bonus_dot * v_s[...]) * _silu(gb_ref[...].astype(F32))
    ob_ref[...] = out.astype(BF16)

    if want_state:
        ri = lax.broadcasted_iota(jnp.int32, (PW, HEAD_DIM_B), 0)
        ci = lax.broadcasted_iota(jnp.int32, (PW, HEAD_DIM_B), 1)
        fold = jnp.where((ri & (HEAD_DIM_B - 1)) == ci, 1.0, 0.0).astype(BF16)
        for d in range(2):
            for h in range(npair):
                hi, lo2 = _split2(st_s[d, h])
                sf = _tn(hi, fold) + _tn(lo2, fold)
                sf_ref[0, d, h * hpp:(h + 1) * hpp] = sf.reshape(hpp, HEAD_DIM_B, HEAD_DIM_B)


def _rwkv(u, ul, seq, gw, s0, want_state, decay_w0, decay_w2, iclr_a0, iclr_a2,
          k_k, k_a, r_k, lnx_g, lnx_b):
    m = u.shape[0]
    nbat = m // seq
    ng = D_MODEL // gw
    hg = gw // HEAD_DIM_B
    ublk = lambda sec: pl.BlockSpec((seq, gw), lambda b, g: (b, sec * ng + g))
    lblk = pl.BlockSpec((seq, LW), lambda b, g: (b, 0))
    vec2 = pl.BlockSpec((2, gw), lambda b, g: (0, g))
    mat2 = pl.BlockSpec((2, LORA_R, gw), lambda b, g: (0, 0, g))
    vec1 = pl.BlockSpec((1, gw), lambda b, g: (0, g))
    sblk = pl.BlockSpec((1, 2, hg, HEAD_DIM_B, HEAD_DIM_B), lambda b, g: (b, 0, g, 0, 0))
    in_specs = [ublk(SEC_R), ublk(SEC_KB), ublk(SEC_VB), lblk, ublk(SEC_GB),
                vec2, mat2, vec2, mat2, vec1, vec1, vec1, vec1, vec1]
    args = [u, u, u, ul, u,
            decay_w0, decay_w2, iclr_a0, iclr_a2,
            k_k.reshape(1, D_MODEL), k_a.reshape(1, D_MODEL), r_k.reshape(1, D_MODEL),
            lnx_g.reshape(1, D_MODEL), lnx_b.reshape(1, D_MODEL)]
    if s0 is not None:
        in_specs.append(sblk)
        args.append(s0)
    out_specs = [pl.BlockSpec((seq, gw), lambda b, g: (b, g))]
    out_shape = [jax.ShapeDtypeStruct((m, D_MODEL), BF16)]
    if want_state:
        out_specs.append(sblk)
        out_shape.append(jax.ShapeDtypeStruct((nbat, 2, N_HEADS_B, HEAD_DIM_B, HEAD_DIM_B), F32))
    tw = pltpu.VMEM((seq, gw), F32)
    tw2 = pltpu.VMEM((2, seq, gw), F32)
    res = pl.pallas_call(
        functools.partial(_rwkv_kernel, seq=seq, gw=gw, has_s0=s0 is not None, want_state=want_state),
        grid=(nbat, ng),
        in_specs=in_specs,
        out_specs=out_specs,
        out_shape=out_shape,
        scratch_shapes=[tw, tw, tw, tw2, tw2, tw2, tw2, pltpu.VMEM((2, gw // PW, PW, PW), F32)],
        compiler_params=pltpu.CompilerParams(
            dimension_semantics=("arbitrary", "arbitrary"), vmem_limit_bytes=VMEM_LIMIT_BIG),
        name="rwkv_state" if want_state else "rwkv",
    )(*args)
    return res


def _outproj_kernel(oa_hbm, ob_hbm, x_hbm, mod_ref, w_ref, fg_ref, y_ref, oa_buf, ob_buf, x_buf, sem,
                    *, nstep):
    step = pl.program_id(0)
    tm = y_ref.shape[0]

    def copies(s):
        rows = pl.ds(pl.multiple_of(s * tm, tm), tm)
        slot = s % W_SLOTS
        return [pltpu.make_async_copy(src.at[rows, :], buf.at[slot], sem.at[k, slot])
                for k, (src, buf) in enumerate(((oa_hbm, oa_buf), (ob_hbm, ob_buf), (x_hbm, x_buf)))]

    def start(s):
        for cp in copies(s):
            cp.start()

    @pl.when(step == 0)
    def _():
        start(step)
        if nstep > 1:
            start(step + 1)

    @pl.when(step + 2 < nstep)
    def _():
        start(step + 2)

    for cp in copies(step):
        cp.wait()
    slot = step % W_SLOTS
    acc = _nn(oa_buf[slot], w_ref[0:D_MODEL, :]) + _nn(ob_buf[slot], w_ref[D_MODEL:, :])
    gate = mod_ref[0][:, 2 * D_MODEL:]
    xo = x_buf[slot] + gate * acc
    y_ref[...] = xo * lax.rsqrt(jnp.mean(xo * xo, axis=-1, keepdims=True) + NORM_EPS) * fg_ref[...]


def _outproj(o_a, o_b, x, mod3, w_out_bf, final_g, mod_row):
    m = x.shape[0]
    tm = OUTPROJ_TM
    hbm = pl.BlockSpec(memory_space=pl.ANY)
    return pl.pallas_call(
        functools.partial(_outproj_kernel, nstep=m // tm),
        grid=(m // tm,),
        in_specs=[hbm, hbm, hbm,
                  pl.BlockSpec((1, 1, 3 * D_MODEL), lambda i: (mod_row(i * tm), 0, 0)),
                  pl.BlockSpec((2 * D_MODEL, D_MODEL), lambda i: (0, 0)),
                  pl.BlockSpec((1, D_MODEL), lambda i: (0, 0))],
        out_specs=pl.BlockSpec((tm, D_MODEL), lambda i: (i, 0)),
        out_shape=jax.ShapeDtypeStruct((m, D_MODEL), F32),
        scratch_shapes=[pltpu.VMEM((W_SLOTS, tm, D_MODEL), BF16), pltpu.VMEM((W_SLOTS, tm, D_MODEL), BF16),
                        pltpu.VMEM((W_SLOTS, tm, D_MODEL), F32), pltpu.SemaphoreType.DMA((3, W_SLOTS))],
        compiler_params=pltpu.CompilerParams(
            dimension_semantics=("arbitrary",), vmem_limit_bytes=VMEM_LIMIT_BIG),
        name="outproj",
    )(o_a, o_b, x, mod3, w_out_bf, final_g)


def kernel(x_prompt, x_sample, cache_k, cache_v, state_rwkv, c, c_ctx, norm_g, w_ada, b_ada, w_in, lam_q1, lam_k1, lam_q2, lam_k2, subln_g, shift_mu, decay_w0, decay_w2, iclr_a0, iclr_a2, k_k, k_a, r_k, lnx_g, lnx_b, w_out, final_g):
    batch, seq, d = x_prompt.shape
    dbatch, dseq, _ = x_sample.shape
    past = cache_k.shape[2]
    assert d == D_MODEL and w_in.shape == (1, D_MODEL, D_IN) and dseq % GRID_W == 0

    cvec = jnp.concatenate([c_ctx[None, :], c, jnp.zeros((8 - 1 - dbatch, d), F32)], axis=0)
    mod3 = _mod(cvec, w_ada[0], b_ada[0]).reshape(8, 1, 3 * d)
    lo0 = 4 * D_MODEL + 3 * D_MODEL
    w_bf = w_in[0].astype(BF16)
    w_lora = w_bf[:, lo0:lo0 + LW]
    w_out_bf = w_out[0].astype(BF16)
    ctx_row = lambda tok: 0
    lat_row = lambda tok: 1 + tok // dseq
    lp = (decay_w0[0], decay_w2[0], iclr_a0[0], iclr_a2[0], k_k[0], k_a[0], r_k[0],
          lnx_g[0], lnx_b[0])
    mu = shift_mu[0]
    lam = (lam_q1, lam_k1, lam_q2, lam_k2, subln_g)
    fg = final_g.reshape(1, d)

    xp = x_prompt.reshape(batch * seq, d)
    u_c, ul_c, nk, nv = _inproj(xp, mod3, norm_g, w_bf, w_lora, mu, ctx_row, seq, True)
    oa_c = _attn_ctx(u_c, seq, *lam)
    ob_c, s_new = _rwkv(u_c, ul_c, seq, RWKV_GW_CTX, None, True, *lp)
    y_prompt = _outproj(oa_c, ob_c, xp, mod3, w_out_bf, fg, ctx_row).reshape(batch, seq, d)

    xs = x_sample.reshape(dbatch * dseq, d)
    u_l, ul_l = _inproj(xs, mod3, norm_g, w_bf, w_lora, mu, lat_row, dseq, False)
    cv = cache_v[:, 0].reshape(dbatch * past, d)
    ckt = jnp.transpose(cache_k[:, 0], (0, 2, 3, 4, 1))
    oa_l = _attn_lat(u_l, ckt, cv, dseq, past, *lam)
    (ob_l,) = _rwkv(u_l, ul_l, dseq, RWKV_GW_LAT, state_rwkv[:, 0], False, *lp)
    y_sample = _outproj(oa_l, ob_l, xs, mod3, w_out_bf, fg, lat_row).reshape(dbatch, dseq, d)

    new_s = s_new.reshape(batch, 1, 2, N_HEADS_B, HEAD_DIM_B, HEAD_DIM_B)
    return (y_prompt, y_sample, nk, nv, new_s)
```

```python
import functools
import math

import jax
import jax.numpy as jnp
from jax import lax
from jax.experimental import pallas as pl
from jax.experimental.pallas import tpu as pltpu

F32 = jnp.float32
BF16 = jnp.bfloat16
HIGHEST = lax.Precision.HIGHEST

D_MODEL = 1024
GRID_W = 64
HEAD_DIM_A = 64
N_HEADS_A = 8
HEAD_DIM_B = 64
N_HEADS_B = 16
LORA_R = 64
D_SHIFT = 3 * D_MODEL + 4 * LORA_R
D_IN = 4 * D_MODEL + D_SHIFT + D_MODEL
ROPE_BASE = 10000.0
NORM_EPS = 1e-6
SUBLN_EPS = 1e-5
LNX_EPS = 64e-5
LAM_INIT = 0.8 - 0.6 * math.exp(-0.3 * 0)
EXP_NEG_HALF = math.exp(-0.5)
LOG2E = math.log2(math.e)

HL = 2 * HEAD_DIM_A
HD_SHIFT = 6
INPROJ_TM = 1024
W_SLOTS = 3
OUTPROJ_TM = 1024
ATTN_TQ = 512
MOD_TN = 512
GW = 256
LW = 4 * LORA_R
PW = 128
RWKV_GW_CTX = 1024
RWKV_GW_LAT = 512
CHUNK = 64
BLOCK_T = 256
VMEM_LIMIT = 48 * 1024 * 1024
VMEM_LIMIT_BIG = 56 * 1024 * 1024
SEC_Q, SEC_K, SEC_V, SEC_GA, SEC_R, SEC_KB, SEC_VB, SEC_GB = range(8)


def _nn(a, b):
    return jnp.dot(a, b, preferred_element_type=F32)


def _nt(a, b):
    return lax.dot_general(a, b, (((1,), (1,)), ((), ())), preferred_element_type=F32)


def _tn(a, b):
    return lax.dot_general(a, b, (((0,), (0,)), ((), ())), preferred_element_type=F32)


def _split2(x):
    hi = x.astype(BF16)
    lo = (x - hi.astype(F32)).astype(BF16)
    return hi, lo


def _sigmoid(x):
    return 0.5 * jnp.tanh(0.5 * x) + 0.5


def _silu(x):
    return x * _sigmoid(x)


def _mod_kernel(c_ref, w_ref, b_ref, o_ref):
    s = _silu(c_ref[...])
    o_ref[...] = jnp.dot(s, w_ref[...], precision=HIGHEST, preferred_element_type=F32) + b_ref[...]


def _mod(cvec, w_ada, b_ada):
    n = w_ada.shape[1]
    tn = MOD_TN
    return pl.pallas_call(
        _mod_kernel,
        grid=(n // tn,),
        in_specs=[
            pl.BlockSpec((8, D_MODEL), lambda j: (0, 0)),
            pl.BlockSpec((D_MODEL, tn), lambda j: (0, j)),
            pl.BlockSpec((1, tn), lambda j: (0, j)),
        ],
        out_specs=pl.BlockSpec((8, tn), lambda j: (0, j)),
        out_shape=jax.ShapeDtypeStruct((8, n), F32),
        name="mod",
    )(cvec, w_ada, b_ada.reshape(1, n))


def _token_shift(x, mu, seq):
    row8 = lax.broadcasted_iota(jnp.int32, (8, x.shape[1]), 0)
    keep = 1.0 - mu[0:1, :] - mu[1:2, :]
    outs = []
    for b in range(x.shape[0] // seq):
        xs = x[b * seq:(b + 1) * seq]
        prev = pltpu.roll(xs, 1, 0)
        nxt = pltpu.roll(xs, seq - 1, 0)
        prev = jnp.concatenate([jnp.where(row8 == 0, 0.0, prev[0:8]), prev[8:]], axis=0)
        nxt = jnp.concatenate([nxt[:seq - 8], jnp.where(row8 == 7, 0.0, nxt[seq - 8:])], axis=0)
        outs.append(xs * keep + mu[0:1, :] * prev + mu[1:2, :] * nxt)
    return outs[0] if len(outs) == 1 else jnp.concatenate(outs, axis=0)


def _inproj_kernel(*refs, seq, want_cache):
    x_hbm, mod_ref, g_ref, w_hbm, wl_ref, mu_ref, mul_ref, um_ref, ul_ref = refs[:9]
    h_ref, wbuf_ref, wsem, xbuf_ref, xsem = refs[-5:]
    i = pl.program_id(0)
    j = pl.program_id(1)
    nsec = pl.num_programs(1)
    step = i * nsec + j
    nstep = pl.num_programs(0) * nsec
    tm = h_ref.shape[0]

    def x_copy(t):
        rows = pl.ds(pl.multiple_of(t * tm, tm), tm)
        return pltpu.make_async_copy(x_hbm.at[rows, :], xbuf_ref, xsem.at[0])

    @pl.when(step == 0)
    def _():
        x_copy(i).start()

    @pl.when((j == 1) & (i + 1 < pl.num_programs(0)))
    def _():
        x_copy(i + 1).start()

    def w_copy(s):
        sec = s % nsec
        col = pl.multiple_of(jnp.where(sec == SEC_GB, D_IN - D_MODEL, sec * D_MODEL), HL)
        slot = s % W_SLOTS
        return pltpu.make_async_copy(w_hbm.at[:, pl.ds(col, D_MODEL)], wbuf_ref.at[slot], wsem.at[slot])

    @pl.when(step == 0)
    def _():
        w_copy(step).start()
        w_copy(step + 1).start()

    @pl.when(step + 2 < nstep)
    def _():
        w_copy(step + 2).start()

    @pl.when(j == 0)
    def _():
        x_copy(i).wait()
        x = xbuf_ref[...]
        y = x * lax.rsqrt(jnp.mean(x * x, axis=-1, keepdims=True) + NORM_EPS) * g_ref[...]
        mod = mod_ref[0]
        shift = mod[:, 0:D_MODEL]
        scale = mod[:, D_MODEL:2 * D_MODEL]
        h = (y * (1.0 + scale) + shift).astype(BF16)
        h_ref[...] = h
        ul_ref[...] = _token_shift(_nn(h, wl_ref[...]), mul_ref[...], seq).astype(BF16)

    w_copy(step).wait()
    w_ref = wbuf_ref.at[step % W_SLOTS]
    rwkv_sec = (j >= SEC_R) & (j <= SEC_VB)

    @pl.when(rwkv_sec)
    def _():
        acc = _nn(h_ref[...], w_ref[...])
        um_ref[...] = _token_shift(acc, mu_ref[...], seq).astype(BF16)

    @pl.when(jnp.logical_not(rwkv_sec))
    def _():
        acc = _nn(h_ref[...], w_ref[...])
        um_ref[...] = acc.astype(BF16)
        if want_cache:
            nk_ref, nv_ref = refs[9:11]
            nb = acc.shape[0] // seq

            @pl.when(j == SEC_K)
            def _():
                nk_ref[:, 0] = acc.reshape((nb,) + nk_ref.shape[2:])

            @pl.when(j == SEC_V)
            def _():
                nv_ref[:, 0] = acc.reshape((nb,) + nv_ref.shape[2:])


def _inproj(x, mod3, norm_g, w_bf, w_lora, shift_mu, mod_row, seq, want_cache):
    m = x.shape[0]
    tm = INPROJ_TM
    nsec = SEC_GB + 1
    nb = tm // seq
    out_specs = [pl.BlockSpec((tm, D_MODEL), lambda i, j: (i, j)),
                 pl.BlockSpec((tm, LW), lambda i, j: (i, 0))]
    out_shape = [jax.ShapeDtypeStruct((m, nsec * D_MODEL), BF16), jax.ShapeDtypeStruct((m, LW), BF16)]
    if want_cache:
        held = lambda i, j, sec: jnp.maximum(i - jnp.where(j < sec, 1, 0), 0)
        out_specs += [pl.BlockSpec((nb, 1, seq, N_HEADS_A, 2, HEAD_DIM_A),
                                   lambda i, j: (held(i, j, SEC_K), 0, 0, 0, 0, 0)),
                      pl.BlockSpec((nb, 1, seq, N_HEADS_A, 2 * HEAD_DIM_A),
                                   lambda i, j: (held(i, j, SEC_V), 0, 0, 0, 0))]
        out_shape += [jax.ShapeDtypeStruct((m // seq, 1, seq, N_HEADS_A, 2, HEAD_DIM_A), F32),
                      jax.ShapeDtypeStruct((m // seq, 1, seq, N_HEADS_A, 2 * HEAD_DIM_A), F32)]
    return pl.pallas_call(
        functools.partial(_inproj_kernel, seq=seq, want_cache=want_cache),
        grid=(m // tm, nsec),
        in_specs=[
            pl.BlockSpec(memory_space=pl.ANY),
            pl.BlockSpec((1, 1, 3 * D_MODEL), lambda i, j: (mod_row(i * tm), 0, 0)),
            pl.BlockSpec((1, D_MODEL), lambda i, j: (0, 0)),
            pl.BlockSpec(memory_space=pl.ANY),
            pl.BlockSpec((D_MODEL, LW), lambda i, j: (0, 0)),
            pl.BlockSpec((2, D_MODEL), lambda i, j: (0, jnp.clip(j - SEC_R, 0, SEC_VB - SEC_R))),
            pl.BlockSpec((2, LW), lambda i, j: (0, 3 * D_MODEL // LW)),
        ],
        out_specs=out_specs,
        out_shape=out_shape,
        scratch_shapes=[pltpu.VMEM((tm, D_MODEL), BF16),
                        pltpu.VMEM((W_SLOTS, D_MODEL, D_MODEL), BF16),
                        pltpu.SemaphoreType.DMA((W_SLOTS,)),
                        pltpu.VMEM((tm, D_MODEL), F32),
                        pltpu.SemaphoreType.DMA((1,))],
        compiler_params=pltpu.CompilerParams(
            dimension_semantics=("arbitrary", "arbitrary"), vmem_limit_bytes=VMEM_LIMIT_BIG),
        name="inproj_cache" if want_cache else "inproj",
    )(x, mod3, norm_g, w_bf, w_lora, shift_mu, shift_mu)


def _lam(q1_ref, k1_ref, q2_ref, k2_ref):
    s1 = jnp.sum(q1_ref[...] * k1_ref[...], axis=-1, keepdims=True)
    s2 = jnp.sum(q2_ref[...] * k2_ref[...], axis=-1, keepdims=True)
    return jnp.exp(s1) - jnp.exp(s2) + LAM_INIT


def _diff_head(q, kb, vb, g, lam, subg, kct=None):
    lane = lax.broadcasted_iota(jnp.int32, q.shape, 1)
    qs = q * (HEAD_DIM_A ** -0.5 * LOG2E)
    v1 = jnp.concatenate([vb, jnp.ones_like(vb)], axis=1)
    outs = []
    for m in range(2):
        qm = jnp.where((lane >> HD_SHIFT) == m, qs, 0.0).astype(BF16)
        s = _nt(qm, kb)
        if kct is not None:
            s = jnp.concatenate([_nn(qm, kct), s], axis=1)
        p = jnp.exp2(s - jnp.max(s, axis=-1, keepdims=True)).astype(BF16)
        pv = _nn(p, v1)
        outs.append(pv[:, :HL] / pv[:, HL:])
    o = outs[0] - lam * outs[1]
    o = o * lax.rsqrt(jnp.mean(o * o, axis=-1, keepdims=True) + SUBLN_EPS) * subg
    o = o * (1.0 - LAM_INIT)
    return o * _silu(g)


def _attn_ctx_kernel(q_ref, k_ref, v_ref, g_ref, q1_ref, k1_ref, q2_ref, k2_ref, sg_ref, o_ref):
    lam = _lam(q1_ref, k1_ref, q2_ref, k2_ref)
    subg = sg_ref[...]
    for h in range(q_ref.shape[1] // HL):
        sl = slice(h * HL, (h + 1) * HL)
        o = _diff_head(q_ref[:, sl].astype(F32), k_ref[:, sl], v_ref[:, sl],
                       g_ref[:, sl].astype(F32), lam, subg)
        o_ref[:, sl] = o.astype(BF16)


def _attn_ctx(um, seq, lam_q1, lam_k1, lam_q2, lam_k2, subln_g):
    m = um.shape[0]
    small = pl.BlockSpec((1, HEAD_DIM_A), lambda b: (0, 0))
    blk = lambda sec: pl.BlockSpec((seq, D_MODEL), lambda b: (b, sec))
    return pl.pallas_call(
        _attn_ctx_kernel,
        grid=(m // seq,),
        in_specs=[blk(SEC_Q), blk(SEC_K), blk(SEC_V), blk(SEC_GA), small, small, small, small,
                  pl.BlockSpec((1, 2 * HEAD_DIM_A), lambda b: (0, 0))],
        out_specs=blk(0),
        out_shape=jax.ShapeDtypeStruct((m, D_MODEL), BF16),
        compiler_params=pltpu.CompilerParams(vmem_limit_bytes=VMEM_LIMIT),
        name="attn_ctx",
    )(um, um, um, um, lam_q1, lam_k1, lam_q2, lam_k2, subln_g)


def _rope(x, cos, s1, s2):
    return x * cos + pltpu.roll(x, 112, 1) * s1 + pltpu.roll(x, 16, 1) * s2


def _attn_lat_kernel(q_ref, k_ref, v_ref, g_ref, ck_ref, cv_ref, cq_ref, s1q_ref, s2q_ref,
                     ckk_ref, s1k_ref, s2k_ref, q1_ref, k1_ref, q2_ref, k2_ref, sg_ref,
                     o_ref, kct_ref, krot_ref, vall_ref):
    past = cv_ref.shape[0]
    w = q_ref.shape[1]
    nh = w // HL

    @pl.when(pl.program_id(2) == 0)
    def _():
        vall_ref[0:past, :] = cv_ref[...].astype(BF16)
        vall_ref[past:, :] = v_ref[...]
        for h in range(nh):
            sl = slice(h * HL, (h + 1) * HL)
            kct_ref[h] = ck_ref[0, h].reshape(HL, past).astype(BF16)
            krot_ref[:, sl] = _rope(k_ref[:, sl].astype(F32), ckk_ref[...], s1k_ref[...],
                                    s2k_ref[...]).astype(BF16)

    lam = _lam(q1_ref, k1_ref, q2_ref, k2_ref)
    subg = sg_ref[...]
    for h in range(nh):
        sl = slice(h * HL, (h + 1) * HL)
        q = _rope(q_ref[:, sl].astype(F32), cq_ref[...], s1q_ref[...], s2q_ref[...])
        o = _diff_head(q, krot_ref[:, sl], vall_ref[:, sl], g_ref[:, sl].astype(F32), lam, subg,
                       kct=kct_ref[h])
        o_ref[:, sl] = o.astype(BF16)


def _rope_tables(t):
    pos = jnp.arange(t)
    row = (pos // GRID_W).astype(F32)
    col = (pos % GRID_W).astype(F32)
    n_freq = HEAD_DIM_A // 4
    lane = jnp.arange(HL)
    d = lane % HEAD_DIM_A
    use_col = (d // 32) == 1
    second = ((d % 32) // 16) == 1
    inv = ROPE_BASE ** (-(d % n_freq).astype(F32) / n_freq)
    ang = jnp.where(use_col[None, :], col[:, None], row[:, None]) * inv[None, :]
    cos = jnp.cos(ang)
    sin = jnp.sin(ang)
    s1 = jnp.where(second[None, :], 0.0, -sin)
    s2 = jnp.where(second[None, :], sin, 0.0)
    return cos, s1, s2


def _attn_lat(u, cache_kt, cache_v2, seq, past, lam_q1, lam_k1, lam_q2, lam_k2, subln_g):
    m = u.shape[0]
    nbat = m // seq
    w = GW
    ng = D_MODEL // w
    tq = ATTN_TQ
    nq = seq // tq
    cos, s1, s2 = _rope_tables(seq)
    small = pl.BlockSpec((1, HEAD_DIM_A), lambda b, g, i: (0, 0))
    qblk = lambda sec: pl.BlockSpec((tq, w), lambda b, g, i: (b * nq + i, sec * ng + g))
    kblk = lambda sec: pl.BlockSpec((seq, w), lambda b, g, i: (b, sec * ng + g))
    ckblk = pl.BlockSpec((1, w // HL, 2, HEAD_DIM_A, past), lambda b, g, i: (b, g, 0, 0, 0))
    cblk = pl.BlockSpec((past, w), lambda b, g, i: (b, g))
    tq_blk = pl.BlockSpec((tq, HL), lambda b, g, i: (i, 0))
    tk_blk = pl.BlockSpec((seq, HL), lambda b, g, i: (0, 0))
    return pl.pallas_call(
        _attn_lat_kernel,
        grid=(nbat, ng, nq),
        in_specs=[qblk(SEC_Q), kblk(SEC_K), kblk(SEC_V), qblk(SEC_GA), ckblk, cblk,
                  tq_blk, tq_blk, tq_blk, tk_blk, tk_blk, tk_blk,
                  small, small, small, small,
                  pl.BlockSpec((1, 2 * HEAD_DIM_A), lambda b, g, i: (0, 0))],
        out_specs=pl.BlockSpec((tq, w), lambda b, g, i: (b * nq + i, g)),
        out_shape=jax.ShapeDtypeStruct((m, D_MODEL), BF16),
        scratch_shapes=[pltpu.VMEM((w // HL, HL, past), BF16), pltpu.VMEM((seq, w), BF16),
                        pltpu.VMEM((past + seq, w), BF16)],
        compiler_params=pltpu.CompilerParams(
            dimension_semantics=("arbitrary", "arbitrary", "arbitrary"), vmem_limit_bytes=VMEM_LIMIT),
        name="attn_lat",
    )(u, u, u, u, cache_kt, cache_v2, cos, s1, s2, cos, s1, s2,
      lam_q1, lam_k1, lam_q2, lam_k2, subln_g)


def _rwkv_consts():
    ri = lax.broadcasted_iota(jnp.int32, (PW, PW), 0)
    ci = lax.broadcasted_iota(jnp.int32, (PW, PW), 1)
    t = lax.broadcasted_iota(jnp.int32, (CHUNK, PW), 0)
    lane = lax.broadcasted_iota(jnp.int32, (CHUNK, PW), 1)
    j = lane & (CHUNK - 1)
    ti = lax.broadcasted_iota(jnp.int32, (CHUNK, CHUNK), 0)
    tj = lax.broadcasted_iota(jnp.int32, (CHUNK, CHUNK), 1)
    return dict(
        bmask=(ri >> HD_SHIFT) == (ci >> HD_SHIFT),
        eye=ri == ci,
        head0=lane < HEAD_DIM_B,
        eye_sbs=jnp.where(t == j, 1.0, 0.0),
        strict=(j < t, j > t),
        incl=(j <= t, j >= t),
        off=[((t >> (n + 1)) == (j >> (n + 1))) & ((t >> n) != (j >> n)) for n in range(6)],
        tri=(jnp.where(tj <= ti, 1.0, 0.0).astype(BF16), jnp.where(tj >= ti, 1.0, 0.0).astype(BF16)),
    )


def _wbd(x, cst):
    xb = x.astype(BF16)
    z = jnp.zeros_like(xb)
    return jnp.concatenate([jnp.where(cst["head0"], xb, z), jnp.where(cst["head0"], z, xb)], axis=0)


def _group_sum(x, e, exact=True):
    ew = e.shape[0]
    cols = range(0, x.shape[1], ew)
    if exact:
        hi, lo = _split2(x)
        parts = [_nn(hi[:, o:o + ew], e) + _nn(lo[:, o:o + ew], e) for o in cols]
    else:
        xb = x.astype(BF16)
        parts = [_nn(xb[:, o:o + ew], e) for o in cols]
    return parts[0] if len(parts) == 1 else jnp.concatenate(parts, axis=1)


def _pair_pipeline(h, ops, lg, states, cst, same_rows, out):
    L = CHUNK
    ncb = len(ops[0])
    idx = [(d, c) for d in range(2) for c in range(ncb)]
    half = lambda a: a[:, h * PW:(h + 1) * PW]

    ar, wbk, khbh, dg = {}, {}, {}, {}
    for d, c in idx:
        lw, r, kd, v, kk, b = (half(a) for a in ops[d][c])
        g = half(lg[d, c])
        g_l = g[0:1, :] if d == 1 else g[L - 1:L, :]
        at = -(kk * jnp.exp2(g - lw))
        rt = r * jnp.exp2(g)
        e_inv = jnp.exp2(-g)
        eg = jnp.exp2(g_l)
        bt, kt = b * e_inv, kd * e_inv
        ar[d, c] = jnp.concatenate([at, rt], axis=0).astype(BF16)
        wbk[d, c] = jnp.concatenate([_wbd(bt, cst), _wbd(kt, cst)], axis=0)
        khbh[d, c] = jnp.concatenate([(kt * eg).astype(BF16), (bt * eg).astype(BF16)], axis=0)
        dg[d, c] = jnp.where(cst["eye"], eg, 0.0).astype(BF16)
    yield

    gbk = {i: _nt(ar[i], wbk[i]) for i in idx}
    yield
    m_ab = {i: jnp.where(cst["strict"][i[0]], gbk[i][:L, :PW], 0.0) for i in idx}
    gb16 = {i: gbk[i].astype(BF16) for i in idx}
    zb = jnp.zeros((L, PW), BF16)
    m_abb = {i: jnp.where(cst["strict"][i[0]], gb16[i][:L, :PW], zb) for i in idx}
    m_rb = {i: jnp.where(cst["incl"][i[0]], gb16[i][L:, :PW], zb) for i in idx}
    m_akrk = {i: jnp.concatenate([jnp.where(cst["strict"][i[0]], gb16[i][:L, PW:], zb),
                                  jnp.where(cst["incl"][i[0]], gb16[i][L:, PW:], zb)], axis=0)
              for i in idx}

    x = {i: cst["eye_sbs"] + jnp.where(cst["off"][0], m_ab[i], 0.0) for i in idx}
    for off in cst["off"][1:]:
        p = {i: _nn(jnp.where(off, m_abb[i], zb), _wbd(x[i], cst)) for i in idx}
        yield
        x = {i: x[i] + _nn(x[i].astype(BF16), _wbd(p[i], cst)) for i in idx}
        yield
    xb = {i: x[i].astype(BF16) for i in idx}

    vsrc = {(d, c): (0, c) if same_rows else (d, c) for d, c in idx}
    vhalf = {i: half(ops[i[0]][i[1]][3]) for i in dict.fromkeys(vsrc.values())}
    wv = {i: _wbd(vhalf[i], cst) for i in vhalf}
    vb = {i: vhalf[i].astype(BF16) for i in vhalf}
    lk = {i: jnp.concatenate([jnp.concatenate([ar[i], m_akrk[i]], axis=1),
                              jnp.concatenate([dg[i], jnp.zeros_like(dg[i])], axis=1)], axis=0)
          for i in idx}

    s = [states[0][h], states[1][h]]
    for step in range(ncb):
        cur = [(0, step), (1, ncb - 1 - step)]
        ars = {i: _nn(lk[i], jnp.concatenate([s[i[0]].astype(BF16), wv[vsrc[i]]], axis=0)) for i in cur}
        yield
        u = {i: _nn(xb[i], _wbd(ars[i][:L], cst)) for i in cur}
        yield
        for i in cur:
            d, c = i
            out["y"][d, c, h] = ars[i][L:2 * L] + _nn(m_rb[i], _wbd(u[i], cst))
            upd = _tn(khbh[i], jnp.concatenate([vb[vsrc[i]], u[i].astype(BF16)], axis=0))
            s[d] = ars[i][2 * L:] + jnp.where(cst["bmask"], upd, 0.0)
        yield
    out["s"][0][h], out["s"][1][h] = s


def _block_step(ops, states, cst, same_rows):
    ncb = len(ops[0])
    npair = ops[0][0][0].shape[1] // PW

    lg = {}
    for d in range(2):
        for c in range(ncb):
            hi, mid = _split2(ops[d][c][0])
            tri = cst["tri"][d]
            lg[d, c] = _nn(tri, hi) + _nn(tri, mid)

    out = {"y": {}, "s": [[None] * npair, [None] * npair]}
    pending = [_pair_pipeline(h, ops, lg, states, cst, same_rows, out) for h in range(npair)]
    active = pending
    while active:
        active = [gen for gen in active if next(gen, "done") != "done"]
    ys = [[jnp.concatenate([out["y"][d, c, h] for h in range(npair)], axis=1) if npair > 1
           else out["y"][d, c, 0] for c in range(ncb)] for d in range(2)]
    return ys, out["s"]


def _rwkv_kernel(*refs, seq, gw, has_s0, want_state):
    it = iter(refs)
    r_ref, k_ref, v_ref, lo_ref, gb_ref = (next(it) for _ in range(5))
    w0_ref, w2_ref, a0_ref, a2_ref = (next(it) for _ in range(4))
    kk_ref, ka_ref, rk_ref, lng_ref, lnb_ref = (next(it) for _ in range(5))
    s0_ref = next(it) if has_s0 else None
    ob_ref = next(it)
    sf_ref = next(it) if want_state else None
    r_s, v_s, kn_s, lw_s, kd_s, b_s, y_s, st_s = (next(it) for _ in range(8))

    T = seq
    cst = _rwkv_consts()
    npair = gw // PW
    hpp = PW // HEAD_DIM_B
    gi = lax.broadcasted_iota(jnp.int32, (LW, LW), 0)
    gj = lax.broadcasted_iota(jnp.int32, (LW, LW), 1)
    e = jnp.where((gi >> HD_SHIFT) == (gj >> HD_SHIFT), 1.0, 0.0).astype(BF16)

    r = r_ref[...].astype(F32)
    k = k_ref[...].astype(F32)
    v = v_ref[...].astype(F32)
    lo = lo_ref[...].astype(F32)
    r_s[...] = r
    v_s[...] = v
    kk = k * kk_ref[...]
    kk = kk * lax.rsqrt(_group_sum(kk * kk, e, exact=False) + 1e-12)
    kn_s[...] = kk
    bonus_dot = _group_sum(r * k * rk_ref[...], e, exact=False)

    tanh_lo = jnp.tanh(lo).astype(BF16)
    lo_b = lo.astype(BF16)
    kka = k * ka_ref[...]

    def padded(w, before):
        parts = [jnp.zeros((before, gw), F32)] if before else []
        parts.append(w)
        after = LW - before - LORA_R
        if after:
            parts.append(jnp.zeros((after, gw), F32))
        return jnp.concatenate(parts, axis=0).astype(BF16)

    for d in range(2):
        w2p = padded(w2_ref[d], LORA_R * d)
        a2p = padded(a2_ref[d], LORA_R * (2 + d))
        z = w0_ref[d:d + 1, :] + _nn(tanh_lo, w2p)
        half_c = -0.5 * EXP_NEG_HALF * LOG2E
        lw_s[d] = half_c * jnp.tanh(0.5 * z) + half_c
        a = _sigmoid(a0_ref[d:d + 1, :] + _nn(lo_b, a2p))
        kd_s[d] = k + kka * (a - 1.0)
        b_s[d] = kk * a

    if has_s0:
        ident = jnp.where(cst["eye"], 1.0, 0.0).astype(BF16)
        for d in range(2):
            for h in range(npair):
                z0 = s0_ref[0, d, h * hpp:(h + 1) * hpp].reshape(PW, HEAD_DIM_B)
                hi, lo2 = _split2(z0)
                zt = _tn(hi, ident) + _tn(lo2, ident)
                st_s[d, h] = jnp.where(cst["bmask"], jnp.concatenate([zt] * hpp, axis=0), 0.0)
    else:
        st_s[...] = jnp.zeros(st_s.shape, F32)

    nblk = T // BLOCK_T
    ncb = BLOCK_T // CHUNK

    def block(i, carry):
        rows = []
        for d in range(2):
            blk = i if d == 0 else nblk - 1 - i
            rows.append([])
            for c in range(ncb):
                start = blk * BLOCK_T + c * CHUNK
                if not isinstance(start, int):
                    start = pl.multiple_of(start, CHUNK)
                rows[d].append(pl.ds(start, CHUNK))
        ops = [[(lw_s[d, rw, :], r_s[rw, :], kd_s[d, rw, :], v_s[rw, :], kn_s[rw, :], b_s[d, rw, :])
                for rw in rows[d]] for d in range(2)]
        states = [[st_s[d, h] for h in range(npair)] for d in range(2)]
        ys, s_new = _block_step(ops, states, cst, nblk == 1)
        for d in range(2):
            for c in range(ncb):
                y_s[d, rows[d][c], :] = ys[d][c]
            for h in range(npair):
                st_s[d, h] = s_new[d][h]
        return carry

    if nblk == 1:
        block(0, 0)
    else:
        lax.fori_loop(0, nblk, block, 0)

    y = y_s[0] + y_s[1]
    mu = _group_sum(y, e) * (1.0 / HEAD_DIM_B)
    dlt = y - mu
    var = _group_sum(dlt * dlt, e, exact=False) * (1.0 / HEAD_DIM_B)
    yn = dlt * lax.rsqrt(var + LNX_EPS) * lng_ref[...] + lnb_ref[...]
    out = (yn + bonus_dot * v_s[...]) * _silu(gb_ref[...].astype(F32))
    ob_ref[...] = out.astype(BF16)

    if want_state:
        ri = lax.broadcasted_iota(jnp.int32, (PW, HEAD_DIM_B), 0)
        ci = lax.broadcasted_iota(jnp.int32, (PW, HEAD_DIM_B), 1)
        fold = jnp.where((ri & (HEAD_DIM_B - 1)) == ci, 1.0, 0.0).astype(BF16)
        for d in range(2):
            for h in range(npair):
                hi, lo2 = _split2(st_s[d, h])
                sf = _tn(hi, fold) + _tn(lo2, fold)
                sf_ref[0, d, h * hpp:(h + 1) * hpp] = sf.reshape(hpp, HEAD_DIM_B, HEAD_DIM_B)


def _rwkv(u, ul, seq, gw, s0, want_state, decay_w0, decay_w2, iclr_a0, iclr_a2,
          k_k, k_a, r_k, lnx_g, lnx_b):
    m = u.shape[0]
    nbat = m // seq
    ng = D_MODEL // gw
    hg = gw // HEAD_DIM_B
    ublk = lambda sec: pl.BlockSpec((seq, gw), lambda b, g: (b, sec * ng + g))
    lblk = pl.BlockSpec((seq, LW), lambda b, g: (b, 0))
    vec2 = pl.BlockSpec((2, gw), lambda b, g: (0, g))
    mat2 = pl.BlockSpec((2, LORA_R, gw), lambda b, g: (0, 0, g))
    vec1 = pl.BlockSpec((1, gw), lambda b, g: (0, g))
    sblk = pl.BlockSpec((1, 2, hg, HEAD_DIM_B, HEAD_DIM_B), lambda b, g: (b, 0, g, 0, 0))
    in_specs = [ublk(SEC_R), ublk(SEC_KB), ublk(SEC_VB), lblk, ublk(SEC_GB),
                vec2, mat2, vec2, mat2, vec1, vec1, vec1, vec1, vec1]
    args = [u, u, u, ul, u,
            decay_w0, decay_w2, iclr_a0, iclr_a2,
            k_k.reshape(1, D_MODEL), k_a.reshape(1, D_MODEL), r_k.reshape(1, D_MODEL),
            lnx_g.reshape(1, D_MODEL), lnx_b.reshape(1, D_MODEL)]
    if s0 is not None:
        in_specs.append(sblk)
        args.append(s0)
    out_specs = [pl.BlockSpec((seq, gw), lambda b, g: (b, g))]
    out_shape = [jax.ShapeDtypeStruct((m, D_MODEL), BF16)]
    if want_state:
        out_specs.append(sblk)
        out_shape.append(jax.ShapeDtypeStruct((nbat, 2, N_HEADS_B, HEAD_DIM_B, HEAD_DIM_B), F32))
    tw = pltpu.VMEM((seq, gw), F32)
    tw2 = pltpu.VMEM((2, seq, gw), F32)
    res = pl.pallas_call(
        functools.partial(_rwkv_kernel, seq=seq, gw=gw, has_s0=s0 is not None, want_state=want_state),
        grid=(nbat, ng),
        in_specs=in_specs,
        out_specs=out_specs,
        out_shape=out_shape,
        scratch_shapes=[tw, tw, tw, tw2, tw2, tw2, tw2, pltpu.VMEM((2, gw // PW, PW, PW), F32)],
        compiler_params=pltpu.CompilerParams(
            dimension_semantics=("arbitrary", "arbitrary"), vmem_limit_bytes=VMEM_LIMIT_BIG),
        name="rwkv_state" if want_state else "rwkv",
    )(*args)
    return res


def _outproj_kernel(oa_ref, ob_ref, x_ref, mod_ref, w_ref, fg_ref, y_ref):
    acc = _nn(oa_ref[...], w_ref[0:D_MODEL, :]) + _nn(ob_ref[...], w_ref[D_MODEL:, :])
    gate = mod_ref[0][:, 2 * D_MODEL:]
    xo = x_ref[...] + gate * acc
    y_ref[...] = xo * lax.rsqrt(jnp.mean(xo * xo, axis=-1, keepdims=True) + NORM_EPS) * fg_ref[...]


def _outproj(o_a, o_b, x, mod3, w_out_bf, final_g, mod_row):
    m = x.shape[0]
    tm = OUTPROJ_TM
    blk = pl.BlockSpec((tm, D_MODEL), lambda i: (i, 0))
    return pl.pallas_call(
        _outproj_kernel,
        grid=(m // tm,),
        in_specs=[blk, blk, blk,
                  pl.BlockSpec((1, 1, 3 * D_MODEL), lambda i: (mod_row(i * tm), 0, 0)),
                  pl.BlockSpec((2 * D_MODEL, D_MODEL), lambda i: (0, 0)),
                  pl.BlockSpec((1, D_MODEL), lambda i: (0, 0))],
        out_specs=blk,
        out_shape=jax.ShapeDtypeStruct((m, D_MODEL), F32),
        compiler_params=pltpu.CompilerParams(vmem_limit_bytes=VMEM_LIMIT),
        name="outproj",
    )(o_a, o_b, x, mod3, w_out_bf, final_g)


def kernel(x_prompt, x_sample, cache_k, cache_v, state_rwkv, c, c_ctx, norm_g, w_ada, b_ada, w_in, lam_q1, lam_k1, lam_q2, lam_k2, subln_g, shift_mu, decay_w0, decay_w2, iclr_a0, iclr_a2, k_k, k_a, r_k, lnx_g, lnx_b, w_out, final_g):
    batch, seq, d = x_prompt.shape
    dbatch, dseq, _ = x_sample.shape
    past = cache_k.shape[2]
    assert d == D_MODEL and w_in.shape == (1, D_MODEL, D_IN) and dseq % GRID_W == 0

    cvec = jnp.concatenate([c_ctx[None, :], c, jnp.zeros((8 - 1 - dbatch, d), F32)], axis=0)
    mod3 = _mod(cvec, w_ada[0], b_ada[0]).reshape(8, 1, 3 * d)
    lo0 = 4 * D_MODEL + 3 * D_MODEL
    w_bf = w_in[0].astype(BF16)
    w_lora = w_bf[:, lo0:lo0 + LW]
    w_out_bf = w_out[0].astype(BF16)
    ctx_row = lambda tok: 0
    lat_row = lambda tok: 1 + tok // dseq
    lp = (decay_w0[0], decay_w2[0], iclr_a0[0], iclr_a2[0], k_k[0], k_a[0], r_k[0],
          lnx_g[0], lnx_b[0])
    mu = shift_mu[0]
    lam = (lam_q1, lam_k1, lam_q2, lam_k2, subln_g)
    fg = final_g.reshape(1, d)

    xp = x_prompt.reshape(batch * seq, d)
    u_c, ul_c, nk, nv = _inproj(xp, mod3, norm_g, w_bf, w_lora, mu, ctx_row, seq, True)
    oa_c = _attn_ctx(u_c, seq, *lam)
    ob_c, s_new = _rwkv(u_c, ul_c, seq, RWKV_GW_CTX, None, True, *lp)
    y_prompt = _outproj(oa_c, ob_c, xp, mod3, w_out_bf, fg, ctx_row).reshape(batch, seq, d)

    xs = x_sample.reshape(dbatch * dseq, d)
    u_l, ul_l = _inproj(xs, mod3, norm_g, w_bf, w_lora, mu, lat_row, dseq, False)
    cv = cache_v[:, 0].reshape(dbatch * past, d)
    ckt = jnp.transpose(cache_k[:, 0], (0, 2, 3, 4, 1))
    oa_l = _attn_lat(u_l, ckt, cv, dseq, past, *lam)
    (ob_l,) = _rwkv(u_l, ul_l, dseq, RWKV_GW_LAT, state_rwkv[:, 0], False, *lp)
    y_sample = _outproj(oa_l, ob_l, xs, mod3, w_out_bf, fg, lat_row).reshape(dbatch, dseq, d)

    new_s = s_new.reshape(batch, 1, 2, N_HEADS_B, HEAD_DIM_B, HEAD_DIM_B)
    return (y_prompt, y_sample, nk, nv, new_s)
```

```python
import functools
import math

import jax
import jax.numpy as jnp
from jax import lax
from jax.experimental import pallas as pl
from jax.experimental.pallas import tpu as pltpu

F32 = jnp.float32
BF16 = jnp.bfloat16
HIGHEST = lax.Precision.HIGHEST

D_MODEL = 1024
GRID_W = 64
HEAD_DIM_A = 64
N_HEADS_A = 8
HEAD_DIM_B = 64
N_HEADS_B = 16
LORA_R = 64
D_SHIFT = 3 * D_MODEL + 4 * LORA_R
D_IN = 4 * D_MODEL + D_SHIFT + D_MODEL
ROPE_BASE = 10000.0
NORM_EPS = 1e-6
SUBLN_EPS = 1e-5
LNX_EPS = 64e-5
LAM_INIT = 0.8 - 0.6 * math.exp(-0.3 * 0)
EXP_NEG_HALF = math.exp(-0.5)
LOG2E = math.log2(math.e)

HL = 2 * HEAD_DIM_A
HD_SHIFT = 6
INPROJ_TM = 1024
W_SLOTS = 3
OUTPROJ_TM = 1024
ATTN_TQ = 512
ATTN_CTX_SEQS = 2
MOD_TN = 512
GW = 256
LW = 4 * LORA_R
PW = 128
RWKV_GW_CTX = 1024
RWKV_GW_LAT = 512
CHUNK = 64
BLOCK_T = 256
VMEM_LIMIT = 48 * 1024 * 1024
VMEM_LIMIT_BIG = 56 * 1024 * 1024
SEC_Q, SEC_K, SEC_V, SEC_GA, SEC_R, SEC_KB, SEC_VB, SEC_GB = range(8)


def _nn(a, b):
    return jnp.dot(a, b, preferred_element_type=F32)


def _nt(a, b):
    return lax.dot_general(a, b, (((1,), (1,)), ((), ())), preferred_element_type=F32)


def _tn(a, b):
    return lax.dot_general(a, b, (((0,), (0,)), ((), ())), preferred_element_type=F32)


def _split2(x):
    hi = x.astype(BF16)
    lo = (x - hi.astype(F32)).astype(BF16)
    return hi, lo


def _sigmoid(x):
    return 0.5 * jnp.tanh(0.5 * x) + 0.5


def _silu(x):
    return x * _sigmoid(x)


def _mod_kernel(c_ref, w_ref, b_ref, o_ref):
    s = _silu(c_ref[...])
    o_ref[...] = jnp.dot(s, w_ref[...], precision=HIGHEST, preferred_element_type=F32) + b_ref[...]


def _mod(cvec, w_ada, b_ada):
    n = w_ada.shape[1]
    tn = MOD_TN
    return pl.pallas_call(
        _mod_kernel,
        grid=(n // tn,),
        in_specs=[
            pl.BlockSpec((8, D_MODEL), lambda j: (0, 0)),
            pl.BlockSpec((D_MODEL, tn), lambda j: (0, j)),
            pl.BlockSpec((1, tn), lambda j: (0, j)),
        ],
        out_specs=pl.BlockSpec((8, tn), lambda j: (0, j)),
        out_shape=jax.ShapeDtypeStruct((8, n), F32),
        name="mod",
    )(cvec, w_ada, b_ada.reshape(1, n))


def _token_shift(x, mu, seq):
    row8 = lax.broadcasted_iota(jnp.int32, (8, x.shape[1]), 0)
    keep = 1.0 - mu[0:1, :] - mu[1:2, :]
    outs = []
    for b in range(x.shape[0] // seq):
        xs = x[b * seq:(b + 1) * seq]
        prev = pltpu.roll(xs, 1, 0)
        nxt = pltpu.roll(xs, seq - 1, 0)
        prev = jnp.concatenate([jnp.where(row8 == 0, 0.0, prev[0:8]), prev[8:]], axis=0)
        nxt = jnp.concatenate([nxt[:seq - 8], jnp.where(row8 == 7, 0.0, nxt[seq - 8:])], axis=0)
        outs.append(xs * keep + mu[0:1, :] * prev + mu[1:2, :] * nxt)
    return outs[0] if len(outs) == 1 else jnp.concatenate(outs, axis=0)


def _inproj_kernel(*refs, seq, want_cache):
    x_ref, mod_ref, g_ref, w_hbm, wl_ref, mu_ref, mul_ref, um_ref, ul_ref = refs[:9]
    h_ref, wbuf_ref, wsem = refs[-3:]
    j = pl.program_id(1)
    nsec = pl.num_programs(1)
    step = pl.program_id(0) * nsec + j
    nstep = pl.num_programs(0) * nsec

    def w_copy(s):
        sec = s % nsec
        col = pl.multiple_of(jnp.where(sec == SEC_GB, D_IN - D_MODEL, sec * D_MODEL), HL)
        slot = s % W_SLOTS
        return pltpu.make_async_copy(w_hbm.at[:, pl.ds(col, D_MODEL)], wbuf_ref.at[slot], wsem.at[slot])

    @pl.when(step == 0)
    def _():
        w_copy(step).start()
        w_copy(step + 1).start()

    @pl.when(step + 2 < nstep)
    def _():
        w_copy(step + 2).start()

    @pl.when(j == 0)
    def _():
        x = x_ref[...]
        y = x * lax.rsqrt(jnp.mean(x * x, axis=-1, keepdims=True) + NORM_EPS) * g_ref[...]
        mod = mod_ref[0]
        shift = mod[:, 0:D_MODEL]
        scale = mod[:, D_MODEL:2 * D_MODEL]
        h = (y * (1.0 + scale) + shift).astype(BF16)
        h_ref[...] = h
        ul_ref[...] = _token_shift(_nn(h, wl_ref[...]), mul_ref[...], seq).astype(BF16)

    w_copy(step).wait()
    w_ref = wbuf_ref.at[step % W_SLOTS]
    rwkv_sec = (j >= SEC_R) & (j <= SEC_VB)

    @pl.when(rwkv_sec)
    def _():
        acc = _nn(h_ref[...], w_ref[...])
        um_ref[...] = _token_shift(acc, mu_ref[...], seq).astype(BF16)

    @pl.when(jnp.logical_not(rwkv_sec))
    def _():
        acc = _nn(h_ref[...], w_ref[...])
        um_ref[...] = acc.astype(BF16)
        if want_cache:
            nk_ref, nv_ref = refs[9:11]
            nb = acc.shape[0] // seq

            @pl.when(j == SEC_K)
            def _():
                nk_ref[:, 0] = acc.reshape((nb,) + nk_ref.shape[2:])

            @pl.when(j == SEC_V)
            def _():
                nv_ref[:, 0] = acc.reshape((nb,) + nv_ref.shape[2:])


def _inproj(x, mod3, norm_g, w_bf, w_lora, shift_mu, mod_row, seq, want_cache):
    m = x.shape[0]
    tm = INPROJ_TM
    nsec = SEC_GB + 1
    nb = tm // seq
    out_specs = [pl.BlockSpec((tm, D_MODEL), lambda i, j: (i, j)),
                 pl.BlockSpec((tm, LW), lambda i, j: (i, 0))]
    out_shape = [jax.ShapeDtypeStruct((m, nsec * D_MODEL), BF16), jax.ShapeDtypeStruct((m, LW), BF16)]
    if want_cache:
        held = lambda i, j, sec: jnp.maximum(i - jnp.where(j < sec, 1, 0), 0)
        out_specs += [pl.BlockSpec((nb, 1, seq, N_HEADS_A, 2, HEAD_DIM_A),
                                   lambda i, j: (held(i, j, SEC_K), 0, 0, 0, 0, 0)),
                      pl.BlockSpec((nb, 1, seq, N_HEADS_A, 2 * HEAD_DIM_A),
                                   lambda i, j: (held(i, j, SEC_V), 0, 0, 0, 0))]
        out_shape += [jax.ShapeDtypeStruct((m // seq, 1, seq, N_HEADS_A, 2, HEAD_DIM_A), F32),
                      jax.ShapeDtypeStruct((m // seq, 1, seq, N_HEADS_A, 2 * HEAD_DIM_A), F32)]
    return pl.pallas_call(
        functools.partial(_inproj_kernel, seq=seq, want_cache=want_cache),
        grid=(m // tm, nsec),
        in_specs=[
            pl.BlockSpec((tm, D_MODEL), lambda i, j: (i, 0)),
            pl.BlockSpec((1, 1, 3 * D_MODEL), lambda i, j: (mod_row(i * tm), 0, 0)),
            pl.BlockSpec((1, D_MODEL), lambda i, j: (0, 0)),
            pl.BlockSpec(memory_space=pl.ANY),
            pl.BlockSpec((D_MODEL, LW), lambda i, j: (0, 0)),
            pl.BlockSpec((2, D_MODEL), lambda i, j: (0, jnp.clip(j - SEC_R, 0, SEC_VB - SEC_R))),
            pl.BlockSpec((2, LW), lambda i, j: (0, 3 * D_MODEL // LW)),
        ],
        out_specs=out_specs,
        out_shape=out_shape,
        scratch_shapes=[pltpu.VMEM((tm, D_MODEL), BF16),
                        pltpu.VMEM((W_SLOTS, D_MODEL, D_MODEL), BF16),
                        pltpu.SemaphoreType.DMA((W_SLOTS,))],
        compiler_params=pltpu.CompilerParams(
            dimension_semantics=("arbitrary", "arbitrary"), vmem_limit_bytes=VMEM_LIMIT_BIG),
        name="inproj_cache" if want_cache else "inproj",
    )(x, mod3, norm_g, w_bf, w_lora, shift_mu, shift_mu)


def _lam(q1_ref, k1_ref, q2_ref, k2_ref):
    s1 = jnp.sum(q1_ref[...] * k1_ref[...], axis=-1, keepdims=True)
    s2 = jnp.sum(q2_ref[...] * k2_ref[...], axis=-1, keepdims=True)
    return jnp.exp(s1) - jnp.exp(s2) + LAM_INIT


def _diff_head(q, kb, vb, g, lam, subg, kct=None):
    lane = lax.broadcasted_iota(jnp.int32, q.shape, 1)
    qs = q * (HEAD_DIM_A ** -0.5 * LOG2E)
    v1 = jnp.concatenate([vb, jnp.ones_like(vb)], axis=1)
    outs = []
    for m in range(2):
        qm = jnp.where((lane >> HD_SHIFT) == m, qs, 0.0).astype(BF16)
        s = _nt(qm, kb)
        if kct is not None:
            s = jnp.concatenate([_nn(qm, kct), s], axis=1)
        p = jnp.exp2(s - jnp.max(s, axis=-1, keepdims=True)).astype(BF16)
        pv = _nn(p, v1)
        outs.append(pv[:, :HL] / pv[:, HL:])
    o = outs[0] - lam * outs[1]
    o = o * lax.rsqrt(jnp.mean(o * o, axis=-1, keepdims=True) + SUBLN_EPS) * subg
    o = o * (1.0 - LAM_INIT)
    return o * _silu(g)


def _attn_ctx_kernel(q_ref, k_ref, v_ref, g_ref, q1_ref, k1_ref, q2_ref, k2_ref, sg_ref, o_ref, *, seq):
    lam = _lam(q1_ref, k1_ref, q2_ref, k2_ref)
    subg = sg_ref[...]
    for b in range(q_ref.shape[0] // seq):
        rows = slice(b * seq, (b + 1) * seq)
        for h in range(q_ref.shape[1] // HL):
            sl = slice(h * HL, (h + 1) * HL)
            o = _diff_head(q_ref[rows, sl].astype(F32), k_ref[rows, sl], v_ref[rows, sl],
                           g_ref[rows, sl].astype(F32), lam, subg)
            o_ref[rows, sl] = o.astype(BF16)


def _attn_ctx(um, seq, lam_q1, lam_k1, lam_q2, lam_k2, subln_g):
    m = um.shape[0]
    small = pl.BlockSpec((1, HEAD_DIM_A), lambda b: (0, 0))
    tm = ATTN_CTX_SEQS * seq
    blk = lambda sec: pl.BlockSpec((tm, D_MODEL), lambda b: (b, sec))
    return pl.pallas_call(
        functools.partial(_attn_ctx_kernel, seq=seq),
        grid=(m // tm,),
        in_specs=[blk(SEC_Q), blk(SEC_K), blk(SEC_V), blk(SEC_GA), small, small, small, small,
                  pl.BlockSpec((1, 2 * HEAD_DIM_A), lambda b: (0, 0))],
        out_specs=blk(0),
        out_shape=jax.ShapeDtypeStruct((m, D_MODEL), BF16),
        compiler_params=pltpu.CompilerParams(vmem_limit_bytes=VMEM_LIMIT),
        name="attn_ctx",
    )(um, um, um, um, lam_q1, lam_k1, lam_q2, lam_k2, subln_g)


def _rope(x, cos, s1, s2):
    return x * cos + pltpu.roll(x, 112, 1) * s1 + pltpu.roll(x, 16, 1) * s2


def _attn_lat_kernel(q_ref, k_ref, v_ref, g_ref, ck_ref, cv_ref, cq_ref, s1q_ref, s2q_ref,
                     ckk_ref, s1k_ref, s2k_ref, q1_ref, k1_ref, q2_ref, k2_ref, sg_ref,
                     o_ref, kct_ref, krot_ref, vall_ref):
    past = cv_ref.shape[0]
    w = q_ref.shape[1]
    nh = w // HL

    @pl.when(pl.program_id(2) == 0)
    def _():
        vall_ref[0:past, :] = cv_ref[...].astype(BF16)
        vall_ref[past:, :] = v_ref[...]
        for h in range(nh):
            sl = slice(h * HL, (h + 1) * HL)
            kct_ref[h] = ck_ref[0, h].reshape(HL, past).astype(BF16)
            krot_ref[:, sl] = _rope(k_ref[:, sl].astype(F32), ckk_ref[...], s1k_ref[...],
                                    s2k_ref[...]).astype(BF16)

    lam = _lam(q1_ref, k1_ref, q2_ref, k2_ref)
    subg = sg_ref[...]
    for h in range(nh):
        sl = slice(h * HL, (h + 1) * HL)
        q = _rope(q_ref[:, sl].astype(F32), cq_ref[...], s1q_ref[...], s2q_ref[...])
        o = _diff_head(q, krot_ref[:, sl], vall_ref[:, sl], g_ref[:, sl].astype(F32), lam, subg,
                       kct=kct_ref[h])
        o_ref[:, sl] = o.astype(BF16)


def _rope_tables(t):
    pos = jnp.arange(t)
    row = (pos // GRID_W).astype(F32)
    col = (pos % GRID_W).astype(F32)
    n_freq = HEAD_DIM_A // 4
    lane = jnp.arange(HL)
    d = lane % HEAD_DIM_A
    use_col = (d // 32) == 1
    second = ((d % 32) // 16) == 1
    inv = ROPE_BASE ** (-(d % n_freq).astype(F32) / n_freq)
    ang = jnp.where(use_col[None, :], col[:, None], row[:, None]) * inv[None, :]
    cos = jnp.cos(ang)
    sin = jnp.sin(ang)
    s1 = jnp.where(second[None, :], 0.0, -sin)
    s2 = jnp.where(second[None, :], sin, 0.0)
    return cos, s1, s2


def _attn_lat(u, cache_kt, cache_v2, seq, past, lam_q1, lam_k1, lam_q2, lam_k2, subln_g):
    m = u.shape[0]
    nbat = m // seq
    w = GW
    ng = D_MODEL // w
    tq = ATTN_TQ
    nq = seq // tq
    cos, s1, s2 = _rope_tables(seq)
    small = pl.BlockSpec((1, HEAD_DIM_A), lambda b, g, i: (0, 0))
    qblk = lambda sec: pl.BlockSpec((tq, w), lambda b, g, i: (b * nq + i, sec * ng + g))
    kblk = lambda sec: pl.BlockSpec((seq, w), lambda b, g, i: (b, sec * ng + g))
    ckblk = pl.BlockSpec((1, w // HL, 2, HEAD_DIM_A, past), lambda b, g, i: (b, g, 0, 0, 0))
    cblk = pl.BlockSpec((past, w), lambda b, g, i: (b, g))
    tq_blk = pl.BlockSpec((tq, HL), lambda b, g, i: (i, 0))
    tk_blk = pl.BlockSpec((seq, HL), lambda b, g, i: (0, 0))
    return pl.pallas_call(
        _attn_lat_kernel,
        grid=(nbat, ng, nq),
        in_specs=[qblk(SEC_Q), kblk(SEC_K), kblk(SEC_V), qblk(SEC_GA), ckblk, cblk,
                  tq_blk, tq_blk, tq_blk, tk_blk, tk_blk, tk_blk,
                  small, small, small, small,
                  pl.BlockSpec((1, 2 * HEAD_DIM_A), lambda b, g, i: (0, 0))],
        out_specs=pl.BlockSpec((tq, w), lambda b, g, i: (b * nq + i, g)),
        out_shape=jax.ShapeDtypeStruct((m, D_MODEL), BF16),
        scratch_shapes=[pltpu.VMEM((w // HL, HL, past), BF16), pltpu.VMEM((seq, w), BF16),
                        pltpu.VMEM((past + seq, w), BF16)],
        compiler_params=pltpu.CompilerParams(
            dimension_semantics=("arbitrary", "arbitrary", "arbitrary"), vmem_limit_bytes=VMEM_LIMIT),
        name="attn_lat",
    )(u, u, u, u, cache_kt, cache_v2, cos, s1, s2, cos, s1, s2,
      lam_q1, lam_k1, lam_q2, lam_k2, subln_g)


def _rwkv_consts():
    ri = lax.broadcasted_iota(jnp.int32, (PW, PW), 0)
    ci = lax.broadcasted_iota(jnp.int32, (PW, PW), 1)
    t = lax.broadcasted_iota(jnp.int32, (CHUNK, PW), 0)
    lane = lax.broadcasted_iota(jnp.int32, (CHUNK, PW), 1)
    j = lane & (CHUNK - 1)
    ti = lax.broadcasted_iota(jnp.int32, (CHUNK, CHUNK), 0)
    tj = lax.broadcasted_iota(jnp.int32, (CHUNK, CHUNK), 1)
    return dict(
        bmask=(ri >> HD_SHIFT) == (ci >> HD_SHIFT),
        eye=ri == ci,
        head0=lane < HEAD_DIM_B,
        eye_sbs=jnp.where(t == j, 1.0, 0.0),
        strict=(j < t, j > t),
        incl=(j <= t, j >= t),
        off=[((t >> (n + 1)) == (j >> (n + 1))) & ((t >> n) != (j >> n)) for n in range(6)],
        tri=(jnp.where(tj <= ti, 1.0, 0.0).astype(BF16), jnp.where(tj >= ti, 1.0, 0.0).astype(BF16)),
    )


def _wbd(x, cst):
    xb = x.astype(BF16)
    z = jnp.zeros_like(xb)
    return jnp.concatenate([jnp.where(cst["head0"], xb, z), jnp.where(cst["head0"], z, xb)], axis=0)


def _group_sum(x, e, exact=True):
    ew = e.shape[0]
    cols = range(0, x.shape[1], ew)
    if exact:
        hi, lo = _split2(x)
        parts = [_nn(hi[:, o:o + ew], e) + _nn(lo[:, o:o + ew], e) for o in cols]
    else:
        xb = x.astype(BF16)
        parts = [_nn(xb[:, o:o + ew], e) for o in cols]
    return parts[0] if len(parts) == 1 else jnp.concatenate(parts, axis=1)


def _pair_pipeline(h, ops, lg, states, cst, same_rows, out):
    L = CHUNK
    ncb = len(ops[0])
    idx = [(d, c) for d in range(2) for c in range(ncb)]
    half = lambda a: a[:, h * PW:(h + 1) * PW]

    ar, wbk, khbh, dg = {}, {}, {}, {}
    for d, c in idx:
        lw, r, kd, v, kk, b = (half(a) for a in ops[d][c])
        g = half(lg[d, c])
        g_l = g[0:1, :] if d == 1 else g[L - 1:L, :]
        at = -(kk * jnp.exp2(g - lw))
        rt = r * jnp.exp2(g)
        e_inv = jnp.exp2(-g)
        eg = jnp.exp2(g_l)
        bt, kt = b * e_inv, kd * e_inv
        ar[d, c] = jnp.concatenate([at, rt], axis=0).astype(BF16)
        wbk[d, c] = jnp.concatenate([_wbd(bt, cst), _wbd(kt, cst)], axis=0)
        khbh[d, c] = jnp.concatenate([(kt * eg).astype(BF16), (bt * eg).astype(BF16)], axis=0)
        dg[d, c] = jnp.where(cst["eye"], eg, 0.0).astype(BF16)
    yield

    gbk = {i: _nt(ar[i], wbk[i]) for i in idx}
    yield
    m_ab = {i: jnp.where(cst["strict"][i[0]], gbk[i][:L, :PW], 0.0) for i in idx}
    gb16 = {i: gbk[i].astype(BF16) for i in idx}
    zb = jnp.zeros((L, PW), BF16)
    m_abb = {i: jnp.where(cst["strict"][i[0]], gb16[i][:L, :PW], zb) for i in idx}
    m_rb = {i: jnp.where(cst["incl"][i[0]], gb16[i][L:, :PW], zb) for i in idx}
    m_akrk = {i: jnp.concatenate([jnp.where(cst["strict"][i[0]], gb16[i][:L, PW:], zb),
                                  jnp.where(cst["incl"][i[0]], gb16[i][L:, PW:], zb)], axis=0)
              for i in idx}

    x = {i: cst["eye_sbs"] + jnp.where(cst["off"][0], m_ab[i], 0.0) for i in idx}
    for off in cst["off"][1:]:
        p = {i: _nn(jnp.where(off, m_abb[i], zb), _wbd(x[i], cst)) for i in idx}
        yield
        x = {i: x[i] + _nn(x[i].astype(BF16), _wbd(p[i], cst)) for i in idx}
        yield
    xb = {i: x[i].astype(BF16) for i in idx}

    vsrc = {(d, c): (0, c) if same_rows else (d, c) for d, c in idx}
    vhalf = {i: half(ops[i[0]][i[1]][3]) for i in dict.fromkeys(vsrc.values())}
    wv = {i: _wbd(vhalf[i], cst) for i in vhalf}
    vb = {i: vhalf[i].astype(BF16) for i in vhalf}
    lk = {i: jnp.concatenate([jnp.concatenate([ar[i], m_akrk[i]], axis=1),
                              jnp.concatenate([dg[i], jnp.zeros_like(dg[i])], axis=1)], axis=0)
          for i in idx}

    s = [states[0][h], states[1][h]]
    for step in range(ncb):
        cur = [(0, step), (1, ncb - 1 - step)]
        ars = {i: _nn(lk[i], jnp.concatenate([s[i[0]].astype(BF16), wv[vsrc[i]]], axis=0)) for i in cur}
        yield
        u = {i: _nn(xb[i], _wbd(ars[i][:L], cst)) for i in cur}
        yield
        for i in cur:
            d, c = i
            out["y"][d, c, h] = ars[i][L:2 * L] + _nn(m_rb[i], _wbd(u[i], cst))
            upd = _tn(khbh[i], jnp.concatenate([vb[vsrc[i]], u[i].astype(BF16)], axis=0))
            s[d] = ars[i][2 * L:] + jnp.where(cst["bmask"], upd, 0.0)
        yield
    out["s"][0][h], out["s"][1][h] = s


def _block_step(ops, states, cst, same_rows):
    ncb = len(ops[0])
    npair = ops[0][0][0].shape[1] // PW

    lg = {}
    for d in range(2):
        for c in range(ncb):
            hi, mid = _split2(ops[d][c][0])
            tri = cst["tri"][d]
            lg[d, c] = _nn(tri, hi) + _nn(tri, mid)

    out = {"y": {}, "s": [[None] * npair, [None] * npair]}
    pending = [_pair_pipeline(h, ops, lg, states, cst, same_rows, out) for h in range(npair)]
    active = pending
    while active:
        active = [gen for gen in active if next(gen, "done") != "done"]
    ys = [[jnp.concatenate([out["y"][d, c, h] for h in range(npair)], axis=1) if npair > 1
           else out["y"][d, c, 0] for c in range(ncb)] for d in range(2)]
    return ys, out["s"]


def _rwkv_kernel(*refs, seq, gw, has_s0, want_state):
    it = iter(refs)
    r_ref, k_ref, v_ref, lo_ref, gb_ref = (next(it) for _ in range(5))
    w0_ref, w2_ref, a0_ref, a2_ref = (next(it) for _ in range(4))
    kk_ref, ka_ref, rk_ref, lng_ref, lnb_ref = (next(it) for _ in range(5))
    s0_ref = next(it) if has_s0 else None
    ob_ref = next(it)
    sf_ref = next(it) if want_state else None
    r_s, v_s, kn_s, lw_s, kd_s, b_s, y_s, st_s = (next(it) for _ in range(8))

    T = seq
    cst = _rwkv_consts()
    npair = gw // PW
    hpp = PW // HEAD_DIM_B
    gi = lax.broadcasted_iota(jnp.int32, (LW, LW), 0)
    gj = lax.broadcasted_iota(jnp.int32, (LW, LW), 1)
    e = jnp.where((gi >> HD_SHIFT) == (gj >> HD_SHIFT), 1.0, 0.0).astype(BF16)

    r = r_ref[...].astype(F32)
    k = k_ref[...].astype(F32)
    v = v_ref[...].astype(F32)
    lo = lo_ref[...].astype(F32)
    r_s[...] = r
    v_s[...] = v
    kk = k * kk_ref[...]
    kk = kk * lax.rsqrt(_group_sum(kk * kk, e, exact=False) + 1e-12)
    kn_s[...] = kk
    bonus_dot = _group_sum(r * k * rk_ref[...], e, exact=False)

    tanh_lo = jnp.tanh(lo).astype(BF16)
    lo_b = lo.astype(BF16)
    kka = k * ka_ref[...]

    def padded(w, before):
        parts = [jnp.zeros((before, gw), F32)] if before else []
        parts.append(w)
        after = LW - before - LORA_R
        if after:
            parts.append(jnp.zeros((after, gw), F32))
        return jnp.concatenate(parts, axis=0).astype(BF16)

    for d in range(2):
        w2p = padded(w2_ref[d], LORA_R * d)
        a2p = padded(a2_ref[d], LORA_R * (2 + d))
        z = w0_ref[d:d + 1, :] + _nn(tanh_lo, w2p)
        half_c = -0.5 * EXP_NEG_HALF * LOG2E
        lw_s[d] = half_c * jnp.tanh(0.5 * z) + half_c
        a = _sigmoid(a0_ref[d:d + 1, :] + _nn(lo_b, a2p))
        kd_s[d] = k + kka * (a - 1.0)
        b_s[d] = kk * a

    if has_s0:
        ident = jnp.where(cst["eye"], 1.0, 0.0).astype(BF16)
        for d in range(2):
            for h in range(npair):
                z0 = s0_ref[0, d, h * hpp:(h + 1) * hpp].reshape(PW, HEAD_DIM_B)
                hi, lo2 = _split2(z0)
                zt = _tn(hi, ident) + _tn(lo2, ident)
                st_s[d, h] = jnp.where(cst["bmask"], jnp.concatenate([zt] * hpp, axis=0), 0.0)
    else:
        st_s[...] = jnp.zeros(st_s.shape, F32)

    nblk = T // BLOCK_T
    ncb = BLOCK_T // CHUNK

    def block(i, carry):
        rows = []
        for d in range(2):
            blk = i if d == 0 else nblk - 1 - i
            rows.append([])
            for c in range(ncb):
                start = blk * BLOCK_T + c * CHUNK
                if not isinstance(start, int):
                    start = pl.multiple_of(start, CHUNK)
                rows[d].append(pl.ds(start, CHUNK))
        ops = [[(lw_s[d, rw, :], r_s[rw, :], kd_s[d, rw, :], v_s[rw, :], kn_s[rw, :], b_s[d, rw, :])
                for rw in rows[d]] for d in range(2)]
        states = [[st_s[d, h] for h in range(npair)] for d in range(2)]
        ys, s_new = _block_step(ops, states, cst, nblk == 1)
        for d in range(2):
            for c in range(ncb):
                y_s[d, rows[d][c], :] = ys[d][c]
            for h in range(npair):
                st_s[d, h] = s_new[d][h]
        return carry

    if nblk == 1:
        block(0, 0)
    else:
        lax.fori_loop(0, nblk, block, 0)

    y = y_s[0] + y_s[1]
    mu = _group_sum(y, e) * (1.0 / HEAD_DIM_B)
    dlt = y - mu
    var = _group_sum(dlt * dlt, e, exact=False) * (1.0 / HEAD_DIM_B)
    yn = dlt * lax.rsqrt(var + LNX_EPS) * lng_ref[...] + lnb_ref[...]
    out = (yn + bonus_dot * v_s[...]) * _silu(gb_ref[...].astype(F32))
    ob_ref[...] = out.astype(BF16)

    if want_state:
        ri = lax.broadcasted_iota(jnp.int32, (PW, HEAD_DIM_B), 0)
        ci = lax.broadcasted_iota(jnp.int32, (PW, HEAD_DIM_B), 1)
        fold = jnp.where((ri & (HEAD_DIM_B - 1)) == ci, 1.0, 0.0).astype(BF16)
        for d in range(2):
            for h in range(npair):
                hi, lo2 = _split2(st_s[d, h])
                sf = _tn(hi, fold) + _tn(lo2, fold)
                sf_ref[0, d, h * hpp:(h + 1) * hpp] = sf.reshape(hpp, HEAD_DIM_B, HEAD_DIM_B)


def _rwkv(u, ul, seq, gw, s0, want_state, decay_w0, decay_w2, iclr_a0, iclr_a2,
          k_k, k_a, r_k, lnx_g, lnx_b):
    m = u.shape[0]
    nbat = m // seq
    ng = D_MODEL // gw
    hg = gw // HEAD_DIM_B
    ublk = lambda sec: pl.BlockSpec((seq, gw), lambda b, g: (b, sec * ng + g))
    lblk = pl.BlockSpec((seq, LW), lambda b, g: (b, 0))
    vec2 = pl.BlockSpec((2, gw), lambda b, g: (0, g))
    mat2 = pl.BlockSpec((2, LORA_R, gw), lambda b, g: (0, 0, g))
    vec1 = pl.BlockSpec((1, gw), lambda b, g: (0, g))
    sblk = pl.BlockSpec((1, 2, hg, HEAD_DIM_B, HEAD_DIM_B), lambda b, g: (b, 0, g, 0, 0))
    in_specs = [ublk(SEC_R), ublk(SEC_KB), ublk(SEC_VB), lblk, ublk(SEC_GB),
                vec2, mat2, vec2, mat2, vec1, vec1, vec1, vec1, vec1]
    args = [u, u, u, ul, u,
            decay_w0, decay_w2, iclr_a0, iclr_a2,
            k_k.reshape(1, D_MODEL), k_a.reshape(1, D_MODEL), r_k.reshape(1, D_MODEL),
            lnx_g.reshape(1, D_MODEL), lnx_b.reshape(1, D_MODEL)]
    if s0 is not None:
        in_specs.append(sblk)
        args.append(s0)
    out_specs = [pl.BlockSpec((seq, gw), lambda b, g: (b, g))]
    out_shape = [jax.ShapeDtypeStruct((m, D_MODEL), BF16)]
    if want_state:
        out_specs.append(sblk)
        out_shape.append(jax.ShapeDtypeStruct((nbat, 2, N_HEADS_B, HEAD_DIM_B, HEAD_DIM_B), F32))
    tw = pltpu.VMEM((seq, gw), F32)
    tw2 = pltpu.VMEM((2, seq, gw), F32)
    res = pl.pallas_call(
        functools.partial(_rwkv_kernel, seq=seq, gw=gw, has_s0=s0 is not None, want_state=want_state),
        grid=(nbat, ng),
        in_specs=in_specs,
        out_specs=out_specs,
        out_shape=out_shape,
        scratch_shapes=[tw, tw, tw, tw2, tw2, tw2, tw2, pltpu.VMEM((2, gw // PW, PW, PW), F32)],
        compiler_params=pltpu.CompilerParams(
            dimension_semantics=("arbitrary", "arbitrary"), vmem_limit_bytes=VMEM_LIMIT_BIG),
        name="rwkv_state" if want_state else "rwkv",
    )(*args)
    return res


def _outproj_kernel(oa_ref, ob_ref, x_ref, mod_ref, w_ref, fg_ref, y_ref):
    acc = _nn(oa_ref[...], w_ref[0:D_MODEL, :]) + _nn(ob_ref[...], w_ref[D_MODEL:, :])
    gate = mod_ref[0][:, 2 * D_MODEL:]
    xo = x_ref[...] + gate * acc
    y_ref[...] = xo * lax.rsqrt(jnp.mean(xo * xo, axis=-1, keepdims=True) + NORM_EPS) * fg_ref[...]


def _outproj(o_a, o_b, x, mod3, w_out_bf, final_g, mod_row):
    m = x.shape[0]
    tm = OUTPROJ_TM
    blk = pl.BlockSpec((tm, D_MODEL), lambda i: (i, 0))
    return pl.pallas_call(
        _outproj_kernel,
        grid=(m // tm,),
        in_specs=[blk, blk, blk,
                  pl.BlockSpec((1, 1, 3 * D_MODEL), lambda i: (mod_row(i * tm), 0, 0)),
                  pl.BlockSpec((2 * D_MODEL, D_MODEL), lambda i: (0, 0)),
                  pl.BlockSpec((1, D_MODEL), lambda i: (0, 0))],
        out_specs=blk,
        out_shape=jax.ShapeDtypeStruct((m, D_MODEL), F32),
        compiler_params=pltpu.CompilerParams(vmem_limit_bytes=VMEM_LIMIT),
        name="outproj",
    )(o_a, o_b, x, mod3, w_out_bf, final_g)


def kernel(x_prompt, x_sample, cache_k, cache_v, state_rwkv, c, c_ctx, norm_g, w_ada, b_ada, w_in, lam_q1, lam_k1, lam_q2, lam_k2, subln_g, shift_mu, decay_w0, decay_w2, iclr_a0, iclr_a2, k_k, k_a, r_k, lnx_g, lnx_b, w_out, final_g):
    batch, seq, d = x_prompt.shape
    dbatch, dseq, _ = x_sample.shape
    past = cache_k.shape[2]
    assert d == D_MODEL and w_in.shape == (1, D_MODEL, D_IN) and dseq % GRID_W == 0

    cvec = jnp.concatenate([c_ctx[None, :], c, jnp.zeros((8 - 1 - dbatch, d), F32)], axis=0)
    mod3 = _mod(cvec, w_ada[0], b_ada[0]).reshape(8, 1, 3 * d)
    lo0 = 4 * D_MODEL + 3 * D_MODEL
    w_bf = w_in[0].astype(BF16)
    w_lora = w_bf[:, lo0:lo0 + LW]
    w_out_bf = w_out[0].astype(BF16)
    ctx_row = lambda tok: 0
    lat_row = lambda tok: 1 + tok // dseq
    lp = (decay_w0[0], decay_w2[0], iclr_a0[0], iclr_a2[0], k_k[0], k_a[0], r_k[0],
          lnx_g[0], lnx_b[0])
    mu = shift_mu[0]
    lam = (lam_q1, lam_k1, lam_q2, lam_k2, subln_g)
    fg = final_g.reshape(1, d)

    xp = x_prompt.reshape(batch * seq, d)
    u_c, ul_c, nk, nv = _inproj(xp, mod3, norm_g, w_bf, w_lora, mu, ctx_row, seq, True)
    oa_c = _attn_ctx(u_c, seq, *lam)
    ob_c, s_new = _rwkv(u_c, ul_c, seq, RWKV_GW_CTX, None, True, *lp)
    y_prompt = _outproj(oa_c, ob_c, xp, mod3, w_out_bf, fg, ctx_row).reshape(batch, seq, d)

    xs = x_sample.reshape(dbatch * dseq, d)
    u_l, ul_l = _inproj(xs, mod3, norm_g, w_bf, w_lora, mu, lat_row, dseq, False)
    cv = cache_v[:, 0].reshape(dbatch * past, d)
    ckt = jnp.transpose(cache_k[:, 0], (0, 2, 3, 4, 1))
    oa_l = _attn_lat(u_l, ckt, cv, dseq, past, *lam)
    (ob_l,) = _rwkv(u_l, ul_l, dseq, RWKV_GW_LAT, state_rwkv[:, 0], False, *lp)
    y_sample = _outproj(oa_l, ob_l, xs, mod3, w_out_bf, fg, lat_row).reshape(dbatch, dseq, d)

    new_s = s_new.reshape(batch, 1, 2, N_HEADS_B, HEAD_DIM_B, HEAD_DIM_B)
    return (y_prompt, y_sample, nk, nv, new_s)
```

```python
import functools
import math

import jax
import jax.numpy as jnp
from jax import lax
from jax.experimental import pallas as pl
from jax.experimental.pallas import tpu as pltpu

F32 = jnp.float32
BF16 = jnp.bfloat16
HIGHEST = lax.Precision.HIGHEST

D_MODEL = 1024
GRID_W = 64
HEAD_DIM_A = 64
N_HEADS_A = 8
HEAD_DIM_B = 64
N_HEADS_B = 16
LORA_R = 64
D_SHIFT = 3 * D_MODEL + 4 * LORA_R
D_IN = 4 * D_MODEL + D_SHIFT + D_MODEL
ROPE_BASE = 10000.0
NORM_EPS = 1e-6
SUBLN_EPS = 1e-5
LNX_EPS = 64e-5
LAM_INIT = 0.8 - 0.6 * math.exp(-0.3 * 0)
EXP_NEG_HALF = math.exp(-0.5)
LOG2E = math.log2(math.e)

HL = 2 * HEAD_DIM_A
HD_SHIFT = 6
INPROJ_TM = 1024
W_SLOTS = 3
OUTPROJ_TM = 1024
ATTN_TQ = 512
ATTN_CTX_SEQS = 4
MOD_TN = 512
GW = 256
LW = 4 * LORA_R
PW = 128
RWKV_GW_CTX = 1024
RWKV_GW_LAT = 512
CHUNK = 64
BLOCK_T = 256
VMEM_LIMIT = 48 * 1024 * 1024
VMEM_LIMIT_BIG = 56 * 1024 * 1024
SEC_Q, SEC_K, SEC_V, SEC_GA, SEC_R, SEC_KB, SEC_VB, SEC_GB = range(8)


def _nn(a, b):
    return jnp.dot(a, b, preferred_element_type=F32)


def _nt(a, b):
    return lax.dot_general(a, b, (((1,), (1,)), ((), ())), preferred_element_type=F32)


def _tn(a, b):
    return lax.dot_general(a, b, (((0,), (0,)), ((), ())), preferred_element_type=F32)


def _split2(x):
    hi = x.astype(BF16)
    lo = (x - hi.astype(F32)).astype(BF16)
    return hi, lo


def _sigmoid(x):
    return 0.5 * jnp.tanh(0.5 * x) + 0.5


def _silu(x):
    return x * _sigmoid(x)


def _mod_kernel(c_ref, w_ref, b_ref, o_ref):
    s = _silu(c_ref[...])
    o_ref[...] = jnp.dot(s, w_ref[...], precision=HIGHEST, preferred_element_type=F32) + b_ref[...]


def _mod(cvec, w_ada, b_ada):
    n = w_ada.shape[1]
    tn = MOD_TN
    return pl.pallas_call(
        _mod_kernel,
        grid=(n // tn,),
        in_specs=[
            pl.BlockSpec((8, D_MODEL), lambda j: (0, 0)),
            pl.BlockSpec((D_MODEL, tn), lambda j: (0, j)),
            pl.BlockSpec((1, tn), lambda j: (0, j)),
        ],
        out_specs=pl.BlockSpec((8, tn), lambda j: (0, j)),
        out_shape=jax.ShapeDtypeStruct((8, n), F32),
        name="mod",
    )(cvec, w_ada, b_ada.reshape(1, n))


def _token_shift(x, mu, seq):
    row8 = lax.broadcasted_iota(jnp.int32, (8, x.shape[1]), 0)
    keep = 1.0 - mu[0:1, :] - mu[1:2, :]
    outs = []
    for b in range(x.shape[0] // seq):
        xs = x[b * seq:(b + 1) * seq]
        prev = pltpu.roll(xs, 1, 0)
        nxt = pltpu.roll(xs, seq - 1, 0)
        prev = jnp.concatenate([jnp.where(row8 == 0, 0.0, prev[0:8]), prev[8:]], axis=0)
        nxt = jnp.concatenate([nxt[:seq - 8], jnp.where(row8 == 7, 0.0, nxt[seq - 8:])], axis=0)
        outs.append(xs * keep + mu[0:1, :] * prev + mu[1:2, :] * nxt)
    return outs[0] if len(outs) == 1 else jnp.concatenate(outs, axis=0)


def _inproj_kernel(*refs, seq, want_cache):
    x_ref, mod_ref, g_ref, w_hbm, wl_ref, mu_ref, mul_ref, um_ref, ul_ref = refs[:9]
    h_ref, wbuf_ref, wsem = refs[-3:]
    j = pl.program_id(1)
    nsec = pl.num_programs(1)
    step = pl.program_id(0) * nsec + j
    nstep = pl.num_programs(0) * nsec

    def w_copy(s):
        sec = s % nsec
        col = pl.multiple_of(jnp.where(sec == SEC_GB, D_IN - D_MODEL, sec * D_MODEL), HL)
        slot = s % W_SLOTS
        return pltpu.make_async_copy(w_hbm.at[:, pl.ds(col, D_MODEL)], wbuf_ref.at[slot], wsem.at[slot])

    @pl.when(step == 0)
    def _():
        w_copy(step).start()
        w_copy(step + 1).start()

    @pl.when(step + 2 < nstep)
    def _():
        w_copy(step + 2).start()

    @pl.when(j == 0)
    def _():
        x = x_ref[...]
        y = x * lax.rsqrt(jnp.mean(x * x, axis=-1, keepdims=True) + NORM_EPS) * g_ref[...]
        mod = mod_ref[0]
        shift = mod[:, 0:D_MODEL]
        scale = mod[:, D_MODEL:2 * D_MODEL]
        h = (y * (1.0 + scale) + shift).astype(BF16)
        h_ref[...] = h
        ul_ref[...] = _token_shift(_nn(h, wl_ref[...]), mul_ref[...], seq).astype(BF16)

    w_copy(step).wait()
    w_ref = wbuf_ref.at[step % W_SLOTS]
    rwkv_sec = (j >= SEC_R) & (j <= SEC_VB)

    @pl.when(rwkv_sec)
    def _():
        acc = _nn(h_ref[...], w_ref[...])
        um_ref[...] = _token_shift(acc, mu_ref[...], seq).astype(BF16)

    @pl.when(jnp.logical_not(rwkv_sec))
    def _():
        acc = _nn(h_ref[...], w_ref[...])
        um_ref[...] = acc.astype(BF16)
        if want_cache:
            nk_ref, nv_ref = refs[9:11]
            nb = acc.shape[0] // seq

            @pl.when(j == SEC_K)
            def _():
                nk_ref[:, 0] = acc.reshape((nb,) + nk_ref.shape[2:])

            @pl.when(j == SEC_V)
            def _():
                nv_ref[:, 0] = acc.reshape((nb,) + nv_ref.shape[2:])


def _inproj(x, mod3, norm_g, w_bf, w_lora, shift_mu, mod_row, seq, want_cache):
    m = x.shape[0]
    tm = INPROJ_TM
    nsec = SEC_GB + 1
    nb = tm // seq
    out_specs = [pl.BlockSpec((tm, D_MODEL), lambda i, j: (i, j)),
                 pl.BlockSpec((tm, LW), lambda i, j: (i, 0))]
    out_shape = [jax.ShapeDtypeStruct((m, nsec * D_MODEL), BF16), jax.ShapeDtypeStruct((m, LW), BF16)]
    if want_cache:
        held = lambda i, j, sec: jnp.maximum(i - jnp.where(j < sec, 1, 0), 0)
        out_specs += [pl.BlockSpec((nb, 1, seq, N_HEADS_A, 2, HEAD_DIM_A),
                                   lambda i, j: (held(i, j, SEC_K), 0, 0, 0, 0, 0)),
                      pl.BlockSpec((nb, 1, seq, N_HEADS_A, 2 * HEAD_DIM_A),
                                   lambda i, j: (held(i, j, SEC_V), 0, 0, 0, 0))]
        out_shape += [jax.ShapeDtypeStruct((m // seq, 1, seq, N_HEADS_A, 2, HEAD_DIM_A), F32),
                      jax.ShapeDtypeStruct((m // seq, 1, seq, N_HEADS_A, 2 * HEAD_DIM_A), F32)]
    return pl.pallas_call(
        functools.partial(_inproj_kernel, seq=seq, want_cache=want_cache),
        grid=(m // tm, nsec),
        in_specs=[
            pl.BlockSpec((tm, D_MODEL), lambda i, j: (i, 0)),
            pl.BlockSpec((1, 1, 3 * D_MODEL), lambda i, j: (mod_row(i * tm), 0, 0)),
            pl.BlockSpec((1, D_MODEL), lambda i, j: (0, 0)),
            pl.BlockSpec(memory_space=pl.ANY),
            pl.BlockSpec((D_MODEL, LW), lambda i, j: (0, 0)),
            pl.BlockSpec((2, D_MODEL), lambda i, j: (0, jnp.clip(j - SEC_R, 0, SEC_VB - SEC_R))),
            pl.BlockSpec((2, LW), lambda i, j: (0, 3 * D_MODEL // LW)),
        ],
        out_specs=out_specs,
        out_shape=out_shape,
        scratch_shapes=[pltpu.VMEM((tm, D_MODEL), BF16),
                        pltpu.VMEM((W_SLOTS, D_MODEL, D_MODEL), BF16),
                        pltpu.SemaphoreType.DMA((W_SLOTS,))],
        compiler_params=pltpu.CompilerParams(
            dimension_semantics=("arbitrary", "arbitrary"), vmem_limit_bytes=VMEM_LIMIT_BIG),
        name="inproj_cache" if want_cache else "inproj",
    )(x, mod3, norm_g, w_bf, w_lora, shift_mu, shift_mu)


def _lam(q1_ref, k1_ref, q2_ref, k2_ref):
    s1 = jnp.sum(q1_ref[...] * k1_ref[...], axis=-1, keepdims=True)
    s2 = jnp.sum(q2_ref[...] * k2_ref[...], axis=-1, keepdims=True)
    return jnp.exp(s1) - jnp.exp(s2) + LAM_INIT


def _diff_head(q, kb, vb, g, lam, subg, kct=None):
    lane = lax.broadcasted_iota(jnp.int32, q.shape, 1)
    qs = q * (HEAD_DIM_A ** -0.5 * LOG2E)
    v1 = jnp.concatenate([vb, jnp.ones_like(vb)], axis=1)
    outs = []
    for m in range(2):
        qm = jnp.where((lane >> HD_SHIFT) == m, qs, 0.0).astype(BF16)
        s = _nt(qm, kb)
        if kct is not None:
            s = jnp.concatenate([_nn(qm, kct), s], axis=1)
        p = jnp.exp2(s - jnp.max(s, axis=-1, keepdims=True)).astype(BF16)
        pv = _nn(p, v1)
        outs.append(pv[:, :HL] / pv[:, HL:])
    o = outs[0] - lam * outs[1]
    o = o * lax.rsqrt(jnp.mean(o * o, axis=-1, keepdims=True) + SUBLN_EPS) * subg
    o = o * (1.0 - LAM_INIT)
    return o * _silu(g)


def _attn_ctx_kernel(q_ref, k_ref, v_ref, g_ref, q1_ref, k1_ref, q2_ref, k2_ref, sg_ref, o_ref, *, seq):
    lam = _lam(q1_ref, k1_ref, q2_ref, k2_ref)
    subg = sg_ref[...]
    for b in range(q_ref.shape[0] // seq):
        rows = slice(b * seq, (b + 1) * seq)
        for h in range(q_ref.shape[1] // HL):
            sl = slice(h * HL, (h + 1) * HL)
            o = _diff_head(q_ref[rows, sl].astype(F32), k_ref[rows, sl], v_ref[rows, sl],
                           g_ref[rows, sl].astype(F32), lam, subg)
            o_ref[rows, sl] = o.astype(BF16)


def _attn_ctx(um, seq, lam_q1, lam_k1, lam_q2, lam_k2, subln_g):
    m = um.shape[0]
    small = pl.BlockSpec((1, HEAD_DIM_A), lambda b: (0, 0))
    tm = ATTN_CTX_SEQS * seq
    blk = lambda sec: pl.BlockSpec((tm, D_MODEL), lambda b: (b, sec))
    return pl.pallas_call(
        functools.partial(_attn_ctx_kernel, seq=seq),
        grid=(m // tm,),
        in_specs=[blk(SEC_Q), blk(SEC_K), blk(SEC_V), blk(SEC_GA), small, small, small, small,
                  pl.BlockSpec((1, 2 * HEAD_DIM_A), lambda b: (0, 0))],
        out_specs=blk(0),
        out_shape=jax.ShapeDtypeStruct((m, D_MODEL), BF16),
        compiler_params=pltpu.CompilerParams(vmem_limit_bytes=VMEM_LIMIT),
        name="attn_ctx",
    )(um, um, um, um, lam_q1, lam_k1, lam_q2, lam_k2, subln_g)


def _rope(x, cos, s1, s2):
    return x * cos + pltpu.roll(x, 112, 1) * s1 + pltpu.roll(x, 16, 1) * s2


def _attn_lat_kernel(q_ref, k_ref, v_ref, g_ref, ck_ref, cv_ref, cq_ref, s1q_ref, s2q_ref,
                     ckk_ref, s1k_ref, s2k_ref, q1_ref, k1_ref, q2_ref, k2_ref, sg_ref,
                     o_ref, kct_ref, krot_ref, vall_ref):
    past = cv_ref.shape[0]
    w = q_ref.shape[1]
    nh = w // HL

    @pl.when(pl.program_id(2) == 0)
    def _():
        vall_ref[0:past, :] = cv_ref[...].astype(BF16)
        vall_ref[past:, :] = v_ref[...]
        for h in range(nh):
            sl = slice(h * HL, (h + 1) * HL)
            kct_ref[h] = ck_ref[0, h].reshape(HL, past).astype(BF16)
            krot_ref[:, sl] = _rope(k_ref[:, sl].astype(F32), ckk_ref[...], s1k_ref[...],
                                    s2k_ref[...]).astype(BF16)

    lam = _lam(q1_ref, k1_ref, q2_ref, k2_ref)
    subg = sg_ref[...]
    for h in range(nh):
        sl = slice(h * HL, (h + 1) * HL)
        q = _rope(q_ref[:, sl].astype(F32), cq_ref[...], s1q_ref[...], s2q_ref[...])
        o = _diff_head(q, krot_ref[:, sl], vall_ref[:, sl], g_ref[:, sl].astype(F32), lam, subg,
                       kct=kct_ref[h])
        o_ref[:, sl] = o.astype(BF16)


def _rope_tables(t):
    pos = jnp.arange(t)
    row = (pos // GRID_W).astype(F32)
    col = (pos % GRID_W).astype(F32)
    n_freq = HEAD_DIM_A // 4
    lane = jnp.arange(HL)
    d = lane % HEAD_DIM_A
    use_col = (d // 32) == 1
    second = ((d % 32) // 16) == 1
    inv = ROPE_BASE ** (-(d % n_freq).astype(F32) / n_freq)
    ang = jnp.where(use_col[None, :], col[:, None], row[:, None]) * inv[None, :]
    cos = jnp.cos(ang)
    sin = jnp.sin(ang)
    s1 = jnp.where(second[None, :], 0.0, -sin)
    s2 = jnp.where(second[None, :], sin, 0.0)
    return cos, s1, s2


def _attn_lat(u, cache_kt, cache_v2, seq, past, lam_q1, lam_k1, lam_q2, lam_k2, subln_g):
    m = u.shape[0]
    nbat = m // seq
    w = GW
    ng = D_MODEL // w
    tq = ATTN_TQ
    nq = seq // tq
    cos, s1, s2 = _rope_tables(seq)
    small = pl.BlockSpec((1, HEAD_DIM_A), lambda b, g, i: (0, 0))
    qblk = lambda sec: pl.BlockSpec((tq, w), lambda b, g, i: (b * nq + i, sec * ng + g))
    kblk = lambda sec: pl.BlockSpec((seq, w), lambda b, g, i: (b, sec * ng + g))
    ckblk = pl.BlockSpec((1, w // HL, 2, HEAD_DIM_A, past), lambda b, g, i: (b, g, 0, 0, 0))
    cblk = pl.BlockSpec((past, w), lambda b, g, i: (b, g))
    tq_blk = pl.BlockSpec((tq, HL), lambda b, g, i: (i, 0))
    tk_blk = pl.BlockSpec((seq, HL), lambda b, g, i: (0, 0))
    return pl.pallas_call(
        _attn_lat_kernel,
        grid=(nbat, ng, nq),
        in_specs=[qblk(SEC_Q), kblk(SEC_K), kblk(SEC_V), qblk(SEC_GA), ckblk, cblk,
                  tq_blk, tq_blk, tq_blk, tk_blk, tk_blk, tk_blk,
                  small, small, small, small,
                  pl.BlockSpec((1, 2 * HEAD_DIM_A), lambda b, g, i: (0, 0))],
        out_specs=pl.BlockSpec((tq, w), lambda b, g, i: (b * nq + i, g)),
        out_shape=jax.ShapeDtypeStruct((m, D_MODEL), BF16),
        scratch_shapes=[pltpu.VMEM((w // HL, HL, past), BF16), pltpu.VMEM((seq, w), BF16),
                        pltpu.VMEM((past + seq, w), BF16)],
        compiler_params=pltpu.CompilerParams(
            dimension_semantics=("arbitrary", "arbitrary", "arbitrary"), vmem_limit_bytes=VMEM_LIMIT),
        name="attn_lat",
    )(u, u, u, u, cache_kt, cache_v2, cos, s1, s2, cos, s1, s2,
      lam_q1, lam_k1, lam_q2, lam_k2, subln_g)


def _rwkv_consts():
    ri = lax.broadcasted_iota(jnp.int32, (PW, PW), 0)
    ci = lax.broadcasted_iota(jnp.int32, (PW, PW), 1)
    t = lax.broadcasted_iota(jnp.int32, (CHUNK, PW), 0)
    lane = lax.broadcasted_iota(jnp.int32, (CHUNK, PW), 1)
    j = lane & (CHUNK - 1)
    ti = lax.broadcasted_iota(jnp.int32, (CHUNK, CHUNK), 0)
    tj = lax.broadcasted_iota(jnp.int32, (CHUNK, CHUNK), 1)
    return dict(
        bmask=(ri >> HD_SHIFT) == (ci >> HD_SHIFT),
        eye=ri == ci,
        head0=lane < HEAD_DIM_B,
        eye_sbs=jnp.where(t == j, 1.0, 0.0),
        strict=(j < t, j > t),
        incl=(j <= t, j >= t),
        off=[((t >> (n + 1)) == (j >> (n + 1))) & ((t >> n) != (j >> n)) for n in range(6)],
        tri=(jnp.where(tj <= ti, 1.0, 0.0).astype(BF16), jnp.where(tj >= ti, 1.0, 0.0).astype(BF16)),
    )


def _wbd(x, cst):
    xb = x.astype(BF16)
    z = jnp.zeros_like(xb)
    return jnp.concatenate([jnp.where(cst["head0"], xb, z), jnp.where(cst["head0"], z, xb)], axis=0)


def _group_sum(x, e, exact=True):
    ew = e.shape[0]
    cols = range(0, x.shape[1], ew)
    if exact:
        hi, lo = _split2(x)
        parts = [_nn(hi[:, o:o + ew], e) + _nn(lo[:, o:o + ew], e) for o in cols]
    else:
        xb = x.astype(BF16)
        parts = [_nn(xb[:, o:o + ew], e) for o in cols]
    return parts[0] if len(parts) == 1 else jnp.concatenate(parts, axis=1)


def _pair_pipeline(h, ops, lg, states, cst, same_rows, out):
    L = CHUNK
    ncb = len(ops[0])
    idx = [(d, c) for d in range(2) for c in range(ncb)]
    half = lambda a: a[:, h * PW:(h + 1) * PW]

    ar, wbk, khbh, dg = {}, {}, {}, {}
    for d, c in idx:
        lw, r, kd, v, kk, b = (half(a) for a in ops[d][c])
        g = half(lg[d, c])
        g_l = g[0:1, :] if d == 1 else g[L - 1:L, :]
        at = -(kk * jnp.exp2(g - lw))
        rt = r * jnp.exp2(g)
        e_inv = jnp.exp2(-g)
        eg = jnp.exp2(g_l)
        bt, kt = b * e_inv, kd * e_inv
        ar[d, c] = jnp.concatenate([at, rt], axis=0).astype(BF16)
        wbk[d, c] = jnp.concatenate([_wbd(bt, cst), _wbd(kt, cst)], axis=0)
        khbh[d, c] = jnp.concatenate([(kt * eg).astype(BF16), (bt * eg).astype(BF16)], axis=0)
        dg[d, c] = jnp.where(cst["eye"], eg, 0.0).astype(BF16)
    yield

    gbk = {i: _nt(ar[i], wbk[i]) for i in idx}
    yield
    m_ab = {i: jnp.where(cst["strict"][i[0]], gbk[i][:L, :PW], 0.0) for i in idx}
    gb16 = {i: gbk[i].astype(BF16) for i in idx}
    zb = jnp.zeros((L, PW), BF16)
    m_abb = {i: jnp.where(cst["strict"][i[0]], gb16[i][:L, :PW], zb) for i in idx}
    m_rb = {i: jnp.where(cst["incl"][i[0]], gb16[i][L:, :PW], zb) for i in idx}
    m_akrk = {i: jnp.concatenate([jnp.where(cst["strict"][i[0]], gb16[i][:L, PW:], zb),
                                  jnp.where(cst["incl"][i[0]], gb16[i][L:, PW:], zb)], axis=0)
              for i in idx}

    x = {i: cst["eye_sbs"] + jnp.where(cst["off"][0], m_ab[i], 0.0) for i in idx}
    for off in cst["off"][1:]:
        p = {i: _nn(jnp.where(off, m_abb[i], zb), _wbd(x[i], cst)) for i in idx}
        yield
        x = {i: x[i] + _nn(x[i].astype(BF16), _wbd(p[i], cst)) for i in idx}
        yield
    xb = {i: x[i].astype(BF16) for i in idx}

    vsrc = {(d, c): (0, c) if same_rows else (d, c) for d, c in idx}
    vhalf = {i: half(ops[i[0]][i[1]][3]) for i in dict.fromkeys(vsrc.values())}
    wv = {i: _wbd(vhalf[i], cst) for i in vhalf}
    vb = {i: vhalf[i].astype(BF16) for i in vhalf}
    lk = {i: jnp.concatenate([jnp.concatenate([ar[i], m_akrk[i]], axis=1),
                              jnp.concatenate([dg[i], jnp.zeros_like(dg[i])], axis=1)], axis=0)
          for i in idx}

    s = [states[0][h], states[1][h]]
    for step in range(ncb):
        cur = [(0, step), (1, ncb - 1 - step)]
        ars = {i: _nn(lk[i], jnp.concatenate([s[i[0]].astype(BF16), wv[vsrc[i]]], axis=0)) for i in cur}
        yield
        u = {i: _nn(xb[i], _wbd(ars[i][:L], cst)) for i in cur}
        yield
        for i in cur:
            d, c = i
            out["y"][d, c, h] = ars[i][L:2 * L] + _nn(m_rb[i], _wbd(u[i], cst))
            upd = _tn(khbh[i], jnp.concatenate([vb[vsrc[i]], u[i].astype(BF16)], axis=0))
            s[d] = ars[i][2 * L:] + jnp.where(cst["bmask"], upd, 0.0)
        yield
    out["s"][0][h], out["s"][1][h] = s


def _block_step(ops, states, cst, same_rows):
    ncb = len(ops[0])
    npair = ops[0][0][0].shape[1] // PW

    lg = {}
    for d in range(2):
        for c in range(ncb):
            hi, mid = _split2(ops[d][c][0])
            tri = cst["tri"][d]
            lg[d, c] = _nn(tri, hi) + _nn(tri, mid)

    out = {"y": {}, "s": [[None] * npair, [None] * npair]}
    pending = [_pair_pipeline(h, ops, lg, states, cst, same_rows, out) for h in range(npair)]
    active = pending
    while active:
        active = [gen for gen in active if next(gen, "done") != "done"]
    ys = [[jnp.concatenate([out["y"][d, c, h] for h in range(npair)], axis=1) if npair > 1
           else out["y"][d, c, 0] for c in range(ncb)] for d in range(2)]
    return ys, out["s"]


def _rwkv_kernel(*refs, seq, gw, has_s0, want_state):
    it = iter(refs)
    r_ref, k_ref, v_ref, lo_ref, gb_ref = (next(it) for _ in range(5))
    w0_ref, w2_ref, a0_ref, a2_ref = (next(it) for _ in range(4))
    kk_ref, ka_ref, rk_ref, lng_ref, lnb_ref = (next(it) for _ in range(5))
    s0_ref = next(it) if has_s0 else None
    ob_ref = next(it)
    sf_ref = next(it) if want_state else None
    r_s, v_s, kn_s, lw_s, kd_s, b_s, y_s, st_s = (next(it) for _ in range(8))

    T = seq
    cst = _rwkv_consts()
    npair = gw // PW
    hpp = PW // HEAD_DIM_B
    gi = lax.broadcasted_iota(jnp.int32, (LW, LW), 0)
    gj = lax.broadcasted_iota(jnp.int32, (LW, LW), 1)
    e = jnp.where((gi >> HD_SHIFT) == (gj >> HD_SHIFT), 1.0, 0.0).astype(BF16)

    r = r_ref[...].astype(F32)
    k = k_ref[...].astype(F32)
    v = v_ref[...].astype(F32)
    lo = lo_ref[...].astype(F32)
    r_s[...] = r
    v_s[...] = v
    kk = k * kk_ref[...]
    kk = kk * lax.rsqrt(_group_sum(kk * kk, e, exact=False) + 1e-12)
    kn_s[...] = kk
    bonus_dot = _group_sum(r * k * rk_ref[...], e, exact=False)

    tanh_lo = jnp.tanh(lo).astype(BF16)
    lo_b = lo.astype(BF16)
    kka = k * ka_ref[...]

    def padded(w, before):
        parts = [jnp.zeros((before, gw), F32)] if before else []
        parts.append(w)
        after = LW - before - LORA_R
        if after:
            parts.append(jnp.zeros((after, gw), F32))
        return jnp.concatenate(parts, axis=0).astype(BF16)

    for d in range(2):
        w2p = padded(w2_ref[d], LORA_R * d)
        a2p = padded(a2_ref[d], LORA_R * (2 + d))
        z = w0_ref[d:d + 1, :] + _nn(tanh_lo, w2p)
        half_c = -0.5 * EXP_NEG_HALF * LOG2E
        lw_s[d] = half_c * jnp.tanh(0.5 * z) + half_c
        a = _sigmoid(a0_ref[d:d + 1, :] + _nn(lo_b, a2p))
        kd_s[d] = k + kka * (a - 1.0)
        b_s[d] = kk * a

    if has_s0:
        ident = jnp.where(cst["eye"], 1.0, 0.0).astype(BF16)
        for d in range(2):
            for h in range(npair):
                z0 = s0_ref[0, d, h * hpp:(h + 1) * hpp].reshape(PW, HEAD_DIM_B)
                hi, lo2 = _split2(z0)
                zt = _tn(hi, ident) + _tn(lo2, ident)
                st_s[d, h] = jnp.where(cst["bmask"], jnp.concatenate([zt] * hpp, axis=0), 0.0)
    else:
        st_s[...] = jnp.zeros(st_s.shape, F32)

    nblk = T // BLOCK_T
    ncb = BLOCK_T // CHUNK

    def block(i, carry):
        rows = []
        for d in range(2):
            blk = i if d == 0 else nblk - 1 - i
            rows.append([])
            for c in range(ncb):
                start = blk * BLOCK_T + c * CHUNK
                if not isinstance(start, int):
                    start = pl.multiple_of(start, CHUNK)
                rows[d].append(pl.ds(start, CHUNK))
        ops = [[(lw_s[d, rw, :], r_s[rw, :], kd_s[d, rw, :], v_s[rw, :], kn_s[rw, :], b_s[d, rw, :])
                for rw in rows[d]] for d in range(2)]
        states = [[st_s[d, h] for h in range(npair)] for d in range(2)]
        ys, s_new = _block_step(ops, states, cst, nblk == 1)
        for d in range(2):
            for c in range(ncb):
                y_s[d, rows[d][c], :] = ys[d][c]
            for h in range(npair):
                st_s[d, h] = s_new[d][h]
        return carry

    if nblk == 1:
        block(0, 0)
    else:
        lax.fori_loop(0, nblk, block, 0)

    y = y_s[0] + y_s[1]
    mu = _group_sum(y, e) * (1.0 / HEAD_DIM_B)
    dlt = y - mu
    var = _group_sum(dlt * dlt, e, exact=False) * (1.0 / HEAD_DIM_B)
    yn = dlt * lax.rsqrt(var + LNX_EPS) * lng_ref[...] + lnb_ref[...]
    out = (yn + bonus_dot * v_s[...]) * _silu(gb_ref[...].astype(F32))
    ob_ref[...] = out.astype(BF16)

    if want_state:
        ri = lax.broadcasted_iota(jnp.int32, (PW, HEAD_DIM_B), 0)
        ci = lax.broadcasted_iota(jnp.int32, (PW, HEAD_DIM_B), 1)
        fold = jnp.where((ri & (HEAD_DIM_B - 1)) == ci, 1.0, 0.0).astype(BF16)
        for d in range(2):
            for h in range(npair):
                hi, lo2 = _split2(st_s[d, h])
                sf = _tn(hi, fold) + _tn(lo2, fold)
                sf_ref[0, d, h * hpp:(h + 1) * hpp] = sf.reshape(hpp, HEAD_DIM_B, HEAD_DIM_B)


def _rwkv(u, ul, seq, gw, s0, want_state, decay_w0, decay_w2, iclr_a0, iclr_a2,
          k_k, k_a, r_k, lnx_g, lnx_b):
    m = u.shape[0]
    nbat = m // seq
    ng = D_MODEL // gw
    hg = gw // HEAD_DIM_B
    ublk = lambda sec: pl.BlockSpec((seq, gw), lambda b, g: (b, sec * ng + g))
    lblk = pl.BlockSpec((seq, LW), lambda b, g: (b, 0))
    vec2 = pl.BlockSpec((2, gw), lambda b, g: (0, g))
    mat2 = pl.BlockSpec((2, LORA_R, gw), lambda b, g: (0, 0, g))
    vec1 = pl.BlockSpec((1, gw), lambda b, g: (0, g))
    sblk = pl.BlockSpec((1, 2, hg, HEAD_DIM_B, HEAD_DIM_B), lambda b, g: (b, 0, g, 0, 0))
    in_specs = [ublk(SEC_R), ublk(SEC_KB), ublk(SEC_VB), lblk, ublk(SEC_GB),
                vec2, mat2, vec2, mat2, vec1, vec1, vec1, vec1, vec1]
    args = [u, u, u, ul, u,
            decay_w0, decay_w2, iclr_a0, iclr_a2,
            k_k.reshape(1, D_MODEL), k_a.reshape(1, D_MODEL), r_k.reshape(1, D_MODEL),
            lnx_g.reshape(1, D_MODEL), lnx_b.reshape(1, D_MODEL)]
    if s0 is not None:
        in_specs.append(sblk)
        args.append(s0)
    out_specs = [pl.BlockSpec((seq, gw), lambda b, g: (b, g))]
    out_shape = [jax.ShapeDtypeStruct((m, D_MODEL), BF16)]
    if want_state:
        out_specs.append(sblk)
        out_shape.append(jax.ShapeDtypeStruct((nbat, 2, N_HEADS_B, HEAD_DIM_B, HEAD_DIM_B), F32))
    tw = pltpu.VMEM((seq, gw), F32)
    tw2 = pltpu.VMEM((2, seq, gw), F32)
    res = pl.pallas_call(
        functools.partial(_rwkv_kernel, seq=seq, gw=gw, has_s0=s0 is not None, want_state=want_state),
        grid=(nbat, ng),
        in_specs=in_specs,
        out_specs=out_specs,
        out_shape=out_shape,
        scratch_shapes=[tw, tw, tw, tw2, tw2, tw2, tw2, pltpu.VMEM((2, gw // PW, PW, PW), F32)],
        compiler_params=pltpu.CompilerParams(
            dimension_semantics=("arbitrary", "arbitrary"), vmem_limit_bytes=VMEM_LIMIT_BIG),
        name="rwkv_state" if want_state else "rwkv",
    )(*args)
    return res


def _outproj_kernel(oa_ref, ob_ref, x_ref, mod_ref, w_ref, fg_ref, y_ref):
    acc = _nn(oa_ref[...], w_ref[0:D_MODEL, :]) + _nn(ob_ref[...], w_ref[D_MODEL:, :])
    gate = mod_ref[0][:, 2 * D_MODEL:]
    xo = x_ref[...] + gate * acc
    y_ref[...] = xo * lax.rsqrt(jnp.mean(xo * xo, axis=-1, keepdims=True) + NORM_EPS) * fg_ref[...]


def _outproj(o_a, o_b, x, mod3, w_out_bf, final_g, mod_row):
    m = x.shape[0]
    tm = OUTPROJ_TM
    blk = pl.BlockSpec((tm, D_MODEL), lambda i: (i, 0))
    return pl.pallas_call(
        _outproj_kernel,
        grid=(m // tm,),
        in_specs=[blk, blk, blk,
                  pl.BlockSpec((1, 1, 3 * D_MODEL), lambda i: (mod_row(i * tm), 0, 0)),
                  pl.BlockSpec((2 * D_MODEL, D_MODEL), lambda i: (0, 0)),
                  pl.BlockSpec((1, D_MODEL), lambda i: (0, 0))],
        out_specs=blk,
        out_shape=jax.ShapeDtypeStruct((m, D_MODEL), F32),
        compiler_params=pltpu.CompilerParams(vmem_limit_bytes=VMEM_LIMIT),
        name="outproj",
    )(o_a, o_b, x, mod3, w_out_bf, final_g)


def kernel(x_prompt, x_sample, cache_k, cache_v, state_rwkv, c, c_ctx, norm_g, w_ada, b_ada, w_in, lam_q1, lam_k1, lam_q2, lam_k2, subln_g, shift_mu, decay_w0, decay_w2, iclr_a0, iclr_a2, k_k, k_a, r_k, lnx_g, lnx_b, w_out, final_g):
    batch, seq, d = x_prompt.shape
    dbatch, dseq, _ = x_sample.shape
    past = cache_k.shape[2]
    assert d == D_MODEL and w_in.shape == (1, D_MODEL, D_IN) and dseq % GRID_W == 0

    cvec = jnp.concatenate([c_ctx[None, :], c, jnp.zeros((8 - 1 - dbatch, d), F32)], axis=0)
    mod3 = _mod(cvec, w_ada[0], b_ada[0]).reshape(8, 1, 3 * d)
    lo0 = 4 * D_MODEL + 3 * D_MODEL
    w_bf = w_in[0].astype(BF16)
    w_lora = w_bf[:, lo0:lo0 + LW]
    w_out_bf = w_out[0].astype(BF16)
    ctx_row = lambda tok: 0
    lat_row = lambda tok: 1 + tok // dseq
    lp = (decay_w0[0], decay_w2[0], iclr_a0[0], iclr_a2[0], k_k[0], k_a[0], r_k[0],
          lnx_g[0], lnx_b[0])
    mu = shift_mu[0]
    lam = (lam_q1, lam_k1, lam_q2, lam_k2, subln_g)
    fg = final_g.reshape(1, d)

    xp = x_prompt.reshape(batch * seq, d)
    u_c, ul_c, nk, nv = _inproj(xp, mod3, norm_g, w_bf, w_lora, mu, ctx_row, seq, True)
    oa_c = _attn_ctx(u_c, seq, *lam)
    ob_c, s_new = _rwkv(u_c, ul_c, seq, RWKV_GW_CTX, None, True, *lp)
    y_prompt = _outproj(oa_c, ob_c, xp, mod3, w_out_bf, fg, ctx_row).reshape(batch, seq, d)

    xs = x_sample.reshape(dbatch * dseq, d)
    u_l, ul_l = _inproj(xs, mod3, norm_g, w_bf, w_lora, mu, lat_row, dseq, False)
    cv = cache_v[:, 0].reshape(dbatch * past, d)
    ckt = jnp.transpose(cache_k[:, 0], (0, 2, 3, 4, 1))
    oa_l = _attn_lat(u_l, ckt, cv, dseq, past, *lam)
    (ob_l,) = _rwkv(u_l, ul_l, dseq, RWKV_GW_LAT, state_rwkv[:, 0], False, *lp)
    y_sample = _outproj(oa_l, ob_l, xs, mod3, w_out_bf, fg, lat_row).reshape(dbatch, dseq, d)

    new_s = s_new.reshape(batch, 1, 2, N_HEADS_B, HEAD_DIM_B, HEAD_DIM_B)
    return (y_prompt, y_sample, nk, nv, new_s)
```
